```python
import math
import jax, jax.numpy as jnp
from jax import lax
import numpy as np

D_MODEL = 2048
BATCH = 2
SEQ = 8192
DEPTH = 1

MEM_LEN = 256
A_HEADS = 6
A_HEAD_DIM = 128
A_KV_RANK = 256
IDX_HEADS = 16
IDX_DIM = 64
TOPK_MAX = 256
Q_BLOCK = 128
B_HEADS = 12
B_KV_HEADS = 2
B_HEAD_DIM = 64
WINDOW = 128
BAND_BLOCK = 128
C_HEADS = 4
C_HEAD_DIM = 128
NUM_BUCKETS = 32
MAX_DISTANCE = 128
EPS = 1e-6

A_WIDTH = A_HEADS * A_HEAD_DIM
B_WIDTH = B_HEADS * B_HEAD_DIM
B_KV_WIDTH = B_KV_HEADS * B_HEAD_DIM
C_WIDTH = C_HEADS * C_HEAD_DIM
N_BRANCH = 3
IN_SIZES = (A_WIDTH, A_KV_RANK, IDX_HEADS * IDX_DIM, IDX_DIM, IDX_HEADS, A_WIDTH,
            B_WIDTH, B_KV_WIDTH, B_KV_WIDTH, B_WIDTH,
            C_WIDTH, C_WIDTH,
            N_BRANCH * D_MODEL)
IN_COLS = sum(IN_SIZES)

kernel_name = "hybrid_dsa_swa_memory_gated"


def rmsnorm(x, g):
    xf = x.astype(jnp.float32)
    y = xf * lax.rsqrt(jnp.mean(xf * xf, axis=-1, keepdims=True) + EPS)
    return (y * g.astype(jnp.float32)).astype(x.dtype)


def layernorm(x, g, b):
    xf = x.astype(jnp.float32)
    mu = jnp.mean(xf, axis=-1, keepdims=True)
    var = jnp.mean(jnp.square(xf - mu), axis=-1, keepdims=True)
    y = (xf - mu) * lax.rsqrt(var + EPS)
    return (y * g.astype(jnp.float32) + b.astype(jnp.float32)).astype(x.dtype)


def split_cols(y, sizes):
    cuts = np.cumsum(np.array(sizes))[:-1].tolist()
    return jnp.split(y, cuts, axis=-1)


def t5_bucket(dist):
    n = jnp.maximum(dist, 0)
    max_exact = NUM_BUCKETS // 2
    nf = jnp.maximum(n, 1).astype(jnp.float32)
    large = max_exact + (jnp.log(nf / max_exact) / math.log(MAX_DISTANCE / max_exact)
                         * (NUM_BUCKETS - max_exact)).astype(jnp.int32)
    large = jnp.minimum(large, NUM_BUCKETS - 1)
    return jnp.where(n < max_exact, n, large)


def dsa_attention(q, k, v, iq, ik, iw, bias_tab, topk):
    bsz, seq, nh, dh = q.shape
    nb = seq // Q_BLOCK
    key_pos = jnp.arange(seq, dtype=jnp.int32)

    def to_blocks(t):
        return jnp.moveaxis(t.reshape((bsz, nb, Q_BLOCK) + t.shape[2:]), 1, 0)

    def block_fn(args):
        qb, iqb, iwb, t0 = args
        tpos = t0 + jnp.arange(Q_BLOCK, dtype=jnp.int32)
        rel = jax.nn.relu(jnp.einsum('bthd,bsd->bths', iqb, ik))
        score = jnp.einsum('bths,bth->bts', rel, iwb).astype(jnp.float32)
        causal = key_pos[None, :] <= tpos[:, None]
        score = jnp.where(causal[None], score, -jnp.inf)
        _, idx = lax.top_k(score, topk)
        kg = jax.vmap(lambda kk, ii: kk[ii])(k, idx)
        vg = jax.vmap(lambda vv, ii: vv[ii])(v, idx)
        logits = jnp.einsum('bthd,btkhd->bthk', qb, kg).astype(jnp.float32) * (dh ** -0.5)
        dist = tpos[None, :, None] - idx
        bias = jnp.transpose(bias_tab[t5_bucket(dist)], (0, 1, 3, 2)).astype(jnp.float32)
        logits = jnp.where((dist >= 0)[:, :, None, :], logits + bias, -jnp.inf)
        p = jax.nn.softmax(logits, axis=-1).astype(vg.dtype)
        return jnp.einsum('bthk,btkhd->bthd', p, vg)

    t0s = jnp.arange(nb, dtype=jnp.int32) * Q_BLOCK
    out = lax.map(block_fn, (to_blocks(q), to_blocks(iq), to_blocks(iw), t0s))
    return jnp.moveaxis(out, 0, 1).reshape(bsz, seq, nh * dh)


def sliding_window_attention(q, k, v, sinks, bias_tab):
    bsz, seq, hq, dh = q.shape
    hkv = k.shape[2]
    g = hq // hkv
    blk = BAND_BLOCK
    nb = seq // blk
    qb = q.reshape(bsz, nb, blk, hkv, g, dh)

    def band(t):
        tb = t.reshape(bsz, nb, blk, hkv, dh)
        prev = jnp.concatenate([jnp.zeros_like(tb[:, :1]), tb[:, :-1]], axis=1)
        return jnp.concatenate([prev, tb], axis=2)

    kband, vband = band(k), band(v)
    logits = jnp.einsum('bnqhgd,bnkhd->bnhgqk', qb, kband).astype(jnp.float32) * (dh ** -0.5)
    qi = jnp.arange(blk, dtype=jnp.int32)[:, None]
    kj = jnp.arange(2 * blk, dtype=jnp.int32)[None, :]
    dist = blk + qi - kj
    bias = jnp.transpose(bias_tab[t5_bucket(dist)], (2, 0, 1)).reshape(hkv, g, blk, 2 * blk)
    key_pos = jnp.arange(nb, dtype=jnp.int32)[:, None, None] * blk + kj[None] - blk
    valid = (dist >= 0) & (dist < WINDOW) & (key_pos >= 0)
    logits = jnp.where(valid[None, :, None, None], logits + bias.astype(jnp.float32), -jnp.inf)
    sink = jnp.broadcast_to(sinks.astype(jnp.float32).reshape(hkv, g, 1, 1), logits.shape[:-1] + (1,))
    p = jax.nn.softmax(jnp.concatenate([logits, sink], axis=-1), axis=-1)[..., :-1]
    out = jnp.einsum('bnhgqk,bnkhd->bnqhgd', p.astype(v.dtype), vband)
    return out.reshape(bsz, seq, hq * dh)


def memory_attention(q, k, v):
    bsz, seq, nh, dh = q.shape
    logits = jnp.einsum('bthd,bmhd->bhtm', q, k).astype(jnp.float32) * (dh ** -0.5)
    p = jax.nn.softmax(logits, axis=-1).astype(v.dtype)
    return jnp.einsum('bhtm,bmhd->bthd', p, v).reshape(bsz, seq, nh * dh)


def setup_inputs(seed: int = 0) -> dict:
    key = jax.random.key(seed)
    ks = jax.random.split(key, 24)
    f32 = jnp.float32

    def w(k, shape, fan_in):
        return jax.random.normal(k, shape, f32) * (fan_in ** -0.5)

    def gain(k, shape):
        return 1.0 + 0.05 * jax.random.normal(k, shape, f32)

    return {
        "x": jax.random.normal(ks[0], (BATCH, SEQ, D_MODEL), f32),
        "mem": jax.random.normal(ks[1], (BATCH, MEM_LEN, D_MODEL), f32),
        "norm_g": gain(ks[2], (DEPTH, D_MODEL)),
        "w_in": w(ks[3], (DEPTH, D_MODEL, IN_COLS), D_MODEL),
        "kv_norm_g": gain(ks[4], (DEPTH, A_KV_RANK)),
        "w_kv_up": w(ks[5], (DEPTH, A_KV_RANK, 2 * A_WIDTH), A_KV_RANK),
        "idx_k_ln_g": gain(ks[6], (DEPTH, IDX_DIM)),
        "idx_k_ln_b": 0.02 * jax.random.normal(ks[7], (DEPTH, IDX_DIM), f32),
        "q_norm_a": gain(ks[8], (DEPTH, A_HEAD_DIM)),
        "k_norm_a": gain(ks[9], (DEPTH, A_HEAD_DIM)),
        "q_norm_b": gain(ks[10], (DEPTH, B_HEAD_DIM)),
        "k_norm_b": gain(ks[11], (DEPTH, B_HEAD_DIM)),
        "sinks_b": 0.5 * jax.random.normal(ks[12], (DEPTH, B_HEADS), f32),
        "mem_norm_g": gain(ks[13], (DEPTH, D_MODEL)),
        "w_mem_kv": w(ks[14], (DEPTH, D_MODEL, 2 * C_WIDTH), D_MODEL),
        "q_norm_c": gain(ks[15], (DEPTH, C_HEAD_DIM)),
        "k_norm_c": gain(ks[16], (DEPTH, C_HEAD_DIM)),
        "w_up_a": w(ks[17], (DEPTH, A_WIDTH, D_MODEL), A_WIDTH),
        "w_up_b": w(ks[18], (DEPTH, B_WIDTH, D_MODEL), B_WIDTH),
        "w_up_c": w(ks[19], (DEPTH, C_WIDTH, D_MODEL), C_WIDTH),
        "gate_bias": 0.02 * jax.random.normal(ks[20], (DEPTH, N_BRANCH * D_MODEL), f32),
        "w_o": w(ks[21], (DEPTH, D_MODEL, D_MODEL), D_MODEL),
        "rel_bias": 0.5 * jax.random.normal(ks[22], (NUM_BUCKETS, A_HEADS + B_HEADS), f32),
    }


def reference(x, mem, norm_g, w_in, kv_norm_g, w_kv_up, idx_k_ln_g, idx_k_ln_b,
              q_norm_a, k_norm_a, q_norm_b, k_norm_b, sinks_b, mem_norm_g, w_mem_kv,
              q_norm_c, k_norm_c, w_up_a, w_up_b, w_up_c, gate_bias, w_o, rel_bias):
    bsz, seq, _ = x.shape
    mlen = mem.shape[1]
    topk = min(TOPK_MAX, seq // 4)
    bias_a = rel_bias[:, :A_HEADS]
    bias_b = rel_bias[:, A_HEADS:]
    for l in range(DEPTH):
        h = rmsnorm(x, norm_g[l])
        (a_q, a_ckv, i_q, i_k, i_w, a_gate,
         b_q, b_k, b_v, b_gate,
         c_q, c_gate, mix_gate) = split_cols(h @ w_in[l], IN_SIZES)

        ckv = rmsnorm(a_ckv, kv_norm_g[l])
        a_k, a_v = jnp.split(ckv @ w_kv_up[l], 2, axis=-1)
        qa = rmsnorm(a_q.reshape(bsz, seq, A_HEADS, A_HEAD_DIM), q_norm_a[l])
        ka = rmsnorm(a_k.reshape(bsz, seq, A_HEADS, A_HEAD_DIM), k_norm_a[l])
        va = a_v.reshape(bsz, seq, A_HEADS, A_HEAD_DIM)
        iq = i_q.reshape(bsz, seq, IDX_HEADS, IDX_DIM)
        ik = layernorm(i_k, idx_k_ln_g[l], idx_k_ln_b[l])
        iw = i_w * (IDX_HEADS ** -0.5) * (IDX_DIM ** -0.5)
        oa = dsa_attention(qa, ka, va, iq, ik, iw, bias_a, topk) * jax.nn.silu(a_gate)

        qb = rmsnorm(b_q.reshape(bsz, seq, B_HEADS, B_HEAD_DIM), q_norm_b[l])
        kb = rmsnorm(b_k.reshape(bsz, seq, B_KV_HEADS, B_HEAD_DIM), k_norm_b[l])
        vb = b_v.reshape(bsz, seq, B_KV_HEADS, B_HEAD_DIM)
        ob = sliding_window_attention(qb, kb, vb, sinks_b[l], bias_b) * jax.nn.silu(b_gate)

        m_k, m_v = jnp.split(rmsnorm(mem, mem_norm_g[l]) @ w_mem_kv[l], 2, axis=-1)
        qc = rmsnorm(c_q.reshape(bsz, seq, C_HEADS, C_HEAD_DIM), q_norm_c[l])
        kc = rmsnorm(m_k.reshape(bsz, mlen, C_HEADS, C_HEAD_DIM), k_norm_c[l])
        vc = m_v.reshape(bsz, mlen, C_HEADS, C_HEAD_DIM)
        oc = memory_attention(qc, kc, vc) * jax.nn.silu(c_gate)

        g_a, g_b, g_c = jnp.split(jax.nn.sigmoid(mix_gate + gate_bias[l]), N_BRANCH, axis=-1)
        merged = g_a * (oa @ w_up_a[l]) + g_b * (ob @ w_up_b[l]) + g_c * (oc @ w_up_c[l])
        x = x + merged @ w_o[l]
    return x
```

```python
import functools
import math

import numpy as np
import jax
import jax.numpy as jnp
from jax import lax
from jax.experimental import pallas as pl
from jax.experimental.pallas import tpu as pltpu

F32 = jnp.float32
BF16 = jnp.bfloat16

EPS = 1e-6
A_HEADS, A_HEAD_DIM, A_KV_RANK = 6, 128, 256
IDX_HEADS, IDX_DIM, TOPK_MAX = 16, 64, 256
B_HEADS, B_KV_HEADS, B_HEAD_DIM, WINDOW = 12, 2, 64, 128
C_HEADS, C_HEAD_DIM = 4, 128
NUM_BUCKETS, MAX_DISTANCE = 32, 128
N_BRANCH = 3
A_WIDTH = A_HEADS * A_HEAD_DIM
B_WIDTH = B_HEADS * B_HEAD_DIM
B_KV_WIDTH = B_KV_HEADS * B_HEAD_DIM
C_WIDTH = C_HEADS * C_HEAD_DIM

LANES = 128
VMEM_LIMIT = 56 * 1024 * 1024
NEG_BIG = -1e30

ROW_TILE = 512
DSA_T = 256
SWA_T = 512
FINAL_NCHUNK = 4


def _cparams(sem):
    return pltpu.CompilerParams(dimension_semantics=sem, vmem_limit_bytes=VMEM_LIMIT)


def _dot(a, b):
    return jnp.dot(a, b, preferred_element_type=F32)


def _dot_nt(a, b):
    return lax.dot_general(a, b, (((1,), (1,)), ((), ())), preferred_element_type=F32)


def _rms_rows(x, g):
    ms = jnp.mean(x * x, axis=-1, keepdims=True)
    return x * lax.rsqrt(ms + EPS) * g


def _headnorm128(seg, g):
    ms = jnp.mean(seg * seg, axis=-1, keepdims=True)
    return seg * lax.rsqrt(ms + EPS) * g


def _headnorm64_pair(seg, g2):
    sq = seg * seg
    low = lax.broadcasted_iota(jnp.int32, seg.shape, 1) < B_HEAD_DIM
    s_all = jnp.sum(sq, axis=-1, keepdims=True)
    s_lo = jnp.sum(jnp.where(low, sq, 0.0), axis=-1, keepdims=True)
    ms = jnp.where(low, s_lo, s_all - s_lo) * (1.0 / B_HEAD_DIM)
    return seg * lax.rsqrt(ms + EPS) * g2


def _silu(y):
    return y / (1.0 + jnp.exp(-y))


def _proj_a_kernel(x_ref, g_ref, w_ref, kvg_ref, wkv_ref, qg_ref, kg_ref, qa_ref, ka_ref, va_ref):
    h = _rms_rows(x_ref[...], g_ref[...]).astype(BF16)
    y = _dot(h, w_ref[...])
    scale = A_HEAD_DIM ** -0.5
    for hh in range(A_HEADS):
        sl = slice(hh * A_HEAD_DIM, (hh + 1) * A_HEAD_DIM)
        qa_ref[:, sl] = (_headnorm128(y[:, sl], qg_ref[...]) * scale).astype(BF16)
    ckv = _rms_rows(y[:, A_WIDTH:A_WIDTH + A_KV_RANK], kvg_ref[...]).astype(BF16)
    kv = _dot(ckv, wkv_ref[...])
    for hh in range(A_HEADS):
        sl = slice(hh * A_HEAD_DIM, (hh + 1) * A_HEAD_DIM)
        ka_ref[:, sl] = _headnorm128(kv[:, sl], kg_ref[...]).astype(BF16)
    va_ref[...] = kv[:, A_WIDTH:].astype(BF16)


def _proj_i_kernel(x_ref, g_ref, w_ref, lng_ref, lnb_ref, iq_ref, ik_ref, iw_ref):
    h = _rms_rows(x_ref[...], g_ref[...]).astype(BF16)
    y = _dot(h, w_ref[...])
    for hh in range(IDX_HEADS):
        iq_ref[hh] = y[:, hh * IDX_DIM:(hh + 1) * IDX_DIM].astype(BF16)
    k0 = IDX_HEADS * IDX_DIM
    ik = y[:, k0:k0 + IDX_DIM]
    mu = jnp.mean(ik, axis=-1, keepdims=True)
    d = ik - mu
    var = jnp.mean(d * d, axis=-1, keepdims=True)
    ik_ref[...] = (d * lax.rsqrt(var + EPS) * lng_ref[...] + lnb_ref[...]).astype(BF16)
    w0 = k0 + LANES
    iw_ref[...] = y[:, w0:w0 + IDX_HEADS] * (IDX_HEADS ** -0.5) * (IDX_DIM ** -0.5)


def _proj_g_kernel(x_ref, g_ref, w_ref, ga_ref, gb_ref, gc_ref):
    h = _rms_rows(x_ref[...], g_ref[...]).astype(BF16)
    y = _silu(_dot(h, w_ref[...]))
    ga_ref[...] = y[:, :A_WIDTH].astype(BF16)
    gb_ref[...] = y[:, A_WIDTH:A_WIDTH + B_WIDTH].astype(BF16)
    gc_ref[...] = y[:, A_WIDTH + B_WIDTH:].astype(BF16)


def _proj_bc_kernel(x_ref, g_ref, w_ref, qbg_ref, kbg_ref, qcg_ref, qb_ref, kb_ref, vb_ref, qc_ref):
    h = _rms_rows(x_ref[...], g_ref[...]).astype(BF16)
    y = _dot(h, w_ref[...])
    sb = B_HEAD_DIM ** -0.5
    for p in range(B_WIDTH // LANES):
        sl = slice(p * LANES, (p + 1) * LANES)
        qb_ref[:, sl] = (_headnorm64_pair(y[:, sl], qbg_ref[...]) * sb).astype(BF16)
    k0 = B_WIDTH
    kb_ref[...] = _headnorm64_pair(y[:, k0:k0 + B_KV_WIDTH], kbg_ref[...]).astype(BF16)
    vb_ref[...] = y[:, k0 + B_KV_WIDTH:k0 + 2 * B_KV_WIDTH].astype(BF16)
    c0 = k0 + 2 * B_KV_WIDTH
    sc = C_HEAD_DIM ** -0.5
    for hh in range(C_HEADS):
        sl = slice(hh * C_HEAD_DIM, (hh + 1) * C_HEAD_DIM)
        qc_ref[:, sl] = (_headnorm128(y[:, c0 + hh * C_HEAD_DIM:c0 + (hh + 1) * C_HEAD_DIM],
                                      qcg_ref[...]) * sc).astype(BF16)


def _proj_mem_kernel(x_ref, g_ref, w_ref, kg_ref, kc_ref, vc_ref):
    h = _rms_rows(x_ref[...], g_ref[...]).astype(BF16)
    y = _dot(h, w_ref[...])
    for hh in range(C_HEADS):
        sl = slice(hh * C_HEAD_DIM, (hh + 1) * C_HEAD_DIM)
        kc_ref[:, sl] = _headnorm128(y[:, sl], kg_ref[...]).astype(BF16)
    vc_ref[...] = y[:, C_WIDTH:].astype(BF16)


def _row_spec(tm, cols):
    return pl.BlockSpec((tm, cols), lambda i: (i, 0))


def _full_spec(shape):
    nd = len(shape)
    return pl.BlockSpec(shape, lambda i: (0,) * nd)


def _proj_call(kernel_fn, name, x2d, tm, consts, out_shapes, out_specs):
    n, d = x2d.shape
    in_specs = [_row_spec(tm, d)] + [_full_spec(c.shape) for c in consts]
    return pl.pallas_call(
        kernel_fn,
        grid=(n // tm,),
        in_specs=in_specs,
        out_specs=out_specs,
        out_shape=out_shapes,
        compiler_params=_cparams(("parallel",)),
        name=name,
    )(x2d, *consts)


def _t5_bucket_np(dist):
    n = np.maximum(dist, 0)
    max_exact = NUM_BUCKETS // 2
    nf = np.maximum(n, 1).astype(np.float32)
    large = max_exact + (np.log(nf / max_exact) / math.log(MAX_DISTANCE / max_exact)
                         * (NUM_BUCKETS - max_exact)).astype(np.int32)
    large = np.minimum(large, NUM_BUCKETS - 1)
    return np.where(n < max_exact, n, large).astype(np.int32)


def _bias_tile_kernel(tab_ref, bucket_ref, o_ref, *, head0, shift_bucket):
    h = pl.program_id(0) + head0
    bucket = bucket_ref[...]
    acc = jnp.zeros(bucket.shape, F32)
    for b in range(NUM_BUCKETS):
        acc = jnp.where(bucket == b, tab_ref[b, h], acc)
    if shift_bucket is not None:
        acc = acc - tab_ref[shift_bucket, h]
    o_ref[...] = jnp.where(bucket < 0, NEG_BIG, acc)


def _bias_tiles(rel_bias, bucket_np, head0, nheads, shift_bucket):
    r, c = bucket_np.shape
    return pl.pallas_call(
        functools.partial(_bias_tile_kernel, head0=head0, shift_bucket=shift_bucket),
        grid=(nheads,),
        in_specs=[pl.BlockSpec(memory_space=pltpu.SMEM), pl.BlockSpec((r, c), lambda h: (0, 0))],
        out_specs=pl.BlockSpec((None, r, c), lambda h: (h, 0, 0)),
        out_shape=jax.ShapeDtypeStruct((nheads, r, c), F32),
        compiler_params=_cparams(("arbitrary",)),
        name="bias_tiles",
    )(rel_bias, jnp.asarray(bucket_np))


def _dsa_kernel(iq_ref, iw_ref, ikt_ref, q_ref, k_ref, v_ref, g_ref, rb_ref, o_ref,
                sc_ref, m_ref, l_ref, acc_ref, *, topk):
    T = DSA_T
    qi = pl.program_id(1)
    kf = float(topk)

    def score_chunk(j, diag):
        ik = ikt_ref[:, pl.ds(pl.multiple_of(j * T, T), T)]
        acc = jnp.zeros((T, T), F32)
        for hh in range(IDX_HEADS):
            s = _dot(iq_ref[hh], ik)
            acc = acc + iw_ref[:, hh:hh + 1] * jnp.maximum(s, 0.0)
        if diag:
            row = lax.broadcasted_iota(jnp.int32, (T, T), 0)
            col = lax.broadcasted_iota(jnp.int32, (T, T), 1)
            causal = col <= row
            lo_c = jnp.min(jnp.where(causal, acc, jnp.inf), axis=-1, keepdims=True)
            acc = jnp.where(causal, acc, -jnp.inf)
        else:
            lo_c = jnp.min(acc, axis=-1, keepdims=True)
        sc_ref[j] = acc
        return lo_c, jnp.max(acc, axis=-1, keepdims=True)

    def p1_body(j, carry):
        mn, mx = carry
        lo_c, hi_c = score_chunk(j, False)
        return jnp.minimum(mn, lo_c), jnp.maximum(mx, hi_c)

    mn0 = jnp.full((T, 1), jnp.inf, F32)
    mx0 = jnp.full((T, 1), -jnp.inf, F32)
    mn, mx = lax.fori_loop(0, qi, p1_body, (mn0, mx0))
    lo_c, hi_c = score_chunk(qi, True)
    mn = jnp.minimum(mn, lo_c)
    mx = jnp.maximum(mx, hi_c)

    def count_ge(v, strict=False):
        def body(j, c):
            s = sc_ref[j]
            hit = (s > v) if strict else (s >= v)
            return c + jnp.sum(jnp.where(hit, 1.0, 0.0), axis=-1, keepdims=True)
        return lax.fori_loop(0, qi + 1, body, jnp.zeros((T, 1), F32))

    n_causal = (qi * T + 1 + lax.broadcasted_iota(jnp.int32, (T, 1), 0)).astype(F32)

    def bis_cond(st):
        return st[5] > 0

    def bis_body(st):
        lo, hi, mid, c_lo, done, _ = st
        cnt = count_ge(mid)
        up = jnp.logical_and(cnt >= kf, done == 0.0)
        dn = jnp.logical_and(cnt < kf, done == 0.0)
        lo = jnp.where(up, mid, lo)
        c_lo = jnp.where(up, cnt, c_lo)
        hi = jnp.where(dn, mid, hi)
        mid = lo + 0.5 * (hi - lo)
        fin = jnp.logical_or(c_lo == kf, jnp.logical_or(mid <= lo, mid >= hi))
        done = jnp.where(fin, 1.0, done)
        n_active = jnp.sum(1.0 - done).astype(jnp.int32)
        return lo, hi, mid, c_lo, done, n_active

    done0 = jnp.where(n_causal <= kf, 1.0, 0.0)
    st0 = (mn, mx, mx, n_causal, done0, jnp.sum(1.0 - done0).astype(jnp.int32))
    thr, _, _, c_thr, _, _ = lax.while_loop(bis_cond, bis_body, st0)

    n_tied_rows = jnp.sum(jnp.where(c_thr > kf, 1.0, 0.0)).astype(jnp.int32)

    @pl.when(n_tied_rows > 0)
    def _():
        need = kf - count_ge(thr, strict=True)
        r = lax.broadcasted_iota(jnp.int32, (T, T), 0)
        c = lax.broadcasted_iota(jnp.int32, (T, T), 1)
        before = jnp.where(r < c, 1.0, 0.0).astype(BF16)

        def body(j, seen):
            s = sc_ref[j]
            eq = jnp.where(s == thr, 1.0, 0.0)
            rank = seen + _dot(eq.astype(BF16), before)
            drop = jnp.logical_and(c_thr > kf, jnp.logical_and(eq > 0.0, rank >= need))
            sc_ref[j] = jnp.where(drop, -jnp.inf, s)
            return seen + jnp.sum(eq, axis=-1, keepdims=True)

        lax.fori_loop(0, qi + 1, body, jnp.zeros((T, 1), F32))

    m_ref[...] = jnp.full(m_ref.shape, NEG_BIG, F32)
    l_ref[...] = jnp.zeros(l_ref.shape, F32)
    acc_ref[...] = jnp.zeros(acc_ref.shape, F32)

    def attend_chunk(j, band):
        sel = sc_ref[j] >= thr
        start = pl.multiple_of(j * T, T)
        for hh in range(A_HEADS):
            sl = slice(hh * A_HEAD_DIM, (hh + 1) * A_HEAD_DIM)
            s = _dot_nt(q_ref[:, sl], k_ref[pl.ds(start, T), sl])
            if band is not None:
                s = s + rb_ref[hh, :, band * T:(band + 1) * T]
            s = jnp.where(sel, s, NEG_BIG)
            m_prev = m_ref[hh]
            m_new = jnp.maximum(m_prev, jnp.max(s, axis=-1, keepdims=True))
            alpha = jnp.exp(m_prev - m_new)
            p = jnp.exp(s - m_new)
            l_ref[hh] = alpha * l_ref[hh] + jnp.sum(p, axis=-1, keepdims=True)
            acc_ref[hh] = alpha * acc_ref[hh] + _dot(p.astype(BF16), v_ref[pl.ds(start, T), sl])
            m_ref[hh] = m_new

    def p3_body(j, c):
        attend_chunk(j, None)
        return c

    lax.fori_loop(0, qi - 1, p3_body, 0)

    @pl.when(qi > 0)
    def _():
        attend_chunk(qi - 1, 0)

    attend_chunk(qi, 1)

    for hh in range(A_HEADS):
        sl = slice(hh * A_HEAD_DIM, (hh + 1) * A_HEAD_DIM)
        o = acc_ref[hh] / l_ref[hh]
        o_ref[:, sl] = (o * g_ref[:, sl].astype(F32)).astype(BF16)


def _dsa_call(iq, iw, ikt, qa, ka, va, ga, rb, topk):
    bsz, seq, _ = qa.shape
    T = DSA_T
    nq = seq // T
    once = pl.Buffered(1)
    return pl.pallas_call(
        functools.partial(_dsa_kernel, topk=topk),
        grid=(bsz, nq),
        in_specs=[
            pl.BlockSpec((None, IDX_HEADS, T, IDX_DIM), lambda b, i: (b, 0, i, 0)),
            pl.BlockSpec((None, T, IDX_HEADS), lambda b, i: (b, i, 0)),
            pl.BlockSpec((None, IDX_DIM, seq), lambda b, i: (b, 0, 0), pipeline_mode=once),
            pl.BlockSpec((None, T, A_WIDTH), lambda b, i: (b, i, 0)),
            pl.BlockSpec((None, seq, A_WIDTH), lambda b, i: (b, 0, 0), pipeline_mode=once),
            pl.BlockSpec((None, seq, A_WIDTH), lambda b, i: (b, 0, 0), pipeline_mode=once),
            pl.BlockSpec((None, T, A_WIDTH), lambda b, i: (b, i, 0)),
            pl.BlockSpec((A_HEADS, T, 2 * T), lambda b, i: (0, 0, 0), pipeline_mode=once),
        ],
        out_specs=pl.BlockSpec((None, T, A_WIDTH), lambda b, i: (b, i, 0)),
        out_shape=jax.ShapeDtypeStruct((bsz, seq, A_WIDTH), BF16),
        scratch_shapes=[
            pltpu.VMEM((nq, T, T), F32),
            pltpu.VMEM((A_HEADS, T, 1), F32),
            pltpu.VMEM((A_HEADS, T, 1), F32),
            pltpu.VMEM((A_HEADS, T, A_HEAD_DIM), F32),
        ],
        compiler_params=_cparams(("arbitrary", "arbitrary")),
        name="dsa",
    )(iq, iw, ikt, qa, ka, va, ga, rb)


def _swa_kernel(sink_ref, q_ref, kc_ref, vc_ref, kp_ref, vp_ref, g_ref, bias_ref, o_ref):
    first = pl.program_id(1) == 0
    blk = WINDOW
    group = B_HEADS // B_KV_HEADS
    for sb in range(SWA_T // blk):
        rows = slice(sb * blk, (sb + 1) * blk)
        if sb == 0:
            k_prev, v_prev = kp_ref[...], vp_ref[...]
        else:
            prev = slice((sb - 1) * blk, sb * blk)
            k_prev, v_prev = kc_ref[prev, :], vc_ref[prev, :]
        k_cur, v_cur = kc_ref[rows, :], vc_ref[rows, :]
        for hh in range(B_HEADS):
            gsl = slice((hh // group) * B_HEAD_DIM, (hh // group + 1) * B_HEAD_DIM)
            hsl = slice(hh * B_HEAD_DIM, (hh + 1) * B_HEAD_DIM)
            q = q_ref[rows, hsl]
            lp = _dot_nt(q, k_prev[:, gsl]) + bias_ref[hh, :, :blk]
            lc = _dot_nt(q, k_cur[:, gsl]) + bias_ref[hh, :, blk:]
            if sb == 0:
                lp = jnp.where(first, NEG_BIG, lp)
            sink = sink_ref[hh]
            m = jnp.maximum(jnp.max(lp, axis=-1, keepdims=True), jnp.max(lc, axis=-1, keepdims=True))
            m = jnp.maximum(m, sink)
            pp = jnp.exp(lp - m)
            pc = jnp.exp(lc - m)
            den = (jnp.sum(pp, axis=-1, keepdims=True) + jnp.sum(pc, axis=-1, keepdims=True)
                   + jnp.exp(sink - m))
            o = (_dot(pp.astype(BF16), v_prev[:, gsl]) + _dot(pc.astype(BF16), v_cur[:, gsl])) / den
            o_ref[rows, hsl] = (o * g_ref[rows, hsl].astype(F32)).astype(BF16)


def _swa_call(sinks, qb, kb, vb, gb, bias):
    bsz, seq, _ = qb.shape
    T = SWA_T
    per = T // WINDOW
    cur = lambda b, i: (b, i, 0)
    prev = lambda b, i: (b, jnp.maximum(i * per - 1, 0), 0)
    return pl.pallas_call(
        _swa_kernel,
        grid=(bsz, seq // T),
        in_specs=[
            pl.BlockSpec(memory_space=pltpu.SMEM),
            pl.BlockSpec((None, T, B_WIDTH), cur),
            pl.BlockSpec((None, T, B_KV_WIDTH), cur),
            pl.BlockSpec((None, T, B_KV_WIDTH), cur),
            pl.BlockSpec((None, WINDOW, B_KV_WIDTH), prev),
            pl.BlockSpec((None, WINDOW, B_KV_WIDTH), prev),
            pl.BlockSpec((None, T, B_WIDTH), cur),
            pl.BlockSpec((B_HEADS, WINDOW, 2 * WINDOW), lambda b, i: (0, 0, 0)),
        ],
        out_specs=pl.BlockSpec((None, T, B_WIDTH), cur),
        out_shape=jax.ShapeDtypeStruct((bsz, seq, B_WIDTH), BF16),
        compiler_params=_cparams(("parallel", "parallel")),
        name="swa",
    )(sinks, qb, kb, vb, kb, vb, gb, bias)


def _mem_kernel(q_ref, k_ref, v_ref, g_ref, o_ref):
    for hh in range(C_HEADS):
        sl = slice(hh * C_HEAD_DIM, (hh + 1) * C_HEAD_DIM)
        s = _dot_nt(q_ref[:, sl], k_ref[:, sl])
        m = jnp.max(s, axis=-1, keepdims=True)
        p = jnp.exp(s - m)
        den = jnp.sum(p, axis=-1, keepdims=True)
        o = _dot(p.astype(BF16), v_ref[:, sl]) / den
        o_ref[:, sl] = (o * g_ref[:, sl].astype(F32)).astype(BF16)


def _mem_call(qc, kc, vc, gc):
    bsz, seq, _ = qc.shape
    mlen = kc.shape[1]
    T = SWA_T
    cur = lambda b, i: (b, i, 0)
    whole = lambda b, i: (b, 0, 0)
    return pl.pallas_call(
        _mem_kernel,
        grid=(bsz, seq // T),
        in_specs=[
            pl.BlockSpec((None, T, C_WIDTH), cur),
            pl.BlockSpec((None, mlen, C_WIDTH), whole),
            pl.BlockSpec((None, mlen, C_WIDTH), whole),
            pl.BlockSpec((None, T, C_WIDTH), cur),
        ],
        out_specs=pl.BlockSpec((None, T, C_WIDTH), cur),
        out_shape=jax.ShapeDtypeStruct((bsz, seq, C_WIDTH), BF16),
        compiler_params=_cparams(("parallel", "parallel")),
        name="mem_attn",
    )(qc, kc, vc, gc)


def _final_kernel(x_ref, g_ref, oa_ref, ob_ref, oc_ref, wma_ref, wmb_ref, wmc_ref,
                  ba_ref, bb_ref, bc_ref, wua_ref, wub_ref, wuc_ref, wo_ref, out_ref, h_ref, acc_ref):
    n = pl.program_id(1)

    @pl.when(n == 0)
    def _():
        h_ref[...] = _rms_rows(x_ref[...], g_ref[...]).astype(BF16)
        acc_ref[...] = jnp.zeros(acc_ref.shape, F32)

    h = h_ref[...]

    def branch(o_ref, wm_ref, b_ref, wu_ref):
        gate = 1.0 / (1.0 + jnp.exp(-(_dot(h, wm_ref[...]) + b_ref[...])))
        return gate * _dot(o_ref[...], wu_ref[...])

    merged = (branch(oa_ref, wma_ref, ba_ref, wua_ref) + branch(ob_ref, wmb_ref, bb_ref, wub_ref)
              + branch(oc_ref, wmc_ref, bc_ref, wuc_ref))
    acc_ref[...] += _dot(merged.astype(BF16), wo_ref[...])

    @pl.when(n == pl.num_programs(1) - 1)
    def _():
        out_ref[...] = x_ref[...] + acc_ref[...]


def _final_call(x2d, g, oa, ob, oc, wmix, gate_bias, wua, wub, wuc, wo):
    n, d = x2d.shape
    tm = ROW_TILE
    nch = FINAL_NCHUNK
    cw = d // nch
    row = lambda i, c: (i, 0)

    def col(br):
        return lambda i, c: (0, br * nch + c)

    return pl.pallas_call(
        _final_kernel,
        grid=(n // tm, nch),
        in_specs=[
            pl.BlockSpec((tm, d), row),
            pl.BlockSpec((1, d), lambda i, c: (0, 0)),
            pl.BlockSpec((tm, A_WIDTH), row),
            pl.BlockSpec((tm, B_WIDTH), row),
            pl.BlockSpec((tm, C_WIDTH), row),
            pl.BlockSpec((d, cw), col(0)), pl.BlockSpec((d, cw), col(1)), pl.BlockSpec((d, cw), col(2)),
            pl.BlockSpec((1, cw), col(0)), pl.BlockSpec((1, cw), col(1)), pl.BlockSpec((1, cw), col(2)),
            pl.BlockSpec((A_WIDTH, cw), lambda i, c: (0, c)),
            pl.BlockSpec((B_WIDTH, cw), lambda i, c: (0, c)),
            pl.BlockSpec((C_WIDTH, cw), lambda i, c: (0, c)),
            pl.BlockSpec((cw, d), lambda i, c: (c, 0)),
        ],
        out_specs=pl.BlockSpec((tm, d), row),
        out_shape=jax.ShapeDtypeStruct((n, d), F32),
        scratch_shapes=[pltpu.VMEM((tm, d), BF16), pltpu.VMEM((tm, d), F32)],
        compiler_params=_cparams(("parallel", "arbitrary")),
        name="merge_out",
    )(x2d, g, oa, ob, oc, wmix, wmix, wmix, gate_bias, gate_bias, gate_bias, wua, wub, wuc, wo)


def _layer(x, mem, norm_g, w_in, kv_norm_g, w_kv_up, idx_k_ln_g, idx_k_ln_b, q_norm_a, k_norm_a,
           q_norm_b, k_norm_b, sinks_b, mem_norm_g, w_mem_kv, q_norm_c, k_norm_c,
           w_up_a, w_up_b, w_up_c, gate_bias, w_o, rel_bias):
    bsz, seq, d = x.shape
    mlen = mem.shape[1]
    n = bsz * seq
    tm = ROW_TILE
    topk = min(TOPK_MAX, seq // 4)
    x2d = x.reshape(n, d)
    row2 = lambda v: v.reshape(1, -1)

    sizes = (A_WIDTH, A_KV_RANK, IDX_HEADS * IDX_DIM, IDX_DIM, IDX_HEADS, A_WIDTH,
             B_WIDTH, B_KV_WIDTH, B_KV_WIDTH, B_WIDTH, C_WIDTH, C_WIDTH, N_BRANCH * d)
    cuts = np.cumsum((0,) + sizes)
    (w_aq, w_ackv, w_iq, w_ik, w_iw, w_ag, w_bq, w_bk, w_bv, w_bg, w_cq, w_cg, w_mix) = [
        w_in[:, cuts[i]:cuts[i + 1]].astype(BF16) for i in range(len(sizes))]
    zpad = lambda c: jnp.zeros((d, c), BF16)
    w_grp_a = jnp.concatenate([w_aq, w_ackv], axis=1)
    w_grp_i = jnp.concatenate([w_iq, w_ik, zpad(LANES - IDX_DIM), w_iw, zpad(LANES - IDX_HEADS)], axis=1)
    w_grp_g = jnp.concatenate([w_ag, w_bg, w_cg], axis=1)
    w_grp_bc = jnp.concatenate([w_bq, w_bk, w_bv, w_cq], axis=1)
    g_x = row2(norm_g)

    qa, ka, va = _proj_call(
        _proj_a_kernel, "proj_a", x2d, tm,
        [g_x, w_grp_a, row2(kv_norm_g), w_kv_up.astype(BF16), row2(q_norm_a), row2(k_norm_a)],
        [jax.ShapeDtypeStruct((n, A_WIDTH), BF16)] * 3,
        [_row_spec(tm, A_WIDTH)] * 3)

    per_b = seq // tm
    iq, ik, iw = _proj_call(
        _proj_i_kernel, "proj_i", x2d, tm,
        [g_x, w_grp_i, row2(idx_k_ln_g), row2(idx_k_ln_b)],
        [jax.ShapeDtypeStruct((bsz, IDX_HEADS, seq, IDX_DIM), BF16),
         jax.ShapeDtypeStruct((n, IDX_DIM), BF16),
         jax.ShapeDtypeStruct((n, IDX_HEADS), F32)],
        [pl.BlockSpec((None, IDX_HEADS, tm, IDX_DIM), lambda i: (i // per_b, 0, i % per_b, 0)),
         _row_spec(tm, IDX_DIM), _row_spec(tm, IDX_HEADS)])

    ga, gb, gc = _proj_call(
        _proj_g_kernel, "proj_g", x2d, tm, [g_x, w_grp_g],
        [jax.ShapeDtypeStruct((n, A_WIDTH), BF16), jax.ShapeDtypeStruct((n, B_WIDTH), BF16),
         jax.ShapeDtypeStruct((n, C_WIDTH), BF16)],
        [_row_spec(tm, A_WIDTH), _row_spec(tm, B_WIDTH), _row_spec(tm, C_WIDTH)])

    qb, kb, vb, qc = _proj_call(
        _proj_bc_kernel, "proj_bc", x2d, tm,
        [g_x, w_grp_bc, row2(jnp.tile(q_norm_b, 2)), row2(jnp.tile(k_norm_b, 2)), row2(q_norm_c)],
        [jax.ShapeDtypeStruct((n, B_WIDTH), BF16), jax.ShapeDtypeStruct((n, B_KV_WIDTH), BF16),
         jax.ShapeDtypeStruct((n, B_KV_WIDTH), BF16), jax.ShapeDtypeStruct((n, C_WIDTH), BF16)],
        [_row_spec(tm, B_WIDTH), _row_spec(tm, B_KV_WIDTH), _row_spec(tm, B_KV_WIDTH),
         _row_spec(tm, C_WIDTH)])

    mrows = bsz * mlen
    kc, vc = _proj_call(
        _proj_mem_kernel, "proj_mem", mem.reshape(mrows, d), min(tm, mrows),
        [row2(mem_norm_g), w_mem_kv.astype(BF16), row2(k_norm_c)],
        [jax.ShapeDtypeStruct((mrows, C_WIDTH), BF16)] * 2,
        [_row_spec(min(tm, mrows), C_WIDTH)] * 2)

    T = DSA_T
    i_idx = np.arange(T)[:, None]
    dist_a = T + i_idx - np.arange(2 * T)[None, :]
    rb_a = _bias_tiles(rel_bias, _t5_bucket_np(dist_a), 0, A_HEADS, NUM_BUCKETS - 1)
    i_idx = np.arange(WINDOW)[:, None]
    dist_b = WINDOW + i_idx - np.arange(2 * WINDOW)[None, :]
    bucket_b = np.where((dist_b >= 0) & (dist_b < WINDOW), _t5_bucket_np(dist_b), -1).astype(np.int32)
    bias_b = _bias_tiles(rel_bias, bucket_b, A_HEADS, B_HEADS, None)

    r3 = lambda v, w: v.reshape(bsz, seq, w)
    ikt = jnp.swapaxes(ik.reshape(bsz, seq, IDX_DIM), 1, 2)
    oa = _dsa_call(iq, iw.reshape(bsz, seq, IDX_HEADS), ikt, r3(qa, A_WIDTH), r3(ka, A_WIDTH),
                   r3(va, A_WIDTH), r3(ga, A_WIDTH), rb_a, topk)
    ob = _swa_call(sinks_b, r3(qb, B_WIDTH), r3(kb, B_KV_WIDTH), r3(vb, B_KV_WIDTH), r3(gb, B_WIDTH), bias_b)
    oc = _mem_call(r3(qc, C_WIDTH), kc.reshape(bsz, mlen, C_WIDTH), vc.reshape(bsz, mlen, C_WIDTH),
                   r3(gc, C_WIDTH))

    out = _final_call(x2d, g_x, oa.reshape(n, A_WIDTH), ob.reshape(n, B_WIDTH), oc.reshape(n, C_WIDTH),
                      w_mix, row2(gate_bias), w_up_a.astype(BF16), w_up_b.astype(BF16),
                      w_up_c.astype(BF16), w_o.astype(BF16))
    return out.reshape(bsz, seq, d)


def kernel(x, mem, norm_g, w_in, kv_norm_g, w_kv_up, idx_k_ln_g, idx_k_ln_b, q_norm_a, k_norm_a, q_norm_b, k_norm_b, sinks_b, mem_norm_g, w_mem_kv, q_norm_c, k_norm_c, w_up_a, w_up_b, w_up_c, gate_bias, w_o, rel_bias):
    for l in range(norm_g.shape[0]):
        x = _layer(x, mem, norm_g[l], w_in[l], kv_norm_g[l], w_kv_up[l], idx_k_ln_g[l], idx_k_ln_b[l],
                   q_norm_a[l], k_norm_a[l], q_norm_b[l], k_norm_b[l], sinks_b[l], mem_norm_g[l],
                   w_mem_kv[l], q_norm_c[l], k_norm_c[l], w_up_a[l], w_up_b[l], w_up_c[l],
                   gate_bias[l], w_o[l], rel_bias)
    return x
```

```python
import functools
import math

import numpy as np
import jax
import jax.numpy as jnp
from jax import lax
from jax.experimental import pallas as pl
from jax.experimental.pallas import tpu as pltpu

F32 = jnp.float32
BF16 = jnp.bfloat16

EPS = 1e-6
A_HEADS, A_HEAD_DIM, A_KV_RANK = 6, 128, 256
IDX_HEADS, IDX_DIM, TOPK_MAX = 16, 64, 256
B_HEADS, B_KV_HEADS, B_HEAD_DIM, WINDOW = 12, 2, 64, 128
C_HEADS, C_HEAD_DIM = 4, 128
NUM_BUCKETS, MAX_DISTANCE = 32, 128
N_BRANCH = 3
A_WIDTH = A_HEADS * A_HEAD_DIM
B_WIDTH = B_HEADS * B_HEAD_DIM
B_KV_WIDTH = B_KV_HEADS * B_HEAD_DIM
C_WIDTH = C_HEADS * C_HEAD_DIM

LANES = 128
VMEM_LIMIT = 56 * 1024 * 1024
NEG_BIG = -1e30

ROW_TILE = 512
DSA_T = 256
SWA_T = 512
FINAL_NCHUNK = 4


def _cparams(sem):
    return pltpu.CompilerParams(dimension_semantics=sem, vmem_limit_bytes=VMEM_LIMIT)


def _dot(a, b):
    return jnp.dot(a, b, preferred_element_type=F32)


def _dot_nt(a, b):
    return lax.dot_general(a, b, (((1,), (1,)), ((), ())), preferred_element_type=F32)


def _rms_rows(x, g):
    ms = jnp.mean(x * x, axis=-1, keepdims=True)
    return x * lax.rsqrt(ms + EPS) * g


def _headnorm128(seg, g):
    ms = jnp.mean(seg * seg, axis=-1, keepdims=True)
    return seg * lax.rsqrt(ms + EPS) * g


def _headnorm64_pair(seg, g2):
    sq = seg * seg
    low = lax.broadcasted_iota(jnp.int32, seg.shape, 1) < B_HEAD_DIM
    s_all = jnp.sum(sq, axis=-1, keepdims=True)
    s_lo = jnp.sum(jnp.where(low, sq, 0.0), axis=-1, keepdims=True)
    ms = jnp.where(low, s_lo, s_all - s_lo) * (1.0 / B_HEAD_DIM)
    return seg * lax.rsqrt(ms + EPS) * g2


def _silu(y):
    return y / (1.0 + jnp.exp(-y))


def _proj_a_kernel(x_ref, g_ref, w_ref, kvg_ref, wkv_ref, qg_ref, kg_ref, qa_ref, ka_ref, va_ref):
    h = _rms_rows(x_ref[...], g_ref[...]).astype(BF16)
    y = _dot(h, w_ref[...])
    scale = A_HEAD_DIM ** -0.5
    for hh in range(A_HEADS):
        sl = slice(hh * A_HEAD_DIM, (hh + 1) * A_HEAD_DIM)
        qa_ref[:, sl] = (_headnorm128(y[:, sl], qg_ref[...]) * scale).astype(BF16)
    ckv = _rms_rows(y[:, A_WIDTH:A_WIDTH + A_KV_RANK], kvg_ref[...]).astype(BF16)
    kv = _dot(ckv, wkv_ref[...])
    for hh in range(A_HEADS):
        sl = slice(hh * A_HEAD_DIM, (hh + 1) * A_HEAD_DIM)
        ka_ref[:, sl] = _headnorm128(kv[:, sl], kg_ref[...]).astype(BF16)
    va_ref[...] = kv[:, A_WIDTH:].astype(BF16)


def _proj_i_kernel(x_ref, g_ref, w_ref, lng_ref, lnb_ref, iq_ref, ik_ref, iw_ref):
    h = _rms_rows(x_ref[...], g_ref[...]).astype(BF16)
    y = _dot(h, w_ref[...])
    for hh in range(IDX_HEADS):
        iq_ref[hh] = y[:, hh * IDX_DIM:(hh + 1) * IDX_DIM].astype(BF16)
    k0 = IDX_HEADS * IDX_DIM
    ik = y[:, k0:k0 + IDX_DIM]
    mu = jnp.mean(ik, axis=-1, keepdims=True)
    d = ik - mu
    var = jnp.mean(d * d, axis=-1, keepdims=True)
    ik_ref[...] = (d * lax.rsqrt(var + EPS) * lng_ref[...] + lnb_ref[...]).astype(BF16)
    w0 = k0 + LANES
    iw_ref[...] = y[:, w0:w0 + IDX_HEADS] * (IDX_HEADS ** -0.5) * (IDX_DIM ** -0.5)


def _proj_g_kernel(x_ref, g_ref, w_ref, ga_ref, gb_ref, gc_ref):
    h = _rms_rows(x_ref[...], g_ref[...]).astype(BF16)
    y = _silu(_dot(h, w_ref[...]))
    ga_ref[...] = y[:, :A_WIDTH].astype(BF16)
    gb_ref[...] = y[:, A_WIDTH:A_WIDTH + B_WIDTH].astype(BF16)
    gc_ref[...] = y[:, A_WIDTH + B_WIDTH:].astype(BF16)


def _proj_bc_kernel(x_ref, g_ref, w_ref, qbg_ref, kbg_ref, qcg_ref, qb_ref, kb_ref, vb_ref, qc_ref):
    h = _rms_rows(x_ref[...], g_ref[...]).astype(BF16)
    y = _dot(h, w_ref[...])
    sb = B_HEAD_DIM ** -0.5
    for p in range(B_WIDTH // LANES):
        sl = slice(p * LANES, (p + 1) * LANES)
        qb_ref[:, sl] = (_headnorm64_pair(y[:, sl], qbg_ref[...]) * sb).astype(BF16)
    k0 = B_WIDTH
    kb_ref[...] = _headnorm64_pair(y[:, k0:k0 + B_KV_WIDTH], kbg_ref[...]).astype(BF16)
    vb_ref[...] = y[:, k0 + B_KV_WIDTH:k0 + 2 * B_KV_WIDTH].astype(BF16)
    c0 = k0 + 2 * B_KV_WIDTH
    sc = C_HEAD_DIM ** -0.5
    for hh in range(C_HEADS):
        sl = slice(hh * C_HEAD_DIM, (hh + 1) * C_HEAD_DIM)
        qc_ref[:, sl] = (_headnorm128(y[:, c0 + hh * C_HEAD_DIM:c0 + (hh + 1) * C_HEAD_DIM],
                                      qcg_ref[...]) * sc).astype(BF16)


def _proj_mem_kernel(x_ref, g_ref, w_ref, kg_ref, kc_ref, vc_ref):
    h = _rms_rows(x_ref[...], g_ref[...]).astype(BF16)
    y = _dot(h, w_ref[...])
    for hh in range(C_HEADS):
        sl = slice(hh * C_HEAD_DIM, (hh + 1) * C_HEAD_DIM)
        kc_ref[:, sl] = _headnorm128(y[:, sl], kg_ref[...]).astype(BF16)
    vc_ref[...] = y[:, C_WIDTH:].astype(BF16)


def _row_spec(tm, cols):
    return pl.BlockSpec((tm, cols), lambda i: (i, 0))


def _full_spec(shape):
    nd = len(shape)
    return pl.BlockSpec(shape, lambda i: (0,) * nd)


def _proj_call(kernel_fn, name, x2d, tm, consts, out_shapes, out_specs):
    n, d = x2d.shape
    in_specs = [_row_spec(tm, d)] + [_full_spec(c.shape) for c in consts]
    return pl.pallas_call(
        kernel_fn,
        grid=(n // tm,),
        in_specs=in_specs,
        out_specs=out_specs,
        out_shape=out_shapes,
        compiler_params=_cparams(("parallel",)),
        name=name,
    )(x2d, *consts)


def _t5_bucket_np(dist):
    n = np.maximum(dist, 0)
    max_exact = NUM_BUCKETS // 2
    nf = np.maximum(n, 1).astype(np.float32)
    large = max_exact + (np.log(nf / max_exact) / math.log(MAX_DISTANCE / max_exact)
                         * (NUM_BUCKETS - max_exact)).astype(np.int32)
    large = np.minimum(large, NUM_BUCKETS - 1)
    return np.where(n < max_exact, n, large).astype(np.int32)


def _bias_tile_kernel(tab_ref, bucket_ref, o_ref, *, head0, shift_bucket):
    h = pl.program_id(0) + head0
    bucket = bucket_ref[...]
    acc = jnp.zeros(bucket.shape, F32)
    for b in range(NUM_BUCKETS):
        acc = jnp.where(bucket == b, tab_ref[b, h], acc)
    if shift_bucket is not None:
        acc = acc - tab_ref[shift_bucket, h]
    o_ref[...] = jnp.where(bucket < 0, NEG_BIG, acc)


def _bias_tiles(rel_bias, bucket_np, head0, nheads, shift_bucket):
    r, c = bucket_np.shape
    return pl.pallas_call(
        functools.partial(_bias_tile_kernel, head0=head0, shift_bucket=shift_bucket),
        grid=(nheads,),
        in_specs=[pl.BlockSpec(memory_space=pltpu.SMEM), pl.BlockSpec((r, c), lambda h: (0, 0))],
        out_specs=pl.BlockSpec((None, r, c), lambda h: (h, 0, 0)),
        out_shape=jax.ShapeDtypeStruct((nheads, r, c), F32),
        compiler_params=_cparams(("arbitrary",)),
        name="bias_tiles",
    )(rel_bias, jnp.asarray(bucket_np))


def _dsa_kernel(iq_ref, iw_ref, ikt_ref, q_ref, k_ref, v_ref, g_ref, rb_ref, o_ref,
                sc_ref, wb_ref, m_ref, l_ref, acc_ref, *, topk):
    T = DSA_T
    qi = pl.program_id(1)
    kf = float(topk)

    def lanes(col):
        return jnp.broadcast_to(col, (T, LANES))

    def halves(tile):
        return [tile[:, c * LANES:(c + 1) * LANES] for c in range(T // LANES)]

    for hh in range(IDX_HEADS):
        wb_ref[hh] = lanes(iw_ref[:, hh:hh + 1])

    def score_chunk(j, diag):
        ik = ikt_ref[:, pl.ds(pl.multiple_of(j * T, T), T)]
        accs = [jnp.zeros((T, LANES), F32) for _ in range(T // LANES)]
        for hh in range(IDX_HEADS):
            s = _dot(iq_ref[hh], ik)
            w = wb_ref[hh]
            accs = [a + w * jnp.maximum(sh, 0.0) for a, sh in zip(accs, halves(s))]
        if diag:
            row = lax.broadcasted_iota(jnp.int32, (T, LANES), 0)
            col = lax.broadcasted_iota(jnp.int32, (T, LANES), 1)
            causal = [col + c * LANES <= row for c in range(T // LANES)]
            lows = [jnp.where(cm, a, jnp.inf) for cm, a in zip(causal, accs)]
            accs = [jnp.where(cm, a, -jnp.inf) for cm, a in zip(causal, accs)]
        else:
            lows = accs
        for c, a in enumerate(accs):
            sc_ref[j, :, c * LANES:(c + 1) * LANES] = a
        return functools.reduce(jnp.minimum, lows), functools.reduce(jnp.maximum, accs)

    def p1_body(j, carry):
        mn, mx = carry
        lo_c, hi_c = score_chunk(j, False)
        return jnp.minimum(mn, lo_c), jnp.maximum(mx, hi_c)

    mn0 = jnp.full((T, LANES), jnp.inf, F32)
    mx0 = jnp.full((T, LANES), -jnp.inf, F32)
    mn, mx = lax.fori_loop(0, qi, p1_body, (mn0, mx0))
    lo_c, hi_c = score_chunk(qi, True)
    mn = lanes(jnp.min(jnp.minimum(mn, lo_c), axis=-1, keepdims=True))
    mx = lanes(jnp.max(jnp.maximum(mx, hi_c), axis=-1, keepdims=True))

    def count_ge(v, strict=False):
        def body(j, c):
            for sh in halves(sc_ref[j]):
                hit = (sh > v) if strict else (sh >= v)
                c = c + jnp.where(hit, 1.0, 0.0)
            return c
        c = lax.fori_loop(0, qi + 1, body, jnp.zeros((T, LANES), F32))
        return lanes(jnp.sum(c, axis=-1, keepdims=True))

    n_causal = (qi * T + 1 + lax.broadcasted_iota(jnp.int32, (T, LANES), 0)).astype(F32)

    def bis_cond(st):
        return st[5] > 0.0

    def bis_body(st):
        lo, hi, mid, c_lo, done, _ = st
        cnt = count_ge(mid)
        up = jnp.logical_and(cnt >= kf, done == 0.0)
        dn = jnp.logical_and(cnt < kf, done == 0.0)
        lo = jnp.where(up, mid, lo)
        c_lo = jnp.where(up, cnt, c_lo)
        hi = jnp.where(dn, mid, hi)
        mid = lo + 0.5 * (hi - lo)
        fin = jnp.logical_or(c_lo == kf, jnp.logical_or(mid <= lo, mid >= hi))
        done = jnp.where(fin, 1.0, done)
        return lo, hi, mid, c_lo, done, jnp.max(1.0 - done)

    done0 = jnp.where(n_causal <= kf, 1.0, 0.0)
    st0 = (mn, mx, mx, n_causal, done0, jnp.max(1.0 - done0))
    thr, _, _, c_thr, _, _ = lax.while_loop(bis_cond, bis_body, st0)

    tied_f = jnp.where(c_thr > kf, 1.0, 0.0)

    @pl.when(jnp.max(tied_f) > 0.0)
    def _():
        need = (kf - count_ge(thr, strict=True))[:, :1]
        r = lax.broadcasted_iota(jnp.int32, (T, T), 0)
        c = lax.broadcasted_iota(jnp.int32, (T, T), 1)
        before = jnp.where(r < c, 1.0, 0.0).astype(BF16)
        thr_col = thr[:, :1]
        tied_col = tied_f[:, :1] > 0.0

        def body(j, seen):
            s = sc_ref[j]
            eq = jnp.where(s == thr_col, 1.0, 0.0)
            rank = seen + _dot(eq.astype(BF16), before)
            drop = jnp.logical_and(tied_col, jnp.logical_and(eq > 0.0, rank >= need))
            sc_ref[j] = jnp.where(drop, -jnp.inf, s)
            return seen + jnp.sum(eq, axis=-1, keepdims=True)

        lax.fori_loop(0, qi + 1, body, jnp.zeros((T, 1), F32))

    m_ref[...] = jnp.full(m_ref.shape, NEG_BIG, F32)
    l_ref[...] = jnp.zeros(l_ref.shape, F32)
    acc_ref[...] = jnp.zeros(acc_ref.shape, F32)

    def attend_chunk(j, band):
        sel = [sh >= thr for sh in halves(sc_ref[j])]
        start = pl.multiple_of(j * T, T)
        for hh in range(A_HEADS):
            sl = slice(hh * A_HEAD_DIM, (hh + 1) * A_HEAD_DIM)
            s = _dot_nt(q_ref[:, sl], k_ref[pl.ds(start, T), sl])
            if band is not None:
                s = s + rb_ref[hh, :, band * T:(band + 1) * T]
            sh = [jnp.where(m, x, NEG_BIG) for m, x in zip(sel, halves(s))]
            m_prev = m_ref[hh]
            m_cur = jnp.max(functools.reduce(jnp.maximum, sh), axis=-1, keepdims=True)
            m_new = jnp.maximum(m_prev, lanes(m_cur))
            alpha = jnp.exp(m_prev - m_new)
            ph = [jnp.exp(x - m_new) for x in sh]
            l_ref[hh] = alpha * l_ref[hh] + functools.reduce(jnp.add, ph)
            p = jnp.concatenate([x.astype(BF16) for x in ph], axis=-1)
            acc_ref[hh] = alpha * acc_ref[hh] + _dot(p, v_ref[pl.ds(start, T), sl])
            m_ref[hh] = m_new

    def p3_body(j, c):
        attend_chunk(j, None)
        return c

    lax.fori_loop(0, qi - 1, p3_body, 0)

    @pl.when(qi > 0)
    def _():
        attend_chunk(qi - 1, 0)

    attend_chunk(qi, 1)

    for hh in range(A_HEADS):
        sl = slice(hh * A_HEAD_DIM, (hh + 1) * A_HEAD_DIM)
        den = jnp.sum(l_ref[hh], axis=-1, keepdims=True)
        o_ref[:, sl] = (acc_ref[hh] / den * g_ref[:, sl].astype(F32)).astype(BF16)


def _dsa_call(iq, iw, ikt, qa, ka, va, ga, rb, topk):
    bsz, seq, _ = qa.shape
    T = DSA_T
    nq = seq // T
    once = pl.Buffered(1)
    return pl.pallas_call(
        functools.partial(_dsa_kernel, topk=topk),
        grid=(bsz, nq),
        in_specs=[
            pl.BlockSpec((None, IDX_HEADS, T, IDX_DIM), lambda b, i: (b, 0, i, 0)),
            pl.BlockSpec((None, T, IDX_HEADS), lambda b, i: (b, i, 0)),
            pl.BlockSpec((None, IDX_DIM, seq), lambda b, i: (b, 0, 0), pipeline_mode=once),
            pl.BlockSpec((None, T, A_WIDTH), lambda b, i: (b, i, 0)),
            pl.BlockSpec((None, seq, A_WIDTH), lambda b, i: (b, 0, 0), pipeline_mode=once),
            pl.BlockSpec((None, seq, A_WIDTH), lambda b, i: (b, 0, 0), pipeline_mode=once),
            pl.BlockSpec((None, T, A_WIDTH), lambda b, i: (b, i, 0)),
            pl.BlockSpec((A_HEADS, T, 2 * T), lambda b, i: (0, 0, 0), pipeline_mode=once),
        ],
        out_specs=pl.BlockSpec((None, T, A_WIDTH), lambda b, i: (b, i, 0)),
        out_shape=jax.ShapeDtypeStruct((bsz, seq, A_WIDTH), BF16),
        scratch_shapes=[
            pltpu.VMEM((nq, T, T), F32),
            pltpu.VMEM((IDX_HEADS, T, LANES), F32),
            pltpu.VMEM((A_HEADS, T, LANES), F32),
            pltpu.VMEM((A_HEADS, T, LANES), F32),
            pltpu.VMEM((A_HEADS, T, A_HEAD_DIM), F32),
        ],
        compiler_params=_cparams(("arbitrary", "arbitrary")),
        name="dsa",
    )(iq, iw, ikt, qa, ka, va, ga, rb)


def _swa_kernel(sink_ref, q_ref, kc_ref, vc_ref, kp_ref, vp_ref, g_ref, bias_ref, o_ref):
    first = pl.program_id(1) == 0
    blk = WINDOW
    group = B_HEADS // B_KV_HEADS
    for sb in range(SWA_T // blk):
        rows = slice(sb * blk, (sb + 1) * blk)
        if sb == 0:
            k_prev, v_prev = kp_ref[...], vp_ref[...]
        else:
            prev = slice((sb - 1) * blk, sb * blk)
            k_prev, v_prev = kc_ref[prev, :], vc_ref[prev, :]
        k_cur, v_cur = kc_ref[rows, :], vc_ref[rows, :]
        for hh in range(B_HEADS):
            gsl = slice((hh // group) * B_HEAD_DIM, (hh // group + 1) * B_HEAD_DIM)
            hsl = slice(hh * B_HEAD_DIM, (hh + 1) * B_HEAD_DIM)
            q = q_ref[rows, hsl]
            lp = _dot_nt(q, k_prev[:, gsl]) + bias_ref[hh, :, :blk]
            lc = _dot_nt(q, k_cur[:, gsl]) + bias_ref[hh, :, blk:]
            if sb == 0:
                lp = jnp.where(first, NEG_BIG, lp)
            sink = sink_ref[hh]
            m = jnp.maximum(jnp.max(lp, axis=-1, keepdims=True), jnp.max(lc, axis=-1, keepdims=True))
            m = jnp.maximum(m, sink)
            pp = jnp.exp(lp - m)
            pc = jnp.exp(lc - m)
            den = (jnp.sum(pp, axis=-1, keepdims=True) + jnp.sum(pc, axis=-1, keepdims=True)
                   + jnp.exp(sink - m))
            o = (_dot(pp.astype(BF16), v_prev[:, gsl]) + _dot(pc.astype(BF16), v_cur[:, gsl])) / den
            o_ref[rows, hsl] = (o * g_ref[rows, hsl].astype(F32)).astype(BF16)


def _swa_call(sinks, qb, kb, vb, gb, bias):
    bsz, seq, _ = qb.shape
    T = SWA_T
    per = T // WINDOW
    cur = lambda b, i: (b, i, 0)
    prev = lambda b, i: (b, jnp.maximum(i * per - 1, 0), 0)
    return pl.pallas_call(
        _swa_kernel,
        grid=(bsz, seq // T),
        in_specs=[
            pl.BlockSpec(memory_space=pltpu.SMEM),
            pl.BlockSpec((None, T, B_WIDTH), cur),
            pl.BlockSpec((None, T, B_KV_WIDTH), cur),
            pl.BlockSpec((None, T, B_KV_WIDTH), cur),
            pl.BlockSpec((None, WINDOW, B_KV_WIDTH), prev),
            pl.BlockSpec((None, WINDOW, B_KV_WIDTH), prev),
            pl.BlockSpec((None, T, B_WIDTH), cur),
            pl.BlockSpec((B_HEADS, WINDOW, 2 * WINDOW), lambda b, i: (0, 0, 0)),
        ],
        out_specs=pl.BlockSpec((None, T, B_WIDTH), cur),
        out_shape=jax.ShapeDtypeStruct((bsz, seq, B_WIDTH), BF16),
        compiler_params=_cparams(("parallel", "parallel")),
        name="swa",
    )(sinks, qb, kb, vb, kb, vb, gb, bias)


def _mem_kernel(q_ref, k_ref, v_ref, g_ref, o_ref):
    for hh in range(C_HEADS):
        sl = slice(hh * C_HEAD_DIM, (hh + 1) * C_HEAD_DIM)
        s = _dot_nt(q_ref[:, sl], k_ref[:, sl])
        m = jnp.max(s, axis=-1, keepdims=True)
        p = jnp.exp(s - m)
        den = jnp.sum(p, axis=-1, keepdims=True)
        o = _dot(p.astype(BF16), v_ref[:, sl]) / den
        o_ref[:, sl] = (o * g_ref[:, sl].astype(F32)).astype(BF16)


def _mem_call(qc, kc, vc, gc):
    bsz, seq, _ = qc.shape
    mlen = kc.shape[1]
    T = SWA_T
    cur = lambda b, i: (b, i, 0)
    whole = lambda b, i: (b, 0, 0)
    return pl.pallas_call(
        _mem_kernel,
        grid=(bsz, seq // T),
        in_specs=[
            pl.BlockSpec((None, T, C_WIDTH), cur),
            pl.BlockSpec((None, mlen, C_WIDTH), whole),
            pl.BlockSpec((None, mlen, C_WIDTH), whole),
            pl.BlockSpec((None, T, C_WIDTH), cur),
        ],
        out_specs=pl.BlockSpec((None, T, C_WIDTH), cur),
        out_shape=jax.ShapeDtypeStruct((bsz, seq, C_WIDTH), BF16),
        compiler_params=_cparams(("parallel", "parallel")),
        name="mem_attn",
    )(qc, kc, vc, gc)


def _final_kernel(x_ref, g_ref, oa_ref, ob_ref, oc_ref, wma_ref, wmb_ref, wmc_ref,
                  ba_ref, bb_ref, bc_ref, wua_ref, wub_ref, wuc_ref, wo_ref, out_ref, h_ref, acc_ref):
    n = pl.program_id(1)

    @pl.when(n == 0)
    def _():
        h_ref[...] = _rms_rows(x_ref[...], g_ref[...]).astype(BF16)
        acc_ref[...] = jnp.zeros(acc_ref.shape, F32)

    h = h_ref[...]

    def branch(o_ref, wm_ref, b_ref, wu_ref):
        gate = 1.0 / (1.0 + jnp.exp(-(_dot(h, wm_ref[...]) + b_ref[...])))
        return gate * _dot(o_ref[...], wu_ref[...])

    merged = (branch(oa_ref, wma_ref, ba_ref, wua_ref) + branch(ob_ref, wmb_ref, bb_ref, wub_ref)
              + branch(oc_ref, wmc_ref, bc_ref, wuc_ref))
    acc_ref[...] += _dot(merged.astype(BF16), wo_ref[...])

    @pl.when(n == pl.num_programs(1) - 1)
    def _():
        out_ref[...] = x_ref[...] + acc_ref[...]


def _final_call(x2d, g, oa, ob, oc, wmix, gate_bias, wua, wub, wuc, wo):
    n, d = x2d.shape
    tm = ROW_TILE
    nch = FINAL_NCHUNK
    cw = d // nch
    row = lambda i, c: (i, 0)

    def col(br):
        return lambda i, c: (0, br * nch + c)

    return pl.pallas_call(
        _final_kernel,
        grid=(n // tm, nch),
        in_specs=[
            pl.BlockSpec((tm, d), row),
            pl.BlockSpec((1, d), lambda i, c: (0, 0)),
            pl.BlockSpec((tm, A_WIDTH), row),
            pl.BlockSpec((tm, B_WIDTH), row),
            pl.BlockSpec((tm, C_WIDTH), row),
            pl.BlockSpec((d, cw), col(0)), pl.BlockSpec((d, cw), col(1)), pl.BlockSpec((d, cw), col(2)),
            pl.BlockSpec((1, cw), col(0)), pl.BlockSpec((1, cw), col(1)), pl.BlockSpec((1, cw), col(2)),
            pl.BlockSpec((A_WIDTH, cw), lambda i, c: (0, c)),
            pl.BlockSpec((B_WIDTH, cw), lambda i, c: (0, c)),
            pl.BlockSpec((C_WIDTH, cw), lambda i, c: (0, c)),
            pl.BlockSpec((cw, d), lambda i, c: (c, 0)),
        ],
        out_specs=pl.BlockSpec((tm, d), row),
        out_shape=jax.ShapeDtypeStruct((n, d), F32),
        scratch_shapes=[pltpu.VMEM((tm, d), BF16), pltpu.VMEM((tm, d), F32)],
        compiler_params=_cparams(("parallel", "arbitrary")),
        name="merge_out",
    )(x2d, g, oa, ob, oc, wmix, wmix, wmix, gate_bias, gate_bias, gate_bias, wua, wub, wuc, wo)


def _layer(x, mem, norm_g, w_in, kv_norm_g, w_kv_up, idx_k_ln_g, idx_k_ln_b, q_norm_a, k_norm_a,
           q_norm_b, k_norm_b, sinks_b, mem_norm_g, w_mem_kv, q_norm_c, k_norm_c,
           w_up_a, w_up_b, w_up_c, gate_bias, w_o, rel_bias):
    bsz, seq, d = x.shape
    mlen = mem.shape[1]
    n = bsz * seq
    tm = ROW_TILE
    topk = min(TOPK_MAX, seq // 4)
    x2d = x.reshape(n, d)
    row2 = lambda v: v.reshape(1, -1)

    sizes = (A_WIDTH, A_KV_RANK, IDX_HEADS * IDX_DIM, IDX_DIM, IDX_HEADS, A_WIDTH,
             B_WIDTH, B_KV_WIDTH, B_KV_WIDTH, B_WIDTH, C_WIDTH, C_WIDTH, N_BRANCH * d)
    cuts = np.cumsum((0,) + sizes)
    (w_aq, w_ackv, w_iq, w_ik, w_iw, w_ag, w_bq, w_bk, w_bv, w_bg, w_cq, w_cg, w_mix) = [
        w_in[:, cuts[i]:cuts[i + 1]].astype(BF16) for i in range(len(sizes))]
    zpad = lambda c: jnp.zeros((d, c), BF16)
    w_grp_a = jnp.concatenate([w_aq, w_ackv], axis=1)
    w_grp_i = jnp.concatenate([w_iq, w_ik, zpad(LANES - IDX_DIM), w_iw, zpad(LANES - IDX_HEADS)], axis=1)
    w_grp_g = jnp.concatenate([w_ag, w_bg, w_cg], axis=1)
    w_grp_bc = jnp.concatenate([w_bq, w_bk, w_bv, w_cq], axis=1)
    g_x = row2(norm_g)

    qa, ka, va = _proj_call(
        _proj_a_kernel, "proj_a", x2d, tm,
        [g_x, w_grp_a, row2(kv_norm_g), w_kv_up.astype(BF16), row2(q_norm_a), row2(k_norm_a)],
        [jax.ShapeDtypeStruct((n, A_WIDTH), BF16)] * 3,
        [_row_spec(tm, A_WIDTH)] * 3)

    per_b = seq // tm
    iq, ik, iw = _proj_call(
        _proj_i_kernel, "proj_i", x2d, tm,
        [g_x, w_grp_i, row2(idx_k_ln_g), row2(idx_k_ln_b)],
        [jax.ShapeDtypeStruct((bsz, IDX_HEADS, seq, IDX_DIM), BF16),
         jax.ShapeDtypeStruct((n, IDX_DIM), BF16),
         jax.ShapeDtypeStruct((n, IDX_HEADS), F32)],
        [pl.BlockSpec((None, IDX_HEADS, tm, IDX_DIM), lambda i: (i // per_b, 0, i % per_b, 0)),
         _row_spec(tm, IDX_DIM), _row_spec(tm, IDX_HEADS)])

    ga, gb, gc = _proj_call(
        _proj_g_kernel, "proj_g", x2d, tm, [g_x, w_grp_g],
        [jax.ShapeDtypeStruct((n, A_WIDTH), BF16), jax.ShapeDtypeStruct((n, B_WIDTH), BF16),
         jax.ShapeDtypeStruct((n, C_WIDTH), BF16)],
        [_row_spec(tm, A_WIDTH), _row_spec(tm, B_WIDTH), _row_spec(tm, C_WIDTH)])

    qb, kb, vb, qc = _proj_call(
        _proj_bc_kernel, "proj_bc", x2d, tm,
        [g_x, w_grp_bc, row2(jnp.tile(q_norm_b, 2)), row2(jnp.tile(k_norm_b, 2)), row2(q_norm_c)],
        [jax.ShapeDtypeStruct((n, B_WIDTH), BF16), jax.ShapeDtypeStruct((n, B_KV_WIDTH), BF16),
         jax.ShapeDtypeStruct((n, B_KV_WIDTH), BF16), jax.ShapeDtypeStruct((n, C_WIDTH), BF16)],
        [_row_spec(tm, B_WIDTH), _row_spec(tm, B_KV_WIDTH), _row_spec(tm, B_KV_WIDTH),
         _row_spec(tm, C_WIDTH)])

    mrows = bsz * mlen
    kc, vc = _proj_call(
        _proj_mem_kernel, "proj_mem", mem.reshape(mrows, d), min(tm, mrows),
        [row2(mem_norm_g), w_mem_kv.astype(BF16), row2(k_norm_c)],
        [jax.ShapeDtypeStruct((mrows, C_WIDTH), BF16)] * 2,
        [_row_spec(min(tm, mrows), C_WIDTH)] * 2)

    T = DSA_T
    i_idx = np.arange(T)[:, None]
    dist_a = T + i_idx - np.arange(2 * T)[None, :]
    rb_a = _bias_tiles(rel_bias, _t5_bucket_np(dist_a), 0, A_HEADS, NUM_BUCKETS - 1)
    i_idx = np.arange(WINDOW)[:, None]
    dist_b = WINDOW + i_idx - np.arange(2 * WINDOW)[None, :]
    bucket_b = np.where((dist_b >= 0) & (dist_b < WINDOW), _t5_bucket_np(dist_b), -1).astype(np.int32)
    bias_b = _bias_tiles(rel_bias, bucket_b, A_HEADS, B_HEADS, None)

    r3 = lambda v, w: v.reshape(bsz, seq, w)
    ikt = jnp.swapaxes(ik.reshape(bsz, seq, IDX_DIM), 1, 2)
    oa = _dsa_call(iq, iw.reshape(bsz, seq, IDX_HEADS), ikt, r3(qa, A_WIDTH), r3(ka, A_WIDTH),
                   r3(va, A_WIDTH), r3(ga, A_WIDTH), rb_a, topk)
    ob = _swa_call(sinks_b, r3(qb, B_WIDTH), r3(kb, B_KV_WIDTH), r3(vb, B_KV_WIDTH), r3(gb, B_WIDTH), bias_b)
    oc = _mem_call(r3(qc, C_WIDTH), kc.reshape(bsz, mlen, C_WIDTH), vc.reshape(bsz, mlen, C_WIDTH),
                   r3(gc, C_WIDTH))

    out = _final_call(x2d, g_x, oa.reshape(n, A_WIDTH), ob.reshape(n, B_WIDTH), oc.reshape(n, C_WIDTH),
                      w_mix, row2(gate_bias), w_up_a.astype(BF16), w_up_b.astype(BF16),
                      w_up_c.astype(BF16), w_o.astype(BF16))
    return out.reshape(bsz, seq, d)


def kernel(x, mem, norm_g, w_in, kv_norm_g, w_kv_up, idx_k_ln_g, idx_k_ln_b, q_norm_a, k_norm_a, q_norm_b, k_norm_b, sinks_b, mem_norm_g, w_mem_kv, q_norm_c, k_norm_c, w_up_a, w_up_b, w_up_c, gate_bias, w_o, rel_bias):
    for l in range(norm_g.shape[0]):
        x = _layer(x, mem, norm_g[l], w_in[l], kv_norm_g[l], w_kv_up[l], idx_k_ln_g[l], idx_k_ln_b[l],
                   q_norm_a[l], k_norm_a[l], q_norm_b[l], k_norm_b[l], sinks_b[l], mem_norm_g[l],
                   w_mem_kv[l], q_norm_c[l], k_norm_c[l], w_up_a[l], w_up_b[l], w_up_c[l],
                   gate_bias[l], w_o[l], rel_bias)
    return x
```

```python
import functools
import math

import numpy as np
import jax
import jax.numpy as jnp
from jax import lax
from jax.experimental import pallas as pl
from jax.experimental.pallas import tpu as pltpu

F32 = jnp.float32
BF16 = jnp.bfloat16

EPS = 1e-6
A_HEADS, A_HEAD_DIM, A_KV_RANK = 6, 128, 256
IDX_HEADS, IDX_DIM, TOPK_MAX = 16, 64, 256
B_HEADS, B_KV_HEADS, B_HEAD_DIM, WINDOW = 12, 2, 64, 128
C_HEADS, C_HEAD_DIM = 4, 128
NUM_BUCKETS, MAX_DISTANCE = 32, 128
N_BRANCH = 3
A_WIDTH = A_HEADS * A_HEAD_DIM
B_WIDTH = B_HEADS * B_HEAD_DIM
B_KV_WIDTH = B_KV_HEADS * B_HEAD_DIM
C_WIDTH = C_HEADS * C_HEAD_DIM

LANES = 128
VMEM_LIMIT = 56 * 1024 * 1024
NEG_BIG = -1e30

ROW_TILE = 512
DSA_T = 256
SWA_T = 512
FINAL_NCHUNK = 4
COUNT_ROWS = 128
FAR_TILES = 2
LOG2E = math.log2(math.e)


def _cparams(sem):
    return pltpu.CompilerParams(dimension_semantics=sem, vmem_limit_bytes=VMEM_LIMIT)


def _dot(a, b):
    return jnp.dot(a, b, preferred_element_type=F32)


def _dot_nt(a, b):
    return lax.dot_general(a, b, (((1,), (1,)), ((), ())), preferred_element_type=F32)


def _rms_rows(x, g):
    ms = jnp.mean(x * x, axis=-1, keepdims=True)
    return x * lax.rsqrt(ms + EPS) * g


def _headnorm128(seg, g):
    ms = jnp.mean(seg * seg, axis=-1, keepdims=True)
    return seg * lax.rsqrt(ms + EPS) * g


def _headnorm64_pair(seg, g2):
    sq = seg * seg
    low = lax.broadcasted_iota(jnp.int32, seg.shape, 1) < B_HEAD_DIM
    s_all = jnp.sum(sq, axis=-1, keepdims=True)
    s_lo = jnp.sum(jnp.where(low, sq, 0.0), axis=-1, keepdims=True)
    ms = jnp.where(low, s_lo, s_all - s_lo) * (1.0 / B_HEAD_DIM)
    return seg * lax.rsqrt(ms + EPS) * g2


def _silu(y):
    return y / (1.0 + jnp.exp(-y))


def _proj_a_kernel(x_ref, g_ref, w_ref, kvg_ref, wkv_ref, qg_ref, kg_ref, qa_ref, ka_ref, va_ref):
    h = _rms_rows(x_ref[...], g_ref[...]).astype(BF16)
    y = _dot(h, w_ref[...])
    scale = A_HEAD_DIM ** -0.5 * LOG2E
    for hh in range(A_HEADS):
        sl = slice(hh * A_HEAD_DIM, (hh + 1) * A_HEAD_DIM)
        qa_ref[:, sl] = (_headnorm128(y[:, sl], qg_ref[...]) * scale).astype(BF16)
    ckv = _rms_rows(y[:, A_WIDTH:A_WIDTH + A_KV_RANK], kvg_ref[...]).astype(BF16)
    kv = _dot(ckv, wkv_ref[...])
    for hh in range(A_HEADS):
        sl = slice(hh * A_HEAD_DIM, (hh + 1) * A_HEAD_DIM)
        ka_ref[:, sl] = _headnorm128(kv[:, sl], kg_ref[...]).astype(BF16)
    va_ref[...] = kv[:, A_WIDTH:].astype(BF16)


def _proj_i_kernel(x_ref, g_ref, w_ref, lng_ref, lnb_ref, iq_ref, ik_ref, iw_ref):
    h = _rms_rows(x_ref[...], g_ref[...]).astype(BF16)
    y = _dot(h, w_ref[...])
    for hh in range(IDX_HEADS):
        iq_ref[hh] = y[:, hh * IDX_DIM:(hh + 1) * IDX_DIM].astype(BF16)
    k0 = IDX_HEADS * IDX_DIM
    ik = y[:, k0:k0 + IDX_DIM]
    mu = jnp.mean(ik, axis=-1, keepdims=True)
    d = ik - mu
    var = jnp.mean(d * d, axis=-1, keepdims=True)
    ik_ref[...] = (d * lax.rsqrt(var + EPS) * lng_ref[...] + lnb_ref[...]).astype(BF16)
    w0 = k0 + LANES
    iw_ref[...] = y[:, w0:w0 + IDX_HEADS] * (IDX_HEADS ** -0.5) * (IDX_DIM ** -0.5)


def _proj_g_kernel(x_ref, g_ref, w_ref, ga_ref, gb_ref, gc_ref):
    h = _rms_rows(x_ref[...], g_ref[...]).astype(BF16)
    y = _silu(_dot(h, w_ref[...]))
    ga_ref[...] = y[:, :A_WIDTH].astype(BF16)
    gb_ref[...] = y[:, A_WIDTH:A_WIDTH + B_WIDTH].astype(BF16)
    gc_ref[...] = y[:, A_WIDTH + B_WIDTH:].astype(BF16)


def _proj_bc_kernel(x_ref, g_ref, w_ref, qbg_ref, kbg_ref, qcg_ref, qb_ref, kb_ref, vb_ref, qc_ref):
    h = _rms_rows(x_ref[...], g_ref[...]).astype(BF16)
    y = _dot(h, w_ref[...])
    sb = B_HEAD_DIM ** -0.5
    for p in range(B_WIDTH // LANES):
        pair = (_headnorm64_pair(y[:, p * LANES:(p + 1) * LANES], qbg_ref[...]) * sb).astype(BF16)
        qb_ref[2 * p] = pair[:, :B_HEAD_DIM]
        qb_ref[2 * p + 1] = pair[:, B_HEAD_DIM:]
    k0 = B_WIDTH
    kpair = _headnorm64_pair(y[:, k0:k0 + B_KV_WIDTH], kbg_ref[...]).astype(BF16)
    vpair = y[:, k0 + B_KV_WIDTH:k0 + 2 * B_KV_WIDTH].astype(BF16)
    for g in range(B_KV_HEADS):
        kb_ref[g] = kpair[:, g * B_HEAD_DIM:(g + 1) * B_HEAD_DIM]
        vb_ref[g] = vpair[:, g * B_HEAD_DIM:(g + 1) * B_HEAD_DIM]
    c0 = k0 + 2 * B_KV_WIDTH
    sc = C_HEAD_DIM ** -0.5
    for hh in range(C_HEADS):
        sl = slice(hh * C_HEAD_DIM, (hh + 1) * C_HEAD_DIM)
        qc_ref[:, sl] = (_headnorm128(y[:, c0 + hh * C_HEAD_DIM:c0 + (hh + 1) * C_HEAD_DIM],
                                      qcg_ref[...]) * sc).astype(BF16)


def _proj_mem_kernel(x_ref, g_ref, w_ref, kg_ref, kc_ref, vc_ref):
    h = _rms_rows(x_ref[...], g_ref[...]).astype(BF16)
    y = _dot(h, w_ref[...])
    for hh in range(C_HEADS):
        sl = slice(hh * C_HEAD_DIM, (hh + 1) * C_HEAD_DIM)
        kc_ref[:, sl] = _headnorm128(y[:, sl], kg_ref[...]).astype(BF16)
    vc_ref[...] = y[:, C_WIDTH:].astype(BF16)


def _row_spec(tm, cols):
    return pl.BlockSpec((tm, cols), lambda i: (i, 0))


def _full_spec(shape):
    nd = len(shape)
    return pl.BlockSpec(shape, lambda i: (0,) * nd)


def _proj_call(kernel_fn, name, x2d, tm, consts, out_shapes, out_specs):
    n, d = x2d.shape
    in_specs = [_row_spec(tm, d)] + [_full_spec(c.shape) for c in consts]
    return pl.pallas_call(
        kernel_fn,
        grid=(n // tm,),
        in_specs=in_specs,
        out_specs=out_specs,
        out_shape=out_shapes,
        compiler_params=_cparams(("parallel",)),
        name=name,
    )(x2d, *consts)


def _t5_bucket_np(dist):
    n = np.maximum(dist, 0)
    max_exact = NUM_BUCKETS // 2
    nf = np.maximum(n, 1).astype(np.float32)
    large = max_exact + (np.log(nf / max_exact) / math.log(MAX_DISTANCE / max_exact)
                         * (NUM_BUCKETS - max_exact)).astype(np.int32)
    large = np.minimum(large, NUM_BUCKETS - 1)
    return np.where(n < max_exact, n, large).astype(np.int32)


def _bias_tile_kernel(tab_ref, bucket_ref, o_ref, *, head0, shift_bucket, scale):
    h = pl.program_id(0) + head0
    bucket = bucket_ref[...]
    acc = jnp.zeros(bucket.shape, F32)
    for b in range(NUM_BUCKETS):
        acc = jnp.where(bucket == b, tab_ref[b, h], acc)
    if shift_bucket is not None:
        acc = acc - tab_ref[shift_bucket, h]
    o_ref[...] = jnp.where(bucket < 0, NEG_BIG, acc * scale)


def _bias_tiles(rel_bias, bucket_np, head0, nheads, shift_bucket, scale):
    r, c = bucket_np.shape
    return pl.pallas_call(
        functools.partial(_bias_tile_kernel, head0=head0, shift_bucket=shift_bucket, scale=scale),
        grid=(nheads,),
        in_specs=[pl.BlockSpec(memory_space=pltpu.SMEM), pl.BlockSpec((r, c), lambda h: (0, 0))],
        out_specs=pl.BlockSpec((None, r, c), lambda h: (h, 0, 0)),
        out_shape=jax.ShapeDtypeStruct((nheads, r, c), F32),
        compiler_params=_cparams(("arbitrary",)),
        name="bias_tiles",
    )(rel_bias, jnp.asarray(bucket_np))


def _dsa_kernel(iq_ref, iw_ref, ikt_ref, q_ref, k_ref, v_ref, g_ref, rb_ref, o_ref,
                sc_ref, wb_ref, lo_ref, hi_ref, mid_ref, clo_ref, s_ref, m_ref, l_ref, acc_ref, *, topk):
    T = DSA_T
    qi = pl.program_id(1)
    kf = float(topk)

    def lanes(col):
        return jnp.broadcast_to(col, (T, LANES))

    def halves(tile):
        return [tile[:, c * LANES:(c + 1) * LANES] for c in range(T // LANES)]

    for hh in range(IDX_HEADS):
        wb_ref[hh] = lanes(iw_ref[:, hh:hh + 1])

    def score_chunk(j, diag):
        ik = ikt_ref[:, pl.ds(pl.multiple_of(j * T, T), T)]
        accs = [jnp.zeros((T, LANES), F32) for _ in range(T // LANES)]
        for hh in range(IDX_HEADS):
            s = _dot(iq_ref[hh], ik)
            w = wb_ref[hh]
            accs = [a + w * jnp.maximum(sh, 0.0) for a, sh in zip(accs, halves(s))]
        if diag:
            row = lax.broadcasted_iota(jnp.int32, (T, LANES), 0)
            col = lax.broadcasted_iota(jnp.int32, (T, LANES), 1)
            causal = [col + c * LANES <= row for c in range(T // LANES)]
            lows = [jnp.where(cm, a, jnp.inf) for cm, a in zip(causal, accs)]
            accs = [jnp.where(cm, a, -jnp.inf) for cm, a in zip(causal, accs)]
        else:
            lows = accs
        for c, a in enumerate(accs):
            sc_ref[j, :, c * LANES:(c + 1) * LANES] = a
        return functools.reduce(jnp.minimum, lows), functools.reduce(jnp.maximum, accs)

    def p1_body(j, carry):
        mn, mx = carry
        lo_c, hi_c = score_chunk(j, False)
        return jnp.minimum(mn, lo_c), jnp.maximum(mx, hi_c)

    mn0 = jnp.full((T, LANES), jnp.inf, F32)
    mx0 = jnp.full((T, LANES), -jnp.inf, F32)
    mn, mx = lax.fori_loop(0, qi, p1_body, (mn0, mx0))
    lo_c, hi_c = score_chunk(qi, True)
    mn = lanes(jnp.min(jnp.minimum(mn, lo_c), axis=-1, keepdims=True))
    mx = lanes(jnp.max(jnp.maximum(mx, hi_c), axis=-1, keepdims=True))

    def count_lanes(rows, v, strict=False):
        def body(j, c):
            for c0 in range(0, T, LANES):
                sh = sc_ref[j, rows, c0:c0 + LANES]
                hit = (sh > v) if strict else (sh >= v)
                c = c + jnp.where(hit, 1.0, 0.0)
            return c

        return lax.fori_loop(0, qi + 1, body, jnp.zeros((COUNT_ROWS, LANES), F32))

    def row_total(c):
        return jnp.broadcast_to(jnp.sum(c, axis=-1, keepdims=True), c.shape)

    slabs = [slice(r0, r0 + COUNT_ROWS) for r0 in range(0, T, COUNT_ROWS)]
    n_causal = (qi * T + 1 + lax.broadcasted_iota(jnp.int32, (T, LANES), 0)).astype(F32)
    hi0 = jnp.where(n_causal <= kf, mn, mx)
    lo_ref[...] = mn
    hi_ref[...] = hi0
    mid_ref[...] = hi0
    clo_ref[...] = n_causal

    def bis_body(_):
        partial = [count_lanes(rows, mid_ref[rows]) for rows in slabs]
        widths = [hi_ref[rows] - lo_ref[rows] for rows in slabs]
        open_width = jnp.max(functools.reduce(jnp.maximum, widths))
        for rows, c in zip(slabs, partial):
            lo, hi, mid = lo_ref[rows], hi_ref[rows], mid_ref[rows]
            cnt = row_total(c)
            up = cnt >= kf
            lo = jnp.where(up, mid, lo)
            c_lo = jnp.where(up, cnt, clo_ref[rows])
            hi = jnp.where(up, hi, mid)
            nxt = lo + 0.5 * (hi - lo)
            stop = jnp.logical_or(c_lo == kf, jnp.logical_or(nxt <= lo, nxt >= hi))
            hi = jnp.where(stop, lo, hi)
            lo_ref[rows] = lo
            hi_ref[rows] = hi
            mid_ref[rows] = jnp.where(stop, lo, nxt)
            clo_ref[rows] = c_lo
        return open_width

    lax.while_loop(lambda w: w > 0.0, bis_body, jnp.max(hi0 - mn))
    thr = lo_ref[...]

    tied_f = jnp.where(clo_ref[...] > kf, 1.0, 0.0)

    @pl.when(jnp.max(tied_f) > 0.0)
    def _():
        above = jnp.concatenate(
            [row_total(count_lanes(rows, thr[rows], strict=True)) for rows in slabs], axis=0)
        need = (kf - above)[:, :1]
        r = lax.broadcasted_iota(jnp.int32, (T, T), 0)
        c = lax.broadcasted_iota(jnp.int32, (T, T), 1)
        before = jnp.where(r < c, 1.0, 0.0).astype(BF16)
        thr_col = thr[:, :1]
        tied_col = tied_f[:, :1] > 0.0

        def body(j, seen):
            s = sc_ref[j]
            eq = jnp.where(s == thr_col, 1.0, 0.0)
            rank = seen + _dot(eq.astype(BF16), before)
            drop = jnp.logical_and(tied_col, jnp.logical_and(eq > 0.0, rank >= need))
            sc_ref[j] = jnp.where(drop, -jnp.inf, s)
            return seen + jnp.sum(eq, axis=-1, keepdims=True)

        lax.fori_loop(0, qi + 1, body, jnp.zeros((T, 1), F32))

    m_ref[...] = jnp.full(m_ref.shape, NEG_BIG, F32)
    l_ref[...] = jnp.zeros(l_ref.shape, F32)
    acc_ref[...] = jnp.zeros(acc_ref.shape, F32)

    def attend(j0, ntiles, band):
        sel = [jnp.where(sh >= thr, 0.0, NEG_BIG) for t in range(ntiles) for sh in halves(sc_ref[j0 + t])]
        start = pl.multiple_of(j0 * T, T)
        keys = pl.ds(start, ntiles * T)
        width = ntiles * T

        def logits_to_scratch(hh):
            sl = slice(hh * A_HEAD_DIM, (hh + 1) * A_HEAD_DIM)
            s_ref[hh % 2, :, :width] = _dot_nt(q_ref[:, sl], k_ref[keys, sl])

        logits_to_scratch(0)
        pending = None
        for hh in range(A_HEADS):
            sl = slice(hh * A_HEAD_DIM, (hh + 1) * A_HEAD_DIM)
            if hh + 1 < A_HEADS:
                logits_to_scratch(hh + 1)
            s = s_ref[hh % 2, :, :width]
            if band is not None:
                s = s + rb_ref[hh, :, band * T:(band + 1) * T]
            sh = [s[:, c * LANES:(c + 1) * LANES] + m for c, m in enumerate(sel)]
            m_prev = m_ref[hh]
            m_cur = jnp.max(functools.reduce(jnp.maximum, sh), axis=-1, keepdims=True)
            m_new = jnp.maximum(m_prev, lanes(m_cur))
            alpha = jnp.exp2(m_prev - m_new)
            ph = [jnp.exp2(x - m_new) for x in sh]
            l_ref[hh] = alpha * l_ref[hh] + functools.reduce(jnp.add, ph)
            p = jnp.concatenate([x.astype(BF16) for x in ph], axis=-1)
            acc_ref[hh] = alpha * acc_ref[hh]
            m_ref[hh] = m_new
            if pending is not None:
                acc_ref[pending[0]] += pending[1]
            pending = (hh, _dot(p, v_ref[keys, sl]))
        acc_ref[pending[0]] += pending[1]

    n_far = jnp.maximum(qi - 1, 0)
    n_wide = lax.div(n_far, FAR_TILES)

    def wide_body(jj, c):
        attend(jj * FAR_TILES, FAR_TILES, None)
        return c

    def single_body(j, c):
        attend(j, 1, None)
        return c

    lax.fori_loop(0, n_wide, wide_body, 0)
    lax.fori_loop(n_wide * FAR_TILES, n_far, single_body, 0)

    @pl.when(qi > 0)
    def _():
        attend(qi - 1, 1, 0)

    attend(qi, 1, 1)

    for hh in range(A_HEADS):
        sl = slice(hh * A_HEAD_DIM, (hh + 1) * A_HEAD_DIM)
        den = jnp.sum(l_ref[hh], axis=-1, keepdims=True)
        o_ref[:, sl] = (acc_ref[hh] / den * g_ref[:, sl].astype(F32)).astype(BF16)


def _dsa_call(iq, iw, ikt, qa, ka, va, ga, rb, topk):
    bsz, seq, _ = qa.shape
    T = DSA_T
    nq = seq // T
    once = pl.Buffered(1)
    return pl.pallas_call(
        functools.partial(_dsa_kernel, topk=topk),
        grid=(bsz, nq),
        in_specs=[
            pl.BlockSpec((None, IDX_HEADS, T, IDX_DIM), lambda b, i: (b, 0, i, 0)),
            pl.BlockSpec((None, T, IDX_HEADS), lambda b, i: (b, i, 0)),
            pl.BlockSpec((None, IDX_DIM, seq), lambda b, i: (b, 0, 0), pipeline_mode=once),
            pl.BlockSpec((None, T, A_WIDTH), lambda b, i: (b, i, 0)),
            pl.BlockSpec((None, seq, A_WIDTH), lambda b, i: (b, 0, 0), pipeline_mode=once),
            pl.BlockSpec((None, seq, A_WIDTH), lambda b, i: (b, 0, 0), pipeline_mode=once),
            pl.BlockSpec((None, T, A_WIDTH), lambda b, i: (b, i, 0)),
            pl.BlockSpec((A_HEADS, T, 2 * T), lambda b, i: (0, 0, 0), pipeline_mode=once),
        ],
        out_specs=pl.BlockSpec((None, T, A_WIDTH), lambda b, i: (b, i, 0)),
        out_shape=jax.ShapeDtypeStruct((bsz, seq, A_WIDTH), BF16),
        scratch_shapes=[
            pltpu.VMEM((nq, T, T), F32),
            pltpu.VMEM((IDX_HEADS, T, LANES), F32),
            pltpu.VMEM((T, LANES), F32),
            pltpu.VMEM((T, LANES), F32),
            pltpu.VMEM((T, LANES), F32),
            pltpu.VMEM((T, LANES), F32),
            pltpu.VMEM((2, T, FAR_TILES * T), F32),
            pltpu.VMEM((A_HEADS, T, LANES), F32),
            pltpu.VMEM((A_HEADS, T, LANES), F32),
            pltpu.VMEM((A_HEADS, T, A_HEAD_DIM), F32),
        ],
        compiler_params=_cparams(("arbitrary", "arbitrary")),
        name="dsa",
    )(iq, iw, ikt, qa, ka, va, ga, rb)


def _swa_kernel(sink_ref, q_ref, kc_ref, vc_ref, kp_ref, vp_ref, g_ref, bias_ref, o_ref):
    first = pl.program_id(1) == 0
    blk = WINDOW
    group = B_HEADS // B_KV_HEADS
    grows = group * blk
    for sb in range(SWA_T // blk):
        rows = slice(sb * blk, (sb + 1) * blk)
        prev = slice((sb - 1) * blk, sb * blk)
        pieces = []
        for g in range(B_KV_HEADS):
            heads = slice(g * group, (g + 1) * group)
            k_prev = kp_ref[g] if sb == 0 else kc_ref[g, prev, :]
            v_prev = vp_ref[g] if sb == 0 else vc_ref[g, prev, :]
            q = q_ref[heads, rows, :].reshape(grows, B_HEAD_DIM)
            lp = _dot_nt(q, k_prev) + bias_ref[heads, :, :blk].reshape(grows, blk)
            lc = _dot_nt(q, kc_ref[g, rows, :]) + bias_ref[heads, :, blk:].reshape(grows, blk)
            if sb == 0:
                lp = jnp.where(first, NEG_BIG, lp)
            sink = jnp.concatenate(
                [jnp.full((blk, 1), sink_ref[g * group + h], F32) for h in range(group)], axis=0)
            m = jnp.maximum(jnp.max(lp, axis=-1, keepdims=True), jnp.max(lc, axis=-1, keepdims=True))
            m = jnp.maximum(m, sink)
            pp = jnp.exp(lp - m)
            pc = jnp.exp(lc - m)
            den = (jnp.sum(pp, axis=-1, keepdims=True) + jnp.sum(pc, axis=-1, keepdims=True)
                   + jnp.exp(sink - m))
            o = (_dot(pp.astype(BF16), v_prev) + _dot(pc.astype(BF16), vc_ref[g, rows, :])) / den
            pieces += [o[h * blk:(h + 1) * blk, :] for h in range(group)]
        o_ref[rows, :] = (jnp.concatenate(pieces, axis=1) * g_ref[rows, :].astype(F32)).astype(BF16)


def _swa_call(sinks, qb, kb, vb, gb, bias):
    bsz, _, seq, _ = qb.shape
    T = SWA_T
    per = T // WINDOW
    cur = lambda b, i: (b, 0, i, 0)
    prev = lambda b, i: (b, 0, jnp.maximum(i * per - 1, 0), 0)
    return pl.pallas_call(
        _swa_kernel,
        grid=(bsz, seq // T),
        in_specs=[
            pl.BlockSpec(memory_space=pltpu.SMEM),
            pl.BlockSpec((None, B_HEADS, T, B_HEAD_DIM), cur),
            pl.BlockSpec((None, B_KV_HEADS, T, B_HEAD_DIM), cur),
            pl.BlockSpec((None, B_KV_HEADS, T, B_HEAD_DIM), cur),
            pl.BlockSpec((None, B_KV_HEADS, WINDOW, B_HEAD_DIM), prev),
            pl.BlockSpec((None, B_KV_HEADS, WINDOW, B_HEAD_DIM), prev),
            pl.BlockSpec((None, T, B_WIDTH), lambda b, i: (b, i, 0)),
            pl.BlockSpec((B_HEADS, WINDOW, 2 * WINDOW), lambda b, i: (0, 0, 0)),
        ],
        out_specs=pl.BlockSpec((None, T, B_WIDTH), lambda b, i: (b, i, 0)),
        out_shape=jax.ShapeDtypeStruct((bsz, seq, B_WIDTH), BF16),
        compiler_params=_cparams(("parallel", "parallel")),
        name="swa",
    )(sinks, qb, kb, vb, kb, vb, gb, bias)


def _mem_kernel(q_ref, k_ref, v_ref, g_ref, o_ref):
    for hh in range(C_HEADS):
        sl = slice(hh * C_HEAD_DIM, (hh + 1) * C_HEAD_DIM)
        s = _dot_nt(q_ref[:, sl], k_ref[:, sl])
        m = jnp.max(s, axis=-1, keepdims=True)
        p = jnp.exp(s - m)
        den = jnp.sum(p, axis=-1, keepdims=True)
        o = _dot(p.astype(BF16), v_ref[:, sl]) / den
        o_ref[:, sl] = (o * g_ref[:, sl].astype(F32)).astype(BF16)


def _mem_call(qc, kc, vc, gc):
    bsz, seq, _ = qc.shape
    mlen = kc.shape[1]
    T = SWA_T
    cur = lambda b, i: (b, i, 0)
    whole = lambda b, i: (b, 0, 0)
    return pl.pallas_call(
        _mem_kernel,
        grid=(bsz, seq // T),
        in_specs=[
            pl.BlockSpec((None, T, C_WIDTH), cur),
            pl.BlockSpec((None, mlen, C_WIDTH), whole),
            pl.BlockSpec((None, mlen, C_WIDTH), whole),
            pl.BlockSpec((None, T, C_WIDTH), cur),
        ],
        out_specs=pl.BlockSpec((None, T, C_WIDTH), cur),
        out_shape=jax.ShapeDtypeStruct((bsz, seq, C_WIDTH), BF16),
        compiler_params=_cparams(("parallel", "parallel")),
        name="mem_attn",
    )(qc, kc, vc, gc)


def _final_kernel(x_ref, g_ref, oa_ref, ob_ref, oc_ref, wma_ref, wmb_ref, wmc_ref,
                  ba_ref, bb_ref, bc_ref, wua_ref, wub_ref, wuc_ref, wo_ref, out_ref, h_ref, acc_ref):
    n = pl.program_id(1)

    @pl.when(n == 0)
    def _():
        h_ref[...] = _rms_rows(x_ref[...], g_ref[...]).astype(BF16)
        acc_ref[...] = jnp.zeros(acc_ref.shape, F32)

    h = h_ref[...]

    def branch(o_ref, wm_ref, b_ref, wu_ref):
        gate = 1.0 / (1.0 + jnp.exp(-(_dot(h, wm_ref[...]) + b_ref[...])))
        return gate * _dot(o_ref[...], wu_ref[...])

    merged = (branch(oa_ref, wma_ref, ba_ref, wua_ref) + branch(ob_ref, wmb_ref, bb_ref, wub_ref)
              + branch(oc_ref, wmc_ref, bc_ref, wuc_ref))
    acc_ref[...] += _dot(merged.astype(BF16), wo_ref[...])

    @pl.when(n == pl.num_programs(1) - 1)
    def _():
        out_ref[...] = x_ref[...] + acc_ref[...]


def _final_call(x2d, g, oa, ob, oc, wmix, gate_bias, wua, wub, wuc, wo):
    n, d = x2d.shape
    tm = ROW_TILE
    nch = FINAL_NCHUNK
    cw = d // nch
    row = lambda i, c: (i, 0)

    def col(br):
        return lambda i, c: (0, br * nch + c)

    return pl.pallas_call(
        _final_kernel,
        grid=(n // tm, nch),
        in_specs=[
            pl.BlockSpec((tm, d), row),
            pl.BlockSpec((1, d), lambda i, c: (0, 0)),
            pl.BlockSpec((tm, A_WIDTH), row),
            pl.BlockSpec((tm, B_WIDTH), row),
            pl.BlockSpec((tm, C_WIDTH), row),
            pl.BlockSpec((d, cw), col(0)), pl.BlockSpec((d, cw), col(1)), pl.BlockSpec((d, cw), col(2)),
            pl.BlockSpec((1, cw), col(0)), pl.BlockSpec((1, cw), col(1)), pl.BlockSpec((1, cw), col(2)),
            pl.BlockSpec((A_WIDTH, cw), lambda i, c: (0, c)),
            pl.BlockSpec((B_WIDTH, cw), lambda i, c: (0, c)),
            pl.BlockSpec((C_WIDTH, cw), lambda i, c: (0, c)),
            pl.BlockSpec((cw, d), lambda i, c: (c, 0)),
        ],
        out_specs=pl.BlockSpec((tm, d), row),
        out_shape=jax.ShapeDtypeStruct((n, d), F32),
        scratch_shapes=[pltpu.VMEM((tm, d), BF16), pltpu.VMEM((tm, d), F32)],
        compiler_params=_cparams(("parallel", "arbitrary")),
        name="merge_out",
    )(x2d, g, oa, ob, oc, wmix, wmix, wmix, gate_bias, gate_bias, gate_bias, wua, wub, wuc, wo)


def _layer(x, mem, norm_g, w_in, kv_norm_g, w_kv_up, idx_k_ln_g, idx_k_ln_b, q_norm_a, k_norm_a,
           q_norm_b, k_norm_b, sinks_b, mem_norm_g, w_mem_kv, q_norm_c, k_norm_c,
           w_up_a, w_up_b, w_up_c, gate_bias, w_o, rel_bias):
    bsz, seq, d = x.shape
    mlen = mem.shape[1]
    n = bsz * seq
    tm = ROW_TILE
    topk = min(TOPK_MAX, seq // 4)
    x2d = x.reshape(n, d)
    row2 = lambda v: v.reshape(1, -1)

    sizes = (A_WIDTH, A_KV_RANK, IDX_HEADS * IDX_DIM, IDX_DIM, IDX_HEADS, A_WIDTH,
             B_WIDTH, B_KV_WIDTH, B_KV_WIDTH, B_WIDTH, C_WIDTH, C_WIDTH, N_BRANCH * d)
    cuts = np.cumsum((0,) + sizes)
    (w_aq, w_ackv, w_iq, w_ik, w_iw, w_ag, w_bq, w_bk, w_bv, w_bg, w_cq, w_cg, w_mix) = [
        w_in[:, cuts[i]:cuts[i + 1]].astype(BF16) for i in range(len(sizes))]
    zpad = lambda c: jnp.zeros((d, c), BF16)
    w_grp_a = jnp.concatenate([w_aq, w_ackv], axis=1)
    w_grp_i = jnp.concatenate([w_iq, w_ik, zpad(LANES - IDX_DIM), w_iw, zpad(LANES - IDX_HEADS)], axis=1)
    w_grp_g = jnp.concatenate([w_ag, w_bg, w_cg], axis=1)
    w_grp_bc = jnp.concatenate([w_bq, w_bk, w_bv, w_cq], axis=1)
    g_x = row2(norm_g)

    qa, ka, va = _proj_call(
        _proj_a_kernel, "proj_a", x2d, tm,
        [g_x, w_grp_a, row2(kv_norm_g), w_kv_up.astype(BF16), row2(q_norm_a), row2(k_norm_a)],
        [jax.ShapeDtypeStruct((n, A_WIDTH), BF16)] * 3,
        [_row_spec(tm, A_WIDTH)] * 3)

    per_b = seq // tm
    iq, ik, iw = _proj_call(
        _proj_i_kernel, "proj_i", x2d, tm,
        [g_x, w_grp_i, row2(idx_k_ln_g), row2(idx_k_ln_b)],
        [jax.ShapeDtypeStruct((bsz, IDX_HEADS, seq, IDX_DIM), BF16),
         jax.ShapeDtypeStruct((n, IDX_DIM), BF16),
         jax.ShapeDtypeStruct((n, IDX_HEADS), F32)],
        [pl.BlockSpec((None, IDX_HEADS, tm, IDX_DIM), lambda i: (i // per_b, 0, i % per_b, 0)),
         _row_spec(tm, IDX_DIM), _row_spec(tm, IDX_HEADS)])

    ga, gb, gc = _proj_call(
        _proj_g_kernel, "proj_g", x2d, tm, [g_x, w_grp_g],
        [jax.ShapeDtypeStruct((n, A_WIDTH), BF16), jax.ShapeDtypeStruct((n, B_WIDTH), BF16),
         jax.ShapeDtypeStruct((n, C_WIDTH), BF16)],
        [_row_spec(tm, A_WIDTH), _row_spec(tm, B_WIDTH), _row_spec(tm, C_WIDTH)])

    qb, kb, vb, qc = _proj_call(
        _proj_bc_kernel, "proj_bc", x2d, tm,
        [g_x, w_grp_bc, row2(jnp.tile(q_norm_b, 2)), row2(jnp.tile(k_norm_b, 2)), row2(q_norm_c)],
        [jax.ShapeDtypeStruct((bsz, B_HEADS, seq, B_HEAD_DIM), BF16),
         jax.ShapeDtypeStruct((bsz, B_KV_HEADS, seq, B_HEAD_DIM), BF16),
         jax.ShapeDtypeStruct((bsz, B_KV_HEADS, seq, B_HEAD_DIM), BF16),
         jax.ShapeDtypeStruct((n, C_WIDTH), BF16)],
        [pl.BlockSpec((None, B_HEADS, tm, B_HEAD_DIM), lambda i: (i // per_b, 0, i % per_b, 0)),
         pl.BlockSpec((None, B_KV_HEADS, tm, B_HEAD_DIM), lambda i: (i // per_b, 0, i % per_b, 0)),
         pl.BlockSpec((None, B_KV_HEADS, tm, B_HEAD_DIM), lambda i: (i // per_b, 0, i % per_b, 0)),
         _row_spec(tm, C_WIDTH)])

    mrows = bsz * mlen
    kc, vc = _proj_call(
        _proj_mem_kernel, "proj_mem", mem.reshape(mrows, d), min(tm, mrows),
        [row2(mem_norm_g), w_mem_kv.astype(BF16), row2(k_norm_c)],
        [jax.ShapeDtypeStruct((mrows, C_WIDTH), BF16)] * 2,
        [_row_spec(min(tm, mrows), C_WIDTH)] * 2)

    T = DSA_T
    i_idx = np.arange(T)[:, None]
    dist_a = T + i_idx - np.arange(2 * T)[None, :]
    rb_a = _bias_tiles(rel_bias, _t5_bucket_np(dist_a), 0, A_HEADS, NUM_BUCKETS - 1, LOG2E)
    i_idx = np.arange(WINDOW)[:, None]
    dist_b = WINDOW + i_idx - np.arange(2 * WINDOW)[None, :]
    bucket_b = np.where((dist_b >= 0) & (dist_b < WINDOW), _t5_bucket_np(dist_b), -1).astype(np.int32)
    bias_b = _bias_tiles(rel_bias, bucket_b, A_HEADS, B_HEADS, None, 1.0)

    r3 = lambda v, w: v.reshape(bsz, seq, w)
    ikt = jnp.swapaxes(ik.reshape(bsz, seq, IDX_DIM), 1, 2)
    oa = _dsa_call(iq, iw.reshape(bsz, seq, IDX_HEADS), ikt, r3(qa, A_WIDTH), r3(ka, A_WIDTH),
                   r3(va, A_WIDTH), r3(ga, A_WIDTH), rb_a, topk)
    ob = _swa_call(sinks_b, qb, kb, vb, r3(gb, B_WIDTH), bias_b)
    oc = _mem_call(r3(qc, C_WIDTH), kc.reshape(bsz, mlen, C_WIDTH), vc.reshape(bsz, mlen, C_WIDTH),
                   r3(gc, C_WIDTH))

    out = _final_call(x2d, g_x, oa.reshape(n, A_WIDTH), ob.reshape(n, B_WIDTH), oc.reshape(n, C_WIDTH),
                      w_mix, row2(gate_bias), w_up_a.astype(BF16), w_up_b.astype(BF16),
                      w_up_c.astype(BF16), w_o.astype(BF16))
    return out.reshape(bsz, seq, d)


def kernel(x, mem, norm_g, w_in, kv_norm_g, w_kv_up, idx_k_ln_g, idx_k_ln_b, q_norm_a, k_norm_a, q_norm_b, k_norm_b, sinks_b, mem_norm_g, w_mem_kv, q_norm_c, k_norm_c, w_up_a, w_up_b, w_up_c, gate_bias, w_o, rel_bias):
    for l in range(norm_g.shape[0]):
        x = _layer(x, mem, norm_g[l], w_in[l], kv_norm_g[l], w_kv_up[l], idx_k_ln_g[l], idx_k_ln_b[l],
                   q_norm_a[l], k_norm_a[l], q_norm_b[l], k_norm_b[l], sinks_b[l], mem_norm_g[l],
                   w_mem_kv[l], q_norm_c[l], k_norm_c[l], w_up_a[l], w_up_b[l], w_up_c[l],
                   gate_bias[l], w_o[l], rel_bias)
    return x
```

```python
import functools
import math

import numpy as np
import jax
import jax.numpy as jnp
from jax import lax
from jax.experimental import pallas as pl
from jax.experimental.pallas import tpu as pltpu

F32 = jnp.float32
BF16 = jnp.bfloat16

EPS = 1e-6
A_HEADS, A_HEAD_DIM, A_KV_RANK = 6, 128, 256
IDX_HEADS, IDX_DIM, TOPK_MAX = 16, 64, 256
B_HEADS, B_KV_HEADS, B_HEAD_DIM, WINDOW = 12, 2, 64, 128
C_HEADS, C_HEAD_DIM = 4, 128
NUM_BUCKETS, MAX_DISTANCE = 32, 128
N_BRANCH = 3
A_WIDTH = A_HEADS * A_HEAD_DIM
B_WIDTH = B_HEADS * B_HEAD_DIM
B_KV_WIDTH = B_KV_HEADS * B_HEAD_DIM
C_WIDTH = C_HEADS * C_HEAD_DIM

LANES = 128
VMEM_LIMIT = 56 * 1024 * 1024
NEG_BIG = -1e30

ROW_TILE = 512
DSA_T = 256
SWA_T = 512
FINAL_NCHUNK = 4
COUNT_ROWS = 128
REGROUP_ROWS = 256
COUNT_UNROLL = 4
FAR_TILES = 4
LOG2E = math.log2(math.e)


def _cparams(sem):
    return pltpu.CompilerParams(dimension_semantics=sem, vmem_limit_bytes=VMEM_LIMIT)


def _dot(a, b):
    return jnp.dot(a, b, preferred_element_type=F32)


def _dot_nt(a, b):
    return lax.dot_general(a, b, (((1,), (1,)), ((), ())), preferred_element_type=F32)


def _rms_rows(x, g):
    ms = jnp.mean(x * x, axis=-1, keepdims=True)
    return x * lax.rsqrt(ms + EPS) * g


def _headnorm128(seg, g):
    ms = jnp.mean(seg * seg, axis=-1, keepdims=True)
    return seg * lax.rsqrt(ms + EPS) * g


def _headnorm64_pair(seg, g2):
    sq = seg * seg
    low = lax.broadcasted_iota(jnp.int32, seg.shape, 1) < B_HEAD_DIM
    s_all = jnp.sum(sq, axis=-1, keepdims=True)
    s_lo = jnp.sum(jnp.where(low, sq, 0.0), axis=-1, keepdims=True)
    ms = jnp.where(low, s_lo, s_all - s_lo) * (1.0 / B_HEAD_DIM)
    return seg * lax.rsqrt(ms + EPS) * g2


def _silu(y):
    return y / (1.0 + jnp.exp(-y))


def _proj_a_kernel(x_ref, g_ref, w_ref, kvg_ref, wkv_ref, qg_ref, kg_ref, qa_ref, ka_ref, va_ref):
    h = _rms_rows(x_ref[...], g_ref[...]).astype(BF16)
    y = _dot(h, w_ref[...])
    scale = A_HEAD_DIM ** -0.5 * LOG2E
    for hh in range(A_HEADS):
        sl = slice(hh * A_HEAD_DIM, (hh + 1) * A_HEAD_DIM)
        qa_ref[:, sl] = (_headnorm128(y[:, sl], qg_ref[...]) * scale).astype(BF16)
    ckv = _rms_rows(y[:, A_WIDTH:A_WIDTH + A_KV_RANK], kvg_ref[...]).astype(BF16)
    kv = _dot(ckv, wkv_ref[...])
    for hh in range(A_HEADS):
        sl = slice(hh * A_HEAD_DIM, (hh + 1) * A_HEAD_DIM)
        ka_ref[:, sl] = _headnorm128(kv[:, sl], kg_ref[...]).astype(BF16)
    va_ref[...] = kv[:, A_WIDTH:].astype(BF16)


def _proj_i_kernel(x_ref, g_ref, w_ref, lng_ref, lnb_ref, iq_ref, ik_ref, iw_ref):
    h = _rms_rows(x_ref[...], g_ref[...]).astype(BF16)
    y = _dot(h, w_ref[...])
    k0 = IDX_HEADS * IDX_DIM
    iq_ref[...] = y[:, :k0].astype(BF16)
    ik = y[:, k0:k0 + IDX_DIM]
    mu = jnp.mean(ik, axis=-1, keepdims=True)
    d = ik - mu
    var = jnp.mean(d * d, axis=-1, keepdims=True)
    ik_ref[...] = (d * lax.rsqrt(var + EPS) * lng_ref[...] + lnb_ref[...]).astype(BF16)
    w0 = k0 + LANES
    iw_ref[...] = y[:, w0:w0 + IDX_HEADS] * (IDX_HEADS ** -0.5) * (IDX_DIM ** -0.5)


def _proj_g_kernel(x_ref, g_ref, w_ref, ga_ref, gb_ref, gc_ref):
    h = _rms_rows(x_ref[...], g_ref[...]).astype(BF16)
    y = _silu(_dot(h, w_ref[...]))
    ga_ref[...] = y[:, :A_WIDTH].astype(BF16)
    gb_ref[...] = y[:, A_WIDTH:A_WIDTH + B_WIDTH].astype(BF16)
    gc_ref[...] = y[:, A_WIDTH + B_WIDTH:].astype(BF16)


def _proj_bc_kernel(x_ref, g_ref, w_ref, qbg_ref, kbg_ref, qcg_ref, qb_ref, kb_ref, vb_ref, qc_ref):
    h = _rms_rows(x_ref[...], g_ref[...]).astype(BF16)
    y = _dot(h, w_ref[...])
    sb = B_HEAD_DIM ** -0.5
    for p in range(B_WIDTH // LANES):
        pair = (_headnorm64_pair(y[:, p * LANES:(p + 1) * LANES], qbg_ref[...]) * sb).astype(BF16)
        qb_ref[2 * p] = pair[:, :B_HEAD_DIM]
        qb_ref[2 * p + 1] = pair[:, B_HEAD_DIM:]
    k0 = B_WIDTH
    kpair = _headnorm64_pair(y[:, k0:k0 + B_KV_WIDTH], kbg_ref[...]).astype(BF16)
    vpair = y[:, k0 + B_KV_WIDTH:k0 + 2 * B_KV_WIDTH].astype(BF16)
    for g in range(B_KV_HEADS):
        kb_ref[g] = kpair[:, g * B_HEAD_DIM:(g + 1) * B_HEAD_DIM]
        vb_ref[g] = vpair[:, g * B_HEAD_DIM:(g + 1) * B_HEAD_DIM]
    c0 = k0 + 2 * B_KV_WIDTH
    sc = C_HEAD_DIM ** -0.5
    for hh in range(C_HEADS):
        sl = slice(hh * C_HEAD_DIM, (hh + 1) * C_HEAD_DIM)
        qc_ref[:, sl] = (_headnorm128(y[:, c0 + hh * C_HEAD_DIM:c0 + (hh + 1) * C_HEAD_DIM],
                                      qcg_ref[...]) * sc).astype(BF16)


def _proj_mem_kernel(x_ref, g_ref, w_ref, kg_ref, kc_ref, vc_ref):
    h = _rms_rows(x_ref[...], g_ref[...]).astype(BF16)
    y = _dot(h, w_ref[...])
    for hh in range(C_HEADS):
        sl = slice(hh * C_HEAD_DIM, (hh + 1) * C_HEAD_DIM)
        kc_ref[:, sl] = _headnorm128(y[:, sl], kg_ref[...]).astype(BF16)
    vc_ref[...] = y[:, C_WIDTH:].astype(BF16)


def _row_spec(tm, cols):
    return pl.BlockSpec((tm, cols), lambda i: (i, 0))


def _full_spec(shape):
    nd = len(shape)
    return pl.BlockSpec(shape, lambda i: (0,) * nd)


def _proj_call(kernel_fn, name, x2d, tm, consts, out_shapes, out_specs):
    n, d = x2d.shape
    in_specs = [_row_spec(tm, d)] + [_full_spec(c.shape) for c in consts]
    return pl.pallas_call(
        kernel_fn,
        grid=(n // tm,),
        in_specs=in_specs,
        out_specs=out_specs,
        out_shape=out_shapes,
        compiler_params=_cparams(("parallel",)),
        name=name,
    )(x2d, *consts)


def _t5_bucket_np(dist):
    n = np.maximum(dist, 0)
    max_exact = NUM_BUCKETS // 2
    nf = np.maximum(n, 1).astype(np.float32)
    large = max_exact + (np.log(nf / max_exact) / math.log(MAX_DISTANCE / max_exact)
                         * (NUM_BUCKETS - max_exact)).astype(np.int32)
    large = np.minimum(large, NUM_BUCKETS - 1)
    return np.where(n < max_exact, n, large).astype(np.int32)


def _bias_tile_kernel(tab_ref, bucket_ref, o_ref, *, head0, shift_bucket, scale):
    h = pl.program_id(0) + head0
    bucket = bucket_ref[...]
    acc = jnp.zeros(bucket.shape, F32)
    for b in range(NUM_BUCKETS):
        acc = jnp.where(bucket == b, tab_ref[b, h], acc)
    if shift_bucket is not None:
        acc = acc - tab_ref[shift_bucket, h]
    o_ref[...] = jnp.where(bucket < 0, NEG_BIG, acc * scale)


def _bias_tiles(rel_bias, bucket_np, head0, nheads, shift_bucket, scale):
    r, c = bucket_np.shape
    return pl.pallas_call(
        functools.partial(_bias_tile_kernel, head0=head0, shift_bucket=shift_bucket, scale=scale),
        grid=(nheads,),
        in_specs=[pl.BlockSpec(memory_space=pltpu.SMEM), pl.BlockSpec((r, c), lambda h: (0, 0))],
        out_specs=pl.BlockSpec((None, r, c), lambda h: (h, 0, 0)),
        out_shape=jax.ShapeDtypeStruct((nheads, r, c), F32),
        compiler_params=_cparams(("arbitrary",)),
        name="bias_tiles",
    )(rel_bias, jnp.asarray(bucket_np))


def _dsa_kernel(iq_ref, iw_ref, ikz_ref, q_ref, k_ref, v_ref, g_ref, rb_ref, o_ref,
                sc_ref, wb_ref, lo_ref, hi_ref, mid_ref, clo_ref, s_ref, m_ref, l_ref, acc_ref, *, topk):
    T = DSA_T
    qi = pl.program_id(1)
    kf = float(topk)

    def lanes(col):
        return jnp.broadcast_to(col, (T, LANES))

    def halves(tile):
        return [tile[:, c * LANES:(c + 1) * LANES] for c in range(T // LANES)]

    for hh in range(IDX_HEADS):
        wb_ref[hh] = lanes(iw_ref[:, hh:hh + 1])

    def score_chunk(j, diag):
        keys = pl.ds(pl.multiple_of(j * T, T), T)
        accs = [jnp.zeros((T, LANES), F32) for _ in range(T // LANES)]
        for hh in range(IDX_HEADS):
            pair = iq_ref[:, (hh // 2) * LANES:(hh // 2 + 1) * LANES]
            s = _dot(pair, ikz_ref[hh % 2, :, keys])
            w = wb_ref[hh]
            accs = [a + w * jnp.maximum(sh, 0.0) for a, sh in zip(accs, halves(s))]
        if diag:
            row = lax.broadcasted_iota(jnp.int32, (T, LANES), 0)
            col = lax.broadcasted_iota(jnp.int32, (T, LANES), 1)
            causal = [col + c * LANES <= row for c in range(T // LANES)]
            lows = [jnp.where(cm, a, jnp.inf) for cm, a in zip(causal, accs)]
            accs = [jnp.where(cm, a, -jnp.inf) for cm, a in zip(causal, accs)]
        else:
            lows = accs
        for c, a in enumerate(accs):
            sc_ref[j, :, c * LANES:(c + 1) * LANES] = a
        return functools.reduce(jnp.minimum, lows), functools.reduce(jnp.maximum, accs)

    def p1_body(j, carry):
        mn, mx = carry
        lo_c, hi_c = score_chunk(j, False)
        return jnp.minimum(mn, lo_c), jnp.maximum(mx, hi_c)

    mn0 = jnp.full((T, LANES), jnp.inf, F32)
    mx0 = jnp.full((T, LANES), -jnp.inf, F32)
    def p1_pair(jj, carry):
        return p1_body(2 * jj + 1, p1_body(2 * jj, carry))

    n_pairs = lax.div(qi, 2)
    mn, mx = lax.fori_loop(0, n_pairs, p1_pair, (mn0, mx0))
    mn, mx = lax.fori_loop(2 * n_pairs, qi, p1_body, (mn, mx))
    lo_c, hi_c = score_chunk(qi, True)
    mn = lanes(jnp.min(jnp.minimum(mn, lo_c), axis=-1, keepdims=True))
    mx = lanes(jnp.max(jnp.maximum(mx, hi_c), axis=-1, keepdims=True))

    def count_lanes(rows, v, strict=False):
        def tile(j, c):
            for c0 in range(0, T, LANES):
                sh = sc_ref[j, rows, c0:c0 + LANES]
                hit = (sh > v) if strict else (sh >= v)
                c = c + jnp.where(hit, 1.0, 0.0)
            return c

        def group(g, c):
            for u in range(COUNT_UNROLL):
                c = tile(g * COUNT_UNROLL + u, c)
            return c

        n_groups = lax.div(qi + 1, COUNT_UNROLL)
        c = lax.fori_loop(0, n_groups, group, jnp.zeros((COUNT_ROWS, LANES), F32))
        return lax.fori_loop(n_groups * COUNT_UNROLL, qi + 1, tile, c)

    def row_total(c):
        return jnp.broadcast_to(jnp.sum(c, axis=-1, keepdims=True), c.shape)

    slabs = [slice(r0, r0 + COUNT_ROWS) for r0 in range(0, T, COUNT_ROWS)]
    n_causal = (qi * T + 1 + lax.broadcasted_iota(jnp.int32, (T, LANES), 0)).astype(F32)
    hi0 = jnp.where(n_causal <= kf, mn, mx)
    lo_ref[...] = mn
    hi_ref[...] = hi0
    mid_ref[...] = hi0
    clo_ref[...] = n_causal

    def bis_body(_):
        partial = [count_lanes(rows, mid_ref[rows]) for rows in slabs]
        widths = [hi_ref[rows] - lo_ref[rows] for rows in slabs]
        open_width = jnp.max(functools.reduce(jnp.maximum, widths))
        for rows, c in zip(slabs, partial):
            lo, hi, mid = lo_ref[rows], hi_ref[rows], mid_ref[rows]
            cnt = row_total(c)
            up = cnt >= kf
            lo = jnp.where(up, mid, lo)
            c_lo = jnp.where(up, cnt, clo_ref[rows])
            hi = jnp.where(up, hi, mid)
            nxt = lo + 0.5 * (hi - lo)
            stop = jnp.logical_or(c_lo == kf, jnp.logical_or(nxt <= lo, nxt >= hi))
            hi = jnp.where(stop, lo, hi)
            lo_ref[rows] = lo
            hi_ref[rows] = hi
            mid_ref[rows] = jnp.where(stop, lo, nxt)
            clo_ref[rows] = c_lo
        return open_width

    lax.while_loop(lambda w: w > 0.0, bis_body, jnp.max(hi0 - mn))
    thr = lo_ref[...]

    tied_f = jnp.where(clo_ref[...] > kf, 1.0, 0.0)

    @pl.when(jnp.max(tied_f) > 0.0)
    def _():
        above = jnp.concatenate(
            [row_total(count_lanes(rows, thr[rows], strict=True)) for rows in slabs], axis=0)
        need = (kf - above)[:, :1]
        r = lax.broadcasted_iota(jnp.int32, (T, T), 0)
        c = lax.broadcasted_iota(jnp.int32, (T, T), 1)
        before = jnp.where(r < c, 1.0, 0.0).astype(BF16)
        thr_col = thr[:, :1]
        tied_col = tied_f[:, :1] > 0.0

        def body(j, seen):
            s = sc_ref[j]
            eq = jnp.where(s == thr_col, 1.0, 0.0)
            rank = seen + _dot(eq.astype(BF16), before)
            drop = jnp.logical_and(tied_col, jnp.logical_and(eq > 0.0, rank >= need))
            sc_ref[j] = jnp.where(drop, -jnp.inf, s)
            return seen + jnp.sum(eq, axis=-1, keepdims=True)

        lax.fori_loop(0, qi + 1, body, jnp.zeros((T, 1), F32))

    m_ref[...] = jnp.full(m_ref.shape, NEG_BIG, F32)
    l_ref[...] = jnp.zeros(l_ref.shape, F32)
    acc_ref[...] = jnp.zeros(acc_ref.shape, F32)

    def attend(j0, ntiles, band):
        sel = [jnp.where(sh >= thr, 0.0, NEG_BIG) for t in range(ntiles) for sh in halves(sc_ref[j0 + t])]
        start = pl.multiple_of(j0 * T, T)
        keys = pl.ds(start, ntiles * T)
        width = ntiles * T

        def logits_to_scratch(hh):
            sl = slice(hh * A_HEAD_DIM, (hh + 1) * A_HEAD_DIM)
            s_ref[hh % 2, :, :width] = _dot_nt(q_ref[:, sl], k_ref[keys, sl])

        logits_to_scratch(0)
        pending = None
        for hh in range(A_HEADS):
            sl = slice(hh * A_HEAD_DIM, (hh + 1) * A_HEAD_DIM)
            if hh + 1 < A_HEADS:
                logits_to_scratch(hh + 1)
            s = s_ref[hh % 2, :, :width]
            if band is not None:
                s = s + rb_ref[hh, :, band * T:(band + 1) * T]
            sh = [s[:, c * LANES:(c + 1) * LANES] + m for c, m in enumerate(sel)]
            m_prev = m_ref[hh]
            m_cur = jnp.max(functools.reduce(jnp.maximum, sh), axis=-1, keepdims=True)
            m_new = jnp.maximum(m_prev, lanes(m_cur))
            alpha = jnp.exp2(m_prev - m_new)
            ph = [jnp.exp2(x - m_new) for x in sh]
            l_ref[hh] = alpha * l_ref[hh] + functools.reduce(jnp.add, ph)
            p = jnp.concatenate([x.astype(BF16) for x in ph], axis=-1)
            acc_ref[hh] = alpha * acc_ref[hh]
            m_ref[hh] = m_new
            if pending is not None:
                acc_ref[pending[0]] += pending[1]
            pending = (hh, _dot(p, v_ref[keys, sl]))
        acc_ref[pending[0]] += pending[1]

    n_far = jnp.maximum(qi - 1, 0)
    done_tiles = 0
    width = FAR_TILES
    while width >= 1:
        trips = lax.div(n_far - done_tiles, width)

        def far_body(i, c, width=width, base=done_tiles):
            attend(base + i * width, width, None)
            return c

        lax.fori_loop(0, trips, far_body, 0)
        done_tiles = done_tiles + trips * width
        width //= 2

    @pl.when(qi > 0)
    def _():
        attend(qi - 1, 1, 0)

    attend(qi, 1, 1)

    for hh in range(A_HEADS):
        sl = slice(hh * A_HEAD_DIM, (hh + 1) * A_HEAD_DIM)
        den = jnp.sum(l_ref[hh], axis=-1, keepdims=True)
        o_ref[:, sl] = (acc_ref[hh] / den * g_ref[:, sl].astype(F32)).astype(BF16)


def _dsa_call(iq, iw, ikz, qa, ka, va, ga, rb, topk):
    bsz, seq, _ = qa.shape
    T = DSA_T
    nq = seq // T
    once = pl.Buffered(1)
    return pl.pallas_call(
        functools.partial(_dsa_kernel, topk=topk),
        grid=(bsz, nq),
        in_specs=[
            pl.BlockSpec((None, T, IDX_HEADS * IDX_DIM), lambda b, i: (b, i, 0)),
            pl.BlockSpec((None, T, IDX_HEADS), lambda b, i: (b, i, 0)),
            pl.BlockSpec((None, 2, LANES, seq), lambda b, i: (b, 0, 0, 0), pipeline_mode=once),
            pl.BlockSpec((None, T, A_WIDTH), lambda b, i: (b, i, 0)),
            pl.BlockSpec((None, seq, A_WIDTH), lambda b, i: (b, 0, 0), pipeline_mode=once),
            pl.BlockSpec((None, seq, A_WIDTH), lambda b, i: (b, 0, 0), pipeline_mode=once),
            pl.BlockSpec((None, T, A_WIDTH), lambda b, i: (b, i, 0)),
            pl.BlockSpec((A_HEADS, T, 2 * T), lambda b, i: (0, 0, 0), pipeline_mode=once),
        ],
        out_specs=pl.BlockSpec((None, T, A_WIDTH), lambda b, i: (b, i, 0)),
        out_shape=jax.ShapeDtypeStruct((bsz, seq, A_WIDTH), BF16),
        scratch_shapes=[
            pltpu.VMEM((nq, T, T), F32),
            pltpu.VMEM((IDX_HEADS, T, LANES), F32),
            pltpu.VMEM((T, LANES), F32),
            pltpu.VMEM((T, LANES), F32),
            pltpu.VMEM((T, LANES), F32),
            pltpu.VMEM((T, LANES), F32),
            pltpu.VMEM((2, T, FAR_TILES * T), F32),
            pltpu.VMEM((A_HEADS, T, LANES), F32),
            pltpu.VMEM((A_HEADS, T, LANES), F32),
            pltpu.VMEM((A_HEADS, T, A_HEAD_DIM), F32),
        ],
        compiler_params=_cparams(("arbitrary", "arbitrary")),
        name="dsa",
    )(iq, iw, ikz, qa, ka, va, ga, rb)


def _swa_kernel(sink_ref, q_ref, kc_ref, vc_ref, kp_ref, vp_ref, g_ref, bias_ref, o_ref):
    first = pl.program_id(1) == 0
    blk = WINDOW
    group = B_HEADS // B_KV_HEADS
    grows = group * blk
    for sb in range(SWA_T // blk):
        rows = slice(sb * blk, (sb + 1) * blk)
        prev = slice((sb - 1) * blk, sb * blk)
        pieces = []
        for g in range(B_KV_HEADS):
            heads = slice(g * group, (g + 1) * group)
            k_prev = kp_ref[g] if sb == 0 else kc_ref[g, prev, :]
            v_prev = vp_ref[g] if sb == 0 else vc_ref[g, prev, :]
            q = q_ref[heads, rows, :].reshape(grows, B_HEAD_DIM)
            lp = _dot_nt(q, k_prev) + bias_ref[heads, :, :blk].reshape(grows, blk)
            lc = _dot_nt(q, kc_ref[g, rows, :]) + bias_ref[heads, :, blk:].reshape(grows, blk)
            if sb == 0:
                lp = jnp.where(first, NEG_BIG, lp)
            sink = jnp.concatenate(
                [jnp.full((blk, 1), sink_ref[g * group + h], F32) for h in range(group)], axis=0)
            m = jnp.maximum(jnp.max(lp, axis=-1, keepdims=True), jnp.max(lc, axis=-1, keepdims=True))
            m = jnp.maximum(m, sink)
            pp = jnp.exp(lp - m)
            pc = jnp.exp(lc - m)
            den = (jnp.sum(pp, axis=-1, keepdims=True) + jnp.sum(pc, axis=-1, keepdims=True)
                   + jnp.exp(sink - m))
            o = (_dot(pp.astype(BF16), v_prev) + _dot(pc.astype(BF16), vc_ref[g, rows, :])) / den
            pieces += [o[h * blk:(h + 1) * blk, :] for h in range(group)]
        o_ref[rows, :] = (jnp.concatenate(pieces, axis=1) * g_ref[rows, :].astype(F32)).astype(BF16)


def _swa_call(sinks, qb, kb, vb, gb, bias):
    bsz, _, seq, _ = qb.shape
    T = SWA_T
    per = T // WINDOW
    cur = lambda b, i: (b, 0, i, 0)
    prev = lambda b, i: (b, 0, jnp.maximum(i * per - 1, 0), 0)
    return pl.pallas_call(
        _swa_kernel,
        grid=(bsz, seq // T),
        in_specs=[
            pl.BlockSpec(memory_space=pltpu.SMEM),
            pl.BlockSpec((None, B_HEADS, T, B_HEAD_DIM), cur),
            pl.BlockSpec((None, B_KV_HEADS, T, B_HEAD_DIM), cur),
            pl.BlockSpec((None, B_KV_HEADS, T, B_HEAD_DIM), cur),
            pl.BlockSpec((None, B_KV_HEADS, WINDOW, B_HEAD_DIM), prev),
            pl.BlockSpec((None, B_KV_HEADS, WINDOW, B_HEAD_DIM), prev),
            pl.BlockSpec((None, T, B_WIDTH), lambda b, i: (b, i, 0)),
            pl.BlockSpec((B_HEADS, WINDOW, 2 * WINDOW), lambda b, i: (0, 0, 0)),
        ],
        out_specs=pl.BlockSpec((None, T, B_WIDTH), lambda b, i: (b, i, 0)),
        out_shape=jax.ShapeDtypeStruct((bsz, seq, B_WIDTH), BF16),
        compiler_params=_cparams(("parallel", "parallel")),
        name="swa",
    )(sinks, qb, kb, vb, kb, vb, gb, bias)


def _mem_kernel(q_ref, k_ref, v_ref, g_ref, o_ref):
    for hh in range(C_HEADS):
        sl = slice(hh * C_HEAD_DIM, (hh + 1) * C_HEAD_DIM)
        s = _dot_nt(q_ref[:, sl], k_ref[:, sl])
        m = jnp.max(s, axis=-1, keepdims=True)
        p = jnp.exp(s - m)
        den = jnp.sum(p, axis=-1, keepdims=True)
        o = _dot(p.astype(BF16), v_ref[:, sl]) / den
        o_ref[:, sl] = (o * g_ref[:, sl].astype(F32)).astype(BF16)


def _mem_call(qc, kc, vc, gc):
    bsz, seq, _ = qc.shape
    mlen = kc.shape[1]
    T = SWA_T
    cur = lambda b, i: (b, i, 0)
    whole = lambda b, i: (b, 0, 0)
    return pl.pallas_call(
        _mem_kernel,
        grid=(bsz, seq // T),
        in_specs=[
            pl.BlockSpec((None, T, C_WIDTH), cur),
            pl.BlockSpec((None, mlen, C_WIDTH), whole),
            pl.BlockSpec((None, mlen, C_WIDTH), whole),
            pl.BlockSpec((None, T, C_WIDTH), cur),
        ],
        out_specs=pl.BlockSpec((None, T, C_WIDTH), cur),
        out_shape=jax.ShapeDtypeStruct((bsz, seq, C_WIDTH), BF16),
        compiler_params=_cparams(("parallel", "parallel")),
        name="mem_attn",
    )(qc, kc, vc, gc)


def _final_kernel(x_ref, g_ref, oa_ref, ob_ref, oc_ref, wma_ref, wmb_ref, wmc_ref,
                  ba_ref, bb_ref, bc_ref, wua_ref, wub_ref, wuc_ref, wo_ref, out_ref, h_ref, acc_ref):
    n = pl.program_id(1)

    @pl.when(n == 0)
    def _():
        h_ref[...] = _rms_rows(x_ref[...], g_ref[...]).astype(BF16)
        acc_ref[...] = jnp.zeros(acc_ref.shape, F32)

    h = h_ref[...]

    def branch(o_ref, wm_ref, b_ref, wu_ref):
        gate = 1.0 / (1.0 + jnp.exp(-(_dot(h, wm_ref[...]) + b_ref[...])))
        return gate * _dot(o_ref[...], wu_ref[...])

    merged = (branch(oa_ref, wma_ref, ba_ref, wua_ref) + branch(ob_ref, wmb_ref, bb_ref, wub_ref)
              + branch(oc_ref, wmc_ref, bc_ref, wuc_ref))
    acc_ref[...] += _dot(merged.astype(BF16), wo_ref[...])

    @pl.when(n == pl.num_programs(1) - 1)
    def _():
        out_ref[...] = x_ref[...] + acc_ref[...]


def _final_call(x2d, g, oa, ob, oc, wmix, gate_bias, wua, wub, wuc, wo):
    n, d = x2d.shape
    tm = ROW_TILE
    nch = FINAL_NCHUNK
    cw = d // nch
    row = lambda i, c: (i, 0)

    def col(br):
        return lambda i, c: (0, br * nch + c)

    return pl.pallas_call(
        _final_kernel,
        grid=(n // tm, nch),
        in_specs=[
            pl.BlockSpec((tm, d), row),
            pl.BlockSpec((1, d), lambda i, c: (0, 0)),
            pl.BlockSpec((tm, A_WIDTH), row),
            pl.BlockSpec((tm, B_WIDTH), row),
            pl.BlockSpec((tm, C_WIDTH), row),
            pl.BlockSpec((d, cw), col(0)), pl.BlockSpec((d, cw), col(1)), pl.BlockSpec((d, cw), col(2)),
            pl.BlockSpec((1, cw), col(0)), pl.BlockSpec((1, cw), col(1)), pl.BlockSpec((1, cw), col(2)),
            pl.BlockSpec((A_WIDTH, cw), lambda i, c: (0, c)),
            pl.BlockSpec((B_WIDTH, cw), lambda i, c: (0, c)),
            pl.BlockSpec((C_WIDTH, cw), lambda i, c: (0, c)),
            pl.BlockSpec((cw, d), lambda i, c: (c, 0)),
        ],
        out_specs=pl.BlockSpec((tm, d), row),
        out_shape=jax.ShapeDtypeStruct((n, d), F32),
        scratch_shapes=[pltpu.VMEM((tm, d), BF16), pltpu.VMEM((tm, d), F32)],
        compiler_params=_cparams(("parallel", "arbitrary")),
        name="merge_out",
    )(x2d, g, oa, ob, oc, wmix, wmix, wmix, gate_bias, gate_bias, gate_bias, wua, wub, wuc, wo)


def _w_in_groups(d):
    sizes = (A_WIDTH, A_KV_RANK, IDX_HEADS * IDX_DIM, IDX_DIM, IDX_HEADS, A_WIDTH,
             B_WIDTH, B_KV_WIDTH, B_KV_WIDTH, B_WIDTH, C_WIDTH, C_WIDTH, N_BRANCH * d)
    cuts = np.cumsum((0,) + sizes).tolist()
    (aq, ackv, iq, ik, iw, ag, bq, bk, bv, bg, cq, cg, mix) = [
        (cuts[i], cuts[i + 1]) for i in range(len(sizes))]
    return [
        [(aq[0], ackv[1])],
        [(iq[0], ik[1]), LANES - IDX_DIM, iw, LANES - IDX_HEADS],
        [ag, bg, cg],
        [(bq[0], bv[1]), cq],
        [mix],
    ]


def _group_width(group):
    return sum(p if isinstance(p, int) else p[1] - p[0] for p in group)


def _regroup_kernel(w_ref, *out_refs, groups):
    for o_ref, group in zip(out_refs, groups):
        col = 0
        for part in group:
            if isinstance(part, int):
                o_ref[:, col:col + part] = jnp.zeros((o_ref.shape[0], part), BF16)
                col += part
            else:
                o_ref[:, col:col + part[1] - part[0]] = w_ref[:, part[0]:part[1]].astype(BF16)
                col += part[1] - part[0]


def _regroup_w_in(w_in):
    d, cols = w_in.shape
    groups = _w_in_groups(d)
    rows = REGROUP_ROWS
    return pl.pallas_call(
        functools.partial(_regroup_kernel, groups=groups),
        grid=(d // rows,),
        in_specs=[pl.BlockSpec((rows, cols), lambda i: (i, 0))],
        out_specs=[pl.BlockSpec((rows, _group_width(g)), lambda i: (i, 0)) for g in groups],
        out_shape=[jax.ShapeDtypeStruct((d, _group_width(g)), BF16) for g in groups],
        compiler_params=_cparams(("parallel",)),
        name="regroup_w_in",
    )(w_in)


def _layer(x, mem, norm_g, w_in, kv_norm_g, w_kv_up, idx_k_ln_g, idx_k_ln_b, q_norm_a, k_norm_a,
           q_norm_b, k_norm_b, sinks_b, mem_norm_g, w_mem_kv, q_norm_c, k_norm_c,
           w_up_a, w_up_b, w_up_c, gate_bias, w_o, rel_bias):
    bsz, seq, d = x.shape
    mlen = mem.shape[1]
    n = bsz * seq
    tm = ROW_TILE
    topk = min(TOPK_MAX, seq // 4)
    x2d = x.reshape(n, d)
    row2 = lambda v: v.reshape(1, -1)

    w_grp_a, w_grp_i, w_grp_g, w_grp_bc, w_mix = _regroup_w_in(w_in)
    g_x = row2(norm_g)

    qa, ka, va = _proj_call(
        _proj_a_kernel, "proj_a", x2d, tm,
        [g_x, w_grp_a, row2(kv_norm_g), w_kv_up.astype(BF16), row2(q_norm_a), row2(k_norm_a)],
        [jax.ShapeDtypeStruct((n, A_WIDTH), BF16)] * 3,
        [_row_spec(tm, A_WIDTH)] * 3)

    per_b = seq // tm
    iq, ik, iw = _proj_call(
        _proj_i_kernel, "proj_i", x2d, tm,
        [g_x, w_grp_i, row2(idx_k_ln_g), row2(idx_k_ln_b)],
        [jax.ShapeDtypeStruct((n, IDX_HEADS * IDX_DIM), BF16),
         jax.ShapeDtypeStruct((n, IDX_DIM), BF16),
         jax.ShapeDtypeStruct((n, IDX_HEADS), F32)],
        [_row_spec(tm, IDX_HEADS * IDX_DIM), _row_spec(tm, IDX_DIM), _row_spec(tm, IDX_HEADS)])

    ga, gb, gc = _proj_call(
        _proj_g_kernel, "proj_g", x2d, tm, [g_x, w_grp_g],
        [jax.ShapeDtypeStruct((n, A_WIDTH), BF16), jax.ShapeDtypeStruct((n, B_WIDTH), BF16),
         jax.ShapeDtypeStruct((n, C_WIDTH), BF16)],
        [_row_spec(tm, A_WIDTH), _row_spec(tm, B_WIDTH), _row_spec(tm, C_WIDTH)])

    qb, kb, vb, qc = _proj_call(
        _proj_bc_kernel, "proj_bc", x2d, tm,
        [g_x, w_grp_bc, row2(jnp.tile(q_norm_b, 2)), row2(jnp.tile(k_norm_b, 2)), row2(q_norm_c)],
        [jax.ShapeDtypeStruct((bsz, B_HEADS, seq, B_HEAD_DIM), BF16),
         jax.ShapeDtypeStruct((bsz, B_KV_HEADS, seq, B_HEAD_DIM), BF16),
         jax.ShapeDtypeStruct((bsz, B_KV_HEADS, seq, B_HEAD_DIM), BF16),
         jax.ShapeDtypeStruct((n, C_WIDTH), BF16)],
        [pl.BlockSpec((None, B_HEADS, tm, B_HEAD_DIM), lambda i: (i // per_b, 0, i % per_b, 0)),
         pl.BlockSpec((None, B_KV_HEADS, tm, B_HEAD_DIM), lambda i: (i // per_b, 0, i % per_b, 0)),
         pl.BlockSpec((None, B_KV_HEADS, tm, B_HEAD_DIM), lambda i: (i // per_b, 0, i % per_b, 0)),
         _row_spec(tm, C_WIDTH)])

    mrows = bsz * mlen
    kc, vc = _proj_call(
        _proj_mem_kernel, "proj_mem", mem.reshape(mrows, d), min(tm, mrows),
        [row2(mem_norm_g), w_mem_kv.astype(BF16), row2(k_norm_c)],
        [jax.ShapeDtypeStruct((mrows, C_WIDTH), BF16)] * 2,
        [_row_spec(min(tm, mrows), C_WIDTH)] * 2)

    T = DSA_T
    i_idx = np.arange(T)[:, None]
    dist_a = T + i_idx - np.arange(2 * T)[None, :]
    rb_a = _bias_tiles(rel_bias, _t5_bucket_np(dist_a), 0, A_HEADS, NUM_BUCKETS - 1, LOG2E)
    i_idx = np.arange(WINDOW)[:, None]
    dist_b = WINDOW + i_idx - np.arange(2 * WINDOW)[None, :]
    bucket_b = np.where((dist_b >= 0) & (dist_b < WINDOW), _t5_bucket_np(dist_b), -1).astype(np.int32)
    bias_b = _bias_tiles(rel_bias, bucket_b, A_HEADS, B_HEADS, None, 1.0)

    r3 = lambda v, w: v.reshape(bsz, seq, w)
    ikt = jnp.swapaxes(ik.reshape(bsz, seq, IDX_DIM), 1, 2)
    zeros = jnp.zeros_like(ikt)
    ikz = jnp.stack([jnp.concatenate([ikt, zeros], axis=1), jnp.concatenate([zeros, ikt], axis=1)], axis=1)
    oa = _dsa_call(r3(iq, IDX_HEADS * IDX_DIM), iw.reshape(bsz, seq, IDX_HEADS), ikz,
                   r3(qa, A_WIDTH), r3(ka, A_WIDTH),
                   r3(va, A_WIDTH), r3(ga, A_WIDTH), rb_a, topk)
    ob = _swa_call(sinks_b, qb, kb, vb, r3(gb, B_WIDTH), bias_b)
    oc = _mem_call(r3(qc, C_WIDTH), kc.reshape(bsz, mlen, C_WIDTH), vc.reshape(bsz, mlen, C_WIDTH),
                   r3(gc, C_WIDTH))

    out = _final_call(x2d, g_x, oa.reshape(n, A_WIDTH), ob.reshape(n, B_WIDTH), oc.reshape(n, C_WIDTH),
                      w_mix, row2(gate_bias), w_up_a.astype(BF16), w_up_b.astype(BF16),
                      w_up_c.astype(BF16), w_o.astype(BF16))
    return out.reshape(bsz, seq, d)


def kernel(x, mem, norm_g, w_in, kv_norm_g, w_kv_up, idx_k_ln_g, idx_k_ln_b, q_norm_a, k_norm_a, q_norm_b, k_norm_b, sinks_b, mem_norm_g, w_mem_kv, q_norm_c, k_norm_c, w_up_a, w_up_b, w_up_c, gate_bias, w_o, rel_bias):
    for l in range(norm_g.shape[0]):
        x = _layer(x, mem, norm_g[l], w_in[l], kv_norm_g[l], w_kv_up[l], idx_k_ln_g[l], idx_k_ln_b[l],
                   q_norm_a[l], k_norm_a[l], q_norm_b[l], k_norm_b[l], sinks_b[l], mem_norm_g[l],
                   w_mem_kv[l], q_norm_c[l], k_norm_c[l], w_up_a[l], w_up_b[l], w_up_c[l],
                   gate_bias[l], w_o[l], rel_bias)
    return x
```

```python
import functools
import math

import numpy as np
import jax
import jax.numpy as jnp
from jax import lax
from jax.experimental import pallas as pl
from jax.experimental.pallas import tpu as pltpu

F32 = jnp.float32
BF16 = jnp.bfloat16

EPS = 1e-6
A_HEADS, A_HEAD_DIM, A_KV_RANK = 6, 128, 256
IDX_HEADS, IDX_DIM, TOPK_MAX = 16, 64, 256
B_HEADS, B_KV_HEADS, B_HEAD_DIM, WINDOW = 12, 2, 64, 128
C_HEADS, C_HEAD_DIM = 4, 128
NUM_BUCKETS, MAX_DISTANCE = 32, 128
N_BRANCH = 3
A_WIDTH = A_HEADS * A_HEAD_DIM
B_WIDTH = B_HEADS * B_HEAD_DIM
B_KV_WIDTH = B_KV_HEADS * B_HEAD_DIM
C_WIDTH = C_HEADS * C_HEAD_DIM

LANES = 128
VMEM_LIMIT = 56 * 1024 * 1024
NEG_BIG = -1e30

ROW_TILE = 512
DSA_T = 256
SWA_T = 512
FINAL_NCHUNK = 4
COUNT_ROWS = 128
REGROUP_LANES = 256
COUNT_UNROLL = 4
FAR_TILES = 4
LOG2E = math.log2(math.e)


def _cparams(sem):
    return pltpu.CompilerParams(dimension_semantics=sem, vmem_limit_bytes=VMEM_LIMIT)


def _dot(a, b):
    return jnp.dot(a, b, preferred_element_type=F32)


def _dot_nt(a, b):
    return lax.dot_general(a, b, (((1,), (1,)), ((), ())), preferred_element_type=F32)


def _rms_rows(x, g):
    ms = jnp.mean(x * x, axis=-1, keepdims=True)
    return x * lax.rsqrt(ms + EPS) * g


def _headnorm128(seg, g):
    ms = jnp.mean(seg * seg, axis=-1, keepdims=True)
    return seg * lax.rsqrt(ms + EPS) * g


def _headnorm64_pair(seg, g2):
    sq = seg * seg
    low = lax.broadcasted_iota(jnp.int32, seg.shape, 1) < B_HEAD_DIM
    s_all = jnp.sum(sq, axis=-1, keepdims=True)
    s_lo = jnp.sum(jnp.where(low, sq, 0.0), axis=-1, keepdims=True)
    ms = jnp.where(low, s_lo, s_all - s_lo) * (1.0 / B_HEAD_DIM)
    return seg * lax.rsqrt(ms + EPS) * g2


def _silu(y):
    return y / (1.0 + jnp.exp(-y))


def _proj_a_kernel(x_ref, g_ref, w_ref, kvg_ref, wkv_ref, qg_ref, kg_ref, qa_ref, ka_ref, va_ref):
    h = _rms_rows(x_ref[...], g_ref[...]).astype(BF16)
    y = _dot_nt(h, w_ref[...])
    scale = A_HEAD_DIM ** -0.5 * LOG2E
    for hh in range(A_HEADS):
        sl = slice(hh * A_HEAD_DIM, (hh + 1) * A_HEAD_DIM)
        qa_ref[:, sl] = (_headnorm128(y[:, sl], qg_ref[...]) * scale).astype(BF16)
    ckv = _rms_rows(y[:, A_WIDTH:A_WIDTH + A_KV_RANK], kvg_ref[...]).astype(BF16)
    kv = _dot(ckv, wkv_ref[...])
    for hh in range(A_HEADS):
        sl = slice(hh * A_HEAD_DIM, (hh + 1) * A_HEAD_DIM)
        ka_ref[:, sl] = _headnorm128(kv[:, sl], kg_ref[...]).astype(BF16)
    va_ref[...] = kv[:, A_WIDTH:].astype(BF16)


def _proj_i_kernel(x_ref, g_ref, w_ref, lng_ref, lnb_ref, iq_ref, ik_ref, iw_ref):
    h = _rms_rows(x_ref[...], g_ref[...]).astype(BF16)
    y = _dot_nt(h, w_ref[...])
    k0 = IDX_HEADS * IDX_DIM
    iq_ref[...] = y[:, :k0].astype(BF16)
    ik = y[:, k0:k0 + IDX_DIM]
    mu = jnp.mean(ik, axis=-1, keepdims=True)
    d = ik - mu
    var = jnp.mean(d * d, axis=-1, keepdims=True)
    ik_ref[...] = (d * lax.rsqrt(var + EPS) * lng_ref[...] + lnb_ref[...]).astype(BF16)
    w0 = k0 + LANES
    iw_ref[...] = y[:, w0:w0 + IDX_HEADS] * (IDX_HEADS ** -0.5) * (IDX_DIM ** -0.5)


def _proj_g_kernel(x_ref, g_ref, w_ref, ga_ref, gb_ref, gc_ref):
    h = _rms_rows(x_ref[...], g_ref[...]).astype(BF16)
    y = _silu(_dot_nt(h, w_ref[...]))
    ga_ref[...] = y[:, :A_WIDTH].astype(BF16)
    gb_ref[...] = y[:, A_WIDTH:A_WIDTH + B_WIDTH].astype(BF16)
    gc_ref[...] = y[:, A_WIDTH + B_WIDTH:].astype(BF16)


def _proj_bc_kernel(x_ref, g_ref, w_ref, qbg_ref, kbg_ref, qcg_ref, qb_ref, kb_ref, vb_ref, qc_ref):
    h = _rms_rows(x_ref[...], g_ref[...]).astype(BF16)
    y = _dot_nt(h, w_ref[...])
    sb = B_HEAD_DIM ** -0.5
    for p in range(B_WIDTH // LANES):
        pair = (_headnorm64_pair(y[:, p * LANES:(p + 1) * LANES], qbg_ref[...]) * sb).astype(BF16)
        qb_ref[2 * p] = pair[:, :B_HEAD_DIM]
        qb_ref[2 * p + 1] = pair[:, B_HEAD_DIM:]
    k0 = B_WIDTH
    kpair = _headnorm64_pair(y[:, k0:k0 + B_KV_WIDTH], kbg_ref[...]).astype(BF16)
    vpair = y[:, k0 + B_KV_WIDTH:k0 + 2 * B_KV_WIDTH].astype(BF16)
    for g in range(B_KV_HEADS):
        kb_ref[g] = kpair[:, g * B_HEAD_DIM:(g + 1) * B_HEAD_DIM]
        vb_ref[g] = vpair[:, g * B_HEAD_DIM:(g + 1) * B_HEAD_DIM]
    c0 = k0 + 2 * B_KV_WIDTH
    sc = C_HEAD_DIM ** -0.5
    for hh in range(C_HEADS):
        sl = slice(hh * C_HEAD_DIM, (hh + 1) * C_HEAD_DIM)
        qc_ref[:, sl] = (_headnorm128(y[:, c0 + hh * C_HEAD_DIM:c0 + (hh + 1) * C_HEAD_DIM],
                                      qcg_ref[...]) * sc).astype(BF16)


def _proj_mem_kernel(x_ref, g_ref, w_ref, kg_ref, kc_ref, vc_ref):
    h = _rms_rows(x_ref[...], g_ref[...]).astype(BF16)
    y = _dot(h, w_ref[...])
    for hh in range(C_HEADS):
        sl = slice(hh * C_HEAD_DIM, (hh + 1) * C_HEAD_DIM)
        kc_ref[:, sl] = _headnorm128(y[:, sl], kg_ref[...]).astype(BF16)
    vc_ref[...] = y[:, C_WIDTH:].astype(BF16)


def _row_spec(tm, cols):
    return pl.BlockSpec((tm, cols), lambda i: (i, 0))


def _full_spec(shape):
    nd = len(shape)
    return pl.BlockSpec(shape, lambda i: (0,) * nd)


def _proj_call(kernel_fn, name, x2d, tm, consts, out_shapes, out_specs):
    n, d = x2d.shape
    in_specs = [_row_spec(tm, d)] + [_full_spec(c.shape) for c in consts]
    return pl.pallas_call(
        kernel_fn,
        grid=(n // tm,),
        in_specs=in_specs,
        out_specs=out_specs,
        out_shape=out_shapes,
        compiler_params=_cparams(("parallel",)),
        name=name,
    )(x2d, *consts)


def _t5_bucket_np(dist):
    n = np.maximum(dist, 0)
    max_exact = NUM_BUCKETS // 2
    nf = np.maximum(n, 1).astype(np.float32)
    large = max_exact + (np.log(nf / max_exact) / math.log(MAX_DISTANCE / max_exact)
                         * (NUM_BUCKETS - max_exact)).astype(np.int32)
    large = np.minimum(large, NUM_BUCKETS - 1)
    return np.where(n < max_exact, n, large).astype(np.int32)


def _bias_tile_kernel(tab_ref, bucket_ref, o_ref, *, head0, shift_bucket, scale):
    h = pl.program_id(0) + head0
    bucket = bucket_ref[...]
    acc = jnp.zeros(bucket.shape, F32)
    for b in range(NUM_BUCKETS):
        acc = jnp.where(bucket == b, tab_ref[b, h], acc)
    if shift_bucket is not None:
        acc = acc - tab_ref[shift_bucket, h]
    o_ref[...] = jnp.where(bucket < 0, NEG_BIG, acc * scale)


def _bias_tiles(rel_bias, bucket_np, head0, nheads, shift_bucket, scale):
    r, c = bucket_np.shape
    return pl.pallas_call(
        functools.partial(_bias_tile_kernel, head0=head0, shift_bucket=shift_bucket, scale=scale),
        grid=(nheads,),
        in_specs=[pl.BlockSpec(memory_space=pltpu.SMEM), pl.BlockSpec((r, c), lambda h: (0, 0))],
        out_specs=pl.BlockSpec((None, r, c), lambda h: (h, 0, 0)),
        out_shape=jax.ShapeDtypeStruct((nheads, r, c), F32),
        compiler_params=_cparams(("arbitrary",)),
        name="bias_tiles",
    )(rel_bias, jnp.asarray(bucket_np))


def _dsa_kernel(iq_ref, iw_ref, ikz_ref, q_ref, k_ref, v_ref, g_ref, rb_ref, o_ref,
                sc_ref, wb_ref, lo_ref, hi_ref, mid_ref, clo_ref, s_ref, m_ref, l_ref, acc_ref, *, topk):
    T = DSA_T
    qi = pl.program_id(1)
    kf = float(topk)

    def lanes(col):
        return jnp.broadcast_to(col, (T, LANES))

    def halves(tile):
        return [tile[:, c * LANES:(c + 1) * LANES] for c in range(T // LANES)]

    for hh in range(IDX_HEADS):
        wb_ref[hh] = lanes(iw_ref[:, hh:hh + 1])

    def score_chunk(j, diag):
        keys = pl.ds(pl.multiple_of(j * T, T), T)
        accs = [jnp.zeros((T, LANES), F32) for _ in range(T // LANES)]
        for hh in range(IDX_HEADS):
            pair = iq_ref[:, (hh // 2) * LANES:(hh // 2 + 1) * LANES]
            s = _dot(pair, ikz_ref[hh % 2, :, keys])
            w = wb_ref[hh]
            accs = [a + w * jnp.maximum(sh, 0.0) for a, sh in zip(accs, halves(s))]
        if diag:
            row = lax.broadcasted_iota(jnp.int32, (T, LANES), 0)
            col = lax.broadcasted_iota(jnp.int32, (T, LANES), 1)
            causal = [col + c * LANES <= row for c in range(T // LANES)]
            lows = [jnp.where(cm, a, jnp.inf) for cm, a in zip(causal, accs)]
            accs = [jnp.where(cm, a, -jnp.inf) for cm, a in zip(causal, accs)]
        else:
            lows = accs
        for c, a in enumerate(accs):
            sc_ref[j, :, c * LANES:(c + 1) * LANES] = a
        return functools.reduce(jnp.minimum, lows), accs

    def p1_body(j, carry):
        mn, mxs = carry
        lo_c, his = score_chunk(j, False)
        return jnp.minimum(mn, lo_c), [jnp.maximum(a, b) for a, b in zip(mxs, his)]

    mn0 = jnp.full((T, LANES), jnp.inf, F32)
    mx0 = [jnp.full((T, LANES), -jnp.inf, F32) for _ in range(T // LANES)]

    def p1_pair(jj, carry):
        return p1_body(2 * jj + 1, p1_body(2 * jj, carry))

    n_pairs = lax.div(qi, 2)
    mn, mxs = lax.fori_loop(0, n_pairs, p1_pair, (mn0, mx0))
    mn, mxs = lax.fori_loop(2 * n_pairs, qi, p1_body, (mn, mxs))
    lo_c, his = score_chunk(qi, True)
    mxs = [jnp.maximum(a, b) for a, b in zip(mxs, his)]
    mn = lanes(jnp.min(jnp.minimum(mn, lo_c), axis=-1, keepdims=True))
    mx = lanes(jnp.max(functools.reduce(jnp.maximum, mxs), axis=-1, keepdims=True))
    class_floor = lanes(jnp.min(functools.reduce(jnp.minimum, mxs), axis=-1, keepdims=True))

    def count_lanes(rows, v, strict=False):
        def tile(j, c):
            for c0 in range(0, T, LANES):
                sh = sc_ref[j, rows, c0:c0 + LANES]
                hit = (sh > v) if strict else (sh >= v)
                c = c + jnp.where(hit, 1.0, 0.0)
            return c

        def group(g, c):
            for u in range(COUNT_UNROLL):
                c = tile(g * COUNT_UNROLL + u, c)
            return c

        n_groups = lax.div(qi + 1, COUNT_UNROLL)
        c = lax.fori_loop(0, n_groups, group, jnp.zeros((COUNT_ROWS, LANES), F32))
        return lax.fori_loop(n_groups * COUNT_UNROLL, qi + 1, tile, c)

    def row_total(c):
        return jnp.broadcast_to(jnp.sum(c, axis=-1, keepdims=True), c.shape)

    slabs = [slice(r0, r0 + COUNT_ROWS) for r0 in range(0, T, COUNT_ROWS)]
    n_causal = (qi * T + 1 + lax.broadcasted_iota(jnp.int32, (T, LANES), 0)).astype(F32)
    keep_all = n_causal <= kf
    lo0 = jnp.where(keep_all, mn, jnp.maximum(mn, class_floor))
    above_max = mx + jnp.maximum(jnp.abs(mx) * (2.0 ** -20), 1e-30)
    hi0 = jnp.where(keep_all, mn, above_max)
    lo_ref[...] = lo0
    hi_ref[...] = hi0
    mid_ref[...] = lo0 + 0.5 * (hi0 - lo0)
    clo_ref[...] = n_causal

    def bis_body(_):
        partial = [count_lanes(rows, mid_ref[rows]) for rows in slabs]
        widths = [hi_ref[rows] - lo_ref[rows] for rows in slabs]
        open_width = jnp.max(functools.reduce(jnp.maximum, widths))
        for rows, c in zip(slabs, partial):
            lo, hi, mid = lo_ref[rows], hi_ref[rows], mid_ref[rows]
            cnt = row_total(c)
            up = cnt >= kf
            lo = jnp.where(up, mid, lo)
            c_lo = jnp.where(up, cnt, clo_ref[rows])
            hi = jnp.where(up, hi, mid)
            nxt = lo + 0.5 * (hi - lo)
            stop = jnp.logical_or(c_lo == kf, jnp.logical_or(nxt <= lo, nxt >= hi))
            hi = jnp.where(stop, lo, hi)
            lo_ref[rows] = lo
            hi_ref[rows] = hi
            mid_ref[rows] = jnp.where(stop, lo, nxt)
            clo_ref[rows] = c_lo
        return open_width

    lax.while_loop(lambda w: w > 0.0, bis_body, jnp.max(hi0 - lo0))
    thr = lo_ref[...]

    tied_f = jnp.where(clo_ref[...] > kf, 1.0, 0.0)

    @pl.when(jnp.max(tied_f) > 0.0)
    def _():
        above = jnp.concatenate(
            [row_total(count_lanes(rows, thr[rows], strict=True)) for rows in slabs], axis=0)
        need = (kf - above)[:, :1]
        r = lax.broadcasted_iota(jnp.int32, (T, T), 0)
        c = lax.broadcasted_iota(jnp.int32, (T, T), 1)
        before = jnp.where(r < c, 1.0, 0.0).astype(BF16)
        thr_col = thr[:, :1]
        tied_col = tied_f[:, :1] > 0.0

        def body(j, seen):
            s = sc_ref[j]
            eq = jnp.where(s == thr_col, 1.0, 0.0)
            rank = seen + _dot(eq.astype(BF16), before)
            drop = jnp.logical_and(tied_col, jnp.logical_and(eq > 0.0, rank >= need))
            sc_ref[j] = jnp.where(drop, -jnp.inf, s)
            return seen + jnp.sum(eq, axis=-1, keepdims=True)

        lax.fori_loop(0, qi + 1, body, jnp.zeros((T, 1), F32))

    m_ref[...] = jnp.full(m_ref.shape, NEG_BIG, F32)
    l_ref[...] = jnp.zeros(l_ref.shape, F32)
    acc_ref[...] = jnp.zeros(acc_ref.shape, F32)

    def attend(j0, ntiles, band):
        sel = [jnp.where(sh >= thr, 0.0, NEG_BIG) for t in range(ntiles) for sh in halves(sc_ref[j0 + t])]
        start = pl.multiple_of(j0 * T, T)
        keys = pl.ds(start, ntiles * T)
        width = ntiles * T

        def logits_to_scratch(hh):
            sl = slice(hh * A_HEAD_DIM, (hh + 1) * A_HEAD_DIM)
            s_ref[hh % 2, :, :width] = _dot_nt(q_ref[:, sl], k_ref[keys, sl])

        logits_to_scratch(0)
        pending = None
        for hh in range(A_HEADS):
            sl = slice(hh * A_HEAD_DIM, (hh + 1) * A_HEAD_DIM)
            if hh + 1 < A_HEADS:
                logits_to_scratch(hh + 1)
            s = s_ref[hh % 2, :, :width]
            if band is not None:
                s = s + rb_ref[hh, :, band * T:(band + 1) * T]
            sh = [s[:, c * LANES:(c + 1) * LANES] + m for c, m in enumerate(sel)]
            m_prev = m_ref[hh]
            m_cur = jnp.max(functools.reduce(jnp.maximum, sh), axis=-1, keepdims=True)
            m_new = jnp.maximum(m_prev, lanes(m_cur))
            alpha = jnp.exp2(m_prev - m_new)
            ph = [jnp.exp2(x - m_new) for x in sh]
            l_ref[hh] = alpha * l_ref[hh] + functools.reduce(jnp.add, ph)
            p = jnp.concatenate([x.astype(BF16) for x in ph], axis=-1)
            acc_ref[hh] = alpha * acc_ref[hh]
            m_ref[hh] = m_new
            if pending is not None:
                acc_ref[pending[0]] += pending[1]
            pending = (hh, _dot(p, v_ref[keys, sl]))
        acc_ref[pending[0]] += pending[1]

    n_far = jnp.maximum(qi - 1, 0)
    done_tiles = 0
    width = FAR_TILES
    while width >= 1:
        trips = lax.div(n_far - done_tiles, width)

        def far_body(i, c, width=width, base=done_tiles):
            attend(base + i * width, width, None)
            return c

        lax.fori_loop(0, trips, far_body, 0)
        done_tiles = done_tiles + trips * width
        width //= 2

    @pl.when(qi > 0)
    def _():
        attend(qi - 1, 1, 0)

    attend(qi, 1, 1)

    for hh in range(A_HEADS):
        sl = slice(hh * A_HEAD_DIM, (hh + 1) * A_HEAD_DIM)
        den = jnp.sum(l_ref[hh], axis=-1, keepdims=True)
        o_ref[:, sl] = (acc_ref[hh] / den * g_ref[:, sl].astype(F32)).astype(BF16)


def _dsa_call(iq, iw, ikz, qa, ka, va, ga, rb, topk):
    bsz, seq, _ = qa.shape
    T = DSA_T
    nq = seq // T
    assert topk <= T and seq % T == 0
    once = pl.Buffered(1)
    return pl.pallas_call(
        functools.partial(_dsa_kernel, topk=topk),
        grid=(bsz, nq),
        in_specs=[
            pl.BlockSpec((None, T, IDX_HEADS * IDX_DIM), lambda b, i: (b, i, 0)),
            pl.BlockSpec((None, T, IDX_HEADS), lambda b, i: (b, i, 0)),
            pl.BlockSpec((None, 2, LANES, seq), lambda b, i: (b, 0, 0, 0), pipeline_mode=once),
            pl.BlockSpec((None, T, A_WIDTH), lambda b, i: (b, i, 0)),
            pl.BlockSpec((None, seq, A_WIDTH), lambda b, i: (b, 0, 0), pipeline_mode=once),
            pl.BlockSpec((None, seq, A_WIDTH), lambda b, i: (b, 0, 0), pipeline_mode=once),
            pl.BlockSpec((None, T, A_WIDTH), lambda b, i: (b, i, 0)),
            pl.BlockSpec((A_HEADS, T, 2 * T), lambda b, i: (0, 0, 0), pipeline_mode=once),
        ],
        out_specs=pl.BlockSpec((None, T, A_WIDTH), lambda b, i: (b, i, 0)),
        out_shape=jax.ShapeDtypeStruct((bsz, seq, A_WIDTH), BF16),
        scratch_shapes=[
            pltpu.VMEM((nq, T, T), F32),
            pltpu.VMEM((IDX_HEADS, T, LANES), F32),
            pltpu.VMEM((T, LANES), F32),
            pltpu.VMEM((T, LANES), F32),
            pltpu.VMEM((T, LANES), F32),
            pltpu.VMEM((T, LANES), F32),
            pltpu.VMEM((2, T, FAR_TILES * T), F32),
            pltpu.VMEM((A_HEADS, T, LANES), F32),
            pltpu.VMEM((A_HEADS, T, LANES), F32),
            pltpu.VMEM((A_HEADS, T, A_HEAD_DIM), F32),
        ],
        compiler_params=_cparams(("arbitrary", "arbitrary")),
        name="dsa",
    )(iq, iw, ikz, qa, ka, va, ga, rb)


def _swa_kernel(sink_ref, q_ref, kc_ref, vc_ref, kp_ref, vp_ref, g_ref, bias_ref, o_ref):
    first = pl.program_id(1) == 0
    blk = WINDOW
    group = B_HEADS // B_KV_HEADS
    grows = group * blk
    for sb in range(SWA_T // blk):
        rows = slice(sb * blk, (sb + 1) * blk)
        prev = slice((sb - 1) * blk, sb * blk)
        pieces = []
        for g in range(B_KV_HEADS):
            heads = slice(g * group, (g + 1) * group)
            k_prev = kp_ref[g] if sb == 0 else kc_ref[g, prev, :]
            v_prev = vp_ref[g] if sb == 0 else vc_ref[g, prev, :]
            q = q_ref[heads, rows, :].reshape(grows, B_HEAD_DIM)
            lp = _dot_nt(q, k_prev) + bias_ref[heads, :, :blk].reshape(grows, blk)
            lc = _dot_nt(q, kc_ref[g, rows, :]) + bias_ref[heads, :, blk:].reshape(grows, blk)
            if sb == 0:
                lp = jnp.where(first, NEG_BIG, lp)
            sink = jnp.concatenate(
                [jnp.full((blk, 1), sink_ref[g * group + h], F32) for h in range(group)], axis=0)
            m = jnp.maximum(jnp.max(lp, axis=-1, keepdims=True), jnp.max(lc, axis=-1, keepdims=True))
            m = jnp.maximum(m, sink)
            pp = jnp.exp(lp - m)
            pc = jnp.exp(lc - m)
            den = (jnp.sum(pp, axis=-1, keepdims=True) + jnp.sum(pc, axis=-1, keepdims=True)
                   + jnp.exp(sink - m))
            o = (_dot(pp.astype(BF16), v_prev) + _dot(pc.astype(BF16), vc_ref[g, rows, :])) / den
            pieces += [o[h * blk:(h + 1) * blk, :] for h in range(group)]
        o_ref[rows, :] = (jnp.concatenate(pieces, axis=1) * g_ref[rows, :].astype(F32)).astype(BF16)


def _swa_call(sinks, qb, kb, vb, gb, bias):
    bsz, _, seq, _ = qb.shape
    T = SWA_T
    per = T // WINDOW
    cur = lambda b, i: (b, 0, i, 0)
    prev = lambda b, i: (b, 0, jnp.maximum(i * per - 1, 0), 0)
    return pl.pallas_call(
        _swa_kernel,
        grid=(bsz, seq // T),
        in_specs=[
            pl.BlockSpec(memory_space=pltpu.SMEM),
            pl.BlockSpec((None, B_HEADS, T, B_HEAD_DIM), cur),
            pl.BlockSpec((None, B_KV_HEADS, T, B_HEAD_DIM), cur),
            pl.BlockSpec((None, B_KV_HEADS, T, B_HEAD_DIM), cur),
            pl.BlockSpec((None, B_KV_HEADS, WINDOW, B_HEAD_DIM), prev),
            pl.BlockSpec((None, B_KV_HEADS, WINDOW, B_HEAD_DIM), prev),
            pl.BlockSpec((None, T, B_WIDTH), lambda b, i: (b, i, 0)),
            pl.BlockSpec((B_HEADS, WINDOW, 2 * WINDOW), lambda b, i: (0, 0, 0)),
        ],
        out_specs=pl.BlockSpec((None, T, B_WIDTH), lambda b, i: (b, i, 0)),
        out_shape=jax.ShapeDtypeStruct((bsz, seq, B_WIDTH), BF16),
        compiler_params=_cparams(("parallel", "parallel")),
        name="swa",
    )(sinks, qb, kb, vb, kb, vb, gb, bias)


def _mem_kernel(q_ref, k_ref, v_ref, g_ref, o_ref):
    for hh in range(C_HEADS):
        sl = slice(hh * C_HEAD_DIM, (hh + 1) * C_HEAD_DIM)
        s = _dot_nt(q_ref[:, sl], k_ref[:, sl])
        m = jnp.max(s, axis=-1, keepdims=True)
        p = jnp.exp(s - m)
        den = jnp.sum(p, axis=-1, keepdims=True)
        o = _dot(p.astype(BF16), v_ref[:, sl]) / den
        o_ref[:, sl] = (o * g_ref[:, sl].astype(F32)).astype(BF16)


def _mem_call(qc, kc, vc, gc):
    bsz, seq, _ = qc.shape
    mlen = kc.shape[1]
    T = SWA_T
    cur = lambda b, i: (b, i, 0)
    whole = lambda b, i: (b, 0, 0)
    return pl.pallas_call(
        _mem_kernel,
        grid=(bsz, seq // T),
        in_specs=[
            pl.BlockSpec((None, T, C_WIDTH), cur),
            pl.BlockSpec((None, mlen, C_WIDTH), whole),
            pl.BlockSpec((None, mlen, C_WIDTH), whole),
            pl.BlockSpec((None, T, C_WIDTH), cur),
        ],
        out_specs=pl.BlockSpec((None, T, C_WIDTH), cur),
        out_shape=jax.ShapeDtypeStruct((bsz, seq, C_WIDTH), BF16),
        compiler_params=_cparams(("parallel", "parallel")),
        name="mem_attn",
    )(qc, kc, vc, gc)


def _final_kernel(x_ref, g_ref, oa_ref, ob_ref, oc_ref, wma_ref, wmb_ref, wmc_ref,
                  ba_ref, bb_ref, bc_ref, wua_ref, wub_ref, wuc_ref, wo_ref, out_ref, h_ref, acc_ref):
    n = pl.program_id(1)

    @pl.when(n == 0)
    def _():
        h_ref[...] = _rms_rows(x_ref[...], g_ref[...]).astype(BF16)
        acc_ref[...] = jnp.zeros(acc_ref.shape, F32)

    h = h_ref[...]

    def branch(o_ref, wm_ref, b_ref, wu_ref):
        gate = 1.0 / (1.0 + jnp.exp(-(_dot_nt(h, wm_ref[...]) + b_ref[...])))
        return gate * _dot(o_ref[...], wu_ref[...])

    merged = (branch(oa_ref, wma_ref, ba_ref, wua_ref) + branch(ob_ref, wmb_ref, bb_ref, wub_ref)
              + branch(oc_ref, wmc_ref, bc_ref, wuc_ref))
    acc_ref[...] += _dot(merged.astype(BF16), wo_ref[...])

    @pl.when(n == pl.num_programs(1) - 1)
    def _():
        out_ref[...] = x_ref[...] + acc_ref[...]


def _final_call(x2d, g, oa, ob, oc, wmix, gate_bias, wua, wub, wuc, wo):
    n, d = x2d.shape
    tm = ROW_TILE
    nch = FINAL_NCHUNK
    cw = d // nch
    row = lambda i, c: (i, 0)

    def col(br):
        return lambda i, c: (0, br * nch + c)

    def wrow(br):
        return lambda i, c: (br * nch + c, 0)

    return pl.pallas_call(
        _final_kernel,
        grid=(n // tm, nch),
        in_specs=[
            pl.BlockSpec((tm, d), row),
            pl.BlockSpec((1, d), lambda i, c: (0, 0)),
            pl.BlockSpec((tm, A_WIDTH), row),
            pl.BlockSpec((tm, B_WIDTH), row),
            pl.BlockSpec((tm, C_WIDTH), row),
            pl.BlockSpec((cw, d), wrow(0)), pl.BlockSpec((cw, d), wrow(1)), pl.BlockSpec((cw, d), wrow(2)),
            pl.BlockSpec((1, cw), col(0)), pl.BlockSpec((1, cw), col(1)), pl.BlockSpec((1, cw), col(2)),
            pl.BlockSpec((A_WIDTH, cw), lambda i, c: (0, c)),
            pl.BlockSpec((B_WIDTH, cw), lambda i, c: (0, c)),
            pl.BlockSpec((C_WIDTH, cw), lambda i, c: (0, c)),
            pl.BlockSpec((cw, d), lambda i, c: (c, 0)),
        ],
        out_specs=pl.BlockSpec((tm, d), row),
        out_shape=jax.ShapeDtypeStruct((n, d), F32),
        scratch_shapes=[pltpu.VMEM((tm, d), BF16), pltpu.VMEM((tm, d), F32)],
        compiler_params=_cparams(("parallel", "arbitrary")),
        name="merge_out",
    )(x2d, g, oa, ob, oc, wmix, wmix, wmix, gate_bias, gate_bias, gate_bias, wua, wub, wuc, wo)


def _w_in_groups(d):
    sizes = (A_WIDTH, A_KV_RANK, IDX_HEADS * IDX_DIM, IDX_DIM, IDX_HEADS, A_WIDTH,
             B_WIDTH, B_KV_WIDTH, B_KV_WIDTH, B_WIDTH, C_WIDTH, C_WIDTH, N_BRANCH * d)
    cuts = np.cumsum((0,) + sizes).tolist()
    (aq, ackv, iq, ik, iw, ag, bq, bk, bv, bg, cq, cg, mix) = [
        (cuts[i], cuts[i + 1]) for i in range(len(sizes))]
    return [
        [(aq[0], ackv[1])],
        [(iq[0], ik[1]), LANES - IDX_DIM, iw, LANES - IDX_HEADS],
        [ag, bg, cg],
        [(bq[0], bv[1]), cq],
        [mix],
    ]


def _group_width(group):
    return sum(p if isinstance(p, int) else p[1] - p[0] for p in group)


def _regroup_kernel(w_ref, *out_refs, groups):
    for o_ref, group in zip(out_refs, groups):
        row = 0
        for part in group:
            if isinstance(part, int):
                o_ref[row:row + part, :] = jnp.zeros((part, o_ref.shape[1]), BF16)
                row += part
            else:
                o_ref[row:row + part[1] - part[0], :] = w_ref[part[0]:part[1], :].astype(BF16)
                row += part[1] - part[0]


def _regroup_w_in(w_in_t):
    cols, d = w_in_t.shape
    groups = _w_in_groups(d)
    lanes = REGROUP_LANES
    return pl.pallas_call(
        functools.partial(_regroup_kernel, groups=groups),
        grid=(d // lanes,),
        in_specs=[pl.BlockSpec((cols, lanes), lambda i: (0, i))],
        out_specs=[pl.BlockSpec((_group_width(g), lanes), lambda i: (0, i)) for g in groups],
        out_shape=[jax.ShapeDtypeStruct((_group_width(g), d), BF16) for g in groups],
        compiler_params=_cparams(("parallel",)),
        name="regroup_w_in",
    )(w_in_t)


def _layer(x, mem, norm_g, w_in, kv_norm_g, w_kv_up, idx_k_ln_g, idx_k_ln_b, q_norm_a, k_norm_a,
           q_norm_b, k_norm_b, sinks_b, mem_norm_g, w_mem_kv, q_norm_c, k_norm_c,
           w_up_a, w_up_b, w_up_c, gate_bias, w_o, rel_bias):
    bsz, seq, d = x.shape
    mlen = mem.shape[1]
    n = bsz * seq
    tm = ROW_TILE
    topk = min(TOPK_MAX, seq // 4)
    x2d = x.reshape(n, d)
    row2 = lambda v: v.reshape(1, -1)

    w_grp_a, w_grp_i, w_grp_g, w_grp_bc, w_mix = _regroup_w_in(jnp.swapaxes(w_in, 0, 1))
    g_x = row2(norm_g)

    qa, ka, va = _proj_call(
        _proj_a_kernel, "proj_a", x2d, tm,
        [g_x, w_grp_a, row2(kv_norm_g), w_kv_up.astype(BF16), row2(q_norm_a), row2(k_norm_a)],
        [jax.ShapeDtypeStruct((n, A_WIDTH), BF16)] * 3,
        [_row_spec(tm, A_WIDTH)] * 3)

    per_b = seq // tm
    iq, ik, iw = _proj_call(
        _proj_i_kernel, "proj_i", x2d, tm,
        [g_x, w_grp_i, row2(idx_k_ln_g), row2(idx_k_ln_b)],
        [jax.ShapeDtypeStruct((n, IDX_HEADS * IDX_DIM), BF16),
         jax.ShapeDtypeStruct((n, IDX_DIM), BF16),
         jax.ShapeDtypeStruct((n, IDX_HEADS), F32)],
        [_row_spec(tm, IDX_HEADS * IDX_DIM), _row_spec(tm, IDX_DIM), _row_spec(tm, IDX_HEADS)])

    ga, gb, gc = _proj_call(
        _proj_g_kernel, "proj_g", x2d, tm, [g_x, w_grp_g],
        [jax.ShapeDtypeStruct((n, A_WIDTH), BF16), jax.ShapeDtypeStruct((n, B_WIDTH), BF16),
         jax.ShapeDtypeStruct((n, C_WIDTH), BF16)],
        [_row_spec(tm, A_WIDTH), _row_spec(tm, B_WIDTH), _row_spec(tm, C_WIDTH)])

    qb, kb, vb, qc = _proj_call(
        _proj_bc_kernel, "proj_bc", x2d, tm,
        [g_x, w_grp_bc, row2(jnp.tile(q_norm_b, 2)), row2(jnp.tile(k_norm_b, 2)), row2(q_norm_c)],
        [jax.ShapeDtypeStruct((bsz, B_HEADS, seq, B_HEAD_DIM), BF16),
         jax.ShapeDtypeStruct((bsz, B_KV_HEADS, seq, B_HEAD_DIM), BF16),
         jax.ShapeDtypeStruct((bsz, B_KV_HEADS, seq, B_HEAD_DIM), BF16),
         jax.ShapeDtypeStruct((n, C_WIDTH), BF16)],
        [pl.BlockSpec((None, B_HEADS, tm, B_HEAD_DIM), lambda i: (i // per_b, 0, i % per_b, 0)),
         pl.BlockSpec((None, B_KV_HEADS, tm, B_HEAD_DIM), lambda i: (i // per_b, 0, i % per_b, 0)),
         pl.BlockSpec((None, B_KV_HEADS, tm, B_HEAD_DIM), lambda i: (i // per_b, 0, i % per_b, 0)),
         _row_spec(tm, C_WIDTH)])

    mrows = bsz * mlen
    kc, vc = _proj_call(
        _proj_mem_kernel, "proj_mem", mem.reshape(mrows, d), min(tm, mrows),
        [row2(mem_norm_g), w_mem_kv.astype(BF16), row2(k_norm_c)],
        [jax.ShapeDtypeStruct((mrows, C_WIDTH), BF16)] * 2,
        [_row_spec(min(tm, mrows), C_WIDTH)] * 2)

    T = DSA_T
    i_idx = np.arange(T)[:, None]
    dist_a = T + i_idx - np.arange(2 * T)[None, :]
    rb_a = _bias_tiles(rel_bias, _t5_bucket_np(dist_a), 0, A_HEADS, NUM_BUCKETS - 1, LOG2E)
    i_idx = np.arange(WINDOW)[:, None]
    dist_b = WINDOW + i_idx - np.arange(2 * WINDOW)[None, :]
    bucket_b = np.where((dist_b >= 0) & (dist_b < WINDOW), _t5_bucket_np(dist_b), -1).astype(np.int32)
    bias_b = _bias_tiles(rel_bias, bucket_b, A_HEADS, B_HEADS, None, 1.0)

    r3 = lambda v, w: v.reshape(bsz, seq, w)
    ikt = jnp.swapaxes(ik.reshape(bsz, seq, IDX_DIM), 1, 2)
    zeros = jnp.zeros_like(ikt)
    ikz = jnp.stack([jnp.concatenate([ikt, zeros], axis=1), jnp.concatenate([zeros, ikt], axis=1)], axis=1)
    oa = _dsa_call(r3(iq, IDX_HEADS * IDX_DIM), iw.reshape(bsz, seq, IDX_HEADS), ikz,
                   r3(qa, A_WIDTH), r3(ka, A_WIDTH),
                   r3(va, A_WIDTH), r3(ga, A_WIDTH), rb_a, topk)
    ob = _swa_call(sinks_b, qb, kb, vb, r3(gb, B_WIDTH), bias_b)
    oc = _mem_call(r3(qc, C_WIDTH), kc.reshape(bsz, mlen, C_WIDTH), vc.reshape(bsz, mlen, C_WIDTH),
                   r3(gc, C_WIDTH))

    out = _final_call(x2d, g_x, oa.reshape(n, A_WIDTH), ob.reshape(n, B_WIDTH), oc.reshape(n, C_WIDTH),
                      w_mix, row2(gate_bias), w_up_a.astype(BF16), w_up_b.astype(BF16),
                      w_up_c.astype(BF16), w_o.astype(BF16))
    return out.reshape(bsz, seq, d)


def kernel(x, mem, norm_g, w_in, kv_norm_g, w_kv_up, idx_k_ln_g, idx_k_ln_b, q_norm_a, k_norm_a, q_norm_b, k_norm_b, sinks_b, mem_norm_g, w_mem_kv, q_norm_c, k_norm_c, w_up_a, w_up_b, w_up_c, gate_bias, w_o, rel_bias):
    for l in range(norm_g.shape[0]):
        x = _layer(x, mem, norm_g[l], w_in[l], kv_norm_g[l], w_kv_up[l], idx_k_ln_g[l], idx_k_ln_b[l],
                   q_norm_a[l], k_norm_a[l], q_norm_b[l], k_norm_b[l], sinks_b[l], mem_norm_g[l],
                   w_mem_kv[l], q_norm_c[l], k_norm_c[l], w_up_a[l], w_up_b[l], w_up_c[l],
                   gate_bias[l], w_o[l], rel_bias)
    return x
```

```python
import functools
import math

import numpy as np
import jax
import jax.numpy as jnp
from jax import lax
from jax.experimental import pallas as pl
from jax.experimental.pallas import tpu as pltpu

F32 = jnp.float32
BF16 = jnp.bfloat16

EPS = 1e-6
A_HEADS, A_HEAD_DIM, A_KV_RANK = 6, 128, 256
IDX_HEADS, IDX_DIM, TOPK_MAX = 16, 64, 256
IDX_PAIRS = IDX_HEADS // 2
B_HEADS, B_KV_HEADS, B_HEAD_DIM, WINDOW = 12, 2, 64, 128
C_HEADS, C_HEAD_DIM = 4, 128
NUM_BUCKETS, MAX_DISTANCE = 32, 128
N_BRANCH = 3
A_WIDTH = A_HEADS * A_HEAD_DIM
B_WIDTH = B_HEADS * B_HEAD_DIM
B_KV_WIDTH = B_KV_HEADS * B_HEAD_DIM
C_WIDTH = C_HEADS * C_HEAD_DIM

LANES = 128
VMEM_LIMIT = 56 * 1024 * 1024
NEG_BIG = -1e30

ROW_TILE = 512
DSA_T = 256
SWA_T = 512
FINAL_NCHUNK = 4
COUNT_ROWS = 128
REGROUP_LANES = 256
COUNT_UNROLL = 4
FAR_TILES = 4
LOG2E = math.log2(math.e)


def _cparams(sem):
    return pltpu.CompilerParams(dimension_semantics=sem, vmem_limit_bytes=VMEM_LIMIT)


def _dot(a, b):
    return jnp.dot(a, b, preferred_element_type=F32)


def _dot_nt(a, b):
    return lax.dot_general(a, b, (((1,), (1,)), ((), ())), preferred_element_type=F32)


def _rms_rows(x, g):
    ms = jnp.mean(x * x, axis=-1, keepdims=True)
    return x * lax.rsqrt(ms + EPS) * g


def _headnorm128(seg, g):
    ms = jnp.mean(seg * seg, axis=-1, keepdims=True)
    return seg * lax.rsqrt(ms + EPS) * g


def _headnorm64_pair(seg, g2):
    sq = seg * seg
    low = lax.broadcasted_iota(jnp.int32, seg.shape, 1) < B_HEAD_DIM
    s_all = jnp.sum(sq, axis=-1, keepdims=True)
    s_lo = jnp.sum(jnp.where(low, sq, 0.0), axis=-1, keepdims=True)
    ms = jnp.where(low, s_lo, s_all - s_lo) * (1.0 / B_HEAD_DIM)
    return seg * lax.rsqrt(ms + EPS) * g2


def _silu(y):
    return y / (1.0 + jnp.exp(-y))


def _proj_a_kernel(x_ref, g_ref, w_ref, kvg_ref, wkv_ref, qg_ref, kg_ref, qa_ref, ka_ref, va_ref):
    h = _rms_rows(x_ref[...], g_ref[...]).astype(BF16)
    y = _dot_nt(h, w_ref[...])
    scale = A_HEAD_DIM ** -0.5 * LOG2E
    for hh in range(A_HEADS):
        sl = slice(hh * A_HEAD_DIM, (hh + 1) * A_HEAD_DIM)
        qa_ref[:, sl] = (_headnorm128(y[:, sl], qg_ref[...]) * scale).astype(BF16)
    ckv = _rms_rows(y[:, A_WIDTH:A_WIDTH + A_KV_RANK], kvg_ref[...]).astype(BF16)
    kv = _dot(ckv, wkv_ref[...])
    for hh in range(A_HEADS):
        sl = slice(hh * A_HEAD_DIM, (hh + 1) * A_HEAD_DIM)
        ka_ref[:, sl] = _headnorm128(kv[:, sl], kg_ref[...]).astype(BF16)
    va_ref[...] = kv[:, A_WIDTH:].astype(BF16)


def _proj_i_kernel(x_ref, g_ref, w_ref, lng_ref, lnb_ref, iq_ref, ik_ref, iw_ref):
    h = _rms_rows(x_ref[...], g_ref[...]).astype(BF16)
    y = _dot_nt(h, w_ref[...])
    k0 = IDX_HEADS * IDX_DIM
    for p in range(IDX_PAIRS):
        iq_ref[p] = y[:, p * LANES:(p + 1) * LANES].astype(BF16)
    ik = y[:, k0:k0 + IDX_DIM]
    mu = jnp.mean(ik, axis=-1, keepdims=True)
    d = ik - mu
    var = jnp.mean(d * d, axis=-1, keepdims=True)
    ik_ref[...] = (d * lax.rsqrt(var + EPS) * lng_ref[...] + lnb_ref[...]).astype(BF16)
    w0 = k0 + LANES
    iw_ref[...] = y[:, w0:w0 + IDX_HEADS] * (IDX_HEADS ** -0.5) * (IDX_DIM ** -0.5)


def _proj_g_kernel(x_ref, g_ref, w_ref, ga_ref, gb_ref, gc_ref):
    h = _rms_rows(x_ref[...], g_ref[...]).astype(BF16)
    y = _silu(_dot_nt(h, w_ref[...]))
    ga_ref[...] = y[:, :A_WIDTH].astype(BF16)
    gb_ref[...] = y[:, A_WIDTH:A_WIDTH + B_WIDTH].astype(BF16)
    gc_ref[...] = y[:, A_WIDTH + B_WIDTH:].astype(BF16)


def _proj_bc_kernel(x_ref, g_ref, w_ref, qbg_ref, kbg_ref, qcg_ref, qb_ref, kb_ref, vb_ref, qc_ref):
    h = _rms_rows(x_ref[...], g_ref[...]).astype(BF16)
    y = _dot_nt(h, w_ref[...])
    sb = B_HEAD_DIM ** -0.5
    for p in range(B_WIDTH // LANES):
        pair = (_headnorm64_pair(y[:, p * LANES:(p + 1) * LANES], qbg_ref[...]) * sb).astype(BF16)
        qb_ref[2 * p] = pair[:, :B_HEAD_DIM]
        qb_ref[2 * p + 1] = pair[:, B_HEAD_DIM:]
    k0 = B_WIDTH
    kpair = _headnorm64_pair(y[:, k0:k0 + B_KV_WIDTH], kbg_ref[...]).astype(BF16)
    vpair = y[:, k0 + B_KV_WIDTH:k0 + 2 * B_KV_WIDTH].astype(BF16)
    for g in range(B_KV_HEADS):
        kb_ref[g] = kpair[:, g * B_HEAD_DIM:(g + 1) * B_HEAD_DIM]
        vb_ref[g] = vpair[:, g * B_HEAD_DIM:(g + 1) * B_HEAD_DIM]
    c0 = k0 + 2 * B_KV_WIDTH
    sc = C_HEAD_DIM ** -0.5
    for hh in range(C_HEADS):
        sl = slice(hh * C_HEAD_DIM, (hh + 1) * C_HEAD_DIM)
        qc_ref[:, sl] = (_headnorm128(y[:, c0 + hh * C_HEAD_DIM:c0 + (hh + 1) * C_HEAD_DIM],
                                      qcg_ref[...]) * sc).astype(BF16)


def _proj_mem_kernel(x_ref, g_ref, w_ref, kg_ref, kc_ref, vc_ref):
    h = _rms_rows(x_ref[...], g_ref[...]).astype(BF16)
    y = _dot(h, w_ref[...])
    for hh in range(C_HEADS):
        sl = slice(hh * C_HEAD_DIM, (hh + 1) * C_HEAD_DIM)
        kc_ref[:, sl] = _headnorm128(y[:, sl], kg_ref[...]).astype(BF16)
    vc_ref[...] = y[:, C_WIDTH:].astype(BF16)


def _row_spec(tm, cols):
    return pl.BlockSpec((tm, cols), lambda i: (i, 0))


def _full_spec(shape):
    nd = len(shape)
    return pl.BlockSpec(shape, lambda i: (0,) * nd)


def _proj_call(kernel_fn, name, x2d, tm, consts, out_shapes, out_specs):
    n, d = x2d.shape
    in_specs = [_row_spec(tm, d)] + [_full_spec(c.shape) for c in consts]
    return pl.pallas_call(
        kernel_fn,
        grid=(n // tm,),
        in_specs=in_specs,
        out_specs=out_specs,
        out_shape=out_shapes,
        compiler_params=_cparams(("parallel",)),
        name=name,
    )(x2d, *consts)


def _t5_bucket_np(dist):
    n = np.maximum(dist, 0)
    max_exact = NUM_BUCKETS // 2
    nf = np.maximum(n, 1).astype(np.float32)
    large = max_exact + (np.log(nf / max_exact) / math.log(MAX_DISTANCE / max_exact)
                         * (NUM_BUCKETS - max_exact)).astype(np.int32)
    large = np.minimum(large, NUM_BUCKETS - 1)
    return np.where(n < max_exact, n, large).astype(np.int32)


def _bias_tile_kernel(tab_ref, bucket_ref, o_ref, *, head0, shift_bucket, scale):
    h = pl.program_id(0) + head0
    bucket = bucket_ref[...]
    acc = jnp.zeros(bucket.shape, F32)
    for b in range(NUM_BUCKETS):
        acc = jnp.where(bucket == b, tab_ref[b, h], acc)
    if shift_bucket is not None:
        acc = acc - tab_ref[shift_bucket, h]
    o_ref[...] = jnp.where(bucket < 0, NEG_BIG, acc * scale)


def _bias_tiles(rel_bias, bucket_np, head0, nheads, shift_bucket, scale):
    r, c = bucket_np.shape
    return pl.pallas_call(
        functools.partial(_bias_tile_kernel, head0=head0, shift_bucket=shift_bucket, scale=scale),
        grid=(nheads,),
        in_specs=[pl.BlockSpec(memory_space=pltpu.SMEM), pl.BlockSpec((r, c), lambda h: (0, 0))],
        out_specs=pl.BlockSpec((None, r, c), lambda h: (h, 0, 0)),
        out_shape=jax.ShapeDtypeStruct((nheads, r, c), F32),
        compiler_params=_cparams(("arbitrary",)),
        name="bias_tiles",
    )(rel_bias, jnp.asarray(bucket_np))


def _dsa_kernel(iq_ref, iw_ref, ikz_ref, q_ref, k_ref, v_ref, g_ref, rb_ref, o_ref,
                sc_ref, wb_ref, lo_ref, hi_ref, mid_ref, clo_ref, s_ref, m_ref, l_ref, acc_ref, *, topk):
    T = DSA_T
    qi = pl.program_id(1)
    kf = float(topk)

    def lanes(col):
        return jnp.broadcast_to(col, (T, LANES))

    def halves(tile):
        return [tile[:, c * LANES:(c + 1) * LANES] for c in range(T // LANES)]

    for hh in range(IDX_HEADS):
        wb_ref[hh] = lanes(iw_ref[:, hh:hh + 1])

    def score_chunk(j, diag):
        keys = pl.ds(pl.multiple_of(j * T, T), T)
        accs = [jnp.zeros((T, LANES), F32) for _ in range(T // LANES)]
        for hh in range(IDX_HEADS):
            s = _dot(iq_ref[hh // 2], ikz_ref[hh % 2, :, keys])
            w = wb_ref[hh]
            accs = [a + w * jnp.maximum(sh, 0.0) for a, sh in zip(accs, halves(s))]
        if diag:
            row = lax.broadcasted_iota(jnp.int32, (T, LANES), 0)
            col = lax.broadcasted_iota(jnp.int32, (T, LANES), 1)
            causal = [col + c * LANES <= row for c in range(T // LANES)]
            lows = [jnp.where(cm, a, jnp.inf) for cm, a in zip(causal, accs)]
            accs = [jnp.where(cm, a, -jnp.inf) for cm, a in zip(causal, accs)]
        else:
            lows = accs
        for c, a in enumerate(accs):
            sc_ref[j, :, c * LANES:(c + 1) * LANES] = a
        return functools.reduce(jnp.minimum, lows), accs

    def p1_body(j, carry):
        mn, mxs = carry
        lo_c, his = score_chunk(j, False)
        return jnp.minimum(mn, lo_c), [jnp.maximum(a, b) for a, b in zip(mxs, his)]

    mn0 = jnp.full((T, LANES), jnp.inf, F32)
    mx0 = [jnp.full((T, LANES), -jnp.inf, F32) for _ in range(T // LANES)]

    def p1_pair(jj, carry):
        return p1_body(2 * jj + 1, p1_body(2 * jj, carry))

    n_pairs = lax.div(qi, 2)
    mn, mxs = lax.fori_loop(0, n_pairs, p1_pair, (mn0, mx0))
    mn, mxs = lax.fori_loop(2 * n_pairs, qi, p1_body, (mn, mxs))
    lo_c, his = score_chunk(qi, True)
    mxs = [jnp.maximum(a, b) for a, b in zip(mxs, his)]
    mn = lanes(jnp.min(jnp.minimum(mn, lo_c), axis=-1, keepdims=True))
    mx = lanes(jnp.max(functools.reduce(jnp.maximum, mxs), axis=-1, keepdims=True))
    class_floor = lanes(jnp.min(functools.reduce(jnp.minimum, mxs), axis=-1, keepdims=True))

    def count_lanes(rows, v, strict=False):
        def tile(j, c):
            for c0 in range(0, T, LANES):
                sh = sc_ref[j, rows, c0:c0 + LANES]
                hit = (sh > v) if strict else (sh >= v)
                c = c + jnp.where(hit, 1.0, 0.0)
            return c

        def group(g, c):
            for u in range(COUNT_UNROLL):
                c = tile(g * COUNT_UNROLL + u, c)
            return c

        n_groups = lax.div(qi + 1, COUNT_UNROLL)
        c = lax.fori_loop(0, n_groups, group, jnp.zeros((COUNT_ROWS, LANES), F32))
        return lax.fori_loop(n_groups * COUNT_UNROLL, qi + 1, tile, c)

    def row_total(c):
        return jnp.broadcast_to(jnp.sum(c, axis=-1, keepdims=True), c.shape)

    slabs = [slice(r0, r0 + COUNT_ROWS) for r0 in range(0, T, COUNT_ROWS)]
    n_causal = (qi * T + 1 + lax.broadcasted_iota(jnp.int32, (T, LANES), 0)).astype(F32)
    keep_all = n_causal <= kf
    lo0 = jnp.where(keep_all, mn, jnp.maximum(mn, class_floor))
    above_max = mx + jnp.maximum(jnp.abs(mx) * (2.0 ** -20), 1e-30)
    hi0 = jnp.where(keep_all, mn, above_max)
    lo_ref[...] = lo0
    hi_ref[...] = hi0
    mid_ref[...] = lo0 + 0.5 * (hi0 - lo0)
    clo_ref[...] = n_causal

    def bis_body(_):
        partial = [count_lanes(rows, mid_ref[rows]) for rows in slabs]
        widths = [hi_ref[rows] - lo_ref[rows] for rows in slabs]
        open_width = jnp.max(functools.reduce(jnp.maximum, widths))
        for rows, c in zip(slabs, partial):
            lo, hi, mid = lo_ref[rows], hi_ref[rows], mid_ref[rows]
            cnt = row_total(c)
            up = cnt >= kf
            lo = jnp.where(up, mid, lo)
            c_lo = jnp.where(up, cnt, clo_ref[rows])
            hi = jnp.where(up, hi, mid)
            nxt = lo + 0.5 * (hi - lo)
            stop = jnp.logical_or(c_lo == kf, jnp.logical_or(nxt <= lo, nxt >= hi))
            hi = jnp.where(stop, lo, hi)
            lo_ref[rows] = lo
            hi_ref[rows] = hi
            mid_ref[rows] = jnp.where(stop, lo, nxt)
            clo_ref[rows] = c_lo
        return open_width

    lax.while_loop(lambda w: w > 0.0, bis_body, jnp.max(hi0 - lo0))
    thr = lo_ref[...]

    tied_f = jnp.where(clo_ref[...] > kf, 1.0, 0.0)

    @pl.when(jnp.max(tied_f) > 0.0)
    def _():
        above = jnp.concatenate(
            [row_total(count_lanes(rows, thr[rows], strict=True)) for rows in slabs], axis=0)
        need = (kf - above)[:, :1]
        r = lax.broadcasted_iota(jnp.int32, (T, T), 0)
        c = lax.broadcasted_iota(jnp.int32, (T, T), 1)
        before = jnp.where(r < c, 1.0, 0.0).astype(BF16)
        thr_col = thr[:, :1]
        tied_col = tied_f[:, :1] > 0.0

        def body(j, seen):
            s = sc_ref[j]
            eq = jnp.where(s == thr_col, 1.0, 0.0)
            rank = seen + _dot(eq.astype(BF16), before)
            drop = jnp.logical_and(tied_col, jnp.logical_and(eq > 0.0, rank >= need))
            sc_ref[j] = jnp.where(drop, -jnp.inf, s)
            return seen + jnp.sum(eq, axis=-1, keepdims=True)

        lax.fori_loop(0, qi + 1, body, jnp.zeros((T, 1), F32))

    m_ref[...] = jnp.full(m_ref.shape, NEG_BIG, F32)
    l_ref[...] = jnp.zeros(l_ref.shape, F32)
    acc_ref[...] = jnp.zeros(acc_ref.shape, F32)

    def attend(j0, ntiles, band):
        sel = [jnp.where(sh >= thr, 0.0, NEG_BIG) for t in range(ntiles) for sh in halves(sc_ref[j0 + t])]
        start = pl.multiple_of(j0 * T, T)
        keys = pl.ds(start, ntiles * T)
        width = ntiles * T

        def logits_to_scratch(hh):
            sl = slice(hh * A_HEAD_DIM, (hh + 1) * A_HEAD_DIM)
            s_ref[hh % 2, :, :width] = _dot_nt(q_ref[:, sl], k_ref[keys, sl])

        logits_to_scratch(0)
        pending = None
        for hh in range(A_HEADS):
            sl = slice(hh * A_HEAD_DIM, (hh + 1) * A_HEAD_DIM)
            if hh + 1 < A_HEADS:
                logits_to_scratch(hh + 1)
            s = s_ref[hh % 2, :, :width]
            if band is not None:
                s = s + rb_ref[hh, :, band * T:(band + 1) * T]
            sh = [s[:, c * LANES:(c + 1) * LANES] + m for c, m in enumerate(sel)]
            m_prev = m_ref[hh]
            m_cur = jnp.max(functools.reduce(jnp.maximum, sh), axis=-1, keepdims=True)
            m_new = jnp.maximum(m_prev, lanes(m_cur))
            alpha = jnp.exp2(m_prev - m_new)
            ph = [jnp.exp2(x - m_new) for x in sh]
            l_ref[hh] = alpha * l_ref[hh] + functools.reduce(jnp.add, ph)
            p = jnp.concatenate([x.astype(BF16) for x in ph], axis=-1)
            acc_ref[hh] = alpha * acc_ref[hh]
            m_ref[hh] = m_new
            if pending is not None:
                acc_ref[pending[0]] += pending[1]
            pending = (hh, _dot(p, v_ref[keys, sl]))
        acc_ref[pending[0]] += pending[1]

    n_far = jnp.maximum(qi - 1, 0)
    done_tiles = 0
    width = FAR_TILES
    while width >= 1:
        trips = lax.div(n_far - done_tiles, width)

        def far_body(i, c, width=width, base=done_tiles):
            attend(base + i * width, width, None)
            return c

        lax.fori_loop(0, trips, far_body, 0)
        done_tiles = done_tiles + trips * width
        width //= 2

    @pl.when(qi > 0)
    def _():
        attend(qi - 1, 1, 0)

    attend(qi, 1, 1)

    for hh in range(A_HEADS):
        sl = slice(hh * A_HEAD_DIM, (hh + 1) * A_HEAD_DIM)
        den = jnp.sum(l_ref[hh], axis=-1, keepdims=True)
        o_ref[:, sl] = (acc_ref[hh] / den * g_ref[:, sl].astype(F32)).astype(BF16)


def _dsa_call(iq, iw, ikz, qa, ka, va, ga, rb, topk):
    bsz, seq, _ = qa.shape
    T = DSA_T
    nq = seq // T
    assert topk <= T and seq % T == 0
    once = pl.Buffered(1)
    return pl.pallas_call(
        functools.partial(_dsa_kernel, topk=topk),
        grid=(bsz, nq),
        in_specs=[
            pl.BlockSpec((None, IDX_PAIRS, T, LANES), lambda b, i: (b, 0, i, 0)),
            pl.BlockSpec((None, T, IDX_HEADS), lambda b, i: (b, i, 0)),
            pl.BlockSpec((None, 2, LANES, seq), lambda b, i: (b, 0, 0, 0), pipeline_mode=once),
            pl.BlockSpec((None, T, A_WIDTH), lambda b, i: (b, i, 0)),
            pl.BlockSpec((None, seq, A_WIDTH), lambda b, i: (b, 0, 0), pipeline_mode=once),
            pl.BlockSpec((None, seq, A_WIDTH), lambda b, i: (b, 0, 0), pipeline_mode=once),
            pl.BlockSpec((None, T, A_WIDTH), lambda b, i: (b, i, 0)),
            pl.BlockSpec((A_HEADS, T, 2 * T), lambda b, i: (0, 0, 0), pipeline_mode=once),
        ],
        out_specs=pl.BlockSpec((None, T, A_WIDTH), lambda b, i: (b, i, 0)),
        out_shape=jax.ShapeDtypeStruct((bsz, seq, A_WIDTH), BF16),
        scratch_shapes=[
            pltpu.VMEM((nq, T, T), F32),
            pltpu.VMEM((IDX_HEADS, T, LANES), F32),
            pltpu.VMEM((T, LANES), F32),
            pltpu.VMEM((T, LANES), F32),
            pltpu.VMEM((T, LANES), F32),
            pltpu.VMEM((T, LANES), F32),
            pltpu.VMEM((2, T, FAR_TILES * T), F32),
            pltpu.VMEM((A_HEADS, T, LANES), F32),
            pltpu.VMEM((A_HEADS, T, LANES), F32),
            pltpu.VMEM((A_HEADS, T, A_HEAD_DIM), F32),
        ],
        compiler_params=_cparams(("arbitrary", "arbitrary")),
        name="dsa",
    )(iq, iw, ikz, qa, ka, va, ga, rb)


def _swa_kernel(sink_ref, q_ref, kc_ref, vc_ref, kp_ref, vp_ref, g_ref, bias_ref, o_ref):
    first = pl.program_id(1) == 0
    blk = WINDOW
    group = B_HEADS // B_KV_HEADS
    grows = group * blk
    for sb in range(SWA_T // blk):
        rows = slice(sb * blk, (sb + 1) * blk)
        prev = slice((sb - 1) * blk, sb * blk)
        pieces = []
        for g in range(B_KV_HEADS):
            heads = slice(g * group, (g + 1) * group)
            k_prev = kp_ref[g] if sb == 0 else kc_ref[g, prev, :]
            v_prev = vp_ref[g] if sb == 0 else vc_ref[g, prev, :]
            q = q_ref[heads, rows, :].reshape(grows, B_HEAD_DIM)
            lp = _dot_nt(q, k_prev) + bias_ref[heads, :, :blk].reshape(grows, blk)
            lc = _dot_nt(q, kc_ref[g, rows, :]) + bias_ref[heads, :, blk:].reshape(grows, blk)
            if sb == 0:
                lp = jnp.where(first, NEG_BIG, lp)
            sink = jnp.concatenate(
                [jnp.full((blk, blk), sink_ref[g * group + h], F32) for h in range(group)], axis=0)
            m = jnp.max(jnp.maximum(lp, lc), axis=-1, keepdims=True)
            m = jnp.maximum(jnp.broadcast_to(m, (grows, blk)), sink)
            pp = jnp.exp(lp - m).astype(BF16)
            pc = jnp.exp(lc - m).astype(BF16)
            ones = jnp.ones((blk, blk), BF16)
            den = _dot(pp, ones) + _dot(pc, ones) + jnp.exp(sink - m)
            o = (_dot(pp, v_prev) + _dot(pc, vc_ref[g, rows, :])) / den[:, :B_HEAD_DIM]
            pieces += [o[h * blk:(h + 1) * blk, :] for h in range(group)]
        o_ref[rows, :] = (jnp.concatenate(pieces, axis=1) * g_ref[rows, :].astype(F32)).astype(BF16)


def _swa_call(sinks, qb, kb, vb, gb, bias):
    bsz, _, seq, _ = qb.shape
    T = SWA_T
    per = T // WINDOW
    cur = lambda b, i: (b, 0, i, 0)
    prev = lambda b, i: (b, 0, jnp.maximum(i * per - 1, 0), 0)
    return pl.pallas_call(
        _swa_kernel,
        grid=(bsz, seq // T),
        in_specs=[
            pl.BlockSpec(memory_space=pltpu.SMEM),
            pl.BlockSpec((None, B_HEADS, T, B_HEAD_DIM), cur),
            pl.BlockSpec((None, B_KV_HEADS, T, B_HEAD_DIM), cur),
            pl.BlockSpec((None, B_KV_HEADS, T, B_HEAD_DIM), cur),
            pl.BlockSpec((None, B_KV_HEADS, WINDOW, B_HEAD_DIM), prev),
            pl.BlockSpec((None, B_KV_HEADS, WINDOW, B_HEAD_DIM), prev),
            pl.BlockSpec((None, T, B_WIDTH), lambda b, i: (b, i, 0)),
            pl.BlockSpec((B_HEADS, WINDOW, 2 * WINDOW), lambda b, i: (0, 0, 0)),
        ],
        out_specs=pl.BlockSpec((None, T, B_WIDTH), lambda b, i: (b, i, 0)),
        out_shape=jax.ShapeDtypeStruct((bsz, seq, B_WIDTH), BF16),
        compiler_params=_cparams(("parallel", "parallel")),
        name="swa",
    )(sinks, qb, kb, vb, kb, vb, gb, bias)


def _mem_kernel(q_ref, k_ref, v_ref, g_ref, o_ref):
    for hh in range(C_HEADS):
        sl = slice(hh * C_HEAD_DIM, (hh + 1) * C_HEAD_DIM)
        s = _dot_nt(q_ref[:, sl], k_ref[:, sl])
        m = jnp.max(s, axis=-1, keepdims=True)
        p = jnp.exp(s - m)
        den = jnp.sum(p, axis=-1, keepdims=True)
        o = _dot(p.astype(BF16), v_ref[:, sl]) / den
        o_ref[:, sl] = (o * g_ref[:, sl].astype(F32)).astype(BF16)


def _mem_call(qc, kc, vc, gc):
    bsz, seq, _ = qc.shape
    mlen = kc.shape[1]
    T = SWA_T
    cur = lambda b, i: (b, i, 0)
    whole = lambda b, i: (b, 0, 0)
    return pl.pallas_call(
        _mem_kernel,
        grid=(bsz, seq // T),
        in_specs=[
            pl.BlockSpec((None, T, C_WIDTH), cur),
            pl.BlockSpec((None, mlen, C_WIDTH), whole),
            pl.BlockSpec((None, mlen, C_WIDTH), whole),
            pl.BlockSpec((None, T, C_WIDTH), cur),
        ],
        out_specs=pl.BlockSpec((None, T, C_WIDTH), cur),
        out_shape=jax.ShapeDtypeStruct((bsz, seq, C_WIDTH), BF16),
        compiler_params=_cparams(("parallel", "parallel")),
        name="mem_attn",
    )(qc, kc, vc, gc)


def _final_kernel(x_ref, g_ref, oa_ref, ob_ref, oc_ref, wma_ref, wmb_ref, wmc_ref,
                  ba_ref, bb_ref, bc_ref, wua_ref, wub_ref, wuc_ref, wo_ref, out_ref,
                  h_ref, acc_ref, m_ref):
    c = pl.program_id(1)
    nch = pl.num_programs(1) - 1
    prev_slot = lax.rem(c + 1, 2)

    @pl.when(c == 0)
    def _():
        h_ref[...] = _rms_rows(x_ref[...], g_ref[...]).astype(BF16)
        acc_ref[...] = jnp.zeros(acc_ref.shape, F32)
        m_ref[1] = jnp.zeros(m_ref.shape[1:], BF16)

    @pl.when(c < nch)
    def _():
        h = h_ref[...]

        def branch(o_ref, wm_ref, b_ref, wu_ref):
            gate = 1.0 / (1.0 + jnp.exp(-(_dot_nt(h, wm_ref[...]) + b_ref[...])))
            return gate * _dot(o_ref[...], wu_ref[...])

        acc_ref[...] += _dot(m_ref[prev_slot], wo_ref[...])
        merged = (branch(oa_ref, wma_ref, ba_ref, wua_ref) + branch(ob_ref, wmb_ref, bb_ref, wub_ref)
                  + branch(oc_ref, wmc_ref, bc_ref, wuc_ref))
        m_ref[lax.rem(c, 2)] = merged.astype(BF16)

    @pl.when(c == nch)
    def _():
        out_ref[...] = x_ref[...] + acc_ref[...] + _dot(m_ref[prev_slot], wo_ref[...])


def _final_call(x2d, g, oa, ob, oc, wmix, gate_bias, wua, wub, wuc, wo):
    n, d = x2d.shape
    tm = ROW_TILE
    nch = FINAL_NCHUNK
    cw = d // nch
    row = lambda i, c: (i, 0)

    cur = lambda c: jnp.minimum(c, nch - 1)
    prv = lambda c: jnp.maximum(c - 1, 0)

    def col(br):
        return lambda i, c: (0, br * nch + cur(c))

    def wrow(br):
        return lambda i, c: (br * nch + cur(c), 0)

    return pl.pallas_call(
        _final_kernel,
        grid=(n // tm, nch + 1),
        in_specs=[
            pl.BlockSpec((tm, d), row),
            pl.BlockSpec((1, d), lambda i, c: (0, 0)),
            pl.BlockSpec((tm, A_WIDTH), row),
            pl.BlockSpec((tm, B_WIDTH), row),
            pl.BlockSpec((tm, C_WIDTH), row),
            pl.BlockSpec((cw, d), wrow(0)), pl.BlockSpec((cw, d), wrow(1)), pl.BlockSpec((cw, d), wrow(2)),
            pl.BlockSpec((1, cw), col(0)), pl.BlockSpec((1, cw), col(1)), pl.BlockSpec((1, cw), col(2)),
            pl.BlockSpec((A_WIDTH, cw), lambda i, c: (0, cur(c))),
            pl.BlockSpec((B_WIDTH, cw), lambda i, c: (0, cur(c))),
            pl.BlockSpec((C_WIDTH, cw), lambda i, c: (0, cur(c))),
            pl.BlockSpec((cw, d), lambda i, c: (prv(c), 0)),
        ],
        out_specs=pl.BlockSpec((tm, d), row),
        out_shape=jax.ShapeDtypeStruct((n, d), F32),
        scratch_shapes=[pltpu.VMEM((tm, d), BF16), pltpu.VMEM((tm, d), F32), pltpu.VMEM((2, tm, cw), BF16)],
        compiler_params=_cparams(("parallel", "arbitrary")),
        name="merge_out",
    )(x2d, g, oa, ob, oc, wmix, wmix, wmix, gate_bias, gate_bias, gate_bias, wua, wub, wuc, wo)


def _w_in_groups(d):
    sizes = (A_WIDTH, A_KV_RANK, IDX_HEADS * IDX_DIM, IDX_DIM, IDX_HEADS, A_WIDTH,
             B_WIDTH, B_KV_WIDTH, B_KV_WIDTH, B_WIDTH, C_WIDTH, C_WIDTH, N_BRANCH * d)
    cuts = np.cumsum((0,) + sizes).tolist()
    (aq, ackv, iq, ik, iw, ag, bq, bk, bv, bg, cq, cg, mix) = [
        (cuts[i], cuts[i + 1]) for i in range(len(sizes))]
    return [
        [(aq[0], ackv[1])],
        [(iq[0], ik[1]), LANES - IDX_DIM, iw, LANES - IDX_HEADS],
        [ag, bg, cg],
        [(bq[0], bv[1]), cq],
        [mix],
    ]


def _group_width(group):
    return sum(p if isinstance(p, int) else p[1] - p[0] for p in group)


def _regroup_kernel(w_ref, *out_refs, groups):
    for o_ref, group in zip(out_refs, groups):
        row = 0
        for part in group:
            if isinstance(part, int):
                o_ref[row:row + part, :] = jnp.zeros((part, o_ref.shape[1]), BF16)
                row += part
            else:
                o_ref[row:row + part[1] - part[0], :] = w_ref[part[0]:part[1], :].astype(BF16)
                row += part[1] - part[0]


def _regroup_w_in(w_in_t):
    cols, d = w_in_t.shape
    groups = _w_in_groups(d)
    lanes = REGROUP_LANES
    return pl.pallas_call(
        functools.partial(_regroup_kernel, groups=groups),
        grid=(d // lanes,),
        in_specs=[pl.BlockSpec((cols, lanes), lambda i: (0, i))],
        out_specs=[pl.BlockSpec((_group_width(g), lanes), lambda i: (0, i)) for g in groups],
        out_shape=[jax.ShapeDtypeStruct((_group_width(g), d), BF16) for g in groups],
        compiler_params=_cparams(("parallel",)),
        name="regroup_w_in",
    )(w_in_t)


def _layer(x, mem, norm_g, w_in, kv_norm_g, w_kv_up, idx_k_ln_g, idx_k_ln_b, q_norm_a, k_norm_a,
           q_norm_b, k_norm_b, sinks_b, mem_norm_g, w_mem_kv, q_norm_c, k_norm_c,
           w_up_a, w_up_b, w_up_c, gate_bias, w_o, rel_bias):
    bsz, seq, d = x.shape
    mlen = mem.shape[1]
    n = bsz * seq
    tm = ROW_TILE
    topk = min(TOPK_MAX, seq // 4)
    x2d = x.reshape(n, d)
    row2 = lambda v: v.reshape(1, -1)

    w_grp_a, w_grp_i, w_grp_g, w_grp_bc, w_mix = _regroup_w_in(jnp.swapaxes(w_in, 0, 1))
    g_x = row2(norm_g)

    qa, ka, va = _proj_call(
        _proj_a_kernel, "proj_a", x2d, tm,
        [g_x, w_grp_a, row2(kv_norm_g), w_kv_up.astype(BF16), row2(q_norm_a), row2(k_norm_a)],
        [jax.ShapeDtypeStruct((n, A_WIDTH), BF16)] * 3,
        [_row_spec(tm, A_WIDTH)] * 3)

    per_b = seq // tm
    iq, ik, iw = _proj_call(
        _proj_i_kernel, "proj_i", x2d, tm,
        [g_x, w_grp_i, row2(idx_k_ln_g), row2(idx_k_ln_b)],
        [jax.ShapeDtypeStruct((bsz, IDX_PAIRS, seq, LANES), BF16),
         jax.ShapeDtypeStruct((n, IDX_DIM), BF16),
         jax.ShapeDtypeStruct((n, IDX_HEADS), F32)],
        [pl.BlockSpec((None, IDX_PAIRS, tm, LANES), lambda i: (i // per_b, 0, i % per_b, 0)),
         _row_spec(tm, IDX_DIM), _row_spec(tm, IDX_HEADS)])

    ga, gb, gc = _proj_call(
        _proj_g_kernel, "proj_g", x2d, tm, [g_x, w_grp_g],
        [jax.ShapeDtypeStruct((n, A_WIDTH), BF16), jax.ShapeDtypeStruct((n, B_WIDTH), BF16),
         jax.ShapeDtypeStruct((n, C_WIDTH), BF16)],
        [_row_spec(tm, A_WIDTH), _row_spec(tm, B_WIDTH), _row_spec(tm, C_WIDTH)])

    qb, kb, vb, qc = _proj_call(
        _proj_bc_kernel, "proj_bc", x2d, tm,
        [g_x, w_grp_bc, row2(jnp.tile(q_norm_b, 2)), row2(jnp.tile(k_norm_b, 2)), row2(q_norm_c)],
        [jax.ShapeDtypeStruct((bsz, B_HEADS, seq, B_HEAD_DIM), BF16),
         jax.ShapeDtypeStruct((bsz, B_KV_HEADS, seq, B_HEAD_DIM), BF16),
         jax.ShapeDtypeStruct((bsz, B_KV_HEADS, seq, B_HEAD_DIM), BF16),
         jax.ShapeDtypeStruct((n, C_WIDTH), BF16)],
        [pl.BlockSpec((None, B_HEADS, tm, B_HEAD_DIM), lambda i: (i // per_b, 0, i % per_b, 0)),
         pl.BlockSpec((None, B_KV_HEADS, tm, B_HEAD_DIM), lambda i: (i // per_b, 0, i % per_b, 0)),
         pl.BlockSpec((None, B_KV_HEADS, tm, B_HEAD_DIM), lambda i: (i // per_b, 0, i % per_b, 0)),
         _row_spec(tm, C_WIDTH)])

    mrows = bsz * mlen
    kc, vc = _proj_call(
        _proj_mem_kernel, "proj_mem", mem.reshape(mrows, d), min(tm, mrows),
        [row2(mem_norm_g), w_mem_kv.astype(BF16), row2(k_norm_c)],
        [jax.ShapeDtypeStruct((mrows, C_WIDTH), BF16)] * 2,
        [_row_spec(min(tm, mrows), C_WIDTH)] * 2)

    T = DSA_T
    i_idx = np.arange(T)[:, None]
    dist_a = T + i_idx - np.arange(2 * T)[None, :]
    rb_a = _bias_tiles(rel_bias, _t5_bucket_np(dist_a), 0, A_HEADS, NUM_BUCKETS - 1, LOG2E)
    i_idx = np.arange(WINDOW)[:, None]
    dist_b = WINDOW + i_idx - np.arange(2 * WINDOW)[None, :]
    bucket_b = np.where((dist_b >= 0) & (dist_b < WINDOW), _t5_bucket_np(dist_b), -1).astype(np.int32)
    bias_b = _bias_tiles(rel_bias, bucket_b, A_HEADS, B_HEADS, None, 1.0)

    r3 = lambda v, w: v.reshape(bsz, seq, w)
    ikt = jnp.swapaxes(ik.reshape(bsz, seq, IDX_DIM), 1, 2)
    zeros = jnp.zeros_like(ikt)
    ikz = jnp.stack([jnp.concatenate([ikt, zeros], axis=1), jnp.concatenate([zeros, ikt], axis=1)], axis=1)
    oa = _dsa_call(iq, iw.reshape(bsz, seq, IDX_HEADS), ikz,
                   r3(qa, A_WIDTH), r3(ka, A_WIDTH),
                   r3(va, A_WIDTH), r3(ga, A_WIDTH), rb_a, topk)
    ob = _swa_call(sinks_b, qb, kb, vb, r3(gb, B_WIDTH), bias_b)
    oc = _mem_call(r3(qc, C_WIDTH), kc.reshape(bsz, mlen, C_WIDTH), vc.reshape(bsz, mlen, C_WIDTH),
                   r3(gc, C_WIDTH))

    out = _final_call(x2d, g_x, oa.reshape(n, A_WIDTH), ob.reshape(n, B_WIDTH), oc.reshape(n, C_WIDTH),
                      w_mix, row2(gate_bias), w_up_a.astype(BF16), w_up_b.astype(BF16),
                      w_up_c.astype(BF16), w_o.astype(BF16))
    return out.reshape(bsz, seq, d)


def kernel(x, mem, norm_g, w_in, kv_norm_g, w_kv_up, idx_k_ln_g, idx_k_ln_b, q_norm_a, k_norm_a, q_norm_b, k_norm_b, sinks_b, mem_norm_g, w_mem_kv, q_norm_c, k_norm_c, w_up_a, w_up_b, w_up_c, gate_bias, w_o, rel_bias):
    for l in range(norm_g.shape[0]):
        x = _layer(x, mem, norm_g[l], w_in[l], kv_norm_g[l], w_kv_up[l], idx_k_ln_g[l], idx_k_ln_b[l],
                   q_norm_a[l], k_norm_a[l], q_norm_b[l], k_norm_b[l], sinks_b[l], mem_norm_g[l],
                   w_mem_kv[l], q_norm_c[l], k_norm_c[l], w_up_a[l], w_up_b[l], w_up_c[l],
                   gate_bias[l], w_o[l], rel_bias)
    return x
```

```python
import functools
import math

import numpy as np
import jax
import jax.numpy as jnp
from jax import lax
from jax.experimental import pallas as pl
from jax.experimental.pallas import tpu as pltpu

F32 = jnp.float32
BF16 = jnp.bfloat16

EPS = 1e-6
A_HEADS, A_HEAD_DIM, A_KV_RANK = 6, 128, 256
IDX_HEADS, IDX_DIM, TOPK_MAX = 16, 64, 256
IDX_PAIRS = IDX_HEADS // 2
B_HEADS, B_KV_HEADS, B_HEAD_DIM, WINDOW = 12, 2, 64, 128
C_HEADS, C_HEAD_DIM = 4, 128
NUM_BUCKETS, MAX_DISTANCE = 32, 128
N_BRANCH = 3
A_WIDTH = A_HEADS * A_HEAD_DIM
B_WIDTH = B_HEADS * B_HEAD_DIM
B_KV_WIDTH = B_KV_HEADS * B_HEAD_DIM
C_WIDTH = C_HEADS * C_HEAD_DIM

LANES = 128
VMEM_LIMIT = 56 * 1024 * 1024
NEG_BIG = -1e30

ROW_TILE = 512
DSA_T = 256
SWA_T = 512
FINAL_NCHUNK = 4
COUNT_ROWS = 128
REGROUP_LANES = 256
COUNT_UNROLL = 4
FAR_TILES = 4
LOG2E = math.log2(math.e)


def _cparams(sem):
    return pltpu.CompilerParams(dimension_semantics=sem, vmem_limit_bytes=VMEM_LIMIT)


def _dot(a, b):
    return jnp.dot(a, b, preferred_element_type=F32)


def _dot_nt(a, b):
    return lax.dot_general(a, b, (((1,), (1,)), ((), ())), preferred_element_type=F32)


def _rms_rows(x, g):
    ms = jnp.mean(x * x, axis=-1, keepdims=True)
    return x * lax.rsqrt(ms + EPS) * g


def _headnorm128(seg, g):
    ms = jnp.mean(seg * seg, axis=-1, keepdims=True)
    return seg * lax.rsqrt(ms + EPS) * g


def _headnorm64_pair(seg, g2):
    sq = seg * seg
    low = lax.broadcasted_iota(jnp.int32, seg.shape, 1) < B_HEAD_DIM
    s_all = jnp.sum(sq, axis=-1, keepdims=True)
    s_lo = jnp.sum(jnp.where(low, sq, 0.0), axis=-1, keepdims=True)
    ms = jnp.where(low, s_lo, s_all - s_lo) * (1.0 / B_HEAD_DIM)
    return seg * lax.rsqrt(ms + EPS) * g2


def _silu(y):
    return y / (1.0 + jnp.exp(-y))


def _proj_a_kernel(x_ref, g_ref, w_ref, kvg_ref, wkv_ref, qg_ref, kg_ref, qa_ref, ka_ref, va_ref):
    h = _rms_rows(x_ref[...], g_ref[...]).astype(BF16)
    y = _dot_nt(h, w_ref[...])
    scale = A_HEAD_DIM ** -0.5 * LOG2E
    for hh in range(A_HEADS):
        sl = slice(hh * A_HEAD_DIM, (hh + 1) * A_HEAD_DIM)
        qa_ref[:, sl] = (_headnorm128(y[:, sl], qg_ref[...]) * scale).astype(BF16)
    ckv = _rms_rows(y[:, A_WIDTH:A_WIDTH + A_KV_RANK], kvg_ref[...]).astype(BF16)
    kv = _dot(ckv, wkv_ref[...])
    for hh in range(A_HEADS):
        sl = slice(hh * A_HEAD_DIM, (hh + 1) * A_HEAD_DIM)
        ka_ref[:, sl] = _headnorm128(kv[:, sl], kg_ref[...]).astype(BF16)
    va_ref[...] = kv[:, A_WIDTH:].astype(BF16)


def _proj_i_kernel(x_ref, g_ref, w_ref, lng_ref, lnb_ref, iq_ref, ik_ref, iw_ref):
    h = _rms_rows(x_ref[...], g_ref[...]).astype(BF16)
    y = _dot_nt(h, w_ref[...])
    k0 = IDX_HEADS * IDX_DIM
    for p in range(IDX_PAIRS):
        iq_ref[p] = y[:, p * LANES:(p + 1) * LANES].astype(BF16)
    ik = y[:, k0:k0 + IDX_DIM]
    mu = jnp.mean(ik, axis=-1, keepdims=True)
    d = ik - mu
    var = jnp.mean(d * d, axis=-1, keepdims=True)
    ik_ref[...] = (d * lax.rsqrt(var + EPS) * lng_ref[...] + lnb_ref[...]).astype(BF16)
    w0 = k0 + LANES
    iw_ref[...] = y[:, w0:w0 + IDX_HEADS] * (IDX_HEADS ** -0.5) * (IDX_DIM ** -0.5)


def _proj_g_kernel(x_ref, g_ref, w_ref, ga_ref, gb_ref, gc_ref):
    h = _rms_rows(x_ref[...], g_ref[...]).astype(BF16)
    y = _silu(_dot_nt(h, w_ref[...]))
    ga_ref[...] = y[:, :A_WIDTH].astype(BF16)
    gb_ref[...] = y[:, A_WIDTH:A_WIDTH + B_WIDTH].astype(BF16)
    gc_ref[...] = y[:, A_WIDTH + B_WIDTH:].astype(BF16)


def _proj_bc_kernel(x_ref, g_ref, w_ref, qbg_ref, kbg_ref, qcg_ref, qb_ref, kb_ref, vb_ref, qc_ref):
    h = _rms_rows(x_ref[...], g_ref[...]).astype(BF16)
    y = _dot_nt(h, w_ref[...])
    sb = B_HEAD_DIM ** -0.5
    for p in range(B_WIDTH // LANES):
        pair = (_headnorm64_pair(y[:, p * LANES:(p + 1) * LANES], qbg_ref[...]) * sb).astype(BF16)
        qb_ref[2 * p] = pair[:, :B_HEAD_DIM]
        qb_ref[2 * p + 1] = pair[:, B_HEAD_DIM:]
    k0 = B_WIDTH
    kpair = _headnorm64_pair(y[:, k0:k0 + B_KV_WIDTH], kbg_ref[...]).astype(BF16)
    vpair = y[:, k0 + B_KV_WIDTH:k0 + 2 * B_KV_WIDTH].astype(BF16)
    for g in range(B_KV_HEADS):
        kb_ref[g] = kpair[:, g * B_HEAD_DIM:(g + 1) * B_HEAD_DIM]
        vb_ref[g] = vpair[:, g * B_HEAD_DIM:(g + 1) * B_HEAD_DIM]
    c0 = k0 + 2 * B_KV_WIDTH
    sc = C_HEAD_DIM ** -0.5
    for hh in range(C_HEADS):
        sl = slice(hh * C_HEAD_DIM, (hh + 1) * C_HEAD_DIM)
        qc_ref[:, sl] = (_headnorm128(y[:, c0 + hh * C_HEAD_DIM:c0 + (hh + 1) * C_HEAD_DIM],
                                      qcg_ref[...]) * sc).astype(BF16)


def _proj_mem_kernel(x_ref, g_ref, w_ref, kg_ref, kc_ref, vc_ref):
    h = _rms_rows(x_ref[...], g_ref[...]).astype(BF16)
    y = _dot(h, w_ref[...])
    for hh in range(C_HEADS):
        sl = slice(hh * C_HEAD_DIM, (hh + 1) * C_HEAD_DIM)
        kc_ref[:, sl] = _headnorm128(y[:, sl], kg_ref[...]).astype(BF16)
    vc_ref[...] = y[:, C_WIDTH:].astype(BF16)


def _row_spec(tm, cols):
    return pl.BlockSpec((tm, cols), lambda i: (i, 0))


def _full_spec(shape):
    nd = len(shape)
    return pl.BlockSpec(shape, lambda i: (0,) * nd)


def _proj_call(kernel_fn, name, x2d, tm, consts, out_shapes, out_specs):
    n, d = x2d.shape
    in_specs = [_row_spec(tm, d)] + [_full_spec(c.shape) for c in consts]
    return pl.pallas_call(
        kernel_fn,
        grid=(n // tm,),
        in_specs=in_specs,
        out_specs=out_specs,
        out_shape=out_shapes,
        compiler_params=_cparams(("parallel",)),
        name=name,
    )(x2d, *consts)


def _t5_bucket_np(dist):
    n = np.maximum(dist, 0)
    max_exact = NUM_BUCKETS // 2
    nf = np.maximum(n, 1).astype(np.float32)
    large = max_exact + (np.log(nf / max_exact) / math.log(MAX_DISTANCE / max_exact)
                         * (NUM_BUCKETS - max_exact)).astype(np.int32)
    large = np.minimum(large, NUM_BUCKETS - 1)
    return np.where(n < max_exact, n, large).astype(np.int32)


def _bias_tile_kernel(tab_ref, bucket_ref, o_ref, *, head0, shift_bucket, scale):
    h = pl.program_id(0) + head0
    bucket = bucket_ref[...]
    acc = jnp.zeros(bucket.shape, F32)
    for b in range(NUM_BUCKETS):
        acc = jnp.where(bucket == b, tab_ref[b, h], acc)
    if shift_bucket is not None:
        acc = acc - tab_ref[shift_bucket, h]
    o_ref[...] = jnp.where(bucket < 0, NEG_BIG, acc * scale)


def _bias_tiles(rel_bias, bucket_np, head0, nheads, shift_bucket, scale):
    r, c = bucket_np.shape
    return pl.pallas_call(
        functools.partial(_bias_tile_kernel, head0=head0, shift_bucket=shift_bucket, scale=scale),
        grid=(nheads,),
        in_specs=[pl.BlockSpec(memory_space=pltpu.SMEM), pl.BlockSpec((r, c), lambda h: (0, 0))],
        out_specs=pl.BlockSpec((None, r, c), lambda h: (h, 0, 0)),
        out_shape=jax.ShapeDtypeStruct((nheads, r, c), F32),
        compiler_params=_cparams(("arbitrary",)),
        name="bias_tiles",
    )(rel_bias, jnp.asarray(bucket_np))


def _dsa_kernel(iq_ref, iw_ref, ikz_ref, q_ref, k_ref, v_ref, g_ref, rb_ref, o_ref,
                sc_ref, wb_ref, lo_ref, hi_ref, mid_ref, clo_ref, s_ref, m_ref, l_ref, acc_ref, *, topk):
    T = DSA_T
    qi = pl.program_id(1)
    kf = float(topk)

    def lanes(col):
        return jnp.broadcast_to(col, (T, LANES))

    def halves(tile):
        return [tile[:, c * LANES:(c + 1) * LANES] for c in range(T // LANES)]

    for hh in range(IDX_HEADS):
        wb_ref[hh] = lanes(iw_ref[:, hh:hh + 1])

    def score_chunk(j, diag):
        keys = pl.ds(pl.multiple_of(j * T, T), T)
        accs = [jnp.zeros((T, LANES), F32) for _ in range(T // LANES)]
        for hh in range(IDX_HEADS):
            s = _dot(iq_ref[hh // 2], ikz_ref[hh % 2, :, keys])
            w = wb_ref[hh]
            accs = [a + w * jnp.maximum(sh, 0.0) for a, sh in zip(accs, halves(s))]
        if diag:
            row = lax.broadcasted_iota(jnp.int32, (T, LANES), 0)
            col = lax.broadcasted_iota(jnp.int32, (T, LANES), 1)
            causal = [col + c * LANES <= row for c in range(T // LANES)]
            lows = [jnp.where(cm, a, jnp.inf) for cm, a in zip(causal, accs)]
            accs = [jnp.where(cm, a, -jnp.inf) for cm, a in zip(causal, accs)]
        else:
            lows = accs
        for c, a in enumerate(accs):
            sc_ref[j, :, c * LANES:(c + 1) * LANES] = a
        return functools.reduce(jnp.minimum, lows), accs

    def p1_body(j, carry):
        mn, mxs = carry
        lo_c, his = score_chunk(j, False)
        return jnp.minimum(mn, lo_c), [jnp.maximum(a, b) for a, b in zip(mxs, his)]

    mn0 = jnp.full((T, LANES), jnp.inf, F32)
    mx0 = [jnp.full((T, LANES), -jnp.inf, F32) for _ in range(T // LANES)]

    def p1_pair(jj, carry):
        return p1_body(2 * jj + 1, p1_body(2 * jj, carry))

    n_pairs = lax.div(qi, 2)
    mn, mxs = lax.fori_loop(0, n_pairs, p1_pair, (mn0, mx0))
    mn, mxs = lax.fori_loop(2 * n_pairs, qi, p1_body, (mn, mxs))
    lo_c, his = score_chunk(qi, True)
    mxs = [jnp.maximum(a, b) for a, b in zip(mxs, his)]
    mn = lanes(jnp.min(jnp.minimum(mn, lo_c), axis=-1, keepdims=True))
    mx = lanes(jnp.max(functools.reduce(jnp.maximum, mxs), axis=-1, keepdims=True))
    class_floor = lanes(jnp.min(functools.reduce(jnp.minimum, mxs), axis=-1, keepdims=True))

    def count_lanes(rows, v, strict=False):
        def tile(j, c):
            for c0 in range(0, T, LANES):
                sh = sc_ref[j, rows, c0:c0 + LANES]
                hit = (sh > v) if strict else (sh >= v)
                c = c + jnp.where(hit, 1.0, 0.0)
            return c

        def group(g, c):
            for u in range(COUNT_UNROLL):
                c = tile(g * COUNT_UNROLL + u, c)
            return c

        n_groups = lax.div(qi + 1, COUNT_UNROLL)
        c = lax.fori_loop(0, n_groups, group, jnp.zeros((COUNT_ROWS, LANES), F32))
        return lax.fori_loop(n_groups * COUNT_UNROLL, qi + 1, tile, c)

    def row_total(c):
        return jnp.broadcast_to(jnp.sum(c, axis=-1, keepdims=True), c.shape)

    slabs = [slice(r0, r0 + COUNT_ROWS) for r0 in range(0, T, COUNT_ROWS)]
    n_causal = (qi * T + 1 + lax.broadcasted_iota(jnp.int32, (T, LANES), 0)).astype(F32)
    keep_all = n_causal <= kf
    lo0 = jnp.where(keep_all, mn, jnp.maximum(mn, class_floor))
    above_max = mx + jnp.maximum(jnp.abs(mx) * (2.0 ** -20), 1e-30)
    hi0 = jnp.where(keep_all, mn, above_max)
    lo_ref[...] = lo0
    hi_ref[...] = hi0
    mid_ref[...] = lo0 + 0.5 * (hi0 - lo0)
    clo_ref[...] = n_causal

    def bis_body(_):
        partial = [count_lanes(rows, mid_ref[rows]) for rows in slabs]
        widths = [hi_ref[rows] - lo_ref[rows] for rows in slabs]
        open_width = jnp.max(functools.reduce(jnp.maximum, widths))
        for rows, c in zip(slabs, partial):
            lo, hi, mid = lo_ref[rows], hi_ref[rows], mid_ref[rows]
            cnt = row_total(c)
            up = cnt >= kf
            lo = jnp.where(up, mid, lo)
            c_lo = jnp.where(up, cnt, clo_ref[rows])
            hi = jnp.where(up, hi, mid)
            nxt = lo + 0.5 * (hi - lo)
            stop = jnp.logical_or(c_lo == kf, jnp.logical_or(nxt <= lo, nxt >= hi))
            hi = jnp.where(stop, lo, hi)
            lo_ref[rows] = lo
            hi_ref[rows] = hi
            mid_ref[rows] = jnp.where(stop, lo, nxt)
            clo_ref[rows] = c_lo
        return open_width

    lax.while_loop(lambda w: w > 0.0, bis_body, jnp.max(hi0 - lo0))
    thr = lo_ref[...]

    tied_f = jnp.where(clo_ref[...] > kf, 1.0, 0.0)

    @pl.when(jnp.max(tied_f) > 0.0)
    def _():
        above = jnp.concatenate(
            [row_total(count_lanes(rows, thr[rows], strict=True)) for rows in slabs], axis=0)
        need = (kf - above)[:, :1]
        r = lax.broadcasted_iota(jnp.int32, (T, T), 0)
        c = lax.broadcasted_iota(jnp.int32, (T, T), 1)
        before = jnp.where(r < c, 1.0, 0.0).astype(BF16)
        thr_col = thr[:, :1]
        tied_col = tied_f[:, :1] > 0.0

        def body(j, seen):
            s = sc_ref[j]
            eq = jnp.where(s == thr_col, 1.0, 0.0)
            rank = seen + _dot(eq.astype(BF16), before)
            drop = jnp.logical_and(tied_col, jnp.logical_and(eq > 0.0, rank >= need))
            sc_ref[j] = jnp.where(drop, -jnp.inf, s)
            return seen + jnp.sum(eq, axis=-1, keepdims=True)

        lax.fori_loop(0, qi + 1, body, jnp.zeros((T, 1), F32))

    m_ref[...] = jnp.full(m_ref.shape, NEG_BIG, F32)
    l_ref[...] = jnp.zeros(l_ref.shape, F32)
    acc_ref[...] = jnp.zeros(acc_ref.shape, F32)

    def attend(j0, ntiles, band):
        sel = [jnp.where(sh >= thr, 0.0, NEG_BIG) for t in range(ntiles) for sh in halves(sc_ref[j0 + t])]
        start = pl.multiple_of(j0 * T, T)
        keys = pl.ds(start, ntiles * T)
        width = ntiles * T

        def logits_to_scratch(hh):
            sl = slice(hh * A_HEAD_DIM, (hh + 1) * A_HEAD_DIM)
            s_ref[hh % 2, :, :width] = _dot_nt(q_ref[:, sl], k_ref[keys, sl])

        logits_to_scratch(0)
        pending = None
        for hh in range(A_HEADS):
            sl = slice(hh * A_HEAD_DIM, (hh + 1) * A_HEAD_DIM)
            if hh + 1 < A_HEADS:
                logits_to_scratch(hh + 1)
            s = s_ref[hh % 2, :, :width]
            if band is not None:
                s = s + rb_ref[hh, :, band * T:(band + 1) * T]
            sh = [s[:, c * LANES:(c + 1) * LANES] + m for c, m in enumerate(sel)]
            m_prev = m_ref[hh]
            m_cur = jnp.max(functools.reduce(jnp.maximum, sh), axis=-1, keepdims=True)
            m_new = jnp.maximum(m_prev, lanes(m_cur))
            alpha = jnp.exp2(m_prev - m_new)
            ph = [jnp.exp2(x - m_new) for x in sh]
            l_ref[hh] = alpha * l_ref[hh] + functools.reduce(jnp.add, ph)
            p = jnp.concatenate([x.astype(BF16) for x in ph], axis=-1)
            acc_ref[hh] = alpha * acc_ref[hh]
            m_ref[hh] = m_new
            if pending is not None:
                acc_ref[pending[0]] += pending[1]
            pending = (hh, _dot(p, v_ref[keys, sl]))
        acc_ref[pending[0]] += pending[1]

    n_far = jnp.maximum(qi - 1, 0)
    done_tiles = 0
    width = FAR_TILES
    while width >= 1:
        trips = lax.div(n_far - done_tiles, width)

        def far_body(i, c, width=width, base=done_tiles):
            attend(base + i * width, width, None)
            return c

        lax.fori_loop(0, trips, far_body, 0)
        done_tiles = done_tiles + trips * width
        width //= 2

    @pl.when(qi > 0)
    def _():
        attend(qi - 1, 1, 0)

    attend(qi, 1, 1)

    for hh in range(A_HEADS):
        sl = slice(hh * A_HEAD_DIM, (hh + 1) * A_HEAD_DIM)
        den = jnp.sum(l_ref[hh], axis=-1, keepdims=True)
        o_ref[:, sl] = (acc_ref[hh] / den * g_ref[:, sl].astype(F32)).astype(BF16)


def _dsa_call(iq, iw, ikz, qa, ka, va, ga, rb, topk):
    bsz, seq, _ = qa.shape
    T = DSA_T
    nq = seq // T
    assert topk <= T and seq % T == 0
    once = pl.Buffered(1)
    return pl.pallas_call(
        functools.partial(_dsa_kernel, topk=topk),
        grid=(bsz, nq),
        in_specs=[
            pl.BlockSpec((None, IDX_PAIRS, T, LANES), lambda b, i: (b, 0, i, 0)),
            pl.BlockSpec((None, T, IDX_HEADS), lambda b, i: (b, i, 0)),
            pl.BlockSpec((None, 2, LANES, seq), lambda b, i: (b, 0, 0, 0), pipeline_mode=once),
            pl.BlockSpec((None, T, A_WIDTH), lambda b, i: (b, i, 0)),
            pl.BlockSpec((None, seq, A_WIDTH), lambda b, i: (b, 0, 0), pipeline_mode=once),
            pl.BlockSpec((None, seq, A_WIDTH), lambda b, i: (b, 0, 0), pipeline_mode=once),
            pl.BlockSpec((None, T, A_WIDTH), lambda b, i: (b, i, 0)),
            pl.BlockSpec((A_HEADS, T, 2 * T), lambda b, i: (0, 0, 0), pipeline_mode=once),
        ],
        out_specs=pl.BlockSpec((None, T, A_WIDTH), lambda b, i: (b, i, 0)),
        out_shape=jax.ShapeDtypeStruct((bsz, seq, A_WIDTH), BF16),
        scratch_shapes=[
            pltpu.VMEM((nq, T, T), F32),
            pltpu.VMEM((IDX_HEADS, T, LANES), F32),
            pltpu.VMEM((T, LANES), F32),
            pltpu.VMEM((T, LANES), F32),
            pltpu.VMEM((T, LANES), F32),
            pltpu.VMEM((T, LANES), F32),
            pltpu.VMEM((2, T, FAR_TILES * T), F32),
            pltpu.VMEM((A_HEADS, T, LANES), F32),
            pltpu.VMEM((A_HEADS, T, LANES), F32),
            pltpu.VMEM((A_HEADS, T, A_HEAD_DIM), F32),
        ],
        compiler_params=_cparams(("arbitrary", "arbitrary")),
        name="dsa",
    )(iq, iw, ikz, qa, ka, va, ga, rb)


def _swa_kernel(sink_ref, q_ref, kc_ref, vc_ref, kp_ref, vp_ref, g_ref, bias_ref, o_ref):
    first = pl.program_id(1) == 0
    blk = WINDOW
    group = B_HEADS // B_KV_HEADS
    grows = group * blk
    for sb in range(SWA_T // blk):
        rows = slice(sb * blk, (sb + 1) * blk)
        prev = slice((sb - 1) * blk, sb * blk)
        pieces = []
        for g in range(B_KV_HEADS):
            heads = slice(g * group, (g + 1) * group)
            k_prev = kp_ref[g] if sb == 0 else kc_ref[g, prev, :]
            v_prev = vp_ref[g] if sb == 0 else vc_ref[g, prev, :]
            q = q_ref[heads, rows, :].reshape(grows, B_HEAD_DIM)
            lp = _dot_nt(q, k_prev) + bias_ref[heads, :, :blk].reshape(grows, blk)
            lc = _dot_nt(q, kc_ref[g, rows, :]) + bias_ref[heads, :, blk:].reshape(grows, blk)
            if sb == 0:
                lp = jnp.where(first, NEG_BIG, lp)
            sink = jnp.concatenate(
                [jnp.full((blk, blk), sink_ref[g * group + h], F32) for h in range(group)], axis=0)
            m = jnp.max(jnp.maximum(lp, lc), axis=-1, keepdims=True)
            m = jnp.maximum(jnp.broadcast_to(m, (grows, blk)), sink)
            pp = jnp.exp(lp - m).astype(BF16)
            pc = jnp.exp(lc - m).astype(BF16)
            ones = jnp.ones((blk, blk), BF16)
            den = _dot(pp, ones) + _dot(pc, ones) + jnp.exp(sink - m)
            o = (_dot(pp, v_prev) + _dot(pc, vc_ref[g, rows, :])) / den[:, :B_HEAD_DIM]
            pieces += [o[h * blk:(h + 1) * blk, :] for h in range(group)]
        o_ref[rows, :] = (jnp.concatenate(pieces, axis=1) * g_ref[rows, :].astype(F32)).astype(BF16)


def _swa_call(sinks, qb, kb, vb, gb, bias):
    bsz, _, seq, _ = qb.shape
    T = SWA_T
    per = T // WINDOW
    cur = lambda b, i: (b, 0, i, 0)
    prev = lambda b, i: (b, 0, jnp.maximum(i * per - 1, 0), 0)
    return pl.pallas_call(
        _swa_kernel,
        grid=(bsz, seq // T),
        in_specs=[
            pl.BlockSpec(memory_space=pltpu.SMEM),
            pl.BlockSpec((None, B_HEADS, T, B_HEAD_DIM), cur),
            pl.BlockSpec((None, B_KV_HEADS, T, B_HEAD_DIM), cur),
            pl.BlockSpec((None, B_KV_HEADS, T, B_HEAD_DIM), cur),
            pl.BlockSpec((None, B_KV_HEADS, WINDOW, B_HEAD_DIM), prev),
            pl.BlockSpec((None, B_KV_HEADS, WINDOW, B_HEAD_DIM), prev),
            pl.BlockSpec((None, T, B_WIDTH), lambda b, i: (b, i, 0)),
            pl.BlockSpec((B_HEADS, WINDOW, 2 * WINDOW), lambda b, i: (0, 0, 0)),
        ],
        out_specs=pl.BlockSpec((None, T, B_WIDTH), lambda b, i: (b, i, 0)),
        out_shape=jax.ShapeDtypeStruct((bsz, seq, B_WIDTH), BF16),
        compiler_params=_cparams(("parallel", "parallel")),
        name="swa",
    )(sinks, qb, kb, vb, kb, vb, gb, bias)


def _mem_kernel(q_ref, k_ref, v_ref, g_ref, o_ref):
    for hh in range(C_HEADS):
        sl = slice(hh * C_HEAD_DIM, (hh + 1) * C_HEAD_DIM)
        s = _dot_nt(q_ref[:, sl], k_ref[:, sl])
        m = jnp.max(s, axis=-1, keepdims=True)
        p = jnp.exp(s - m)
        den = jnp.sum(p, axis=-1, keepdims=True)
        o = _dot(p.astype(BF16), v_ref[:, sl]) / den
        o_ref[:, sl] = (o * g_ref[:, sl].astype(F32)).astype(BF16)


def _mem_call(qc, kc, vc, gc):
    bsz, seq, _ = qc.shape
    mlen = kc.shape[1]
    T = SWA_T
    cur = lambda b, i: (b, i, 0)
    whole = lambda b, i: (b, 0, 0)
    return pl.pallas_call(
        _mem_kernel,
        grid=(bsz, seq // T),
        in_specs=[
            pl.BlockSpec((None, T, C_WIDTH), cur),
            pl.BlockSpec((None, mlen, C_WIDTH), whole),
            pl.BlockSpec((None, mlen, C_WIDTH), whole),
            pl.BlockSpec((None, T, C_WIDTH), cur),
        ],
        out_specs=pl.BlockSpec((None, T, C_WIDTH), cur),
        out_shape=jax.ShapeDtypeStruct((bsz, seq, C_WIDTH), BF16),
        compiler_params=_cparams(("parallel", "parallel")),
        name="mem_attn",
    )(qc, kc, vc, gc)


def _final_kernel(x_ref, g_ref, oa_ref, ob_ref, oc_ref, wma_ref, wmb_ref, wmc_ref,
                  ba_ref, bb_ref, bc_ref, wua_ref, wub_ref, wuc_ref, wop_ref, woc_ref, out_ref,
                  h_ref, acc_ref, m_ref):
    c = pl.program_id(1)
    cur_slot = lax.rem(c, 2)

    @pl.when(c == 0)
    def _():
        h_ref[...] = _rms_rows(x_ref[...], g_ref[...]).astype(BF16)
        acc_ref[...] = jnp.zeros(acc_ref.shape, F32)
        m_ref[1] = jnp.zeros(m_ref.shape[1:], BF16)

    h = h_ref[...]

    def branch(o_ref, wm_ref, b_ref, wu_ref):
        gate = 1.0 / (1.0 + jnp.exp(-(_dot_nt(h, wm_ref[...]) + b_ref[...])))
        return gate * _dot(o_ref[...], wu_ref[...])

    acc_ref[...] += _dot(m_ref[1 - cur_slot], wop_ref[...])
    merged = (branch(oa_ref, wma_ref, ba_ref, wua_ref) + branch(ob_ref, wmb_ref, bb_ref, wub_ref)
              + branch(oc_ref, wmc_ref, bc_ref, wuc_ref))
    m_ref[cur_slot] = merged.astype(BF16)

    @pl.when(c == pl.num_programs(1) - 1)
    def _():
        out_ref[...] = x_ref[...] + acc_ref[...] + _dot(m_ref[cur_slot], woc_ref[...])


def _final_call(x2d, g, oa, ob, oc, wmix, gate_bias, wua, wub, wuc, wo):
    n, d = x2d.shape
    tm = ROW_TILE
    nch = FINAL_NCHUNK
    cw = d // nch
    row = lambda i, c: (i, 0)

    prv = lambda c: jnp.maximum(c - 1, 0)

    def col(br):
        return lambda i, c: (0, br * nch + c)

    def wrow(br):
        return lambda i, c: (br * nch + c, 0)

    return pl.pallas_call(
        _final_kernel,
        grid=(n // tm, nch),
        in_specs=[
            pl.BlockSpec((tm, d), row),
            pl.BlockSpec((1, d), lambda i, c: (0, 0)),
            pl.BlockSpec((tm, A_WIDTH), row),
            pl.BlockSpec((tm, B_WIDTH), row),
            pl.BlockSpec((tm, C_WIDTH), row),
            pl.BlockSpec((cw, d), wrow(0)), pl.BlockSpec((cw, d), wrow(1)), pl.BlockSpec((cw, d), wrow(2)),
            pl.BlockSpec((1, cw), col(0)), pl.BlockSpec((1, cw), col(1)), pl.BlockSpec((1, cw), col(2)),
            pl.BlockSpec((A_WIDTH, cw), lambda i, c: (0, c)),
            pl.BlockSpec((B_WIDTH, cw), lambda i, c: (0, c)),
            pl.BlockSpec((C_WIDTH, cw), lambda i, c: (0, c)),
            pl.BlockSpec((cw, d), lambda i, c: (prv(c), 0)),
            pl.BlockSpec((cw, d), lambda i, c: (c, 0)),
        ],
        out_specs=pl.BlockSpec((tm, d), row),
        out_shape=jax.ShapeDtypeStruct((n, d), F32),
        scratch_shapes=[pltpu.VMEM((tm, d), BF16), pltpu.VMEM((tm, d), F32), pltpu.VMEM((2, tm, cw), BF16)],
        compiler_params=_cparams(("parallel", "arbitrary")),
        name="merge_out",
    )(x2d, g, oa, ob, oc, wmix, wmix, wmix, gate_bias, gate_bias, gate_bias, wua, wub, wuc, wo, wo)


def _w_in_groups(d):
    sizes = (A_WIDTH, A_KV_RANK, IDX_HEADS * IDX_DIM, IDX_DIM, IDX_HEADS, A_WIDTH,
             B_WIDTH, B_KV_WIDTH, B_KV_WIDTH, B_WIDTH, C_WIDTH, C_WIDTH, N_BRANCH * d)
    cuts = np.cumsum((0,) + sizes).tolist()
    (aq, ackv, iq, ik, iw, ag, bq, bk, bv, bg, cq, cg, mix) = [
        (cuts[i], cuts[i + 1]) for i in range(len(sizes))]
    return [
        [(aq[0], ackv[1])],
        [(iq[0], ik[1]), LANES - IDX_DIM, iw, LANES - IDX_HEADS],
        [ag, bg, cg],
        [(bq[0], bv[1]), cq],
        [mix],
    ]


def _group_width(group):
    return sum(p if isinstance(p, int) else p[1] - p[0] for p in group)


def _regroup_kernel(w_ref, *out_refs, groups):
    for o_ref, group in zip(out_refs, groups):
        row = 0
        for part in group:
            if isinstance(part, int):
                o_ref[row:row + part, :] = jnp.zeros((part, o_ref.shape[1]), BF16)
                row += part
            else:
                o_ref[row:row + part[1] - part[0], :] = w_ref[part[0]:part[1], :].astype(BF16)
                row += part[1] - part[0]


def _regroup_w_in(w_in_t):
    cols, d = w_in_t.shape
    groups = _w_in_groups(d)
    lanes = REGROUP_LANES
    return pl.pallas_call(
        functools.partial(_regroup_kernel, groups=groups),
        grid=(d // lanes,),
        in_specs=[pl.BlockSpec((cols, lanes), lambda i: (0, i))],
        out_specs=[pl.BlockSpec((_group_width(g), lanes), lambda i: (0, i)) for g in groups],
        out_shape=[jax.ShapeDtypeStruct((_group_width(g), d), BF16) for g in groups],
        compiler_params=_cparams(("parallel",)),
        name="regroup_w_in",
    )(w_in_t)


def _layer(x, mem, norm_g, w_in, kv_norm_g, w_kv_up, idx_k_ln_g, idx_k_ln_b, q_norm_a, k_norm_a,
           q_norm_b, k_norm_b, sinks_b, mem_norm_g, w_mem_kv, q_norm_c, k_norm_c,
           w_up_a, w_up_b, w_up_c, gate_bias, w_o, rel_bias):
    bsz, seq, d = x.shape
    mlen = mem.shape[1]
    n = bsz * seq
    tm = ROW_TILE
    topk = min(TOPK_MAX, seq // 4)
    x2d = x.reshape(n, d)
    row2 = lambda v: v.reshape(1, -1)

    w_grp_a, w_grp_i, w_grp_g, w_grp_bc, w_mix = _regroup_w_in(jnp.swapaxes(w_in, 0, 1))
    g_x = row2(norm_g)

    qa, ka, va = _proj_call(
        _proj_a_kernel, "proj_a", x2d, tm,
        [g_x, w_grp_a, row2(kv_norm_g), w_kv_up.astype(BF16), row2(q_norm_a), row2(k_norm_a)],
        [jax.ShapeDtypeStruct((n, A_WIDTH), BF16)] * 3,
        [_row_spec(tm, A_WIDTH)] * 3)

    per_b = seq // tm
    iq, ik, iw = _proj_call(
        _proj_i_kernel, "proj_i", x2d, tm,
        [g_x, w_grp_i, row2(idx_k_ln_g), row2(idx_k_ln_b)],
        [jax.ShapeDtypeStruct((bsz, IDX_PAIRS, seq, LANES), BF16),
         jax.ShapeDtypeStruct((n, IDX_DIM), BF16),
         jax.ShapeDtypeStruct((n, IDX_HEADS), F32)],
        [pl.BlockSpec((None, IDX_PAIRS, tm, LANES), lambda i: (i // per_b, 0, i % per_b, 0)),
         _row_spec(tm, IDX_DIM), _row_spec(tm, IDX_HEADS)])

    ga, gb, gc = _proj_call(
        _proj_g_kernel, "proj_g", x2d, tm, [g_x, w_grp_g],
        [jax.ShapeDtypeStruct((n, A_WIDTH), BF16), jax.ShapeDtypeStruct((n, B_WIDTH), BF16),
         jax.ShapeDtypeStruct((n, C_WIDTH), BF16)],
        [_row_spec(tm, A_WIDTH), _row_spec(tm, B_WIDTH), _row_spec(tm, C_WIDTH)])

    qb, kb, vb, qc = _proj_call(
        _proj_bc_kernel, "proj_bc", x2d, tm,
        [g_x, w_grp_bc, row2(jnp.tile(q_norm_b, 2)), row2(jnp.tile(k_norm_b, 2)), row2(q_norm_c)],
        [jax.ShapeDtypeStruct((bsz, B_HEADS, seq, B_HEAD_DIM), BF16),
         jax.ShapeDtypeStruct((bsz, B_KV_HEADS, seq, B_HEAD_DIM), BF16),
         jax.ShapeDtypeStruct((bsz, B_KV_HEADS, seq, B_HEAD_DIM), BF16),
         jax.ShapeDtypeStruct((n, C_WIDTH), BF16)],
        [pl.BlockSpec((None, B_HEADS, tm, B_HEAD_DIM), lambda i: (i // per_b, 0, i % per_b, 0)),
         pl.BlockSpec((None, B_KV_HEADS, tm, B_HEAD_DIM), lambda i: (i // per_b, 0, i % per_b, 0)),
         pl.BlockSpec((None, B_KV_HEADS, tm, B_HEAD_DIM), lambda i: (i // per_b, 0, i % per_b, 0)),
         _row_spec(tm, C_WIDTH)])

    mrows = bsz * mlen
    kc, vc = _proj_call(
        _proj_mem_kernel, "proj_mem", mem.reshape(mrows, d), min(tm, mrows),
        [row2(mem_norm_g), w_mem_kv.astype(BF16), row2(k_norm_c)],
        [jax.ShapeDtypeStruct((mrows, C_WIDTH), BF16)] * 2,
        [_row_spec(min(tm, mrows), C_WIDTH)] * 2)

    T = DSA_T
    i_idx = np.arange(T)[:, None]
    dist_a = T + i_idx - np.arange(2 * T)[None, :]
    rb_a = _bias_tiles(rel_bias, _t5_bucket_np(dist_a), 0, A_HEADS, NUM_BUCKETS - 1, LOG2E)
    i_idx = np.arange(WINDOW)[:, None]
    dist_b = WINDOW + i_idx - np.arange(2 * WINDOW)[None, :]
    bucket_b = np.where((dist_b >= 0) & (dist_b < WINDOW), _t5_bucket_np(dist_b), -1).astype(np.int32)
    bias_b = _bias_tiles(rel_bias, bucket_b, A_HEADS, B_HEADS, None, 1.0)

    r3 = lambda v, w: v.reshape(bsz, seq, w)
    ikt = jnp.swapaxes(ik.reshape(bsz, seq, IDX_DIM), 1, 2)
    zeros = jnp.zeros_like(ikt)
    ikz = jnp.stack([jnp.concatenate([ikt, zeros], axis=1), jnp.concatenate([zeros, ikt], axis=1)], axis=1)
    oa = _dsa_call(iq, iw.reshape(bsz, seq, IDX_HEADS), ikz,
                   r3(qa, A_WIDTH), r3(ka, A_WIDTH),
                   r3(va, A_WIDTH), r3(ga, A_WIDTH), rb_a, topk)
    ob = _swa_call(sinks_b, qb, kb, vb, r3(gb, B_WIDTH), bias_b)
    oc = _mem_call(r3(qc, C_WIDTH), kc.reshape(bsz, mlen, C_WIDTH), vc.reshape(bsz, mlen, C_WIDTH),
                   r3(gc, C_WIDTH))

    out = _final_call(x2d, g_x, oa.reshape(n, A_WIDTH), ob.reshape(n, B_WIDTH), oc.reshape(n, C_WIDTH),
                      w_mix, row2(gate_bias), w_up_a.astype(BF16), w_up_b.astype(BF16),
                      w_up_c.astype(BF16), w_o.astype(BF16))
    return out.reshape(bsz, seq, d)


def kernel(x, mem, norm_g, w_in, kv_norm_g, w_kv_up, idx_k_ln_g, idx_k_ln_b, q_norm_a, k_norm_a, q_norm_b, k_norm_b, sinks_b, mem_norm_g, w_mem_kv, q_norm_c, k_norm_c, w_up_a, w_up_b, w_up_c, gate_bias, w_o, rel_bias):
    for l in range(norm_g.shape[0]):
        x = _layer(x, mem, norm_g[l], w_in[l], kv_norm_g[l], w_kv_up[l], idx_k_ln_g[l], idx_k_ln_b[l],
                   q_norm_a[l], k_norm_a[l], q_norm_b[l], k_norm_b[l], sinks_b[l], mem_norm_g[l],
                   w_mem_kv[l], q_norm_c[l], k_norm_c[l], w_up_a[l], w_up_b[l], w_up_c[l],
                   gate_bias[l], w_o[l], rel_bias)
    return x
```

```python
import functools
import math

import numpy as np
import jax
import jax.numpy as jnp
from jax import lax
from jax.experimental import pallas as pl
from jax.experimental.pallas import tpu as pltpu

F32 = jnp.float32
BF16 = jnp.bfloat16

EPS = 1e-6
A_HEADS, A_HEAD_DIM, A_KV_RANK = 6, 128, 256
IDX_HEADS, IDX_DIM, TOPK_MAX = 16, 64, 256
IDX_PAIRS = IDX_HEADS // 2
B_HEADS, B_KV_HEADS, B_HEAD_DIM, WINDOW = 12, 2, 64, 128
C_HEADS, C_HEAD_DIM = 4, 128
NUM_BUCKETS, MAX_DISTANCE = 32, 128
N_BRANCH = 3
A_WIDTH = A_HEADS * A_HEAD_DIM
B_WIDTH = B_HEADS * B_HEAD_DIM
B_KV_WIDTH = B_KV_HEADS * B_HEAD_DIM
C_WIDTH = C_HEADS * C_HEAD_DIM

LANES = 128
VMEM_LIMIT = 56 * 1024 * 1024
NEG_BIG = -1e30

ROW_TILE = 512
DSA_T = 256
SWA_T = 512
FINAL_ROWS = 512
FINAL_NCHUNK = 4
COUNT_ROWS = 128
REGROUP_LANES = 256
COUNT_UNROLL = 4
FAR_TILES = 4
LOG2E = math.log2(math.e)


def _cparams(sem):
    return pltpu.CompilerParams(dimension_semantics=sem, vmem_limit_bytes=VMEM_LIMIT)


def _dot(a, b):
    return jnp.dot(a, b, preferred_element_type=F32)


def _dot_nt(a, b):
    return lax.dot_general(a, b, (((1,), (1,)), ((), ())), preferred_element_type=F32)


def _rms_rows(x, g):
    ms = jnp.mean(x * x, axis=-1, keepdims=True)
    return x * lax.rsqrt(ms + EPS) * g


def _headnorm128(seg, g):
    ms = jnp.mean(seg * seg, axis=-1, keepdims=True)
    return seg * lax.rsqrt(ms + EPS) * g


def _headnorm64_pair(seg, g2):
    sq = seg * seg
    low = lax.broadcasted_iota(jnp.int32, seg.shape, 1) < B_HEAD_DIM
    s_all = jnp.sum(sq, axis=-1, keepdims=True)
    s_lo = jnp.sum(jnp.where(low, sq, 0.0), axis=-1, keepdims=True)
    ms = jnp.where(low, s_lo, s_all - s_lo) * (1.0 / B_HEAD_DIM)
    return seg * lax.rsqrt(ms + EPS) * g2


def _silu(y):
    return y / (1.0 + jnp.exp(-y))


def _proj_a_kernel(x_ref, g_ref, w_ref, kvg_ref, wkv_ref, qg_ref, kg_ref, qa_ref, ka_ref, va_ref):
    h = _rms_rows(x_ref[...], g_ref[...]).astype(BF16)
    y = _dot_nt(h, w_ref[...])
    scale = A_HEAD_DIM ** -0.5 * LOG2E
    for hh in range(A_HEADS):
        sl = slice(hh * A_HEAD_DIM, (hh + 1) * A_HEAD_DIM)
        qa_ref[:, sl] = (_headnorm128(y[:, sl], qg_ref[...]) * scale).astype(BF16)
    ckv = _rms_rows(y[:, A_WIDTH:A_WIDTH + A_KV_RANK], kvg_ref[...]).astype(BF16)
    kv = _dot(ckv, wkv_ref[...])
    for hh in range(A_HEADS):
        sl = slice(hh * A_HEAD_DIM, (hh + 1) * A_HEAD_DIM)
        ka_ref[:, sl] = _headnorm128(kv[:, sl], kg_ref[...]).astype(BF16)
    va_ref[...] = kv[:, A_WIDTH:].astype(BF16)


def _proj_i_kernel(x_ref, g_ref, w_ref, lng_ref, lnb_ref, iq_ref, ik_ref, iw_ref):
    h = _rms_rows(x_ref[...], g_ref[...]).astype(BF16)
    y = _dot_nt(h, w_ref[...])
    k0 = IDX_HEADS * IDX_DIM
    for p in range(IDX_PAIRS):
        iq_ref[p] = y[:, p * LANES:(p + 1) * LANES].astype(BF16)
    ik = y[:, k0:k0 + IDX_DIM]
    mu = jnp.mean(ik, axis=-1, keepdims=True)
    d = ik - mu
    var = jnp.mean(d * d, axis=-1, keepdims=True)
    ik_ref[...] = (d * lax.rsqrt(var + EPS) * lng_ref[...] + lnb_ref[...]).astype(BF16)
    w0 = k0 + LANES
    iw_ref[...] = y[:, w0:w0 + IDX_HEADS] * (IDX_HEADS ** -0.5) * (IDX_DIM ** -0.5)


def _proj_g_kernel(x_ref, g_ref, w_ref, ga_ref, gb_ref, gc_ref):
    h = _rms_rows(x_ref[...], g_ref[...]).astype(BF16)
    y = _silu(_dot_nt(h, w_ref[...]))
    ga_ref[...] = y[:, :A_WIDTH].astype(BF16)
    gb_ref[...] = y[:, A_WIDTH:A_WIDTH + B_WIDTH].astype(BF16)
    gc_ref[...] = y[:, A_WIDTH + B_WIDTH:].astype(BF16)


def _proj_bc_kernel(x_ref, g_ref, w_ref, qbg_ref, kbg_ref, qcg_ref,
                    qb_ref, kb_ref, vb_ref, qc_ref, kbs_ref, vbs_ref):
    h = _rms_rows(x_ref[...], g_ref[...]).astype(BF16)
    y = _dot_nt(h, w_ref[...])
    sb = B_HEAD_DIM ** -0.5
    for p in range(B_WIDTH // LANES):
        sl = slice(p * LANES, (p + 1) * LANES)
        qb_ref[:, sl] = (_headnorm64_pair(y[:, sl], qbg_ref[...]) * sb).astype(BF16)
    k0 = B_WIDTH
    kb_ref[...] = _headnorm64_pair(y[:, k0:k0 + B_KV_WIDTH], kbg_ref[...]).astype(BF16)
    vb_ref[...] = y[:, k0 + B_KV_WIDTH:k0 + 2 * B_KV_WIDTH].astype(BF16)
    s0 = k0 + 2 * B_KV_WIDTH + C_WIDTH
    kbs_ref[...] = _headnorm64_pair(y[:, s0:s0 + B_KV_WIDTH], kbg_ref[...]).astype(BF16)
    vbs_ref[...] = y[:, s0 + B_KV_WIDTH:s0 + 2 * B_KV_WIDTH].astype(BF16)
    c0 = k0 + 2 * B_KV_WIDTH
    sc = C_HEAD_DIM ** -0.5
    for hh in range(C_HEADS):
        sl = slice(hh * C_HEAD_DIM, (hh + 1) * C_HEAD_DIM)
        qc_ref[:, sl] = (_headnorm128(y[:, c0 + hh * C_HEAD_DIM:c0 + (hh + 1) * C_HEAD_DIM],
                                      qcg_ref[...]) * sc).astype(BF16)


def _proj_mem_kernel(x_ref, g_ref, w_ref, kg_ref, kc_ref, vc_ref):
    h = _rms_rows(x_ref[...], g_ref[...]).astype(BF16)
    y = _dot(h, w_ref[...])
    for hh in range(C_HEADS):
        sl = slice(hh * C_HEAD_DIM, (hh + 1) * C_HEAD_DIM)
        kc_ref[:, sl] = _headnorm128(y[:, sl], kg_ref[...]).astype(BF16)
    vc_ref[...] = y[:, C_WIDTH:].astype(BF16)


def _row_spec(tm, cols):
    return pl.BlockSpec((tm, cols), lambda i: (i, 0))


def _full_spec(shape):
    nd = len(shape)
    return pl.BlockSpec(shape, lambda i: (0,) * nd)


def _proj_call(kernel_fn, name, x2d, tm, consts, out_shapes, out_specs):
    n, d = x2d.shape
    in_specs = [_row_spec(tm, d)] + [_full_spec(c.shape) for c in consts]
    return pl.pallas_call(
        kernel_fn,
        grid=(n // tm,),
        in_specs=in_specs,
        out_specs=out_specs,
        out_shape=out_shapes,
        compiler_params=_cparams(("parallel",)),
        name=name,
    )(x2d, *consts)


def _t5_bucket_np(dist):
    n = np.maximum(dist, 0)
    max_exact = NUM_BUCKETS // 2
    nf = np.maximum(n, 1).astype(np.float32)
    large = max_exact + (np.log(nf / max_exact) / math.log(MAX_DISTANCE / max_exact)
                         * (NUM_BUCKETS - max_exact)).astype(np.int32)
    large = np.minimum(large, NUM_BUCKETS - 1)
    return np.where(n < max_exact, n, large).astype(np.int32)


def _bias_tile_kernel(tab_ref, bucket_ref, o_ref, *, head0, shift_bucket, scale):
    h = pl.program_id(0) + head0
    bucket = bucket_ref[...]
    acc = jnp.zeros(bucket.shape, F32)
    for b in range(NUM_BUCKETS):
        acc = jnp.where(bucket == b, tab_ref[b, h], acc)
    if shift_bucket is not None:
        acc = acc - tab_ref[shift_bucket, h]
    o_ref[...] = jnp.where(bucket < 0, NEG_BIG, acc * scale)


def _bias_tiles(rel_bias, bucket_np, head0, nheads, shift_bucket, scale):
    r, c = bucket_np.shape
    return pl.pallas_call(
        functools.partial(_bias_tile_kernel, head0=head0, shift_bucket=shift_bucket, scale=scale),
        grid=(nheads,),
        in_specs=[pl.BlockSpec(memory_space=pltpu.SMEM), pl.BlockSpec((r, c), lambda h: (0, 0))],
        out_specs=pl.BlockSpec((None, r, c), lambda h: (h, 0, 0)),
        out_shape=jax.ShapeDtypeStruct((nheads, r, c), F32),
        compiler_params=_cparams(("arbitrary",)),
        name="bias_tiles",
    )(rel_bias, jnp.asarray(bucket_np))


def _dsa_kernel(iq_ref, iw_ref, ikz_ref, q_ref, k_ref, v_ref, g_ref, rb_ref, o_ref,
                sc_ref, wb_ref, lo_ref, hi_ref, mid_ref, clo_ref, s_ref, m_ref, l_ref, acc_ref, *, topk):
    T = DSA_T
    qi = pl.program_id(1)
    kf = float(topk)

    def lanes(col):
        return jnp.broadcast_to(col, (T, LANES))

    def halves(tile):
        return [tile[:, c * LANES:(c + 1) * LANES] for c in range(T // LANES)]

    for hh in range(IDX_HEADS):
        wb_ref[hh] = lanes(iw_ref[:, hh:hh + 1])

    def score_chunk(j, diag):
        keys = pl.ds(pl.multiple_of(j * T, T), T)
        accs = [jnp.zeros((T, LANES), F32) for _ in range(T // LANES)]
        for hh in range(IDX_HEADS):
            s = _dot(iq_ref[hh // 2], ikz_ref[hh % 2, :, keys])
            w = wb_ref[hh]
            accs = [a + w * jnp.maximum(sh, 0.0) for a, sh in zip(accs, halves(s))]
        if diag:
            row = lax.broadcasted_iota(jnp.int32, (T, LANES), 0)
            col = lax.broadcasted_iota(jnp.int32, (T, LANES), 1)
            causal = [col + c * LANES <= row for c in range(T // LANES)]
            lows = [jnp.where(cm, a, jnp.inf) for cm, a in zip(causal, accs)]
            accs = [jnp.where(cm, a, -jnp.inf) for cm, a in zip(causal, accs)]
        else:
            lows = accs
        for c, a in enumerate(accs):
            sc_ref[j, :, c * LANES:(c + 1) * LANES] = a
        return functools.reduce(jnp.minimum, lows), accs

    def p1_body(j, carry):
        mn, mxs = carry
        lo_c, his = score_chunk(j, False)
        return jnp.minimum(mn, lo_c), [jnp.maximum(a, b) for a, b in zip(mxs, his)]

    mn0 = jnp.full((T, LANES), jnp.inf, F32)
    mx0 = [jnp.full((T, LANES), -jnp.inf, F32) for _ in range(T // LANES)]

    def p1_pair(jj, carry):
        return p1_body(2 * jj + 1, p1_body(2 * jj, carry))

    n_pairs = lax.div(qi, 2)
    mn, mxs = lax.fori_loop(0, n_pairs, p1_pair, (mn0, mx0))
    mn, mxs = lax.fori_loop(2 * n_pairs, qi, p1_body, (mn, mxs))
    lo_c, his = score_chunk(qi, True)
    mxs = [jnp.maximum(a, b) for a, b in zip(mxs, his)]
    mn = lanes(jnp.min(jnp.minimum(mn, lo_c), axis=-1, keepdims=True))
    mx = lanes(jnp.max(functools.reduce(jnp.maximum, mxs), axis=-1, keepdims=True))
    class_floor = lanes(jnp.min(functools.reduce(jnp.minimum, mxs), axis=-1, keepdims=True))

    def count_lanes(rows, v, strict=False):
        def tile(j, c):
            for c0 in range(0, T, LANES):
                sh = sc_ref[j, rows, c0:c0 + LANES]
                hit = (sh > v) if strict else (sh >= v)
                c = c + jnp.where(hit, 1.0, 0.0)
            return c

        def group(g, c):
            for u in range(COUNT_UNROLL):
                c = tile(g * COUNT_UNROLL + u, c)
            return c

        n_groups = lax.div(qi + 1, COUNT_UNROLL)
        c = lax.fori_loop(0, n_groups, group, jnp.zeros((COUNT_ROWS, LANES), F32))
        return lax.fori_loop(n_groups * COUNT_UNROLL, qi + 1, tile, c)

    def row_total(c):
        return jnp.broadcast_to(jnp.sum(c, axis=-1, keepdims=True), c.shape)

    slabs = [slice(r0, r0 + COUNT_ROWS) for r0 in range(0, T, COUNT_ROWS)]
    n_causal = (qi * T + 1 + lax.broadcasted_iota(jnp.int32, (T, LANES), 0)).astype(F32)
    keep_all = n_causal <= kf
    lo0 = jnp.where(keep_all, mn, jnp.maximum(mn, class_floor))
    above_max = mx + jnp.maximum(jnp.abs(mx) * (2.0 ** -20), 1e-30)
    hi0 = jnp.where(keep_all, mn, above_max)
    lo_ref[...] = lo0
    hi_ref[...] = hi0
    mid_ref[...] = lo0 + 0.5 * (hi0 - lo0)
    clo_ref[...] = n_causal

    def bis_body(_):
        partial = [count_lanes(rows, mid_ref[rows]) for rows in slabs]
        widths = [hi_ref[rows] - lo_ref[rows] for rows in slabs]
        open_width = jnp.max(functools.reduce(jnp.maximum, widths))
        for rows, c in zip(slabs, partial):
            lo, hi, mid = lo_ref[rows], hi_ref[rows], mid_ref[rows]
            cnt = row_total(c)
            up = cnt >= kf
            lo = jnp.where(up, mid, lo)
            c_lo = jnp.where(up, cnt, clo_ref[rows])
            hi = jnp.where(up, hi, mid)
            nxt = lo + 0.5 * (hi - lo)
            stop = jnp.logical_or(c_lo == kf, jnp.logical_or(nxt <= lo, nxt >= hi))
            hi = jnp.where(stop, lo, hi)
            lo_ref[rows] = lo
            hi_ref[rows] = hi
            mid_ref[rows] = jnp.where(stop, lo, nxt)
            clo_ref[rows] = c_lo
        return open_width

    lax.while_loop(lambda w: w > 0.0, bis_body, jnp.max(hi0 - lo0))
    thr = lo_ref[...]

    tied_f = jnp.where(clo_ref[...] > kf, 1.0, 0.0)

    @pl.when(jnp.max(tied_f) > 0.0)
    def _():
        above = jnp.concatenate(
            [row_total(count_lanes(rows, thr[rows], strict=True)) for rows in slabs], axis=0)
        need = (kf - above)[:, :1]
        r = lax.broadcasted_iota(jnp.int32, (T, T), 0)
        c = lax.broadcasted_iota(jnp.int32, (T, T), 1)
        before = jnp.where(r < c, 1.0, 0.0).astype(BF16)
        thr_col = thr[:, :1]
        tied_col = tied_f[:, :1] > 0.0

        def body(j, seen):
            s = sc_ref[j]
            eq = jnp.where(s == thr_col, 1.0, 0.0)
            rank = seen + _dot(eq.astype(BF16), before)
            drop = jnp.logical_and(tied_col, jnp.logical_and(eq > 0.0, rank >= need))
            sc_ref[j] = jnp.where(drop, -jnp.inf, s)
            return seen + jnp.sum(eq, axis=-1, keepdims=True)

        lax.fori_loop(0, qi + 1, body, jnp.zeros((T, 1), F32))

    m_ref[...] = jnp.full(m_ref.shape, NEG_BIG, F32)
    l_ref[...] = jnp.zeros(l_ref.shape, F32)
    acc_ref[...] = jnp.zeros(acc_ref.shape, F32)

    def attend(j0, ntiles, band):
        sel = [jnp.where(sh >= thr, 0.0, NEG_BIG) for t in range(ntiles) for sh in halves(sc_ref[j0 + t])]
        start = pl.multiple_of(j0 * T, T)
        keys = pl.ds(start, ntiles * T)
        width = ntiles * T

        def logits_to_scratch(hh):
            sl = slice(hh * A_HEAD_DIM, (hh + 1) * A_HEAD_DIM)
            s_ref[hh % 2, :, :width] = _dot_nt(q_ref[:, sl], k_ref[keys, sl])

        logits_to_scratch(0)
        pending = None
        for hh in range(A_HEADS):
            sl = slice(hh * A_HEAD_DIM, (hh + 1) * A_HEAD_DIM)
            if hh + 1 < A_HEADS:
                logits_to_scratch(hh + 1)
            s = s_ref[hh % 2, :, :width]
            if band is not None:
                s = s + rb_ref[hh, :, band * T:(band + ntiles) * T]
            sh = [s[:, c * LANES:(c + 1) * LANES] + m for c, m in enumerate(sel)]
            m_prev = m_ref[hh]
            m_cur = jnp.max(functools.reduce(jnp.maximum, sh), axis=-1, keepdims=True)
            m_new = jnp.maximum(m_prev, lanes(m_cur))
            alpha = jnp.exp2(m_prev - m_new)
            ph = [jnp.exp2(x - m_new) for x in sh]
            l_ref[hh] = alpha * l_ref[hh] + functools.reduce(jnp.add, ph)
            p = jnp.concatenate([x.astype(BF16) for x in ph], axis=-1)
            acc_ref[hh] = alpha * acc_ref[hh]
            m_ref[hh] = m_new
            if pending is not None:
                acc_ref[pending[0]] += pending[1]
            pending = (hh, _dot(p, v_ref[keys, sl]))
        acc_ref[pending[0]] += pending[1]

    n_far = jnp.maximum(qi - 1, 0)
    done_tiles = 0
    width = FAR_TILES
    while width >= 1:
        trips = lax.div(n_far - done_tiles, width)

        def far_body(i, c, width=width, base=done_tiles):
            attend(base + i * width, width, None)
            return c

        lax.fori_loop(0, trips, far_body, 0)
        done_tiles = done_tiles + trips * width
        width //= 2

    @pl.when(qi > 0)
    def _():
        attend(qi - 1, 2, 0)

    @pl.when(qi == 0)
    def _():
        attend(0, 1, 1)

    for hh in range(A_HEADS):
        sl = slice(hh * A_HEAD_DIM, (hh + 1) * A_HEAD_DIM)
        den = jnp.sum(l_ref[hh], axis=-1, keepdims=True)
        o_ref[:, sl] = (acc_ref[hh] / den * g_ref[:, sl].astype(F32)).astype(BF16)


def _dsa_call(iq, iw, ikz, qa, ka, va, ga, rb, topk):
    bsz, seq, _ = qa.shape
    T = DSA_T
    nq = seq // T
    assert topk <= T and seq % T == 0
    once = pl.Buffered(1)
    return pl.pallas_call(
        functools.partial(_dsa_kernel, topk=topk),
        grid=(bsz, nq),
        in_specs=[
            pl.BlockSpec((None, IDX_PAIRS, T, LANES), lambda b, i: (b, 0, i, 0)),
            pl.BlockSpec((None, T, IDX_HEADS), lambda b, i: (b, i, 0)),
            pl.BlockSpec((None, 2, LANES, seq), lambda b, i: (b, 0, 0, 0), pipeline_mode=once),
            pl.BlockSpec((None, T, A_WIDTH), lambda b, i: (b, i, 0)),
            pl.BlockSpec((None, seq, A_WIDTH), lambda b, i: (b, 0, 0), pipeline_mode=once),
            pl.BlockSpec((None, seq, A_WIDTH), lambda b, i: (b, 0, 0), pipeline_mode=once),
            pl.BlockSpec((None, T, A_WIDTH), lambda b, i: (b, i, 0)),
            pl.BlockSpec((A_HEADS, T, 2 * T), lambda b, i: (0, 0, 0), pipeline_mode=once),
        ],
        out_specs=pl.BlockSpec((None, T, A_WIDTH), lambda b, i: (b, i, 0)),
        out_shape=jax.ShapeDtypeStruct((bsz, seq, A_WIDTH), BF16),
        scratch_shapes=[
            pltpu.VMEM((nq, T, T), F32),
            pltpu.VMEM((IDX_HEADS, T, LANES), F32),
            pltpu.VMEM((T, LANES), F32),
            pltpu.VMEM((T, LANES), F32),
            pltpu.VMEM((T, LANES), F32),
            pltpu.VMEM((T, LANES), F32),
            pltpu.VMEM((2, T, FAR_TILES * T), F32),
            pltpu.VMEM((A_HEADS, T, LANES), F32),
            pltpu.VMEM((A_HEADS, T, LANES), F32),
            pltpu.VMEM((A_HEADS, T, A_HEAD_DIM), F32),
        ],
        compiler_params=_cparams(("arbitrary", "arbitrary")),
        name="dsa",
    )(iq, iw, ikz, qa, ka, va, ga, rb)


def _swa_kernel(sink_ref, q_ref, kc_ref, vc_ref, kcs_ref, vcs_ref, kp_ref, vp_ref, kps_ref, vps_ref,
                g_ref, bias_ref, o_ref):
    first = pl.program_id(1) == 0
    blk = WINDOW
    tiles = B_HEADS // B_KV_HEADS // 2
    low_half = lax.broadcasted_iota(jnp.int32, (2 * blk, LANES), 1) < B_HEAD_DIM
    ones = jnp.ones((2 * blk, LANES), BF16)
    for sb in range(SWA_T // blk):
        rows = slice(sb * blk, (sb + 1) * blk)
        prev = slice((sb - 1) * blk, sb * blk)

        def window(cur_ref, prev_ref):
            before = prev_ref[...] if sb == 0 else cur_ref[prev, :]
            return jnp.concatenate([before, cur_ref[rows, :]], axis=0)

        k_nat, v_nat = window(kc_ref, kp_ref), window(vc_ref, vp_ref)
        k_swp, v_swp = window(kcs_ref, kps_ref), window(vcs_ref, vps_ref)
        for g in range(B_KV_HEADS):
            q = jnp.concatenate(
                [q_ref[rows, (tiles * g + i) * LANES:(tiles * g + i + 1) * LANES] for i in range(tiles)], axis=0)
            res = None
            for e in range(2):
                keep = low_half if e == 0 else jnp.logical_not(low_half)
                k_src, v_src = (k_nat, v_nat) if g == e else (k_swp, v_swp)
                kk = jnp.where(keep, k_src, jnp.zeros_like(k_src))
                vv = jnp.where(keep, v_src, jnp.zeros_like(v_src))
                logits = _dot_nt(q, kk) + bias_ref[2 * g + e]
                l_prev, l_cur = logits[:, :blk], logits[:, blk:]
                if sb == 0:
                    l_prev = jnp.where(first, NEG_BIG, l_prev)
                sink = jnp.concatenate(
                    [jnp.full((blk, blk), sink_ref[2 * (tiles * g + i) + e], F32) for i in range(tiles)], axis=0)
                m = jnp.max(jnp.maximum(l_prev, l_cur), axis=-1, keepdims=True)
                m = jnp.maximum(jnp.broadcast_to(m, sink.shape), sink)
                p = jnp.concatenate([jnp.exp(l_prev - m).astype(BF16), jnp.exp(l_cur - m).astype(BF16)], axis=1)
                pv = _dot(p, jnp.concatenate([vv, ones], axis=1))
                part = pv[:, :LANES] / (pv[:, LANES:] + jnp.exp(sink - m))
                res = part if res is None else res + part
            for i in range(tiles):
                cols = slice((tiles * g + i) * LANES, (tiles * g + i + 1) * LANES)
                o_ref[rows, cols] = (res[i * blk:(i + 1) * blk] * g_ref[rows, cols].astype(F32)).astype(BF16)


def _swa_call(sinks, qb, kb, vb, kbs, vbs, gb, bias):
    bsz, seq, _ = qb.shape
    T = SWA_T
    per = T // WINDOW
    cur = lambda b, i: (b, i, 0)
    prev = lambda b, i: (b, jnp.maximum(i * per - 1, 0), 0)
    kv_cur = pl.BlockSpec((None, T, B_KV_WIDTH), cur)
    kv_prev = pl.BlockSpec((None, WINDOW, B_KV_WIDTH), prev)
    return pl.pallas_call(
        _swa_kernel,
        grid=(bsz, seq // T),
        in_specs=[
            pl.BlockSpec(memory_space=pltpu.SMEM),
            pl.BlockSpec((None, T, B_WIDTH), cur),
            kv_cur, kv_cur, kv_cur, kv_cur,
            kv_prev, kv_prev, kv_prev, kv_prev,
            pl.BlockSpec((None, T, B_WIDTH), cur),
            pl.BlockSpec(bias.shape, lambda b, i: (0, 0, 0)),
        ],
        out_specs=pl.BlockSpec((None, T, B_WIDTH), cur),
        out_shape=jax.ShapeDtypeStruct((bsz, seq, B_WIDTH), BF16),
        compiler_params=_cparams(("parallel", "parallel")),
        name="swa",
    )(sinks, qb, kb, vb, kbs, vbs, kb, vb, kbs, vbs, gb, bias)


def _mem_kernel(q_ref, k_ref, v_ref, g_ref, o_ref):
    for hh in range(C_HEADS):
        sl = slice(hh * C_HEAD_DIM, (hh + 1) * C_HEAD_DIM)
        s = _dot_nt(q_ref[:, sl], k_ref[:, sl])
        m = jnp.max(s, axis=-1, keepdims=True)
        p = jnp.exp(s - m)
        den = jnp.sum(p, axis=-1, keepdims=True)
        o = _dot(p.astype(BF16), v_ref[:, sl]) / den
        o_ref[:, sl] = (o * g_ref[:, sl].astype(F32)).astype(BF16)


def _mem_call(qc, kc, vc, gc):
    bsz, seq, _ = qc.shape
    mlen = kc.shape[1]
    T = SWA_T
    cur = lambda b, i: (b, i, 0)
    whole = lambda b, i: (b, 0, 0)
    return pl.pallas_call(
        _mem_kernel,
        grid=(bsz, seq // T),
        in_specs=[
            pl.BlockSpec((None, T, C_WIDTH), cur),
            pl.BlockSpec((None, mlen, C_WIDTH), whole),
            pl.BlockSpec((None, mlen, C_WIDTH), whole),
            pl.BlockSpec((None, T, C_WIDTH), cur),
        ],
        out_specs=pl.BlockSpec((None, T, C_WIDTH), cur),
        out_shape=jax.ShapeDtypeStruct((bsz, seq, C_WIDTH), BF16),
        compiler_params=_cparams(("parallel", "parallel")),
        name="mem_attn",
    )(qc, kc, vc, gc)


def _final_kernel(x_ref, g_ref, oa_ref, ob_ref, oc_ref, wma_ref, wmb_ref, wmc_ref,
                  ba_ref, bb_ref, bc_ref, wua_ref, wub_ref, wuc_ref, wop_ref, wol_ref, out_ref,
                  h_ref, acc_ref, m_ref):
    c = pl.program_id(1)
    cur_slot = lax.rem(c, 2)

    @pl.when(c == 0)
    def _():
        h_ref[...] = _rms_rows(x_ref[...], g_ref[...]).astype(BF16)
        acc_ref[...] = jnp.zeros(acc_ref.shape, F32)
        m_ref[1] = jnp.zeros(m_ref.shape[1:], BF16)

    h = h_ref[...]

    def branch(o_ref, wm_ref, b_ref, wu_ref):
        gate = 1.0 / (1.0 + jnp.exp(-(_dot_nt(h, wm_ref[...]) + b_ref[...])))
        return gate * _dot(o_ref[...], wu_ref[...])

    acc_ref[...] += _dot(m_ref[1 - cur_slot], wop_ref[...])
    merged = (branch(oa_ref, wma_ref, ba_ref, wua_ref) + branch(ob_ref, wmb_ref, bb_ref, wub_ref)
              + branch(oc_ref, wmc_ref, bc_ref, wuc_ref))
    m_ref[cur_slot] = merged.astype(BF16)

    @pl.when(c == pl.num_programs(1) - 1)
    def _():
        out_ref[...] = x_ref[...] + acc_ref[...] + _dot(m_ref[cur_slot], wol_ref[...])


def _final_call(x2d, g, oa, ob, oc, wmix, gate_bias, wua, wub, wuc, wo):
    n, d = x2d.shape
    tm = FINAL_ROWS
    nch = FINAL_NCHUNK
    cw = d // nch
    row = lambda i, c: (i, 0)

    prv = lambda c: jnp.maximum(c - 1, 0)

    def col(br):
        return lambda i, c: (0, br * nch + c)

    def wrow(br):
        return lambda i, c: (br * nch + c, 0)

    return pl.pallas_call(
        _final_kernel,
        grid=(n // tm, nch),
        in_specs=[
            pl.BlockSpec((tm, d), row),
            pl.BlockSpec((1, d), lambda i, c: (0, 0)),
            pl.BlockSpec((tm, A_WIDTH), row),
            pl.BlockSpec((tm, B_WIDTH), row),
            pl.BlockSpec((tm, C_WIDTH), row),
            pl.BlockSpec((cw, d), wrow(0)), pl.BlockSpec((cw, d), wrow(1)), pl.BlockSpec((cw, d), wrow(2)),
            pl.BlockSpec((1, cw), col(0)), pl.BlockSpec((1, cw), col(1)), pl.BlockSpec((1, cw), col(2)),
            pl.BlockSpec((A_WIDTH, cw), lambda i, c: (0, c)),
            pl.BlockSpec((B_WIDTH, cw), lambda i, c: (0, c)),
            pl.BlockSpec((C_WIDTH, cw), lambda i, c: (0, c)),
            pl.BlockSpec((cw, d), lambda i, c: (prv(c), 0)),
            pl.BlockSpec((cw, d), lambda i, c: (nch - 1, 0)),
        ],
        out_specs=pl.BlockSpec((tm, d), row),
        out_shape=jax.ShapeDtypeStruct((n, d), F32),
        scratch_shapes=[pltpu.VMEM((tm, d), BF16), pltpu.VMEM((tm, d), F32), pltpu.VMEM((2, tm, cw), BF16)],
        compiler_params=_cparams(("parallel", "arbitrary")),
        name="merge_out",
    )(x2d, g, oa, ob, oc, wmix, wmix, wmix, gate_bias, gate_bias, gate_bias, wua, wub, wuc, wo, wo)


def _w_in_groups(d):
    sizes = (A_WIDTH, A_KV_RANK, IDX_HEADS * IDX_DIM, IDX_DIM, IDX_HEADS, A_WIDTH,
             B_WIDTH, B_KV_WIDTH, B_KV_WIDTH, B_WIDTH, C_WIDTH, C_WIDTH, N_BRANCH * d)
    cuts = np.cumsum((0,) + sizes).tolist()
    (aq, ackv, iq, ik, iw, ag, bq, bk, bv, bg, cq, cg, mix) = [
        (cuts[i], cuts[i + 1]) for i in range(len(sizes))]
    return [
        [(aq[0], ackv[1])],
        [(iq[0], ik[1]), LANES - IDX_DIM, iw, LANES - IDX_HEADS],
        [ag, bg, cg],
        [(bq[0], bv[1]), cq,
         (bk[0] + B_HEAD_DIM, bk[1]), (bk[0], bk[0] + B_HEAD_DIM),
         (bv[0] + B_HEAD_DIM, bv[1]), (bv[0], bv[0] + B_HEAD_DIM)],
        [mix],
    ]


def _group_width(group):
    return sum(p if isinstance(p, int) else p[1] - p[0] for p in group)


def _regroup_kernel(w_ref, *out_refs, groups):
    for o_ref, group in zip(out_refs, groups):
        row = 0
        for part in group:
            if isinstance(part, int):
                o_ref[row:row + part, :] = jnp.zeros((part, o_ref.shape[1]), BF16)
                row += part
            else:
                o_ref[row:row + part[1] - part[0], :] = w_ref[part[0]:part[1], :].astype(BF16)
                row += part[1] - part[0]


def _regroup_w_in(w_in_t):
    cols, d = w_in_t.shape
    groups = _w_in_groups(d)
    lanes = REGROUP_LANES
    return pl.pallas_call(
        functools.partial(_regroup_kernel, groups=groups),
        grid=(d // lanes,),
        in_specs=[pl.BlockSpec((cols, lanes), lambda i: (0, i))],
        out_specs=[pl.BlockSpec((_group_width(g), lanes), lambda i: (0, i)) for g in groups],
        out_shape=[jax.ShapeDtypeStruct((_group_width(g), d), BF16) for g in groups],
        compiler_params=_cparams(("parallel",)),
        name="regroup_w_in",
    )(w_in_t)


def _layer(x, mem, norm_g, w_in, kv_norm_g, w_kv_up, idx_k_ln_g, idx_k_ln_b, q_norm_a, k_norm_a,
           q_norm_b, k_norm_b, sinks_b, mem_norm_g, w_mem_kv, q_norm_c, k_norm_c,
           w_up_a, w_up_b, w_up_c, gate_bias, w_o, rel_bias):
    bsz, seq, d = x.shape
    mlen = mem.shape[1]
    n = bsz * seq
    tm = ROW_TILE
    topk = min(TOPK_MAX, seq // 4)
    x2d = x.reshape(n, d)
    row2 = lambda v: v.reshape(1, -1)

    w_grp_a, w_grp_i, w_grp_g, w_grp_bc, w_mix = _regroup_w_in(jnp.swapaxes(w_in, 0, 1))
    g_x = row2(norm_g)

    qa, ka, va = _proj_call(
        _proj_a_kernel, "proj_a", x2d, tm,
        [g_x, w_grp_a, row2(kv_norm_g), w_kv_up.astype(BF16), row2(q_norm_a), row2(k_norm_a)],
        [jax.ShapeDtypeStruct((n, A_WIDTH), BF16)] * 3,
        [_row_spec(tm, A_WIDTH)] * 3)

    per_b = seq // tm
    iq, ik, iw = _proj_call(
        _proj_i_kernel, "proj_i", x2d, tm,
        [g_x, w_grp_i, row2(idx_k_ln_g), row2(idx_k_ln_b)],
        [jax.ShapeDtypeStruct((bsz, IDX_PAIRS, seq, LANES), BF16),
         jax.ShapeDtypeStruct((n, IDX_DIM), BF16),
         jax.ShapeDtypeStruct((n, IDX_HEADS), F32)],
        [pl.BlockSpec((None, IDX_PAIRS, tm, LANES), lambda i: (i // per_b, 0, i % per_b, 0)),
         _row_spec(tm, IDX_DIM), _row_spec(tm, IDX_HEADS)])

    ga, gb, gc = _proj_call(
        _proj_g_kernel, "proj_g", x2d, tm, [g_x, w_grp_g],
        [jax.ShapeDtypeStruct((n, A_WIDTH), BF16), jax.ShapeDtypeStruct((n, B_WIDTH), BF16),
         jax.ShapeDtypeStruct((n, C_WIDTH), BF16)],
        [_row_spec(tm, A_WIDTH), _row_spec(tm, B_WIDTH), _row_spec(tm, C_WIDTH)])

    bc_widths = (B_WIDTH, B_KV_WIDTH, B_KV_WIDTH, C_WIDTH, B_KV_WIDTH, B_KV_WIDTH)
    qb, kb, vb, qc, kbs, vbs = _proj_call(
        _proj_bc_kernel, "proj_bc", x2d, tm,
        [g_x, w_grp_bc, row2(jnp.tile(q_norm_b, 2)), row2(jnp.tile(k_norm_b, 2)), row2(q_norm_c)],
        [jax.ShapeDtypeStruct((n, w), BF16) for w in bc_widths],
        [_row_spec(tm, w) for w in bc_widths])

    mrows = bsz * mlen
    kc, vc = _proj_call(
        _proj_mem_kernel, "proj_mem", mem.reshape(mrows, d), min(tm, mrows),
        [row2(mem_norm_g), w_mem_kv.astype(BF16), row2(k_norm_c)],
        [jax.ShapeDtypeStruct((mrows, C_WIDTH), BF16)] * 2,
        [_row_spec(min(tm, mrows), C_WIDTH)] * 2)

    T = DSA_T
    i_idx = np.arange(T)[:, None]
    dist_a = T + i_idx - np.arange(2 * T)[None, :]
    rb_a = _bias_tiles(rel_bias, _t5_bucket_np(dist_a), 0, A_HEADS, NUM_BUCKETS - 1, LOG2E)
    i_idx = np.arange(WINDOW)[:, None]
    dist_b = WINDOW + i_idx - np.arange(2 * WINDOW)[None, :]
    bucket_b = np.where((dist_b >= 0) & (dist_b < WINDOW), _t5_bucket_np(dist_b), -1).astype(np.int32)
    bias_b = _bias_tiles(rel_bias, bucket_b, A_HEADS, B_HEADS, None, 1.0)
    tiles_b = B_HEADS // B_KV_HEADS // 2
    slot_heads = [2 * (tiles_b * g + i) + e for g in range(B_KV_HEADS) for e in range(2) for i in range(tiles_b)]
    bias_b = bias_b[np.array(slot_heads)].reshape(2 * B_KV_HEADS, tiles_b * WINDOW, 2 * WINDOW)

    r3 = lambda v, w: v.reshape(bsz, seq, w)
    ikt = jnp.swapaxes(ik.reshape(bsz, seq, IDX_DIM), 1, 2)
    zeros = jnp.zeros_like(ikt)
    ikz = jnp.stack([jnp.concatenate([ikt, zeros], axis=1), jnp.concatenate([zeros, ikt], axis=1)], axis=1)
    oa = _dsa_call(iq, iw.reshape(bsz, seq, IDX_HEADS), ikz,
                   r3(qa, A_WIDTH), r3(ka, A_WIDTH),
                   r3(va, A_WIDTH), r3(ga, A_WIDTH), rb_a, topk)
    ob = _swa_call(sinks_b, r3(qb, B_WIDTH), r3(kb, B_KV_WIDTH), r3(vb, B_KV_WIDTH),
                   r3(kbs, B_KV_WIDTH), r3(vbs, B_KV_WIDTH), r3(gb, B_WIDTH), bias_b)
    oc = _mem_call(r3(qc, C_WIDTH), kc.reshape(bsz, mlen, C_WIDTH), vc.reshape(bsz, mlen, C_WIDTH),
                   r3(gc, C_WIDTH))

    out = _final_call(x2d, g_x, oa.reshape(n, A_WIDTH), ob.reshape(n, B_WIDTH), oc.reshape(n, C_WIDTH),
                      w_mix, row2(gate_bias), w_up_a.astype(BF16), w_up_b.astype(BF16),
                      w_up_c.astype(BF16), w_o.astype(BF16))
    return out.reshape(bsz, seq, d)


def kernel(x, mem, norm_g, w_in, kv_norm_g, w_kv_up, idx_k_ln_g, idx_k_ln_b, q_norm_a, k_norm_a, q_norm_b, k_norm_b, sinks_b, mem_norm_g, w_mem_kv, q_norm_c, k_norm_c, w_up_a, w_up_b, w_up_c, gate_bias, w_o, rel_bias):
    for l in range(norm_g.shape[0]):
        x = _layer(x, mem, norm_g[l], w_in[l], kv_norm_g[l], w_kv_up[l], idx_k_ln_g[l], idx_k_ln_b[l],
                   q_norm_a[l], k_norm_a[l], q_norm_b[l], k_norm_b[l], sinks_b[l], mem_norm_g[l],
                   w_mem_kv[l], q_norm_c[l], k_norm_c[l], w_up_a[l], w_up_b[l], w_up_c[l],
                   gate_bias[l], w_o[l], rel_bias)
    return x
```

```python
import functools
import math

import numpy as np
import jax
import jax.numpy as jnp
from jax import lax
from jax.experimental import pallas as pl
from jax.experimental.pallas import tpu as pltpu

F32 = jnp.float32
BF16 = jnp.bfloat16

EPS = 1e-6
A_HEADS, A_HEAD_DIM, A_KV_RANK = 6, 128, 256
IDX_HEADS, IDX_DIM, TOPK_MAX = 16, 64, 256
IDX_PAIRS = IDX_HEADS // 2
B_HEADS, B_KV_HEADS, B_HEAD_DIM, WINDOW = 12, 2, 64, 128
C_HEADS, C_HEAD_DIM = 4, 128
NUM_BUCKETS, MAX_DISTANCE = 32, 128
N_BRANCH = 3
A_WIDTH = A_HEADS * A_HEAD_DIM
B_WIDTH = B_HEADS * B_HEAD_DIM
B_KV_WIDTH = B_KV_HEADS * B_HEAD_DIM
C_WIDTH = C_HEADS * C_HEAD_DIM

LANES = 128
VMEM_LIMIT = 56 * 1024 * 1024
NEG_BIG = -1e30

ROW_TILE = 512
DSA_T = 256
SWA_T = 512
FINAL_ROWS = 512
FINAL_NCHUNK = 4
COUNT_ROWS = 128
REGROUP_LANES = 256
COUNT_UNROLL = 4
FAR_TILES = 4
LOG2E = math.log2(math.e)


def _cparams(sem):
    return pltpu.CompilerParams(dimension_semantics=sem, vmem_limit_bytes=VMEM_LIMIT)


def _dot(a, b):
    return jnp.dot(a, b, preferred_element_type=F32)


def _dot_nt(a, b):
    return lax.dot_general(a, b, (((1,), (1,)), ((), ())), preferred_element_type=F32)


def _rms_rows(x, g):
    ms = jnp.mean(x * x, axis=-1, keepdims=True)
    return x * lax.rsqrt(ms + EPS) * g


def _headnorm128(seg, g):
    ms = jnp.mean(seg * seg, axis=-1, keepdims=True)
    return seg * lax.rsqrt(ms + EPS) * g


def _headnorm64_pair(seg, g2):
    sq = seg * seg
    low = lax.broadcasted_iota(jnp.int32, seg.shape, 1) < B_HEAD_DIM
    s_all = jnp.sum(sq, axis=-1, keepdims=True)
    s_lo = jnp.sum(jnp.where(low, sq, 0.0), axis=-1, keepdims=True)
    ms = jnp.where(low, s_lo, s_all - s_lo) * (1.0 / B_HEAD_DIM)
    return seg * lax.rsqrt(ms + EPS) * g2


def _silu(y):
    return y / (1.0 + jnp.exp(-y))


def _proj_a_kernel(x_ref, g_ref, w_ref, kvg_ref, wkv_ref, qg_ref, kg_ref, qa_ref, ka_ref, va_ref):
    h = _rms_rows(x_ref[...], g_ref[...]).astype(BF16)
    y = _dot_nt(h, w_ref[...])
    scale = A_HEAD_DIM ** -0.5 * LOG2E
    for hh in range(A_HEADS):
        sl = slice(hh * A_HEAD_DIM, (hh + 1) * A_HEAD_DIM)
        qa_ref[:, sl] = (_headnorm128(y[:, sl], qg_ref[...]) * scale).astype(BF16)
    ckv = _rms_rows(y[:, A_WIDTH:A_WIDTH + A_KV_RANK], kvg_ref[...]).astype(BF16)
    kv = _dot(ckv, wkv_ref[...])
    for hh in range(A_HEADS):
        sl = slice(hh * A_HEAD_DIM, (hh + 1) * A_HEAD_DIM)
        ka_ref[:, sl] = _headnorm128(kv[:, sl], kg_ref[...]).astype(BF16)
    va_ref[...] = kv[:, A_WIDTH:].astype(BF16)


def _proj_i_kernel(x_ref, g_ref, w_ref, lng_ref, lnb_ref, iq_ref, ik_ref, iw_ref):
    h = _rms_rows(x_ref[...], g_ref[...]).astype(BF16)
    y = _dot_nt(h, w_ref[...])
    k0 = IDX_HEADS * IDX_DIM
    for p in range(IDX_PAIRS):
        iq_ref[p] = y[:, p * LANES:(p + 1) * LANES].astype(BF16)
    ik = y[:, k0:k0 + IDX_DIM]
    mu = jnp.mean(ik, axis=-1, keepdims=True)
    d = ik - mu
    var = jnp.mean(d * d, axis=-1, keepdims=True)
    ik_ref[...] = (d * lax.rsqrt(var + EPS) * lng_ref[...] + lnb_ref[...]).astype(BF16)
    w0 = k0 + LANES
    iw_ref[...] = y[:, w0:w0 + IDX_HEADS] * (IDX_HEADS ** -0.5) * (IDX_DIM ** -0.5)


def _proj_g_kernel(x_ref, g_ref, w_ref, ga_ref, gb_ref, gc_ref):
    h = _rms_rows(x_ref[...], g_ref[...]).astype(BF16)
    y = _silu(_dot_nt(h, w_ref[...]))
    ga_ref[...] = y[:, :A_WIDTH].astype(BF16)
    gb_ref[...] = y[:, A_WIDTH:A_WIDTH + B_WIDTH].astype(BF16)
    gc_ref[...] = y[:, A_WIDTH + B_WIDTH:].astype(BF16)


def _proj_bc_kernel(x_ref, g_ref, w_ref, qbg_ref, kbg_ref, qcg_ref,
                    qb_ref, kb_ref, vb_ref, qc_ref, kbs_ref, vbs_ref):
    h = _rms_rows(x_ref[...], g_ref[...]).astype(BF16)
    y = _dot_nt(h, w_ref[...])
    sb = B_HEAD_DIM ** -0.5
    for p in range(B_WIDTH // LANES):
        sl = slice(p * LANES, (p + 1) * LANES)
        qb_ref[:, sl] = (_headnorm64_pair(y[:, sl], qbg_ref[...]) * sb).astype(BF16)
    k0 = B_WIDTH
    kb_ref[...] = _headnorm64_pair(y[:, k0:k0 + B_KV_WIDTH], kbg_ref[...]).astype(BF16)
    vb_ref[...] = y[:, k0 + B_KV_WIDTH:k0 + 2 * B_KV_WIDTH].astype(BF16)
    s0 = k0 + 2 * B_KV_WIDTH + C_WIDTH
    kbs_ref[...] = _headnorm64_pair(y[:, s0:s0 + B_KV_WIDTH], kbg_ref[...]).astype(BF16)
    vbs_ref[...] = y[:, s0 + B_KV_WIDTH:s0 + 2 * B_KV_WIDTH].astype(BF16)
    c0 = k0 + 2 * B_KV_WIDTH
    sc = C_HEAD_DIM ** -0.5
    for hh in range(C_HEADS):
        sl = slice(hh * C_HEAD_DIM, (hh + 1) * C_HEAD_DIM)
        qc_ref[:, sl] = (_headnorm128(y[:, c0 + hh * C_HEAD_DIM:c0 + (hh + 1) * C_HEAD_DIM],
                                      qcg_ref[...]) * sc).astype(BF16)


def _proj_mem_kernel(x_ref, g_ref, w_ref, kg_ref, kc_ref, vc_ref):
    h = _rms_rows(x_ref[...], g_ref[...]).astype(BF16)
    y = _dot(h, w_ref[...])
    for hh in range(C_HEADS):
        sl = slice(hh * C_HEAD_DIM, (hh + 1) * C_HEAD_DIM)
        kc_ref[:, sl] = _headnorm128(y[:, sl], kg_ref[...]).astype(BF16)
    vc_ref[...] = y[:, C_WIDTH:].astype(BF16)


def _row_spec(tm, cols):
    return pl.BlockSpec((tm, cols), lambda i: (i, 0))


def _full_spec(shape):
    nd = len(shape)
    return pl.BlockSpec(shape, lambda i: (0,) * nd)


def _proj_call(kernel_fn, name, x2d, tm, consts, out_shapes, out_specs):
    n, d = x2d.shape
    in_specs = [_row_spec(tm, d)] + [_full_spec(c.shape) for c in consts]
    return pl.pallas_call(
        kernel_fn,
        grid=(n // tm,),
        in_specs=in_specs,
        out_specs=out_specs,
        out_shape=out_shapes,
        compiler_params=_cparams(("parallel",)),
        name=name,
    )(x2d, *consts)


def _t5_bucket_np(dist):
    n = np.maximum(dist, 0)
    max_exact = NUM_BUCKETS // 2
    nf = np.maximum(n, 1).astype(np.float32)
    large = max_exact + (np.log(nf / max_exact) / math.log(MAX_DISTANCE / max_exact)
                         * (NUM_BUCKETS - max_exact)).astype(np.int32)
    large = np.minimum(large, NUM_BUCKETS - 1)
    return np.where(n < max_exact, n, large).astype(np.int32)


def _bias_tile_kernel(tab_ref, bucket_ref, o_ref, *, head0, shift_bucket, scale):
    h = pl.program_id(0) + head0
    bucket = bucket_ref[...]
    acc = jnp.zeros(bucket.shape, F32)
    for b in range(NUM_BUCKETS):
        acc = jnp.where(bucket == b, tab_ref[b, h], acc)
    if shift_bucket is not None:
        acc = acc - tab_ref[shift_bucket, h]
    o_ref[...] = jnp.where(bucket < 0, NEG_BIG, acc * scale)


def _bias_tiles(rel_bias, bucket_np, head0, nheads, shift_bucket, scale):
    r, c = bucket_np.shape
    return pl.pallas_call(
        functools.partial(_bias_tile_kernel, head0=head0, shift_bucket=shift_bucket, scale=scale),
        grid=(nheads,),
        in_specs=[pl.BlockSpec(memory_space=pltpu.SMEM), pl.BlockSpec((r, c), lambda h: (0, 0))],
        out_specs=pl.BlockSpec((None, r, c), lambda h: (h, 0, 0)),
        out_shape=jax.ShapeDtypeStruct((nheads, r, c), F32),
        compiler_params=_cparams(("arbitrary",)),
        name="bias_tiles",
    )(rel_bias, jnp.asarray(bucket_np))


def _dsa_kernel(iq_ref, iw_ref, ikz_ref, q_ref, k_ref, v_ref, g_ref, rb_ref, o_ref,
                sc_ref, wb_ref, lo_ref, hi_ref, mid_ref, clo_ref, s_ref, m_ref, l_ref, acc_ref, *, topk):
    T = DSA_T
    qi = pl.program_id(1)
    kf = float(topk)

    def lanes(col):
        return jnp.broadcast_to(col, (T, LANES))

    def halves(tile):
        return [tile[:, c * LANES:(c + 1) * LANES] for c in range(T // LANES)]

    for hh in range(IDX_HEADS):
        wb_ref[hh] = lanes(iw_ref[:, hh:hh + 1])

    def score_chunk(j, diag):
        keys = pl.ds(pl.multiple_of(j * T, T), T)
        accs = [jnp.zeros((T, LANES), F32) for _ in range(T // LANES)]
        for hh in range(IDX_HEADS):
            s = _dot(iq_ref[hh // 2], ikz_ref[hh % 2, :, keys])
            w = wb_ref[hh]
            accs = [a + w * jnp.maximum(sh, 0.0) for a, sh in zip(accs, halves(s))]
        if diag:
            row = lax.broadcasted_iota(jnp.int32, (T, LANES), 0)
            col = lax.broadcasted_iota(jnp.int32, (T, LANES), 1)
            causal = [col + c * LANES <= row for c in range(T // LANES)]
            lows = [jnp.where(cm, a, jnp.inf) for cm, a in zip(causal, accs)]
            accs = [jnp.where(cm, a, -jnp.inf) for cm, a in zip(causal, accs)]
        else:
            lows = accs
        for c, a in enumerate(accs):
            sc_ref[j, :, c * LANES:(c + 1) * LANES] = a
        return functools.reduce(jnp.minimum, lows), accs

    def p1_body(j, carry):
        mn, mxs = carry
        lo_c, his = score_chunk(j, False)
        return jnp.minimum(mn, lo_c), [jnp.maximum(a, b) for a, b in zip(mxs, his)]

    mn0 = jnp.full((T, LANES), jnp.inf, F32)
    mx0 = [jnp.full((T, LANES), -jnp.inf, F32) for _ in range(T // LANES)]

    def p1_pair(jj, carry):
        return p1_body(2 * jj + 1, p1_body(2 * jj, carry))

    n_pairs = lax.div(qi, 2)
    mn, mxs = lax.fori_loop(0, n_pairs, p1_pair, (mn0, mx0))
    mn, mxs = lax.fori_loop(2 * n_pairs, qi, p1_body, (mn, mxs))
    lo_c, his = score_chunk(qi, True)
    mxs = [jnp.maximum(a, b) for a, b in zip(mxs, his)]
    mn = lanes(jnp.min(jnp.minimum(mn, lo_c), axis=-1, keepdims=True))
    mx = lanes(jnp.max(functools.reduce(jnp.maximum, mxs), axis=-1, keepdims=True))
    class_floor = lanes(jnp.min(functools.reduce(jnp.minimum, mxs), axis=-1, keepdims=True))

    def count_lanes(rows, v, strict=False):
        def tile(j, c):
            for c0 in range(0, T, LANES):
                sh = sc_ref[j, rows, c0:c0 + LANES]
                hit = (sh > v) if strict else (sh >= v)
                c = c + jnp.where(hit, 1.0, 0.0)
            return c

        def group(g, c):
            for u in range(COUNT_UNROLL):
                c = tile(g * COUNT_UNROLL + u, c)
            return c

        n_groups = lax.div(qi + 1, COUNT_UNROLL)
        c = lax.fori_loop(0, n_groups, group, jnp.zeros((COUNT_ROWS, LANES), F32))
        return lax.fori_loop(n_groups * COUNT_UNROLL, qi + 1, tile, c)

    def row_total(c):
        return jnp.broadcast_to(jnp.sum(c, axis=-1, keepdims=True), c.shape)

    slabs = [slice(r0, r0 + COUNT_ROWS) for r0 in range(0, T, COUNT_ROWS)]
    n_causal = (qi * T + 1 + lax.broadcasted_iota(jnp.int32, (T, LANES), 0)).astype(F32)
    keep_all = n_causal <= kf
    lo0 = jnp.where(keep_all, mn, jnp.maximum(mn, class_floor))
    above_max = mx + jnp.maximum(jnp.abs(mx) * (2.0 ** -20), 1e-30)
    hi0 = jnp.where(keep_all, mn, above_max)
    lo_ref[...] = lo0
    hi_ref[...] = hi0
    mid_ref[...] = lo0 + 0.5 * (hi0 - lo0)
    clo_ref[...] = n_causal

    def bis_body(_):
        partial = [count_lanes(rows, mid_ref[rows]) for rows in slabs]
        widths = [hi_ref[rows] - lo_ref[rows] for rows in slabs]
        open_width = jnp.max(functools.reduce(jnp.maximum, widths))
        for rows, c in zip(slabs, partial):
            lo, hi, mid = lo_ref[rows], hi_ref[rows], mid_ref[rows]
            cnt = row_total(c)
            up = cnt >= kf
            lo = jnp.where(up, mid, lo)
            c_lo = jnp.where(up, cnt, clo_ref[rows])
            hi = jnp.where(up, hi, mid)
            nxt = lo + 0.5 * (hi - lo)
            stop = jnp.logical_or(c_lo == kf, jnp.logical_or(nxt <= lo, nxt >= hi))
            hi = jnp.where(stop, lo, hi)
            lo_ref[rows] = lo
            hi_ref[rows] = hi
            mid_ref[rows] = jnp.where(stop, lo, nxt)
            clo_ref[rows] = c_lo
        return open_width

    lax.while_loop(lambda w: w > 0.0, bis_body, jnp.max(hi0 - lo0))
    thr = lo_ref[...]

    tied_f = jnp.where(clo_ref[...] > kf, 1.0, 0.0)

    @pl.when(jnp.max(tied_f) > 0.0)
    def _():
        above = jnp.concatenate(
            [row_total(count_lanes(rows, thr[rows], strict=True)) for rows in slabs], axis=0)
        need = (kf - above)[:, :1]
        r = lax.broadcasted_iota(jnp.int32, (T, T), 0)
        c = lax.broadcasted_iota(jnp.int32, (T, T), 1)
        before = jnp.where(r < c, 1.0, 0.0).astype(BF16)
        thr_col = thr[:, :1]
        tied_col = tied_f[:, :1] > 0.0

        def body(j, seen):
            s = sc_ref[j]
            eq = jnp.where(s == thr_col, 1.0, 0.0)
            rank = seen + _dot(eq.astype(BF16), before)
            drop = jnp.logical_and(tied_col, jnp.logical_and(eq > 0.0, rank >= need))
            sc_ref[j] = jnp.where(drop, -jnp.inf, s)
            return seen + jnp.sum(eq, axis=-1, keepdims=True)

        lax.fori_loop(0, qi + 1, body, jnp.zeros((T, 1), F32))

    m_ref[...] = jnp.full(m_ref.shape, NEG_BIG, F32)
    l_ref[...] = jnp.zeros(l_ref.shape, F32)
    acc_ref[...] = jnp.zeros(acc_ref.shape, F32)

    def attend(j0, ntiles, band):
        sel = [jnp.where(sh >= thr, 0.0, NEG_BIG) for t in range(ntiles) for sh in halves(sc_ref[j0 + t])]
        start = pl.multiple_of(j0 * T, T)
        keys = pl.ds(start, ntiles * T)
        width = ntiles * T

        def logits_to_scratch(hh):
            sl = slice(hh * A_HEAD_DIM, (hh + 1) * A_HEAD_DIM)
            s_ref[hh % 2, :, :width] = _dot_nt(q_ref[:, sl], k_ref[keys, sl])

        logits_to_scratch(0)
        pending = None
        for hh in range(A_HEADS):
            sl = slice(hh * A_HEAD_DIM, (hh + 1) * A_HEAD_DIM)
            if hh + 1 < A_HEADS:
                logits_to_scratch(hh + 1)
            s = s_ref[hh % 2, :, :width]
            if band is not None:
                s = s + rb_ref[hh, :, band * T:(band + ntiles) * T]
            sh = [s[:, c * LANES:(c + 1) * LANES] + m for c, m in enumerate(sel)]
            m_prev = m_ref[hh]
            m_cur = jnp.max(functools.reduce(jnp.maximum, sh), axis=-1, keepdims=True)
            m_new = jnp.maximum(m_prev, lanes(m_cur))
            alpha = jnp.exp2(m_prev - m_new)
            ph = [jnp.exp2(x - m_new) for x in sh]
            l_ref[hh] = alpha * l_ref[hh] + functools.reduce(jnp.add, ph)
            p = jnp.concatenate([x.astype(BF16) for x in ph], axis=-1)
            acc_ref[hh] = alpha * acc_ref[hh]
            m_ref[hh] = m_new
            if pending is not None:
                acc_ref[pending[0]] += pending[1]
            pending = (hh, _dot(p, v_ref[keys, sl]))
        acc_ref[pending[0]] += pending[1]

    n_far = jnp.maximum(qi - 1, 0)
    done_tiles = 0
    width = FAR_TILES
    while width >= 1:
        trips = lax.div(n_far - done_tiles, width)

        def far_body(i, c, width=width, base=done_tiles):
            attend(base + i * width, width, None)
            return c

        lax.fori_loop(0, trips, far_body, 0)
        done_tiles = done_tiles + trips * width
        width //= 2

    @pl.when(qi > 0)
    def _():
        attend(qi - 1, 2, 0)

    @pl.when(qi == 0)
    def _():
        attend(0, 1, 1)

    for hh in range(A_HEADS):
        sl = slice(hh * A_HEAD_DIM, (hh + 1) * A_HEAD_DIM)
        den = jnp.sum(l_ref[hh], axis=-1, keepdims=True)
        o_ref[:, sl] = (acc_ref[hh] / den * g_ref[:, sl].astype(F32)).astype(BF16)


def _dsa_call(iq, iw, ikz, qa, ka, va, ga, rb, topk):
    bsz, seq, _ = qa.shape
    T = DSA_T
    nq = seq // T
    assert topk <= T and seq % T == 0
    once = pl.Buffered(1)
    return pl.pallas_call(
        functools.partial(_dsa_kernel, topk=topk),
        grid=(bsz, nq),
        in_specs=[
            pl.BlockSpec((None, IDX_PAIRS, T, LANES), lambda b, i: (b, 0, i, 0)),
            pl.BlockSpec((None, T, IDX_HEADS), lambda b, i: (b, i, 0)),
            pl.BlockSpec((None, 2, LANES, seq), lambda b, i: (b, 0, 0, 0), pipeline_mode=once),
            pl.BlockSpec((None, T, A_WIDTH), lambda b, i: (b, i, 0)),
            pl.BlockSpec((None, seq, A_WIDTH), lambda b, i: (b, 0, 0), pipeline_mode=once),
            pl.BlockSpec((None, seq, A_WIDTH), lambda b, i: (b, 0, 0), pipeline_mode=once),
            pl.BlockSpec((None, T, A_WIDTH), lambda b, i: (b, i, 0)),
            pl.BlockSpec((A_HEADS, T, 2 * T), lambda b, i: (0, 0, 0), pipeline_mode=once),
        ],
        out_specs=pl.BlockSpec((None, T, A_WIDTH), lambda b, i: (b, i, 0)),
        out_shape=jax.ShapeDtypeStruct((bsz, seq, A_WIDTH), BF16),
        scratch_shapes=[
            pltpu.VMEM((nq, T, T), F32),
            pltpu.VMEM((IDX_HEADS, T, LANES), F32),
            pltpu.VMEM((T, LANES), F32),
            pltpu.VMEM((T, LANES), F32),
            pltpu.VMEM((T, LANES), F32),
            pltpu.VMEM((T, LANES), F32),
            pltpu.VMEM((2, T, FAR_TILES * T), F32),
            pltpu.VMEM((A_HEADS, T, LANES), F32),
            pltpu.VMEM((A_HEADS, T, LANES), F32),
            pltpu.VMEM((A_HEADS, T, A_HEAD_DIM), F32),
        ],
        compiler_params=_cparams(("arbitrary", "arbitrary")),
        name="dsa",
    )(iq, iw, ikz, qa, ka, va, ga, rb)


def _swa_kernel(sink_ref, q_ref, kc_ref, vc_ref, kcs_ref, vcs_ref, kp_ref, vp_ref, kps_ref, vps_ref,
                g_ref, bias_ref, o_ref):
    first = pl.program_id(1) == 0
    blk = WINDOW
    tiles = B_HEADS // B_KV_HEADS // 2
    low_half = lax.broadcasted_iota(jnp.int32, (2 * blk, LANES), 1) < B_HEAD_DIM
    ones = jnp.ones((2 * blk, LANES), BF16)
    for sb in range(SWA_T // blk):
        rows = slice(sb * blk, (sb + 1) * blk)
        prev = slice((sb - 1) * blk, sb * blk)

        def window(cur_ref, prev_ref):
            before = prev_ref[...] if sb == 0 else cur_ref[prev, :]
            return jnp.concatenate([before, cur_ref[rows, :]], axis=0)

        k_nat, v_nat = window(kc_ref, kp_ref), window(vc_ref, vp_ref)
        k_swp, v_swp = window(kcs_ref, kps_ref), window(vcs_ref, vps_ref)
        for g in range(B_KV_HEADS):
            q = jnp.concatenate(
                [q_ref[rows, (tiles * g + i) * LANES:(tiles * g + i + 1) * LANES] for i in range(tiles)], axis=0)
            res = None
            for e in range(2):
                keep = low_half if e == 0 else jnp.logical_not(low_half)
                k_src, v_src = (k_nat, v_nat) if g == e else (k_swp, v_swp)
                kk = jnp.where(keep, k_src, jnp.zeros_like(k_src))
                vv = jnp.where(keep, v_src, jnp.zeros_like(v_src))
                logits = _dot_nt(q, kk) + bias_ref[2 * g + e]
                l_prev, l_cur = logits[:, :blk], logits[:, blk:]
                if sb == 0:
                    l_prev = jnp.where(first, NEG_BIG, l_prev)
                sink = jnp.concatenate(
                    [jnp.full((blk, blk), sink_ref[2 * (tiles * g + i) + e], F32) for i in range(tiles)], axis=0)
                m = jnp.max(jnp.maximum(l_prev, l_cur), axis=-1, keepdims=True)
                m = jnp.maximum(jnp.broadcast_to(m, sink.shape), sink)
                p = jnp.concatenate([jnp.exp(l_prev - m).astype(BF16), jnp.exp(l_cur - m).astype(BF16)], axis=1)
                pv = _dot(p, jnp.concatenate([vv, ones], axis=1))
                part = pv[:, :LANES] / (pv[:, LANES:] + jnp.exp(sink - m))
                res = part if res is None else res + part
            for i in range(tiles):
                cols = slice((tiles * g + i) * LANES, (tiles * g + i + 1) * LANES)
                o_ref[rows, cols] = (res[i * blk:(i + 1) * blk] * g_ref[rows, cols].astype(F32)).astype(BF16)


def _swa_call(sinks, qb, kb, vb, kbs, vbs, gb, bias):
    bsz, seq, _ = qb.shape
    T = SWA_T
    per = T // WINDOW
    cur = lambda b, i: (b, i, 0)
    prev = lambda b, i: (b, jnp.maximum(i * per - 1, 0), 0)
    kv_cur = pl.BlockSpec((None, T, B_KV_WIDTH), cur)
    kv_prev = pl.BlockSpec((None, WINDOW, B_KV_WIDTH), prev)
    return pl.pallas_call(
        _swa_kernel,
        grid=(bsz, seq // T),
        in_specs=[
            pl.BlockSpec(memory_space=pltpu.SMEM),
            pl.BlockSpec((None, T, B_WIDTH), cur),
            kv_cur, kv_cur, kv_cur, kv_cur,
            kv_prev, kv_prev, kv_prev, kv_prev,
            pl.BlockSpec((None, T, B_WIDTH), cur),
            pl.BlockSpec(bias.shape, lambda b, i: (0, 0, 0)),
        ],
        out_specs=pl.BlockSpec((None, T, B_WIDTH), cur),
        out_shape=jax.ShapeDtypeStruct((bsz, seq, B_WIDTH), BF16),
        compiler_params=_cparams(("parallel", "parallel")),
        name="swa",
    )(sinks, qb, kb, vb, kbs, vbs, kb, vb, kbs, vbs, gb, bias)


def _mem_kernel(q_ref, k_ref, v_ref, g_ref, o_ref):
    for hh in range(C_HEADS):
        sl = slice(hh * C_HEAD_DIM, (hh + 1) * C_HEAD_DIM)
        s = _dot_nt(q_ref[:, sl], k_ref[:, sl])
        m = jnp.max(s, axis=-1, keepdims=True)
        p = jnp.exp(s - m)
        den = jnp.sum(p, axis=-1, keepdims=True)
        o = _dot(p.astype(BF16), v_ref[:, sl]) / den
        o_ref[:, sl] = (o * g_ref[:, sl].astype(F32)).astype(BF16)


def _mem_call(qc, kc, vc, gc):
    bsz, seq, _ = qc.shape
    mlen = kc.shape[1]
    T = SWA_T
    cur = lambda b, i: (b, i, 0)
    whole = lambda b, i: (b, 0, 0)
    return pl.pallas_call(
        _mem_kernel,
        grid=(bsz, seq // T),
        in_specs=[
            pl.BlockSpec((None, T, C_WIDTH), cur),
            pl.BlockSpec((None, mlen, C_WIDTH), whole),
            pl.BlockSpec((None, mlen, C_WIDTH), whole),
            pl.BlockSpec((None, T, C_WIDTH), cur),
        ],
        out_specs=pl.BlockSpec((None, T, C_WIDTH), cur),
        out_shape=jax.ShapeDtypeStruct((bsz, seq, C_WIDTH), BF16),
        compiler_params=_cparams(("parallel", "parallel")),
        name="mem_attn",
    )(qc, kc, vc, gc)


def _final_kernel(x_ref, g_ref, oa_ref, ob_ref, oc_ref, wma_ref, wmb_ref, wmc_ref,
                  ba_ref, bb_ref, bc_ref, wua_ref, wub_ref, wuc_ref, wo_ref, out_ref, h_ref, acc_ref):
    c = pl.program_id(1)

    @pl.when(c == 0)
    def _():
        h_ref[...] = _rms_rows(x_ref[...], g_ref[...]).astype(BF16)
        acc_ref[...] = jnp.zeros(acc_ref.shape, F32)

    h = h_ref[...]

    def branch(o_ref, wm_ref, b_ref, wu_ref):
        gate = 1.0 / (1.0 + jnp.exp(-(_dot_nt(h, wm_ref[...]) + b_ref[...])))
        return gate * _dot(o_ref[...], wu_ref[...])

    merged = (branch(oa_ref, wma_ref, ba_ref, wua_ref) + branch(ob_ref, wmb_ref, bb_ref, wub_ref)
              + branch(oc_ref, wmc_ref, bc_ref, wuc_ref))
    acc_ref[...] += _dot(merged.astype(BF16), wo_ref[...])

    @pl.when(c == pl.num_programs(1) - 1)
    def _():
        out_ref[...] = x_ref[...] + acc_ref[...]


def _final_call(x2d, g, oa, ob, oc, wmix, gate_bias, wua, wub, wuc, wo):
    n, d = x2d.shape
    tm = FINAL_ROWS
    nch = FINAL_NCHUNK
    cw = d // nch
    row = lambda i, c: (i, 0)

    def col(br):
        return lambda i, c: (0, br * nch + c)

    def wrow(br):
        return lambda i, c: (br * nch + c, 0)

    return pl.pallas_call(
        _final_kernel,
        grid=(n // tm, nch),
        in_specs=[
            pl.BlockSpec((tm, d), row),
            pl.BlockSpec((1, d), lambda i, c: (0, 0)),
            pl.BlockSpec((tm, A_WIDTH), row),
            pl.BlockSpec((tm, B_WIDTH), row),
            pl.BlockSpec((tm, C_WIDTH), row),
            pl.BlockSpec((cw, d), wrow(0)), pl.BlockSpec((cw, d), wrow(1)), pl.BlockSpec((cw, d), wrow(2)),
            pl.BlockSpec((1, cw), col(0)), pl.BlockSpec((1, cw), col(1)), pl.BlockSpec((1, cw), col(2)),
            pl.BlockSpec((A_WIDTH, cw), lambda i, c: (0, c)),
            pl.BlockSpec((B_WIDTH, cw), lambda i, c: (0, c)),
            pl.BlockSpec((C_WIDTH, cw), lambda i, c: (0, c)),
            pl.BlockSpec((cw, d), lambda i, c: (c, 0)),
        ],
        out_specs=pl.BlockSpec((tm, d), row),
        out_shape=jax.ShapeDtypeStruct((n, d), F32),
        scratch_shapes=[pltpu.VMEM((tm, d), BF16), pltpu.VMEM((tm, d), F32)],
        compiler_params=_cparams(("parallel", "arbitrary")),
        name="merge_out",
    )(x2d, g, oa, ob, oc, wmix, wmix, wmix, gate_bias, gate_bias, gate_bias, wua, wub, wuc, wo)


def _w_in_groups(d):
    sizes = (A_WIDTH, A_KV_RANK, IDX_HEADS * IDX_DIM, IDX_DIM, IDX_HEADS, A_WIDTH,
             B_WIDTH, B_KV_WIDTH, B_KV_WIDTH, B_WIDTH, C_WIDTH, C_WIDTH, N_BRANCH * d)
    cuts = np.cumsum((0,) + sizes).tolist()
    (aq, ackv, iq, ik, iw, ag, bq, bk, bv, bg, cq, cg, mix) = [
        (cuts[i], cuts[i + 1]) for i in range(len(sizes))]
    return [
        [(aq[0], ackv[1])],
        [(iq[0], ik[1]), LANES - IDX_DIM, iw, LANES - IDX_HEADS],
        [ag, bg, cg],
        [(bq[0], bv[1]), cq,
         (bk[0] + B_HEAD_DIM, bk[1]), (bk[0], bk[0] + B_HEAD_DIM),
         (bv[0] + B_HEAD_DIM, bv[1]), (bv[0], bv[0] + B_HEAD_DIM)],
        [mix],
    ]


def _group_width(group):
    return sum(p if isinstance(p, int) else p[1] - p[0] for p in group)


def _regroup_kernel(w_ref, *out_refs, groups):
    for o_ref, group in zip(out_refs, groups):
        row = 0
        for part in group:
            if isinstance(part, int):
                o_ref[row:row + part, :] = jnp.zeros((part, o_ref.shape[1]), BF16)
                row += part
            else:
                o_ref[row:row + part[1] - part[0], :] = w_ref[part[0]:part[1], :].astype(BF16)
                row += part[1] - part[0]


def _regroup_w_in(w_in_t):
    cols, d = w_in_t.shape
    groups = _w_in_groups(d)
    lanes = REGROUP_LANES
    return pl.pallas_call(
        functools.partial(_regroup_kernel, groups=groups),
        grid=(d // lanes,),
        in_specs=[pl.BlockSpec((cols, lanes), lambda i: (0, i))],
        out_specs=[pl.BlockSpec((_group_width(g), lanes), lambda i: (0, i)) for g in groups],
        out_shape=[jax.ShapeDtypeStruct((_group_width(g), d), BF16) for g in groups],
        compiler_params=_cparams(("parallel",)),
        name="regroup_w_in",
    )(w_in_t)


def _layer(x, mem, norm_g, w_in, kv_norm_g, w_kv_up, idx_k_ln_g, idx_k_ln_b, q_norm_a, k_norm_a,
           q_norm_b, k_norm_b, sinks_b, mem_norm_g, w_mem_kv, q_norm_c, k_norm_c,
           w_up_a, w_up_b, w_up_c, gate_bias, w_o, rel_bias):
    bsz, seq, d = x.shape
    mlen = mem.shape[1]
    n = bsz * seq
    tm = ROW_TILE
    topk = min(TOPK_MAX, seq // 4)
    x2d = x.reshape(n, d)
    row2 = lambda v: v.reshape(1, -1)

    w_grp_a, w_grp_i, w_grp_g, w_grp_bc, w_mix = _regroup_w_in(jnp.swapaxes(w_in, 0, 1))
    g_x = row2(norm_g)

    qa, ka, va = _proj_call(
        _proj_a_kernel, "proj_a", x2d, tm,
        [g_x, w_grp_a, row2(kv_norm_g), w_kv_up.astype(BF16), row2(q_norm_a), row2(k_norm_a)],
        [jax.ShapeDtypeStruct((n, A_WIDTH), BF16)] * 3,
        [_row_spec(tm, A_WIDTH)] * 3)

    per_b = seq // tm
    iq, ik, iw = _proj_call(
        _proj_i_kernel, "proj_i", x2d, tm,
        [g_x, w_grp_i, row2(idx_k_ln_g), row2(idx_k_ln_b)],
        [jax.ShapeDtypeStruct((bsz, IDX_PAIRS, seq, LANES), BF16),
         jax.ShapeDtypeStruct((n, IDX_DIM), BF16),
         jax.ShapeDtypeStruct((n, IDX_HEADS), F32)],
        [pl.BlockSpec((None, IDX_PAIRS, tm, LANES), lambda i: (i // per_b, 0, i % per_b, 0)),
         _row_spec(tm, IDX_DIM), _row_spec(tm, IDX_HEADS)])

    ga, gb, gc = _proj_call(
        _proj_g_kernel, "proj_g", x2d, tm, [g_x, w_grp_g],
        [jax.ShapeDtypeStruct((n, A_WIDTH), BF16), jax.ShapeDtypeStruct((n, B_WIDTH), BF16),
         jax.ShapeDtypeStruct((n, C_WIDTH), BF16)],
        [_row_spec(tm, A_WIDTH), _row_spec(tm, B_WIDTH), _row_spec(tm, C_WIDTH)])

    bc_widths = (B_WIDTH, B_KV_WIDTH, B_KV_WIDTH, C_WIDTH, B_KV_WIDTH, B_KV_WIDTH)
    qb, kb, vb, qc, kbs, vbs = _proj_call(
        _proj_bc_kernel, "proj_bc", x2d, tm,
        [g_x, w_grp_bc, row2(jnp.tile(q_norm_b, 2)), row2(jnp.tile(k_norm_b, 2)), row2(q_norm_c)],
        [jax.ShapeDtypeStruct((n, w), BF16) for w in bc_widths],
        [_row_spec(tm, w) for w in bc_widths])

    mrows = bsz * mlen
    kc, vc = _proj_call(
        _proj_mem_kernel, "proj_mem", mem.reshape(mrows, d), min(tm, mrows),
        [row2(mem_norm_g), w_mem_kv.astype(BF16), row2(k_norm_c)],
        [jax.ShapeDtypeStruct((mrows, C_WIDTH), BF16)] * 2,
        [_row_spec(min(tm, mrows), C_WIDTH)] * 2)

    T = DSA_T
    i_idx = np.arange(T)[:, None]
    dist_a = T + i_idx - np.arange(2 * T)[None, :]
    rb_a = _bias_tiles(rel_bias, _t5_bucket_np(dist_a), 0, A_HEADS, NUM_BUCKETS - 1, LOG2E)
    i_idx = np.arange(WINDOW)[:, None]
    dist_b = WINDOW + i_idx - np.arange(2 * WINDOW)[None, :]
    bucket_b = np.where((dist_b >= 0) & (dist_b < WINDOW), _t5_bucket_np(dist_b), -1).astype(np.int32)
    bias_b = _bias_tiles(rel_bias, bucket_b, A_HEADS, B_HEADS, None, 1.0)
    tiles_b = B_HEADS // B_KV_HEADS // 2
    slot_heads = [2 * (tiles_b * g + i) + e for g in range(B_KV_HEADS) for e in range(2) for i in range(tiles_b)]
    bias_b = bias_b[np.array(slot_heads)].reshape(2 * B_KV_HEADS, tiles_b * WINDOW, 2 * WINDOW)

    r3 = lambda v, w: v.reshape(bsz, seq, w)
    ikt = jnp.swapaxes(ik.reshape(bsz, seq, IDX_DIM), 1, 2)
    zeros = jnp.zeros_like(ikt)
    ikz = jnp.stack([jnp.concatenate([ikt, zeros], axis=1), jnp.concatenate([zeros, ikt], axis=1)], axis=1)
    oa = _dsa_call(iq, iw.reshape(bsz, seq, IDX_HEADS), ikz,
                   r3(qa, A_WIDTH), r3(ka, A_WIDTH),
                   r3(va, A_WIDTH), r3(ga, A_WIDTH), rb_a, topk)
    ob = _swa_call(sinks_b, r3(qb, B_WIDTH), r3(kb, B_KV_WIDTH), r3(vb, B_KV_WIDTH),
                   r3(kbs, B_KV_WIDTH), r3(vbs, B_KV_WIDTH), r3(gb, B_WIDTH), bias_b)
    oc = _mem_call(r3(qc, C_WIDTH), kc.reshape(bsz, mlen, C_WIDTH), vc.reshape(bsz, mlen, C_WIDTH),
                   r3(gc, C_WIDTH))

    out = _final_call(x2d, g_x, oa.reshape(n, A_WIDTH), ob.reshape(n, B_WIDTH), oc.reshape(n, C_WIDTH),
                      w_mix, row2(gate_bias), w_up_a.astype(BF16), w_up_b.astype(BF16),
                      w_up_c.astype(BF16), w_o.astype(BF16))
    return out.reshape(bsz, seq, d)


def kernel(x, mem, norm_g, w_in, kv_norm_g, w_kv_up, idx_k_ln_g, idx_k_ln_b, q_norm_a, k_norm_a, q_norm_b, k_norm_b, sinks_b, mem_norm_g, w_mem_kv, q_norm_c, k_norm_c, w_up_a, w_up_b, w_up_c, gate_bias, w_o, rel_bias):
    for l in range(norm_g.shape[0]):
        x = _layer(x, mem, norm_g[l], w_in[l], kv_norm_g[l], w_kv_up[l], idx_k_ln_g[l], idx_k_ln_b[l],
                   q_norm_a[l], k_norm_a[l], q_norm_b[l], k_norm_b[l], sinks_b[l], mem_norm_g[l],
                   w_mem_kv[l], q_norm_c[l], k_norm_c[l], w_up_a[l], w_up_b[l], w_up_c[l],
                   gate_bias[l], w_o[l], rel_bias)
    return x
```

```python
import functools
import math

import numpy as np
import jax
import jax.numpy as jnp
from jax import lax
from jax.experimental import pallas as pl
from jax.experimental.pallas import tpu as pltpu

F32 = jnp.float32
BF16 = jnp.bfloat16

EPS = 1e-6
A_HEADS, A_HEAD_DIM, A_KV_RANK = 6, 128, 256
IDX_HEADS, IDX_DIM, TOPK_MAX = 16, 64, 256
IDX_PAIRS = IDX_HEADS // 2
B_HEADS, B_KV_HEADS, B_HEAD_DIM, WINDOW = 12, 2, 64, 128
C_HEADS, C_HEAD_DIM = 4, 128
NUM_BUCKETS, MAX_DISTANCE = 32, 128
N_BRANCH = 3
A_WIDTH = A_HEADS * A_HEAD_DIM
B_WIDTH = B_HEADS * B_HEAD_DIM
B_KV_WIDTH = B_KV_HEADS * B_HEAD_DIM
C_WIDTH = C_HEADS * C_HEAD_DIM

LANES = 128
VMEM_LIMIT = 56 * 1024 * 1024
NEG_BIG = -1e30

ROW_TILE = 1024
DSA_T = 256
SWA_T = 512
FINAL_ROWS = 512
FINAL_NCHUNK = 4
COUNT_ROWS = 128
REGROUP_LANES = 256
COUNT_UNROLL = 4
FAR_TILES = 4
LOG2E = math.log2(math.e)


def _cparams(sem):
    return pltpu.CompilerParams(dimension_semantics=sem, vmem_limit_bytes=VMEM_LIMIT)


def _dot(a, b):
    return jnp.dot(a, b, preferred_element_type=F32)


def _dot_nt(a, b):
    return lax.dot_general(a, b, (((1,), (1,)), ((), ())), preferred_element_type=F32)


def _rms_rows(x, g):
    ms = jnp.mean(x * x, axis=-1, keepdims=True)
    return x * lax.rsqrt(ms + EPS) * g


def _headnorm128(seg, g):
    ms = jnp.mean(seg * seg, axis=-1, keepdims=True)
    return seg * lax.rsqrt(ms + EPS) * g


def _headnorm64_pair(seg, g2):
    sq = seg * seg
    low = lax.broadcasted_iota(jnp.int32, seg.shape, 1) < B_HEAD_DIM
    s_all = jnp.sum(sq, axis=-1, keepdims=True)
    s_lo = jnp.sum(jnp.where(low, sq, 0.0), axis=-1, keepdims=True)
    ms = jnp.where(low, s_lo, s_all - s_lo) * (1.0 / B_HEAD_DIM)
    return seg * lax.rsqrt(ms + EPS) * g2


def _silu(y):
    return y / (1.0 + jnp.exp(-y))


def _proj_a_kernel(x_ref, g_ref, w_ref, kvg_ref, wkv_ref, qg_ref, kg_ref, qa_ref, ka_ref, va_ref):
    h = _rms_rows(x_ref[...], g_ref[...]).astype(BF16)
    y = _dot_nt(h, w_ref[...])
    scale = A_HEAD_DIM ** -0.5 * LOG2E
    for hh in range(A_HEADS):
        sl = slice(hh * A_HEAD_DIM, (hh + 1) * A_HEAD_DIM)
        qa_ref[:, sl] = (_headnorm128(y[:, sl], qg_ref[...]) * scale).astype(BF16)
    ckv = _rms_rows(y[:, A_WIDTH:A_WIDTH + A_KV_RANK], kvg_ref[...]).astype(BF16)
    kv = _dot(ckv, wkv_ref[...])
    for hh in range(A_HEADS):
        sl = slice(hh * A_HEAD_DIM, (hh + 1) * A_HEAD_DIM)
        ka_ref[:, sl] = _headnorm128(kv[:, sl], kg_ref[...]).astype(BF16)
    va_ref[...] = kv[:, A_WIDTH:].astype(BF16)


def _proj_i_kernel(x_ref, g_ref, w_ref, lng_ref, lnb_ref, iq_ref, ik_ref, iw_ref):
    h = _rms_rows(x_ref[...], g_ref[...]).astype(BF16)
    y = _dot_nt(h, w_ref[...])
    k0 = IDX_HEADS * IDX_DIM
    for p in range(IDX_PAIRS):
        iq_ref[p] = y[:, p * LANES:(p + 1) * LANES].astype(BF16)
    ik = y[:, k0:k0 + IDX_DIM]
    mu = jnp.mean(ik, axis=-1, keepdims=True)
    d = ik - mu
    var = jnp.mean(d * d, axis=-1, keepdims=True)
    ik_ref[...] = (d * lax.rsqrt(var + EPS) * lng_ref[...] + lnb_ref[...]).astype(BF16)
    w0 = k0 + LANES
    iw_ref[...] = y[:, w0:w0 + IDX_HEADS] * (IDX_HEADS ** -0.5) * (IDX_DIM ** -0.5)


def _proj_g_kernel(x_ref, g_ref, w_ref, ga_ref, gb_ref, gc_ref):
    h = _rms_rows(x_ref[...], g_ref[...]).astype(BF16)
    y = _silu(_dot_nt(h, w_ref[...]))
    ga_ref[...] = y[:, :A_WIDTH].astype(BF16)
    gb_ref[...] = y[:, A_WIDTH:A_WIDTH + B_WIDTH].astype(BF16)
    gc_ref[...] = y[:, A_WIDTH + B_WIDTH:].astype(BF16)


def _proj_bc_kernel(x_ref, g_ref, w_ref, qbg_ref, kbg_ref, qcg_ref,
                    qb_ref, kb_ref, vb_ref, qc_ref, kbs_ref, vbs_ref):
    h = _rms_rows(x_ref[...], g_ref[...]).astype(BF16)
    y = _dot_nt(h, w_ref[...])
    sb = B_HEAD_DIM ** -0.5
    for p in range(B_WIDTH // LANES):
        sl = slice(p * LANES, (p + 1) * LANES)
        qb_ref[:, sl] = (_headnorm64_pair(y[:, sl], qbg_ref[...]) * sb).astype(BF16)
    k0 = B_WIDTH
    kb_ref[...] = _headnorm64_pair(y[:, k0:k0 + B_KV_WIDTH], kbg_ref[...]).astype(BF16)
    vb_ref[...] = y[:, k0 + B_KV_WIDTH:k0 + 2 * B_KV_WIDTH].astype(BF16)
    s0 = k0 + 2 * B_KV_WIDTH + C_WIDTH
    kbs_ref[...] = _headnorm64_pair(y[:, s0:s0 + B_KV_WIDTH], kbg_ref[...]).astype(BF16)
    vbs_ref[...] = y[:, s0 + B_KV_WIDTH:s0 + 2 * B_KV_WIDTH].astype(BF16)
    c0 = k0 + 2 * B_KV_WIDTH
    sc = C_HEAD_DIM ** -0.5
    for hh in range(C_HEADS):
        sl = slice(hh * C_HEAD_DIM, (hh + 1) * C_HEAD_DIM)
        qc_ref[:, sl] = (_headnorm128(y[:, c0 + hh * C_HEAD_DIM:c0 + (hh + 1) * C_HEAD_DIM],
                                      qcg_ref[...]) * sc).astype(BF16)


def _proj_mem_kernel(x_ref, g_ref, w_ref, kg_ref, kc_ref, vc_ref):
    h = _rms_rows(x_ref[...], g_ref[...]).astype(BF16)
    y = _dot(h, w_ref[...])
    for hh in range(C_HEADS):
        sl = slice(hh * C_HEAD_DIM, (hh + 1) * C_HEAD_DIM)
        kc_ref[:, sl] = _headnorm128(y[:, sl], kg_ref[...]).astype(BF16)
    vc_ref[...] = y[:, C_WIDTH:].astype(BF16)


def _row_spec(tm, cols):
    return pl.BlockSpec((tm, cols), lambda i: (i, 0))


def _full_spec(shape):
    nd = len(shape)
    return pl.BlockSpec(shape, lambda i: (0,) * nd, pipeline_mode=pl.Buffered(1))


def _proj_call(kernel_fn, name, x2d, tm, consts, out_shapes, out_specs):
    n, d = x2d.shape
    in_specs = [_row_spec(tm, d)] + [_full_spec(c.shape) for c in consts]
    return pl.pallas_call(
        kernel_fn,
        grid=(n // tm,),
        in_specs=in_specs,
        out_specs=out_specs,
        out_shape=out_shapes,
        compiler_params=_cparams(("parallel",)),
        name=name,
    )(x2d, *consts)


def _t5_bucket_np(dist):
    n = np.maximum(dist, 0)
    max_exact = NUM_BUCKETS // 2
    nf = np.maximum(n, 1).astype(np.float32)
    large = max_exact + (np.log(nf / max_exact) / math.log(MAX_DISTANCE / max_exact)
                         * (NUM_BUCKETS - max_exact)).astype(np.int32)
    large = np.minimum(large, NUM_BUCKETS - 1)
    return np.where(n < max_exact, n, large).astype(np.int32)


def _bias_tile_kernel(tab_ref, bucket_ref, o_ref, *, head0, shift_bucket, scale):
    h = pl.program_id(0) + head0
    bucket = bucket_ref[...]
    acc = jnp.zeros(bucket.shape, F32)
    for b in range(NUM_BUCKETS):
        acc = jnp.where(bucket == b, tab_ref[b, h], acc)
    if shift_bucket is not None:
        acc = acc - tab_ref[shift_bucket, h]
    o_ref[...] = jnp.where(bucket < 0, NEG_BIG, acc * scale)


def _bias_tiles(rel_bias, bucket_np, head0, nheads, shift_bucket, scale):
    r, c = bucket_np.shape
    return pl.pallas_call(
        functools.partial(_bias_tile_kernel, head0=head0, shift_bucket=shift_bucket, scale=scale),
        grid=(nheads,),
        in_specs=[pl.BlockSpec(memory_space=pltpu.SMEM), pl.BlockSpec((r, c), lambda h: (0, 0))],
        out_specs=pl.BlockSpec((None, r, c), lambda h: (h, 0, 0)),
        out_shape=jax.ShapeDtypeStruct((nheads, r, c), F32),
        compiler_params=_cparams(("arbitrary",)),
        name="bias_tiles",
    )(rel_bias, jnp.asarray(bucket_np))


def _dsa_kernel(iq_ref, iw_ref, ikz_ref, q_ref, k_ref, v_ref, g_ref, rb_ref, o_ref,
                sc_ref, wb_ref, lo_ref, hi_ref, mid_ref, clo_ref, s_ref, m_ref, l_ref, acc_ref, *, topk):
    T = DSA_T
    qi = pl.program_id(1)
    kf = float(topk)

    def lanes(col):
        return jnp.broadcast_to(col, (T, LANES))

    def halves(tile):
        return [tile[:, c * LANES:(c + 1) * LANES] for c in range(T // LANES)]

    for hh in range(IDX_HEADS):
        wb_ref[hh] = lanes(iw_ref[:, hh:hh + 1])

    def score_chunk(j, diag):
        keys = pl.ds(pl.multiple_of(j * T, T), T)
        accs = [jnp.zeros((T, LANES), F32) for _ in range(T // LANES)]
        for hh in range(IDX_HEADS):
            s = _dot(iq_ref[hh // 2], ikz_ref[hh % 2, :, keys])
            w = wb_ref[hh]
            accs = [a + w * jnp.maximum(sh, 0.0) for a, sh in zip(accs, halves(s))]
        if diag:
            row = lax.broadcasted_iota(jnp.int32, (T, LANES), 0)
            col = lax.broadcasted_iota(jnp.int32, (T, LANES), 1)
            causal = [col + c * LANES <= row for c in range(T // LANES)]
            lows = [jnp.where(cm, a, jnp.inf) for cm, a in zip(causal, accs)]
            accs = [jnp.where(cm, a, -jnp.inf) for cm, a in zip(causal, accs)]
        else:
            lows = accs
        for c, a in enumerate(accs):
            sc_ref[j, :, c * LANES:(c + 1) * LANES] = a
        return functools.reduce(jnp.minimum, lows), accs

    def p1_body(j, carry):
        mn, mxs = carry
        lo_c, his = score_chunk(j, False)
        return jnp.minimum(mn, lo_c), [jnp.maximum(a, b) for a, b in zip(mxs, his)]

    mn0 = jnp.full((T, LANES), jnp.inf, F32)
    mx0 = [jnp.full((T, LANES), -jnp.inf, F32) for _ in range(T // LANES)]

    def p1_pair(jj, carry):
        return p1_body(2 * jj + 1, p1_body(2 * jj, carry))

    n_pairs = lax.div(qi, 2)
    mn, mxs = lax.fori_loop(0, n_pairs, p1_pair, (mn0, mx0))
    mn, mxs = lax.fori_loop(2 * n_pairs, qi, p1_body, (mn, mxs))
    lo_c, his = score_chunk(qi, True)
    mxs = [jnp.maximum(a, b) for a, b in zip(mxs, his)]
    mn = lanes(jnp.min(jnp.minimum(mn, lo_c), axis=-1, keepdims=True))
    mx = lanes(jnp.max(functools.reduce(jnp.maximum, mxs), axis=-1, keepdims=True))
    class_floor = lanes(jnp.min(functools.reduce(jnp.minimum, mxs), axis=-1, keepdims=True))

    def count_lanes(rows, v, strict=False):
        def tile(j, c):
            for c0 in range(0, T, LANES):
                sh = sc_ref[j, rows, c0:c0 + LANES]
                hit = (sh > v) if strict else (sh >= v)
                c = c + jnp.where(hit, 1.0, 0.0)
            return c

        def group(g, c):
            for u in range(COUNT_UNROLL):
                c = tile(g * COUNT_UNROLL + u, c)
            return c

        n_groups = lax.div(qi + 1, COUNT_UNROLL)
        c = lax.fori_loop(0, n_groups, group, jnp.zeros((COUNT_ROWS, LANES), F32))
        return lax.fori_loop(n_groups * COUNT_UNROLL, qi + 1, tile, c)

    def row_total(c):
        return jnp.broadcast_to(jnp.sum(c, axis=-1, keepdims=True), c.shape)

    slabs = [slice(r0, r0 + COUNT_ROWS) for r0 in range(0, T, COUNT_ROWS)]
    n_causal = (qi * T + 1 + lax.broadcasted_iota(jnp.int32, (T, LANES), 0)).astype(F32)
    keep_all = n_causal <= kf
    lo0 = jnp.where(keep_all, mn, jnp.maximum(mn, class_floor))
    above_max = mx + jnp.maximum(jnp.abs(mx) * (2.0 ** -20), 1e-30)
    hi0 = jnp.where(keep_all, mn, above_max)
    lo_ref[...] = lo0
    hi_ref[...] = hi0
    mid_ref[...] = lo0 + 0.5 * (hi0 - lo0)
    clo_ref[...] = n_causal

    def bis_body(_):
        partial = [count_lanes(rows, mid_ref[rows]) for rows in slabs]
        widths = [hi_ref[rows] - lo_ref[rows] for rows in slabs]
        open_width = jnp.max(functools.reduce(jnp.maximum, widths))
        for rows, c in zip(slabs, partial):
            lo, hi, mid = lo_ref[rows], hi_ref[rows], mid_ref[rows]
            cnt = row_total(c)
            up = cnt >= kf
            lo = jnp.where(up, mid, lo)
            c_lo = jnp.where(up, cnt, clo_ref[rows])
            hi = jnp.where(up, hi, mid)
            nxt = lo + 0.5 * (hi - lo)
            stop = jnp.logical_or(c_lo == kf, jnp.logical_or(nxt <= lo, nxt >= hi))
            hi = jnp.where(stop, lo, hi)
            lo_ref[rows] = lo
            hi_ref[rows] = hi
            mid_ref[rows] = jnp.where(stop, lo, nxt)
            clo_ref[rows] = c_lo
        return open_width

    lax.while_loop(lambda w: w > 0.0, bis_body, jnp.max(hi0 - lo0))
    thr = lo_ref[...]

    tied_f = jnp.where(clo_ref[...] > kf, 1.0, 0.0)

    @pl.when(jnp.max(tied_f) > 0.0)
    def _():
        above = jnp.concatenate(
            [row_total(count_lanes(rows, thr[rows], strict=True)) for rows in slabs], axis=0)
        need = (kf - above)[:, :1]
        r = lax.broadcasted_iota(jnp.int32, (T, T), 0)
        c = lax.broadcasted_iota(jnp.int32, (T, T), 1)
        before = jnp.where(r < c, 1.0, 0.0).astype(BF16)
        thr_col = thr[:, :1]
        tied_col = tied_f[:, :1] > 0.0

        def body(j, seen):
            s = sc_ref[j]
            eq = jnp.where(s == thr_col, 1.0, 0.0)
            rank = seen + _dot(eq.astype(BF16), before)
            drop = jnp.logical_and(tied_col, jnp.logical_and(eq > 0.0, rank >= need))
            sc_ref[j] = jnp.where(drop, -jnp.inf, s)
            return seen + jnp.sum(eq, axis=-1, keepdims=True)

        lax.fori_loop(0, qi + 1, body, jnp.zeros((T, 1), F32))

    m_ref[...] = jnp.full(m_ref.shape, NEG_BIG, F32)
    l_ref[...] = jnp.zeros(l_ref.shape, F32)
    acc_ref[...] = jnp.zeros(acc_ref.shape, F32)

    def attend(j0, ntiles, band):
        sel = [jnp.where(sh >= thr, 0.0, NEG_BIG) for t in range(ntiles) for sh in halves(sc_ref[j0 + t])]
        start = pl.multiple_of(j0 * T, T)
        keys = pl.ds(start, ntiles * T)
        width = ntiles * T

        def logits_to_scratch(hh):
            sl = slice(hh * A_HEAD_DIM, (hh + 1) * A_HEAD_DIM)
            s_ref[hh % 2, :, :width] = _dot_nt(q_ref[:, sl], k_ref[keys, sl])

        logits_to_scratch(0)
        pending = None
        for hh in range(A_HEADS):
            sl = slice(hh * A_HEAD_DIM, (hh + 1) * A_HEAD_DIM)
            if hh + 1 < A_HEADS:
                logits_to_scratch(hh + 1)
            s = s_ref[hh % 2, :, :width]
            if band is not None:
                s = s + rb_ref[hh, :, band * T:(band + ntiles) * T]
            sh = [s[:, c * LANES:(c + 1) * LANES] + m for c, m in enumerate(sel)]
            m_prev = m_ref[hh]
            m_cur = jnp.max(functools.reduce(jnp.maximum, sh), axis=-1, keepdims=True)
            m_new = jnp.maximum(m_prev, lanes(m_cur))
            alpha = jnp.exp2(m_prev - m_new)
            ph = [jnp.exp2(x - m_new) for x in sh]
            l_ref[hh] = alpha * l_ref[hh] + functools.reduce(jnp.add, ph)
            p = jnp.concatenate([x.astype(BF16) for x in ph], axis=-1)
            acc_ref[hh] = alpha * acc_ref[hh]
            m_ref[hh] = m_new
            if pending is not None:
                acc_ref[pending[0]] += pending[1]
            pending = (hh, _dot(p, v_ref[keys, sl]))
        acc_ref[pending[0]] += pending[1]

    n_far = jnp.maximum(qi - 1, 0)
    done_tiles = 0
    width = FAR_TILES
    while width >= 1:
        trips = lax.div(n_far - done_tiles, width)

        def far_body(i, c, width=width, base=done_tiles):
            attend(base + i * width, width, None)
            return c

        lax.fori_loop(0, trips, far_body, 0)
        done_tiles = done_tiles + trips * width
        width //= 2

    @pl.when(qi > 0)
    def _():
        attend(qi - 1, 2, 0)

    @pl.when(qi == 0)
    def _():
        attend(0, 1, 1)

    for hh in range(A_HEADS):
        sl = slice(hh * A_HEAD_DIM, (hh + 1) * A_HEAD_DIM)
        den = jnp.sum(l_ref[hh], axis=-1, keepdims=True)
        o_ref[:, sl] = (acc_ref[hh] / den * g_ref[:, sl].astype(F32)).astype(BF16)


def _dsa_call(iq, iw, ikz, qa, ka, va, ga, rb, topk):
    bsz, seq, _ = qa.shape
    T = DSA_T
    nq = seq // T
    assert topk <= T and seq % T == 0
    once = pl.Buffered(1)
    return pl.pallas_call(
        functools.partial(_dsa_kernel, topk=topk),
        grid=(bsz, nq),
        in_specs=[
            pl.BlockSpec((None, IDX_PAIRS, T, LANES), lambda b, i: (b, 0, i, 0)),
            pl.BlockSpec((None, T, IDX_HEADS), lambda b, i: (b, i, 0)),
            pl.BlockSpec((None, 2, LANES, seq), lambda b, i: (b, 0, 0, 0), pipeline_mode=once),
            pl.BlockSpec((None, T, A_WIDTH), lambda b, i: (b, i, 0)),
            pl.BlockSpec((None, seq, A_WIDTH), lambda b, i: (b, 0, 0), pipeline_mode=once),
            pl.BlockSpec((None, seq, A_WIDTH), lambda b, i: (b, 0, 0), pipeline_mode=once),
            pl.BlockSpec((None, T, A_WIDTH), lambda b, i: (b, i, 0)),
            pl.BlockSpec((A_HEADS, T, 2 * T), lambda b, i: (0, 0, 0), pipeline_mode=once),
        ],
        out_specs=pl.BlockSpec((None, T, A_WIDTH), lambda b, i: (b, i, 0)),
        out_shape=jax.ShapeDtypeStruct((bsz, seq, A_WIDTH), BF16),
        scratch_shapes=[
            pltpu.VMEM((nq, T, T), F32),
            pltpu.VMEM((IDX_HEADS, T, LANES), F32),
            pltpu.VMEM((T, LANES), F32),
            pltpu.VMEM((T, LANES), F32),
            pltpu.VMEM((T, LANES), F32),
            pltpu.VMEM((T, LANES), F32),
            pltpu.VMEM((2, T, FAR_TILES * T), F32),
            pltpu.VMEM((A_HEADS, T, LANES), F32),
            pltpu.VMEM((A_HEADS, T, LANES), F32),
            pltpu.VMEM((A_HEADS, T, A_HEAD_DIM), F32),
        ],
        compiler_params=_cparams(("arbitrary", "arbitrary")),
        name="dsa",
    )(iq, iw, ikz, qa, ka, va, ga, rb)


def _swa_kernel(sink_ref, q_ref, kc_ref, vc_ref, kcs_ref, vcs_ref, kp_ref, vp_ref, kps_ref, vps_ref,
                g_ref, bias_ref, o_ref):
    first = pl.program_id(1) == 0
    blk = WINDOW
    tiles = B_HEADS // B_KV_HEADS // 2
    low_half = lax.broadcasted_iota(jnp.int32, (2 * blk, LANES), 1) < B_HEAD_DIM
    ones = jnp.ones((2 * blk, LANES), BF16)
    for sb in range(SWA_T // blk):
        rows = slice(sb * blk, (sb + 1) * blk)
        prev = slice((sb - 1) * blk, sb * blk)

        def window(cur_ref, prev_ref):
            before = prev_ref[...] if sb == 0 else cur_ref[prev, :]
            return jnp.concatenate([before, cur_ref[rows, :]], axis=0)

        k_nat, v_nat = window(kc_ref, kp_ref), window(vc_ref, vp_ref)
        k_swp, v_swp = window(kcs_ref, kps_ref), window(vcs_ref, vps_ref)
        for g in range(B_KV_HEADS):
            q = jnp.concatenate(
                [q_ref[rows, (tiles * g + i) * LANES:(tiles * g + i + 1) * LANES] for i in range(tiles)], axis=0)
            res = None
            for e in range(2):
                keep = low_half if e == 0 else jnp.logical_not(low_half)
                k_src, v_src = (k_nat, v_nat) if g == e else (k_swp, v_swp)
                kk = jnp.where(keep, k_src, jnp.zeros_like(k_src))
                vv = jnp.where(keep, v_src, jnp.zeros_like(v_src))
                logits = _dot_nt(q, kk) + bias_ref[2 * g + e]
                l_prev, l_cur = logits[:, :blk], logits[:, blk:]
                if sb == 0:
                    l_prev = jnp.where(first, NEG_BIG, l_prev)
                sink = jnp.concatenate(
                    [jnp.full((blk, blk), sink_ref[2 * (tiles * g + i) + e], F32) for i in range(tiles)], axis=0)
                m = jnp.max(jnp.maximum(l_prev, l_cur), axis=-1, keepdims=True)
                m = jnp.maximum(jnp.broadcast_to(m, sink.shape), sink)
                p = jnp.concatenate([jnp.exp(l_prev - m).astype(BF16), jnp.exp(l_cur - m).astype(BF16)], axis=1)
                pv = _dot(p, jnp.concatenate([vv, ones], axis=1))
                part = pv[:, :LANES] / (pv[:, LANES:] + jnp.exp(sink - m))
                res = part if res is None else res + part
            for i in range(tiles):
                cols = slice((tiles * g + i) * LANES, (tiles * g + i + 1) * LANES)
                o_ref[rows, cols] = (res[i * blk:(i + 1) * blk] * g_ref[rows, cols].astype(F32)).astype(BF16)


def _swa_call(sinks, qb, kb, vb, kbs, vbs, gb, bias):
    bsz, seq, _ = qb.shape
    T = SWA_T
    per = T // WINDOW
    cur = lambda b, i: (b, i, 0)
    prev = lambda b, i: (b, jnp.maximum(i * per - 1, 0), 0)
    kv_cur = pl.BlockSpec((None, T, B_KV_WIDTH), cur)
    kv_prev = pl.BlockSpec((None, WINDOW, B_KV_WIDTH), prev)
    return pl.pallas_call(
        _swa_kernel,
        grid=(bsz, seq // T),
        in_specs=[
            pl.BlockSpec(memory_space=pltpu.SMEM),
            pl.BlockSpec((None, T, B_WIDTH), cur),
            kv_cur, kv_cur, kv_cur, kv_cur,
            kv_prev, kv_prev, kv_prev, kv_prev,
            pl.BlockSpec((None, T, B_WIDTH), cur),
            pl.BlockSpec(bias.shape, lambda b, i: (0, 0, 0)),
        ],
        out_specs=pl.BlockSpec((None, T, B_WIDTH), cur),
        out_shape=jax.ShapeDtypeStruct((bsz, seq, B_WIDTH), BF16),
        compiler_params=_cparams(("parallel", "parallel")),
        name="swa",
    )(sinks, qb, kb, vb, kbs, vbs, kb, vb, kbs, vbs, gb, bias)


def _mem_kernel(q_ref, k_ref, v_ref, g_ref, o_ref):
    for hh in range(C_HEADS):
        sl = slice(hh * C_HEAD_DIM, (hh + 1) * C_HEAD_DIM)
        s = _dot_nt(q_ref[:, sl], k_ref[:, sl])
        m = jnp.max(s, axis=-1, keepdims=True)
        p = jnp.exp(s - m)
        den = jnp.sum(p, axis=-1, keepdims=True)
        o = _dot(p.astype(BF16), v_ref[:, sl]) / den
        o_ref[:, sl] = (o * g_ref[:, sl].astype(F32)).astype(BF16)


def _mem_call(qc, kc, vc, gc):
    bsz, seq, _ = qc.shape
    mlen = kc.shape[1]
    T = SWA_T
    cur = lambda b, i: (b, i, 0)
    whole = lambda b, i: (b, 0, 0)
    return pl.pallas_call(
        _mem_kernel,
        grid=(bsz, seq // T),
        in_specs=[
            pl.BlockSpec((None, T, C_WIDTH), cur),
            pl.BlockSpec((None, mlen, C_WIDTH), whole),
            pl.BlockSpec((None, mlen, C_WIDTH), whole),
            pl.BlockSpec((None, T, C_WIDTH), cur),
        ],
        out_specs=pl.BlockSpec((None, T, C_WIDTH), cur),
        out_shape=jax.ShapeDtypeStruct((bsz, seq, C_WIDTH), BF16),
        compiler_params=_cparams(("parallel", "parallel")),
        name="mem_attn",
    )(qc, kc, vc, gc)


def _final_kernel(x_ref, g_ref, oa_ref, ob_ref, oc_ref, wma_ref, wmb_ref, wmc_ref,
                  ba_ref, bb_ref, bc_ref, wua_ref, wub_ref, wuc_ref, wo_ref, out_ref, h_ref, acc_ref):
    c = pl.program_id(1)

    @pl.when(c == 0)
    def _():
        h_ref[...] = _rms_rows(x_ref[...], g_ref[...]).astype(BF16)
        acc_ref[...] = jnp.zeros(acc_ref.shape, F32)

    h = h_ref[...]

    def branch(o_ref, wm_ref, b_ref, wu_ref):
        gate = 1.0 / (1.0 + jnp.exp(-(_dot_nt(h, wm_ref[...]) + b_ref[...])))
        return gate * _dot(o_ref[...], wu_ref[...])

    merged = (branch(oa_ref, wma_ref, ba_ref, wua_ref) + branch(ob_ref, wmb_ref, bb_ref, wub_ref)
              + branch(oc_ref, wmc_ref, bc_ref, wuc_ref))
    acc_ref[...] += _dot(merged.astype(BF16), wo_ref[...])

    @pl.when(c == pl.num_programs(1) - 1)
    def _():
        out_ref[...] = x_ref[...] + acc_ref[...]


def _final_call(x2d, g, oa, ob, oc, wmix, gate_bias, wua, wub, wuc, wo):
    n, d = x2d.shape
    tm = FINAL_ROWS
    nch = FINAL_NCHUNK
    cw = d // nch
    row = lambda i, c: (i, 0)

    def col(br):
        return lambda i, c: (0, br * nch + c)

    def wrow(br):
        return lambda i, c: (br * nch + c, 0)

    return pl.pallas_call(
        _final_kernel,
        grid=(n // tm, nch),
        in_specs=[
            pl.BlockSpec((tm, d), row),
            pl.BlockSpec((1, d), lambda i, c: (0, 0)),
            pl.BlockSpec((tm, A_WIDTH), row),
            pl.BlockSpec((tm, B_WIDTH), row),
            pl.BlockSpec((tm, C_WIDTH), row),
            pl.BlockSpec((cw, d), wrow(0)), pl.BlockSpec((cw, d), wrow(1)), pl.BlockSpec((cw, d), wrow(2)),
            pl.BlockSpec((1, cw), col(0)), pl.BlockSpec((1, cw), col(1)), pl.BlockSpec((1, cw), col(2)),
            pl.BlockSpec((A_WIDTH, cw), lambda i, c: (0, c)),
            pl.BlockSpec((B_WIDTH, cw), lambda i, c: (0, c)),
            pl.BlockSpec((C_WIDTH, cw), lambda i, c: (0, c)),
            pl.BlockSpec((cw, d), lambda i, c: (c, 0)),
        ],
        out_specs=pl.BlockSpec((tm, d), row),
        out_shape=jax.ShapeDtypeStruct((n, d), F32),
        scratch_shapes=[pltpu.VMEM((tm, d), BF16), pltpu.VMEM((tm, d), F32)],
        compiler_params=_cparams(("parallel", "arbitrary")),
        name="merge_out",
    )(x2d, g, oa, ob, oc, wmix, wmix, wmix, gate_bias, gate_bias, gate_bias, wua, wub, wuc, wo)


def _w_in_groups(d):
    sizes = (A_WIDTH, A_KV_RANK, IDX_HEADS * IDX_DIM, IDX_DIM, IDX_HEADS, A_WIDTH,
             B_WIDTH, B_KV_WIDTH, B_KV_WIDTH, B_WIDTH, C_WIDTH, C_WIDTH, N_BRANCH * d)
    cuts = np.cumsum((0,) + sizes).tolist()
    (aq, ackv, iq, ik, iw, ag, bq, bk, bv, bg, cq, cg, mix) = [
        (cuts[i], cuts[i + 1]) for i in range(len(sizes))]
    return [
        [(aq[0], ackv[1])],
        [(iq[0], ik[1]), LANES - IDX_DIM, iw, LANES - IDX_HEADS],
        [ag, bg, cg],
        [(bq[0], bv[1]), cq,
         (bk[0] + B_HEAD_DIM, bk[1]), (bk[0], bk[0] + B_HEAD_DIM),
         (bv[0] + B_HEAD_DIM, bv[1]), (bv[0], bv[0] + B_HEAD_DIM)],
        [mix],
    ]


def _group_width(group):
    return sum(p if isinstance(p, int) else p[1] - p[0] for p in group)


def _regroup_kernel(w_ref, *out_refs, groups):
    for o_ref, group in zip(out_refs, groups):
        row = 0
        for part in group:
            if isinstance(part, int):
                o_ref[row:row + part, :] = jnp.zeros((part, o_ref.shape[1]), BF16)
                row += part
            else:
                o_ref[row:row + part[1] - part[0], :] = w_ref[part[0]:part[1], :].astype(BF16)
                row += part[1] - part[0]


def _regroup_w_in(w_in_t):
    cols, d = w_in_t.shape
    groups = _w_in_groups(d)
    lanes = REGROUP_LANES
    return pl.pallas_call(
        functools.partial(_regroup_kernel, groups=groups),
        grid=(d // lanes,),
        in_specs=[pl.BlockSpec((cols, lanes), lambda i: (0, i))],
        out_specs=[pl.BlockSpec((_group_width(g), lanes), lambda i: (0, i)) for g in groups],
        out_shape=[jax.ShapeDtypeStruct((_group_width(g), d), BF16) for g in groups],
        compiler_params=_cparams(("parallel",)),
        name="regroup_w_in",
    )(w_in_t)


def _layer(x, mem, norm_g, w_in, kv_norm_g, w_kv_up, idx_k_ln_g, idx_k_ln_b, q_norm_a, k_norm_a,
           q_norm_b, k_norm_b, sinks_b, mem_norm_g, w_mem_kv, q_norm_c, k_norm_c,
           w_up_a, w_up_b, w_up_c, gate_bias, w_o, rel_bias):
    bsz, seq, d = x.shape
    mlen = mem.shape[1]
    n = bsz * seq
    tm = ROW_TILE
    topk = min(TOPK_MAX, seq // 4)
    x2d = x.reshape(n, d)
    row2 = lambda v: v.reshape(1, -1)

    w_grp_a, w_grp_i, w_grp_g, w_grp_bc, w_mix = _regroup_w_in(jnp.swapaxes(w_in, 0, 1))
    g_x = row2(norm_g)

    qa, ka, va = _proj_call(
        _proj_a_kernel, "proj_a", x2d, tm,
        [g_x, w_grp_a, row2(kv_norm_g), w_kv_up.astype(BF16), row2(q_norm_a), row2(k_norm_a)],
        [jax.ShapeDtypeStruct((n, A_WIDTH), BF16)] * 3,
        [_row_spec(tm, A_WIDTH)] * 3)

    per_b = seq // tm
    iq, ik, iw = _proj_call(
        _proj_i_kernel, "proj_i", x2d, tm,
        [g_x, w_grp_i, row2(idx_k_ln_g), row2(idx_k_ln_b)],
        [jax.ShapeDtypeStruct((bsz, IDX_PAIRS, seq, LANES), BF16),
         jax.ShapeDtypeStruct((n, IDX_DIM), BF16),
         jax.ShapeDtypeStruct((n, IDX_HEADS), F32)],
        [pl.BlockSpec((None, IDX_PAIRS, tm, LANES), lambda i: (i // per_b, 0, i % per_b, 0)),
         _row_spec(tm, IDX_DIM), _row_spec(tm, IDX_HEADS)])

    ga, gb, gc = _proj_call(
        _proj_g_kernel, "proj_g", x2d, tm, [g_x, w_grp_g],
        [jax.ShapeDtypeStruct((n, A_WIDTH), BF16), jax.ShapeDtypeStruct((n, B_WIDTH), BF16),
         jax.ShapeDtypeStruct((n, C_WIDTH), BF16)],
        [_row_spec(tm, A_WIDTH), _row_spec(tm, B_WIDTH), _row_spec(tm, C_WIDTH)])

    bc_widths = (B_WIDTH, B_KV_WIDTH, B_KV_WIDTH, C_WIDTH, B_KV_WIDTH, B_KV_WIDTH)
    qb, kb, vb, qc, kbs, vbs = _proj_call(
        _proj_bc_kernel, "proj_bc", x2d, tm,
        [g_x, w_grp_bc, row2(jnp.tile(q_norm_b, 2)), row2(jnp.tile(k_norm_b, 2)), row2(q_norm_c)],
        [jax.ShapeDtypeStruct((n, w), BF16) for w in bc_widths],
        [_row_spec(tm, w) for w in bc_widths])

    mrows = bsz * mlen
    kc, vc = _proj_call(
        _proj_mem_kernel, "proj_mem", mem.reshape(mrows, d), min(tm, mrows),
        [row2(mem_norm_g), w_mem_kv.astype(BF16), row2(k_norm_c)],
        [jax.ShapeDtypeStruct((mrows, C_WIDTH), BF16)] * 2,
        [_row_spec(min(tm, mrows), C_WIDTH)] * 2)

    T = DSA_T
    i_idx = np.arange(T)[:, None]
    dist_a = T + i_idx - np.arange(2 * T)[None, :]
    rb_a = _bias_tiles(rel_bias, _t5_bucket_np(dist_a), 0, A_HEADS, NUM_BUCKETS - 1, LOG2E)
    i_idx = np.arange(WINDOW)[:, None]
    dist_b = WINDOW + i_idx - np.arange(2 * WINDOW)[None, :]
    bucket_b = np.where((dist_b >= 0) & (dist_b < WINDOW), _t5_bucket_np(dist_b), -1).astype(np.int32)
    bias_b = _bias_tiles(rel_bias, bucket_b, A_HEADS, B_HEADS, None, 1.0)
    tiles_b = B_HEADS // B_KV_HEADS // 2
    slot_heads = [2 * (tiles_b * g + i) + e for g in range(B_KV_HEADS) for e in range(2) for i in range(tiles_b)]
    bias_b = bias_b[np.array(slot_heads)].reshape(2 * B_KV_HEADS, tiles_b * WINDOW, 2 * WINDOW)

    r3 = lambda v, w: v.reshape(bsz, seq, w)
    ikt = jnp.swapaxes(ik.reshape(bsz, seq, IDX_DIM), 1, 2)
    zeros = jnp.zeros_like(ikt)
    ikz = jnp.stack([jnp.concatenate([ikt, zeros], axis=1), jnp.concatenate([zeros, ikt], axis=1)], axis=1)
    oa = _dsa_call(iq, iw.reshape(bsz, seq, IDX_HEADS), ikz,
                   r3(qa, A_WIDTH), r3(ka, A_WIDTH),
                   r3(va, A_WIDTH), r3(ga, A_WIDTH), rb_a, topk)
    ob = _swa_call(sinks_b, r3(qb, B_WIDTH), r3(kb, B_KV_WIDTH), r3(vb, B_KV_WIDTH),
                   r3(kbs, B_KV_WIDTH), r3(vbs, B_KV_WIDTH), r3(gb, B_WIDTH), bias_b)
    oc = _mem_call(r3(qc, C_WIDTH), kc.reshape(bsz, mlen, C_WIDTH), vc.reshape(bsz, mlen, C_WIDTH),
                   r3(gc, C_WIDTH))

    out = _final_call(x2d, g_x, oa.reshape(n, A_WIDTH), ob.reshape(n, B_WIDTH), oc.reshape(n, C_WIDTH),
                      w_mix, row2(gate_bias), w_up_a.astype(BF16), w_up_b.astype(BF16),
                      w_up_c.astype(BF16), w_o.astype(BF16))
    return out.reshape(bsz, seq, d)


def kernel(x, mem, norm_g, w_in, kv_norm_g, w_kv_up, idx_k_ln_g, idx_k_ln_b, q_norm_a, k_norm_a, q_norm_b, k_norm_b, sinks_b, mem_norm_g, w_mem_kv, q_norm_c, k_norm_c, w_up_a, w_up_b, w_up_c, gate_bias, w_o, rel_bias):
    for l in range(norm_g.shape[0]):
        x = _layer(x, mem, norm_g[l], w_in[l], kv_norm_g[l], w_kv_up[l], idx_k_ln_g[l], idx_k_ln_b[l],
                   q_norm_a[l], k_norm_a[l], q_norm_b[l], k_norm_b[l], sinks_b[l], mem_norm_g[l],
                   w_mem_kv[l], q_norm_c[l], k_norm_c[l], w_up_a[l], w_up_b[l], w_up_c[l],
                   gate_bias[l], w_o[l], rel_bias)
    return x
```

```python
import functools
import math

import numpy as np
import jax
import jax.numpy as jnp
from jax import lax
from jax.experimental import pallas as pl
from jax.experimental.pallas import tpu as pltpu

F32 = jnp.float32
BF16 = jnp.bfloat16

EPS = 1e-6
A_HEADS, A_HEAD_DIM, A_KV_RANK = 6, 128, 256
IDX_HEADS, IDX_DIM, TOPK_MAX = 16, 64, 256
IDX_PAIRS = IDX_HEADS // 2
B_HEADS, B_KV_HEADS, B_HEAD_DIM, WINDOW = 12, 2, 64, 128
C_HEADS, C_HEAD_DIM = 4, 128
NUM_BUCKETS, MAX_DISTANCE = 32, 128
N_BRANCH = 3
A_WIDTH = A_HEADS * A_HEAD_DIM
B_WIDTH = B_HEADS * B_HEAD_DIM
B_KV_WIDTH = B_KV_HEADS * B_HEAD_DIM
C_WIDTH = C_HEADS * C_HEAD_DIM

LANES = 128
VMEM_LIMIT = 56 * 1024 * 1024
NEG_BIG = -1e30

ROW_TILE = 1024
DSA_T = 256
SWA_T = 512
FINAL_ROWS = 512
FINAL_NCHUNK = 4
COUNT_ROWS = 128
REGROUP_LANES = 256
COUNT_UNROLL = 4
FAR_TILES = 4
LOG2E = math.log2(math.e)


def _cparams(sem):
    return pltpu.CompilerParams(dimension_semantics=sem, vmem_limit_bytes=VMEM_LIMIT)


def _dot(a, b):
    return jnp.dot(a, b, preferred_element_type=F32)


def _dot_nt(a, b):
    return lax.dot_general(a, b, (((1,), (1,)), ((), ())), preferred_element_type=F32)


def _rms_rows(x, g):
    ms = jnp.mean(x * x, axis=-1, keepdims=True)
    return x * lax.rsqrt(ms + EPS) * g


def _headnorm128(seg, g):
    ms = jnp.mean(seg * seg, axis=-1, keepdims=True)
    return seg * lax.rsqrt(ms + EPS) * g


def _headnorm64_pair(seg, g2):
    sq = seg * seg
    low = lax.broadcasted_iota(jnp.int32, seg.shape, 1) < B_HEAD_DIM
    s_all = jnp.sum(sq, axis=-1, keepdims=True)
    s_lo = jnp.sum(jnp.where(low, sq, 0.0), axis=-1, keepdims=True)
    ms = jnp.where(low, s_lo, s_all - s_lo) * (1.0 / B_HEAD_DIM)
    return seg * lax.rsqrt(ms + EPS) * g2


def _silu(y):
    return y / (1.0 + jnp.exp(-y))


def _proj_a_kernel(x_ref, g_ref, w_ref, kvg_ref, wkv_ref, qg_ref, kg_ref, qa_ref, ka_ref, va_ref, h_ref):
    h = _rms_rows(x_ref[...], g_ref[...]).astype(BF16)
    h_ref[...] = h
    y = _dot_nt(h, w_ref[...])
    scale = A_HEAD_DIM ** -0.5 * LOG2E
    for hh in range(A_HEADS):
        sl = slice(hh * A_HEAD_DIM, (hh + 1) * A_HEAD_DIM)
        qa_ref[:, sl] = (_headnorm128(y[:, sl], qg_ref[...]) * scale).astype(BF16)
    ckv = _rms_rows(y[:, A_WIDTH:A_WIDTH + A_KV_RANK], kvg_ref[...]).astype(BF16)
    kv = _dot(ckv, wkv_ref[...])
    for hh in range(A_HEADS):
        sl = slice(hh * A_HEAD_DIM, (hh + 1) * A_HEAD_DIM)
        ka_ref[:, sl] = _headnorm128(kv[:, sl], kg_ref[...]).astype(BF16)
    va_ref[...] = kv[:, A_WIDTH:].astype(BF16)


def _proj_i_kernel(h_ref, w_ref, lng_ref, lnb_ref, iq_ref, ik_ref, iw_ref):
    y = _dot_nt(h_ref[...], w_ref[...])
    k0 = IDX_HEADS * IDX_DIM
    for p in range(IDX_PAIRS):
        iq_ref[p] = y[:, p * LANES:(p + 1) * LANES].astype(BF16)
    ik = y[:, k0:k0 + IDX_DIM]
    mu = jnp.mean(ik, axis=-1, keepdims=True)
    d = ik - mu
    var = jnp.mean(d * d, axis=-1, keepdims=True)
    ik_ref[...] = (d * lax.rsqrt(var + EPS) * lng_ref[...] + lnb_ref[...]).astype(BF16)
    w0 = k0 + LANES
    iw_ref[...] = y[:, w0:w0 + IDX_HEADS] * (IDX_HEADS ** -0.5) * (IDX_DIM ** -0.5)


def _proj_g_kernel(h_ref, w_ref, ga_ref, gb_ref, gc_ref):
    y = _silu(_dot_nt(h_ref[...], w_ref[...]))
    ga_ref[...] = y[:, :A_WIDTH].astype(BF16)
    gb_ref[...] = y[:, A_WIDTH:A_WIDTH + B_WIDTH].astype(BF16)
    gc_ref[...] = y[:, A_WIDTH + B_WIDTH:].astype(BF16)


def _proj_bc_kernel(h_ref, w_ref, qbg_ref, kbg_ref, qcg_ref,
                    qb_ref, kb_ref, vb_ref, qc_ref, kbs_ref, vbs_ref):
    y = _dot_nt(h_ref[...], w_ref[...])
    sb = B_HEAD_DIM ** -0.5
    for p in range(B_WIDTH // LANES):
        sl = slice(p * LANES, (p + 1) * LANES)
        qb_ref[:, sl] = (_headnorm64_pair(y[:, sl], qbg_ref[...]) * sb).astype(BF16)
    k0 = B_WIDTH
    kb_ref[...] = _headnorm64_pair(y[:, k0:k0 + B_KV_WIDTH], kbg_ref[...]).astype(BF16)
    vb_ref[...] = y[:, k0 + B_KV_WIDTH:k0 + 2 * B_KV_WIDTH].astype(BF16)
    s0 = k0 + 2 * B_KV_WIDTH + C_WIDTH
    kbs_ref[...] = _headnorm64_pair(y[:, s0:s0 + B_KV_WIDTH], kbg_ref[...]).astype(BF16)
    vbs_ref[...] = y[:, s0 + B_KV_WIDTH:s0 + 2 * B_KV_WIDTH].astype(BF16)
    c0 = k0 + 2 * B_KV_WIDTH
    sc = C_HEAD_DIM ** -0.5
    for hh in range(C_HEADS):
        sl = slice(hh * C_HEAD_DIM, (hh + 1) * C_HEAD_DIM)
        qc_ref[:, sl] = (_headnorm128(y[:, c0 + hh * C_HEAD_DIM:c0 + (hh + 1) * C_HEAD_DIM],
                                      qcg_ref[...]) * sc).astype(BF16)


def _proj_mem_kernel(x_ref, g_ref, w_ref, kg_ref, kc_ref, vc_ref):
    h = _rms_rows(x_ref[...], g_ref[...]).astype(BF16)
    y = _dot(h, w_ref[...])
    for hh in range(C_HEADS):
        sl = slice(hh * C_HEAD_DIM, (hh + 1) * C_HEAD_DIM)
        kc_ref[:, sl] = _headnorm128(y[:, sl], kg_ref[...]).astype(BF16)
    vc_ref[...] = y[:, C_WIDTH:].astype(BF16)


def _row_spec(tm, cols):
    return pl.BlockSpec((tm, cols), lambda i: (i, 0))


def _full_spec(shape):
    nd = len(shape)
    return pl.BlockSpec(shape, lambda i: (0,) * nd, pipeline_mode=pl.Buffered(1))


def _proj_call(kernel_fn, name, x2d, tm, consts, out_shapes, out_specs):
    n, d = x2d.shape
    in_specs = [_row_spec(tm, d)] + [_full_spec(c.shape) for c in consts]
    return pl.pallas_call(
        kernel_fn,
        grid=(n // tm,),
        in_specs=in_specs,
        out_specs=out_specs,
        out_shape=out_shapes,
        compiler_params=_cparams(("parallel",)),
        name=name,
    )(x2d, *consts)


def _t5_bucket_np(dist):
    n = np.maximum(dist, 0)
    max_exact = NUM_BUCKETS // 2
    nf = np.maximum(n, 1).astype(np.float32)
    large = max_exact + (np.log(nf / max_exact) / math.log(MAX_DISTANCE / max_exact)
                         * (NUM_BUCKETS - max_exact)).astype(np.int32)
    large = np.minimum(large, NUM_BUCKETS - 1)
    return np.where(n < max_exact, n, large).astype(np.int32)


def _bias_tile_kernel(tab_ref, bucket_ref, o_ref, *, head0, shift_bucket, scale):
    h = pl.program_id(0) + head0
    bucket = bucket_ref[...]
    acc = jnp.zeros(bucket.shape, F32)
    for b in range(NUM_BUCKETS):
        acc = jnp.where(bucket == b, tab_ref[b, h], acc)
    if shift_bucket is not None:
        acc = acc - tab_ref[shift_bucket, h]
    o_ref[...] = jnp.where(bucket < 0, NEG_BIG, acc * scale)


def _bias_tiles(rel_bias, bucket_np, head0, nheads, shift_bucket, scale):
    r, c = bucket_np.shape
    return pl.pallas_call(
        functools.partial(_bias_tile_kernel, head0=head0, shift_bucket=shift_bucket, scale=scale),
        grid=(nheads,),
        in_specs=[pl.BlockSpec(memory_space=pltpu.SMEM), pl.BlockSpec((r, c), lambda h: (0, 0))],
        out_specs=pl.BlockSpec((None, r, c), lambda h: (h, 0, 0)),
        out_shape=jax.ShapeDtypeStruct((nheads, r, c), F32),
        compiler_params=_cparams(("arbitrary",)),
        name="bias_tiles",
    )(rel_bias, jnp.asarray(bucket_np))


def _dsa_kernel(iq_ref, iw_ref, ikz_ref, q_ref, k_ref, v_ref, g_ref, rb_ref, o_ref,
                sc_ref, wb_ref, lo_ref, hi_ref, mid_ref, clo_ref, s_ref, m_ref, l_ref, acc_ref, *, topk):
    T = DSA_T
    qi = pl.program_id(1)
    kf = float(topk)

    def lanes(col):
        return jnp.broadcast_to(col, (T, LANES))

    def halves(tile):
        return [tile[:, c * LANES:(c + 1) * LANES] for c in range(T // LANES)]

    for hh in range(IDX_HEADS):
        wb_ref[hh] = lanes(iw_ref[:, hh:hh + 1])

    def score_chunk(j, diag):
        keys = pl.ds(pl.multiple_of(j * T, T), T)
        accs = [jnp.zeros((T, LANES), F32) for _ in range(T // LANES)]
        for hh in range(IDX_HEADS):
            s = _dot(iq_ref[hh // 2], ikz_ref[hh % 2, :, keys])
            w = wb_ref[hh]
            accs = [a + w * jnp.maximum(sh, 0.0) for a, sh in zip(accs, halves(s))]
        if diag:
            row = lax.broadcasted_iota(jnp.int32, (T, LANES), 0)
            col = lax.broadcasted_iota(jnp.int32, (T, LANES), 1)
            causal = [col + c * LANES <= row for c in range(T // LANES)]
            lows = [jnp.where(cm, a, jnp.inf) for cm, a in zip(causal, accs)]
            accs = [jnp.where(cm, a, -jnp.inf) for cm, a in zip(causal, accs)]
        else:
            lows = accs
        for c, a in enumerate(accs):
            sc_ref[j, :, c * LANES:(c + 1) * LANES] = a
        return functools.reduce(jnp.minimum, lows), accs

    def p1_body(j, carry):
        mn, mxs = carry
        lo_c, his = score_chunk(j, False)
        return jnp.minimum(mn, lo_c), [jnp.maximum(a, b) for a, b in zip(mxs, his)]

    mn0 = jnp.full((T, LANES), jnp.inf, F32)
    mx0 = [jnp.full((T, LANES), -jnp.inf, F32) for _ in range(T // LANES)]

    def p1_pair(jj, carry):
        return p1_body(2 * jj + 1, p1_body(2 * jj, carry))

    n_pairs = lax.div(qi, 2)
    mn, mxs = lax.fori_loop(0, n_pairs, p1_pair, (mn0, mx0))
    mn, mxs = lax.fori_loop(2 * n_pairs, qi, p1_body, (mn, mxs))
    lo_c, his = score_chunk(qi, True)
    mxs = [jnp.maximum(a, b) for a, b in zip(mxs, his)]
    mn = lanes(jnp.min(jnp.minimum(mn, lo_c), axis=-1, keepdims=True))
    mx = lanes(jnp.max(functools.reduce(jnp.maximum, mxs), axis=-1, keepdims=True))
    class_floor = lanes(jnp.min(functools.reduce(jnp.minimum, mxs), axis=-1, keepdims=True))

    def count_lanes(rows, v, strict=False):
        def tile(j, c):
            for c0 in range(0, T, LANES):
                sh = sc_ref[j, rows, c0:c0 + LANES]
                hit = (sh > v) if strict else (sh >= v)
                c = c + jnp.where(hit, 1.0, 0.0)
            return c

        def group(g, c):
            for u in range(COUNT_UNROLL):
                c = tile(g * COUNT_UNROLL + u, c)
            return c

        n_groups = lax.div(qi + 1, COUNT_UNROLL)
        c = lax.fori_loop(0, n_groups, group, jnp.zeros((COUNT_ROWS, LANES), F32))
        return lax.fori_loop(n_groups * COUNT_UNROLL, qi + 1, tile, c)

    def row_total(c):
        return jnp.broadcast_to(jnp.sum(c, axis=-1, keepdims=True), c.shape)

    slabs = [slice(r0, r0 + COUNT_ROWS) for r0 in range(0, T, COUNT_ROWS)]
    n_causal = (qi * T + 1 + lax.broadcasted_iota(jnp.int32, (T, LANES), 0)).astype(F32)
    keep_all = n_causal <= kf
    lo0 = jnp.where(keep_all, mn, jnp.maximum(mn, class_floor))
    above_max = mx + jnp.maximum(jnp.abs(mx) * (2.0 ** -20), 1e-30)
    hi0 = jnp.where(keep_all, mn, above_max)
    lo_ref[...] = lo0
    hi_ref[...] = hi0
    mid_ref[...] = lo0 + 0.5 * (hi0 - lo0)
    clo_ref[...] = n_causal

    def bis_body(_):
        partial = [count_lanes(rows, mid_ref[rows]) for rows in slabs]
        widths = [hi_ref[rows] - lo_ref[rows] for rows in slabs]
        open_width = jnp.max(functools.reduce(jnp.maximum, widths))
        for rows, c in zip(slabs, partial):
            lo, hi, mid = lo_ref[rows], hi_ref[rows], mid_ref[rows]
            cnt = row_total(c)
            up = cnt >= kf
            lo = jnp.where(up, mid, lo)
            c_lo = jnp.where(up, cnt, clo_ref[rows])
            hi = jnp.where(up, hi, mid)
            nxt = lo + 0.5 * (hi - lo)
            stop = jnp.logical_or(c_lo == kf, jnp.logical_or(nxt <= lo, nxt >= hi))
            hi = jnp.where(stop, lo, hi)
            lo_ref[rows] = lo
            hi_ref[rows] = hi
            mid_ref[rows] = jnp.where(stop, lo, nxt)
            clo_ref[rows] = c_lo
        return open_width

    lax.while_loop(lambda w: w > 0.0, bis_body, jnp.max(hi0 - lo0))
    thr = lo_ref[...]

    tied_f = jnp.where(clo_ref[...] > kf, 1.0, 0.0)

    @pl.when(jnp.max(tied_f) > 0.0)
    def _():
        above = jnp.concatenate(
            [row_total(count_lanes(rows, thr[rows], strict=True)) for rows in slabs], axis=0)
        need = (kf - above)[:, :1]
        r = lax.broadcasted_iota(jnp.int32, (T, T), 0)
        c = lax.broadcasted_iota(jnp.int32, (T, T), 1)
        before = jnp.where(r < c, 1.0, 0.0).astype(BF16)
        thr_col = thr[:, :1]
        tied_col = tied_f[:, :1] > 0.0

        def body(j, seen):
            s = sc_ref[j]
            eq = jnp.where(s == thr_col, 1.0, 0.0)
            rank = seen + _dot(eq.astype(BF16), before)
            drop = jnp.logical_and(tied_col, jnp.logical_and(eq > 0.0, rank >= need))
            sc_ref[j] = jnp.where(drop, -jnp.inf, s)
            return seen + jnp.sum(eq, axis=-1, keepdims=True)

        lax.fori_loop(0, qi + 1, body, jnp.zeros((T, 1), F32))

    m_ref[...] = jnp.full(m_ref.shape, NEG_BIG, F32)
    l_ref[...] = jnp.zeros(l_ref.shape, F32)
    acc_ref[...] = jnp.zeros(acc_ref.shape, F32)

    def attend(j0, ntiles, band):
        sel = [jnp.where(sh >= thr, 0.0, NEG_BIG) for t in range(ntiles) for sh in halves(sc_ref[j0 + t])]
        start = pl.multiple_of(j0 * T, T)
        keys = pl.ds(start, ntiles * T)
        width = ntiles * T

        def logits_to_scratch(hh):
            sl = slice(hh * A_HEAD_DIM, (hh + 1) * A_HEAD_DIM)
            s_ref[hh % 2, :, :width] = _dot_nt(q_ref[:, sl], k_ref[keys, sl])

        logits_to_scratch(0)
        pending = None
        for hh in range(A_HEADS):
            sl = slice(hh * A_HEAD_DIM, (hh + 1) * A_HEAD_DIM)
            if hh + 1 < A_HEADS:
                logits_to_scratch(hh + 1)
            s = s_ref[hh % 2, :, :width]
            if band is not None:
                s = s + rb_ref[hh, :, band * T:(band + ntiles) * T]
            sh = [s[:, c * LANES:(c + 1) * LANES] + m for c, m in enumerate(sel)]
            m_prev = m_ref[hh]
            m_cur = jnp.max(functools.reduce(jnp.maximum, sh), axis=-1, keepdims=True)
            m_new = jnp.maximum(m_prev, lanes(m_cur))
            alpha = jnp.exp2(m_prev - m_new)
            ph = [jnp.exp2(x - m_new) for x in sh]
            l_ref[hh] = alpha * l_ref[hh] + functools.reduce(jnp.add, ph)
            p = jnp.concatenate([x.astype(BF16) for x in ph], axis=-1)
            acc_ref[hh] = alpha * acc_ref[hh]
            m_ref[hh] = m_new
            if pending is not None:
                acc_ref[pending[0]] += pending[1]
            pending = (hh, _dot(p, v_ref[keys, sl]))
        acc_ref[pending[0]] += pending[1]

    n_far = jnp.maximum(qi - 1, 0)
    done_tiles = 0
    width = FAR_TILES
    while width >= 1:
        trips = lax.div(n_far - done_tiles, width)

        def far_body(i, c, width=width, base=done_tiles):
            attend(base + i * width, width, None)
            return c

        lax.fori_loop(0, trips, far_body, 0)
        done_tiles = done_tiles + trips * width
        width //= 2

    @pl.when(qi > 0)
    def _():
        attend(qi - 1, 2, 0)

    @pl.when(qi == 0)
    def _():
        attend(0, 1, 1)

    for hh in range(A_HEADS):
        sl = slice(hh * A_HEAD_DIM, (hh + 1) * A_HEAD_DIM)
        den = jnp.sum(l_ref[hh], axis=-1, keepdims=True)
        o_ref[:, sl] = (acc_ref[hh] / den * g_ref[:, sl].astype(F32)).astype(BF16)


def _dsa_call(iq, iw, ikz, qa, ka, va, ga, rb, topk):
    bsz, seq, _ = qa.shape
    T = DSA_T
    nq = seq // T
    assert topk <= T and seq % T == 0
    once = pl.Buffered(1)
    return pl.pallas_call(
        functools.partial(_dsa_kernel, topk=topk),
        grid=(bsz, nq),
        in_specs=[
            pl.BlockSpec((None, IDX_PAIRS, T, LANES), lambda b, i: (b, 0, i, 0)),
            pl.BlockSpec((None, T, IDX_HEADS), lambda b, i: (b, i, 0)),
            pl.BlockSpec((None, 2, LANES, seq), lambda b, i: (b, 0, 0, 0), pipeline_mode=once),
            pl.BlockSpec((None, T, A_WIDTH), lambda b, i: (b, i, 0)),
            pl.BlockSpec((None, seq, A_WIDTH), lambda b, i: (b, 0, 0), pipeline_mode=once),
            pl.BlockSpec((None, seq, A_WIDTH), lambda b, i: (b, 0, 0), pipeline_mode=once),
            pl.BlockSpec((None, T, A_WIDTH), lambda b, i: (b, i, 0)),
            pl.BlockSpec((A_HEADS, T, 2 * T), lambda b, i: (0, 0, 0), pipeline_mode=once),
        ],
        out_specs=pl.BlockSpec((None, T, A_WIDTH), lambda b, i: (b, i, 0)),
        out_shape=jax.ShapeDtypeStruct((bsz, seq, A_WIDTH), BF16),
        scratch_shapes=[
            pltpu.VMEM((nq, T, T), F32),
            pltpu.VMEM((IDX_HEADS, T, LANES), F32),
            pltpu.VMEM((T, LANES), F32),
            pltpu.VMEM((T, LANES), F32),
            pltpu.VMEM((T, LANES), F32),
            pltpu.VMEM((T, LANES), F32),
            pltpu.VMEM((2, T, FAR_TILES * T), F32),
            pltpu.VMEM((A_HEADS, T, LANES), F32),
            pltpu.VMEM((A_HEADS, T, LANES), F32),
            pltpu.VMEM((A_HEADS, T, A_HEAD_DIM), F32),
        ],
        compiler_params=_cparams(("arbitrary", "arbitrary")),
        name="dsa",
    )(iq, iw, ikz, qa, ka, va, ga, rb)


def _swa_kernel(sink_ref, q_ref, kc_ref, vc_ref, kcs_ref, vcs_ref, kp_ref, vp_ref, kps_ref, vps_ref,
                g_ref, bias_ref, o_ref):
    first = pl.program_id(1) == 0
    blk = WINDOW
    tiles = B_HEADS // B_KV_HEADS // 2
    low_half = lax.broadcasted_iota(jnp.int32, (2 * blk, LANES), 1) < B_HEAD_DIM
    ones = jnp.ones((2 * blk, LANES), BF16)
    for sb in range(SWA_T // blk):
        rows = slice(sb * blk, (sb + 1) * blk)
        prev = slice((sb - 1) * blk, sb * blk)

        def window(cur_ref, prev_ref):
            before = prev_ref[...] if sb == 0 else cur_ref[prev, :]
            return jnp.concatenate([before, cur_ref[rows, :]], axis=0)

        k_nat, v_nat = window(kc_ref, kp_ref), window(vc_ref, vp_ref)
        k_swp, v_swp = window(kcs_ref, kps_ref), window(vcs_ref, vps_ref)
        for g in range(B_KV_HEADS):
            q = jnp.concatenate(
                [q_ref[rows, (tiles * g + i) * LANES:(tiles * g + i + 1) * LANES] for i in range(tiles)], axis=0)
            res = None
            for e in range(2):
                keep = low_half if e == 0 else jnp.logical_not(low_half)
                k_src, v_src = (k_nat, v_nat) if g == e else (k_swp, v_swp)
                kk = jnp.where(keep, k_src, jnp.zeros_like(k_src))
                vv = jnp.where(keep, v_src, jnp.zeros_like(v_src))
                logits = _dot_nt(q, kk) + bias_ref[2 * g + e]
                l_prev, l_cur = logits[:, :blk], logits[:, blk:]
                if sb == 0:
                    l_prev = jnp.where(first, NEG_BIG, l_prev)
                sink = jnp.concatenate(
                    [jnp.full((blk, blk), sink_ref[2 * (tiles * g + i) + e], F32) for i in range(tiles)], axis=0)
                m = jnp.max(jnp.maximum(l_prev, l_cur), axis=-1, keepdims=True)
                m = jnp.maximum(jnp.broadcast_to(m, sink.shape), sink)
                p = jnp.concatenate([jnp.exp(l_prev - m).astype(BF16), jnp.exp(l_cur - m).astype(BF16)], axis=1)
                pv = _dot(p, jnp.concatenate([vv, ones], axis=1))
                part = pv[:, :LANES] / (pv[:, LANES:] + jnp.exp(sink - m))
                res = part if res is None else res + part
            for i in range(tiles):
                cols = slice((tiles * g + i) * LANES, (tiles * g + i + 1) * LANES)
                o_ref[rows, cols] = (res[i * blk:(i + 1) * blk] * g_ref[rows, cols].astype(F32)).astype(BF16)


def _swa_call(sinks, qb, kb, vb, kbs, vbs, gb, bias):
    bsz, seq, _ = qb.shape
    T = SWA_T
    per = T // WINDOW
    cur = lambda b, i: (b, i, 0)
    prev = lambda b, i: (b, jnp.maximum(i * per - 1, 0), 0)
    kv_cur = pl.BlockSpec((None, T, B_KV_WIDTH), cur)
    kv_prev = pl.BlockSpec((None, WINDOW, B_KV_WIDTH), prev)
    return pl.pallas_call(
        _swa_kernel,
        grid=(bsz, seq // T),
        in_specs=[
            pl.BlockSpec(memory_space=pltpu.SMEM),
            pl.BlockSpec((None, T, B_WIDTH), cur),
            kv_cur, kv_cur, kv_cur, kv_cur,
            kv_prev, kv_prev, kv_prev, kv_prev,
            pl.BlockSpec((None, T, B_WIDTH), cur),
            pl.BlockSpec(bias.shape, lambda b, i: (0, 0, 0)),
        ],
        out_specs=pl.BlockSpec((None, T, B_WIDTH), cur),
        out_shape=jax.ShapeDtypeStruct((bsz, seq, B_WIDTH), BF16),
        compiler_params=_cparams(("parallel", "parallel")),
        name="swa",
    )(sinks, qb, kb, vb, kbs, vbs, kb, vb, kbs, vbs, gb, bias)


def _mem_kernel(q_ref, k_ref, v_ref, g_ref, o_ref):
    for hh in range(C_HEADS):
        sl = slice(hh * C_HEAD_DIM, (hh + 1) * C_HEAD_DIM)
        s = _dot_nt(q_ref[:, sl], k_ref[:, sl])
        m = jnp.max(s, axis=-1, keepdims=True)
        p = jnp.exp(s - m)
        den = jnp.sum(p, axis=-1, keepdims=True)
        o = _dot(p.astype(BF16), v_ref[:, sl]) / den
        o_ref[:, sl] = (o * g_ref[:, sl].astype(F32)).astype(BF16)


def _mem_call(qc, kc, vc, gc):
    bsz, seq, _ = qc.shape
    mlen = kc.shape[1]
    T = SWA_T
    cur = lambda b, i: (b, i, 0)
    whole = lambda b, i: (b, 0, 0)
    return pl.pallas_call(
        _mem_kernel,
        grid=(bsz, seq // T),
        in_specs=[
            pl.BlockSpec((None, T, C_WIDTH), cur),
            pl.BlockSpec((None, mlen, C_WIDTH), whole),
            pl.BlockSpec((None, mlen, C_WIDTH), whole),
            pl.BlockSpec((None, T, C_WIDTH), cur),
        ],
        out_specs=pl.BlockSpec((None, T, C_WIDTH), cur),
        out_shape=jax.ShapeDtypeStruct((bsz, seq, C_WIDTH), BF16),
        compiler_params=_cparams(("parallel", "parallel")),
        name="mem_attn",
    )(qc, kc, vc, gc)


def _final_kernel(x_ref, h_ref, oa_ref, ob_ref, oc_ref, wma_ref, wmb_ref, wmc_ref,
                  ba_ref, bb_ref, bc_ref, wua_ref, wub_ref, wuc_ref, wo_ref, out_ref, acc_ref):
    c = pl.program_id(1)

    @pl.when(c == 0)
    def _():
        acc_ref[...] = jnp.zeros(acc_ref.shape, F32)

    h = h_ref[...]

    def branch(o_ref, wm_ref, b_ref, wu_ref):
        gate = 1.0 / (1.0 + jnp.exp(-(_dot_nt(h, wm_ref[...]) + b_ref[...])))
        return gate * _dot(o_ref[...], wu_ref[...])

    merged = (branch(oa_ref, wma_ref, ba_ref, wua_ref) + branch(ob_ref, wmb_ref, bb_ref, wub_ref)
              + branch(oc_ref, wmc_ref, bc_ref, wuc_ref))
    acc_ref[...] += _dot(merged.astype(BF16), wo_ref[...])

    @pl.when(c == pl.num_programs(1) - 1)
    def _():
        out_ref[...] = x_ref[...] + acc_ref[...]


def _final_call(x2d, h, oa, ob, oc, wmix, gate_bias, wua, wub, wuc, wo):
    n, d = x2d.shape
    tm = FINAL_ROWS
    nch = FINAL_NCHUNK
    cw = d // nch
    row = lambda i, c: (i, 0)

    def col(br):
        return lambda i, c: (0, br * nch + c)

    def wrow(br):
        return lambda i, c: (br * nch + c, 0)

    return pl.pallas_call(
        _final_kernel,
        grid=(n // tm, nch),
        in_specs=[
            pl.BlockSpec((tm, d), row),
            pl.BlockSpec((tm, d), row),
            pl.BlockSpec((tm, A_WIDTH), row),
            pl.BlockSpec((tm, B_WIDTH), row),
            pl.BlockSpec((tm, C_WIDTH), row),
            pl.BlockSpec((cw, d), wrow(0)), pl.BlockSpec((cw, d), wrow(1)), pl.BlockSpec((cw, d), wrow(2)),
            pl.BlockSpec((1, cw), col(0)), pl.BlockSpec((1, cw), col(1)), pl.BlockSpec((1, cw), col(2)),
            pl.BlockSpec((A_WIDTH, cw), lambda i, c: (0, c)),
            pl.BlockSpec((B_WIDTH, cw), lambda i, c: (0, c)),
            pl.BlockSpec((C_WIDTH, cw), lambda i, c: (0, c)),
            pl.BlockSpec((cw, d), lambda i, c: (c, 0)),
        ],
        out_specs=pl.BlockSpec((tm, d), row),
        out_shape=jax.ShapeDtypeStruct((n, d), F32),
        scratch_shapes=[pltpu.VMEM((tm, d), F32)],
        compiler_params=_cparams(("parallel", "arbitrary")),
        name="merge_out",
    )(x2d, h, oa, ob, oc, wmix, wmix, wmix, gate_bias, gate_bias, gate_bias, wua, wub, wuc, wo)


def _w_in_groups(d):
    sizes = (A_WIDTH, A_KV_RANK, IDX_HEADS * IDX_DIM, IDX_DIM, IDX_HEADS, A_WIDTH,
             B_WIDTH, B_KV_WIDTH, B_KV_WIDTH, B_WIDTH, C_WIDTH, C_WIDTH, N_BRANCH * d)
    cuts = np.cumsum((0,) + sizes).tolist()
    (aq, ackv, iq, ik, iw, ag, bq, bk, bv, bg, cq, cg, mix) = [
        (cuts[i], cuts[i + 1]) for i in range(len(sizes))]
    return [
        [(aq[0], ackv[1])],
        [(iq[0], ik[1]), LANES - IDX_DIM, iw, LANES - IDX_HEADS],
        [ag, bg, cg],
        [(bq[0], bv[1]), cq,
         (bk[0] + B_HEAD_DIM, bk[1]), (bk[0], bk[0] + B_HEAD_DIM),
         (bv[0] + B_HEAD_DIM, bv[1]), (bv[0], bv[0] + B_HEAD_DIM)],
        [mix],
    ]


def _group_width(group):
    return sum(p if isinstance(p, int) else p[1] - p[0] for p in group)


def _regroup_kernel(w_ref, *out_refs, groups):
    for o_ref, group in zip(out_refs, groups):
        row = 0
        for part in group:
            if isinstance(part, int):
                o_ref[row:row + part, :] = jnp.zeros((part, o_ref.shape[1]), BF16)
                row += part
            else:
                o_ref[row:row + part[1] - part[0], :] = w_ref[part[0]:part[1], :].astype(BF16)
                row += part[1] - part[0]


def _regroup_w_in(w_in_t):
    cols, d = w_in_t.shape
    groups = _w_in_groups(d)
    lanes = REGROUP_LANES
    return pl.pallas_call(
        functools.partial(_regroup_kernel, groups=groups),
        grid=(d // lanes,),
        in_specs=[pl.BlockSpec((cols, lanes), lambda i: (0, i))],
        out_specs=[pl.BlockSpec((_group_width(g), lanes), lambda i: (0, i)) for g in groups],
        out_shape=[jax.ShapeDtypeStruct((_group_width(g), d), BF16) for g in groups],
        compiler_params=_cparams(("parallel",)),
        name="regroup_w_in",
    )(w_in_t)


def _layer(x, mem, norm_g, w_in, kv_norm_g, w_kv_up, idx_k_ln_g, idx_k_ln_b, q_norm_a, k_norm_a,
           q_norm_b, k_norm_b, sinks_b, mem_norm_g, w_mem_kv, q_norm_c, k_norm_c,
           w_up_a, w_up_b, w_up_c, gate_bias, w_o, rel_bias):
    bsz, seq, d = x.shape
    mlen = mem.shape[1]
    n = bsz * seq
    tm = ROW_TILE
    topk = min(TOPK_MAX, seq // 4)
    x2d = x.reshape(n, d)
    row2 = lambda v: v.reshape(1, -1)

    w_grp_a, w_grp_i, w_grp_g, w_grp_bc, w_mix = _regroup_w_in(jnp.swapaxes(w_in, 0, 1))
    g_x = row2(norm_g)

    qa, ka, va, h = _proj_call(
        _proj_a_kernel, "proj_a", x2d, tm,
        [g_x, w_grp_a, row2(kv_norm_g), w_kv_up.astype(BF16), row2(q_norm_a), row2(k_norm_a)],
        [jax.ShapeDtypeStruct((n, A_WIDTH), BF16)] * 3 + [jax.ShapeDtypeStruct((n, d), BF16)],
        [_row_spec(tm, A_WIDTH)] * 3 + [_row_spec(tm, d)])

    per_b = seq // tm
    iq, ik, iw = _proj_call(
        _proj_i_kernel, "proj_i", h, tm,
        [w_grp_i, row2(idx_k_ln_g), row2(idx_k_ln_b)],
        [jax.ShapeDtypeStruct((bsz, IDX_PAIRS, seq, LANES), BF16),
         jax.ShapeDtypeStruct((n, IDX_DIM), BF16),
         jax.ShapeDtypeStruct((n, IDX_HEADS), F32)],
        [pl.BlockSpec((None, IDX_PAIRS, tm, LANES), lambda i: (i // per_b, 0, i % per_b, 0)),
         _row_spec(tm, IDX_DIM), _row_spec(tm, IDX_HEADS)])

    ga, gb, gc = _proj_call(
        _proj_g_kernel, "proj_g", h, tm, [w_grp_g],
        [jax.ShapeDtypeStruct((n, A_WIDTH), BF16), jax.ShapeDtypeStruct((n, B_WIDTH), BF16),
         jax.ShapeDtypeStruct((n, C_WIDTH), BF16)],
        [_row_spec(tm, A_WIDTH), _row_spec(tm, B_WIDTH), _row_spec(tm, C_WIDTH)])

    bc_widths = (B_WIDTH, B_KV_WIDTH, B_KV_WIDTH, C_WIDTH, B_KV_WIDTH, B_KV_WIDTH)
    qb, kb, vb, qc, kbs, vbs = _proj_call(
        _proj_bc_kernel, "proj_bc", h, tm,
        [w_grp_bc, row2(jnp.tile(q_norm_b, 2)), row2(jnp.tile(k_norm_b, 2)), row2(q_norm_c)],
        [jax.ShapeDtypeStruct((n, w), BF16) for w in bc_widths],
        [_row_spec(tm, w) for w in bc_widths])

    mrows = bsz * mlen
    kc, vc = _proj_call(
        _proj_mem_kernel, "proj_mem", mem.reshape(mrows, d), min(tm, mrows),
        [row2(mem_norm_g), w_mem_kv.astype(BF16), row2(k_norm_c)],
        [jax.ShapeDtypeStruct((mrows, C_WIDTH), BF16)] * 2,
        [_row_spec(min(tm, mrows), C_WIDTH)] * 2)

    T = DSA_T
    i_idx = np.arange(T)[:, None]
    dist_a = T + i_idx - np.arange(2 * T)[None, :]
    rb_a = _bias_tiles(rel_bias, _t5_bucket_np(dist_a), 0, A_HEADS, NUM_BUCKETS - 1, LOG2E)
    i_idx = np.arange(WINDOW)[:, None]
    dist_b = WINDOW + i_idx - np.arange(2 * WINDOW)[None, :]
    bucket_b = np.where((dist_b >= 0) & (dist_b < WINDOW), _t5_bucket_np(dist_b), -1).astype(np.int32)
    bias_b = _bias_tiles(rel_bias, bucket_b, A_HEADS, B_HEADS, None, 1.0)
    tiles_b = B_HEADS // B_KV_HEADS // 2
    slot_heads = [2 * (tiles_b * g + i) + e for g in range(B_KV_HEADS) for e in range(2) for i in range(tiles_b)]
    bias_b = bias_b[np.array(slot_heads)].reshape(2 * B_KV_HEADS, tiles_b * WINDOW, 2 * WINDOW)

    r3 = lambda v, w: v.reshape(bsz, seq, w)
    ikt = jnp.swapaxes(ik.reshape(bsz, seq, IDX_DIM), 1, 2)
    zeros = jnp.zeros_like(ikt)
    ikz = jnp.stack([jnp.concatenate([ikt, zeros], axis=1), jnp.concatenate([zeros, ikt], axis=1)], axis=1)
    oa = _dsa_call(iq, iw.reshape(bsz, seq, IDX_HEADS), ikz,
                   r3(qa, A_WIDTH), r3(ka, A_WIDTH),
                   r3(va, A_WIDTH), r3(ga, A_WIDTH), rb_a, topk)
    ob = _swa_call(sinks_b, r3(qb, B_WIDTH), r3(kb, B_KV_WIDTH), r3(vb, B_KV_WIDTH),
                   r3(kbs, B_KV_WIDTH), r3(vbs, B_KV_WIDTH), r3(gb, B_WIDTH), bias_b)
    oc = _mem_call(r3(qc, C_WIDTH), kc.reshape(bsz, mlen, C_WIDTH), vc.reshape(bsz, mlen, C_WIDTH),
                   r3(gc, C_WIDTH))

    out = _final_call(x2d, h, oa.reshape(n, A_WIDTH), ob.reshape(n, B_WIDTH), oc.reshape(n, C_WIDTH),
                      w_mix, row2(gate_bias), w_up_a.astype(BF16), w_up_b.astype(BF16),
                      w_up_c.astype(BF16), w_o.astype(BF16))
    return out.reshape(bsz, seq, d)


def kernel(x, mem, norm_g, w_in, kv_norm_g, w_kv_up, idx_k_ln_g, idx_k_ln_b, q_norm_a, k_norm_a, q_norm_b, k_norm_b, sinks_b, mem_norm_g, w_mem_kv, q_norm_c, k_norm_c, w_up_a, w_up_b, w_up_c, gate_bias, w_o, rel_bias):
    for l in range(norm_g.shape[0]):
        x = _layer(x, mem, norm_g[l], w_in[l], kv_norm_g[l], w_kv_up[l], idx_k_ln_g[l], idx_k_ln_b[l],
                   q_norm_a[l], k_norm_a[l], q_norm_b[l], k_norm_b[l], sinks_b[l], mem_norm_g[l],
                   w_mem_kv[l], q_norm_c[l], k_norm_c[l], w_up_a[l], w_up_b[l], w_up_c[l],
                   gate_bias[l], w_o[l], rel_bias)
    return x
```

```python
import functools
import math

import numpy as np
import jax
import jax.numpy as jnp
from jax import lax
from jax.experimental import pallas as pl
from jax.experimental.pallas import tpu as pltpu

F32 = jnp.float32
BF16 = jnp.bfloat16

EPS = 1e-6
A_HEADS, A_HEAD_DIM, A_KV_RANK = 6, 128, 256
IDX_HEADS, IDX_DIM, TOPK_MAX = 16, 64, 256
IDX_PAIRS = IDX_HEADS // 2
B_HEADS, B_KV_HEADS, B_HEAD_DIM, WINDOW = 12, 2, 64, 128
C_HEADS, C_HEAD_DIM = 4, 128
NUM_BUCKETS, MAX_DISTANCE = 32, 128
N_BRANCH = 3
A_WIDTH = A_HEADS * A_HEAD_DIM
B_WIDTH = B_HEADS * B_HEAD_DIM
B_KV_WIDTH = B_KV_HEADS * B_HEAD_DIM
C_WIDTH = C_HEADS * C_HEAD_DIM

LANES = 128
VMEM_LIMIT = 56 * 1024 * 1024
NEG_BIG = -1e30

ROW_TILE = 1024
DSA_T = 256
SWA_T = 512
FINAL_ROWS = 512
FINAL_NCHUNK = 4
COUNT_ROWS = 128
REGROUP_LANES = 256
COUNT_UNROLL = 4
FAR_TILES = 4
LOG2E = math.log2(math.e)


def _cparams(sem):
    return pltpu.CompilerParams(dimension_semantics=sem, vmem_limit_bytes=VMEM_LIMIT)


def _dot(a, b):
    return jnp.dot(a, b, preferred_element_type=F32)


def _dot_nt(a, b):
    return lax.dot_general(a, b, (((1,), (1,)), ((), ())), preferred_element_type=F32)


def _rms_rows(x, g):
    ms = jnp.mean(x * x, axis=-1, keepdims=True)
    return x * lax.rsqrt(ms + EPS) * g


def _headnorm128(seg, g):
    ms = jnp.mean(seg * seg, axis=-1, keepdims=True)
    return seg * lax.rsqrt(ms + EPS) * g


def _headnorm64_pair(seg, g2):
    sq = seg * seg
    low = lax.broadcasted_iota(jnp.int32, seg.shape, 1) < B_HEAD_DIM
    s_all = jnp.sum(sq, axis=-1, keepdims=True)
    s_lo = jnp.sum(jnp.where(low, sq, 0.0), axis=-1, keepdims=True)
    ms = jnp.where(low, s_lo, s_all - s_lo) * (1.0 / B_HEAD_DIM)
    return seg * lax.rsqrt(ms + EPS) * g2


def _silu(y):
    return y / (1.0 + jnp.exp(-y))


def _proj_a_kernel(x_ref, g_ref, w_ref, kvg_ref, wkv_ref, qg_ref, kg_ref, qa_ref, ka_ref, va_ref, h_ref):
    h = _rms_rows(x_ref[...], g_ref[...]).astype(BF16)
    h_ref[...] = h
    y = _dot_nt(h, w_ref[...])
    scale = A_HEAD_DIM ** -0.5 * LOG2E
    for hh in range(A_HEADS):
        sl = slice(hh * A_HEAD_DIM, (hh + 1) * A_HEAD_DIM)
        qa_ref[:, sl] = (_headnorm128(y[:, sl], qg_ref[...]) * scale).astype(BF16)
    ckv = _rms_rows(y[:, A_WIDTH:A_WIDTH + A_KV_RANK], kvg_ref[...]).astype(BF16)
    kv = _dot(ckv, wkv_ref[...])
    for hh in range(A_HEADS):
        sl = slice(hh * A_HEAD_DIM, (hh + 1) * A_HEAD_DIM)
        ka_ref[:, sl] = _headnorm128(kv[:, sl], kg_ref[...]).astype(BF16)
    va_ref[...] = kv[:, A_WIDTH:].astype(BF16)


def _proj_i_kernel(h_ref, w_ref, lng_ref, lnb_ref, iq_ref, ik_ref, iw_ref):
    y = _dot_nt(h_ref[...], w_ref[...])
    k0 = IDX_HEADS * IDX_DIM
    for p in range(IDX_PAIRS):
        iq_ref[p] = y[:, p * LANES:(p + 1) * LANES].astype(BF16)
    ik = y[:, k0:k0 + IDX_DIM]
    mu = jnp.mean(ik, axis=-1, keepdims=True)
    d = ik - mu
    var = jnp.mean(d * d, axis=-1, keepdims=True)
    ik_ref[...] = (d * lax.rsqrt(var + EPS) * lng_ref[...] + lnb_ref[...]).astype(BF16)
    w0 = k0 + LANES
    iw_ref[...] = y[:, w0:w0 + IDX_HEADS] * (IDX_HEADS ** -0.5) * (IDX_DIM ** -0.5)


def _proj_g_kernel(h_ref, w_ref, ga_ref, gb_ref, gc_ref):
    y = _silu(_dot_nt(h_ref[...], w_ref[...]))
    ga_ref[...] = y[:, :A_WIDTH].astype(BF16)
    gb_ref[...] = y[:, A_WIDTH:A_WIDTH + B_WIDTH].astype(BF16)
    gc_ref[...] = y[:, A_WIDTH + B_WIDTH:].astype(BF16)


def _proj_bc_kernel(h_ref, w_ref, qbg_ref, kbg_ref, qcg_ref,
                    qb_ref, kb_ref, vb_ref, qc_ref, kbs_ref, vbs_ref):
    y = _dot_nt(h_ref[...], w_ref[...])
    sb = B_HEAD_DIM ** -0.5
    for p in range(B_WIDTH // LANES):
        sl = slice(p * LANES, (p + 1) * LANES)
        qb_ref[:, sl] = (_headnorm64_pair(y[:, sl], qbg_ref[...]) * sb).astype(BF16)
    k0 = B_WIDTH
    kb_ref[...] = _headnorm64_pair(y[:, k0:k0 + B_KV_WIDTH], kbg_ref[...]).astype(BF16)
    vb_ref[...] = y[:, k0 + B_KV_WIDTH:k0 + 2 * B_KV_WIDTH].astype(BF16)
    s0 = k0 + 2 * B_KV_WIDTH + C_WIDTH
    kbs_ref[...] = _headnorm64_pair(y[:, s0:s0 + B_KV_WIDTH], kbg_ref[...]).astype(BF16)
    vbs_ref[...] = y[:, s0 + B_KV_WIDTH:s0 + 2 * B_KV_WIDTH].astype(BF16)
    c0 = k0 + 2 * B_KV_WIDTH
    sc = C_HEAD_DIM ** -0.5
    for hh in range(C_HEADS):
        sl = slice(hh * C_HEAD_DIM, (hh + 1) * C_HEAD_DIM)
        qc_ref[:, sl] = (_headnorm128(y[:, c0 + hh * C_HEAD_DIM:c0 + (hh + 1) * C_HEAD_DIM],
                                      qcg_ref[...]) * sc).astype(BF16)


def _proj_mem_kernel(x_ref, g_ref, w_ref, kg_ref, kc_ref, vc_ref):
    h = _rms_rows(x_ref[...], g_ref[...]).astype(BF16)
    y = _dot(h, w_ref[...])
    for hh in range(C_HEADS):
        sl = slice(hh * C_HEAD_DIM, (hh + 1) * C_HEAD_DIM)
        kc_ref[:, sl] = _headnorm128(y[:, sl], kg_ref[...]).astype(BF16)
    vc_ref[...] = y[:, C_WIDTH:].astype(BF16)


def _row_spec(tm, cols):
    return pl.BlockSpec((tm, cols), lambda i: (i, 0))


def _full_spec(shape):
    nd = len(shape)
    return pl.BlockSpec(shape, lambda i: (0,) * nd, pipeline_mode=pl.Buffered(1))


def _proj_call(kernel_fn, name, x2d, tm, consts, out_shapes, out_specs):
    n, d = x2d.shape
    in_specs = [_row_spec(tm, d)] + [_full_spec(c.shape) for c in consts]
    return pl.pallas_call(
        kernel_fn,
        grid=(n // tm,),
        in_specs=in_specs,
        out_specs=out_specs,
        out_shape=out_shapes,
        compiler_params=_cparams(("parallel",)),
        name=name,
    )(x2d, *consts)


def _t5_bucket_np(dist):
    n = np.maximum(dist, 0)
    max_exact = NUM_BUCKETS // 2
    nf = np.maximum(n, 1).astype(np.float32)
    large = max_exact + (np.log(nf / max_exact) / math.log(MAX_DISTANCE / max_exact)
                         * (NUM_BUCKETS - max_exact)).astype(np.int32)
    large = np.minimum(large, NUM_BUCKETS - 1)
    return np.where(n < max_exact, n, large).astype(np.int32)


def _bias_tile_kernel(tab_ref, bucket_ref, o_ref, *, head0, shift_bucket, scale):
    h = pl.program_id(0) + head0
    bucket = bucket_ref[...]
    acc = jnp.zeros(bucket.shape, F32)
    for b in range(NUM_BUCKETS):
        acc = jnp.where(bucket == b, tab_ref[b, h], acc)
    if shift_bucket is not None:
        acc = acc - tab_ref[shift_bucket, h]
    o_ref[...] = jnp.where(bucket < 0, NEG_BIG, acc * scale)


def _bias_tiles(rel_bias, bucket_np, head0, nheads, shift_bucket, scale):
    r, c = bucket_np.shape
    return pl.pallas_call(
        functools.partial(_bias_tile_kernel, head0=head0, shift_bucket=shift_bucket, scale=scale),
        grid=(nheads,),
        in_specs=[pl.BlockSpec(memory_space=pltpu.SMEM), pl.BlockSpec((r, c), lambda h: (0, 0))],
        out_specs=pl.BlockSpec((None, r, c), lambda h: (h, 0, 0)),
        out_shape=jax.ShapeDtypeStruct((nheads, r, c), F32),
        compiler_params=_cparams(("arbitrary",)),
        name="bias_tiles",
    )(rel_bias, jnp.asarray(bucket_np))


def _dsa_kernel(iq_ref, iw_ref, ikz_ref, q_ref, k_ref, v_ref, g_ref, rb_ref, o_ref,
                sc_ref, wb_ref, lo_ref, hi_ref, mid_ref, clo_ref, s_ref, m_ref, l_ref, acc_ref, *, topk):
    T = DSA_T
    qi = pl.program_id(1)
    kf = float(topk)

    def lanes(col):
        return jnp.broadcast_to(col, (T, LANES))

    def halves(tile):
        return [tile[:, c * LANES:(c + 1) * LANES] for c in range(T // LANES)]

    for hh in range(IDX_HEADS):
        wb_ref[hh] = lanes(iw_ref[:, hh:hh + 1])

    def score_chunk(j, diag):
        keys = pl.ds(pl.multiple_of(j * T, T), T)
        accs = [jnp.zeros((T, LANES), F32) for _ in range(T // LANES)]
        for hh in range(IDX_HEADS):
            s = _dot(iq_ref[hh // 2], ikz_ref[hh % 2, :, keys])
            w = wb_ref[hh]
            accs = [a + w * jnp.maximum(sh, 0.0) for a, sh in zip(accs, halves(s))]
        if diag:
            row = lax.broadcasted_iota(jnp.int32, (T, LANES), 0)
            col = lax.broadcasted_iota(jnp.int32, (T, LANES), 1)
            causal = [col + c * LANES <= row for c in range(T // LANES)]
            lows = [jnp.where(cm, a, jnp.inf) for cm, a in zip(causal, accs)]
            accs = [jnp.where(cm, a, -jnp.inf) for cm, a in zip(causal, accs)]
        else:
            lows = accs
        for c, a in enumerate(accs):
            sc_ref[j, :, c * LANES:(c + 1) * LANES] = a
        return functools.reduce(jnp.minimum, lows), accs

    def p1_body(j, carry):
        mn, mxs = carry
        lo_c, his = score_chunk(j, False)
        return jnp.minimum(mn, lo_c), [jnp.maximum(a, b) for a, b in zip(mxs, his)]

    mn0 = jnp.full((T, LANES), jnp.inf, F32)
    mx0 = [jnp.full((T, LANES), -jnp.inf, F32) for _ in range(T // LANES)]

    def p1_pair(jj, carry):
        return p1_body(2 * jj + 1, p1_body(2 * jj, carry))

    n_pairs = lax.div(qi, 2)
    mn, mxs = lax.fori_loop(0, n_pairs, p1_pair, (mn0, mx0))
    mn, mxs = lax.fori_loop(2 * n_pairs, qi, p1_body, (mn, mxs))
    lo_c, his = score_chunk(qi, True)
    mxs = [jnp.maximum(a, b) for a, b in zip(mxs, his)]
    mn = lanes(jnp.min(jnp.minimum(mn, lo_c), axis=-1, keepdims=True))
    mx = lanes(jnp.max(functools.reduce(jnp.maximum, mxs), axis=-1, keepdims=True))
    class_floor = lanes(jnp.min(functools.reduce(jnp.minimum, mxs), axis=-1, keepdims=True))

    def count_lanes(rows, v, strict=False):
        def tile(j, c):
            for c0 in range(0, T, LANES):
                sh = sc_ref[j, rows, c0:c0 + LANES]
                hit = (sh > v) if strict else (sh >= v)
                c = c + jnp.where(hit, 1.0, 0.0)
            return c

        def group(g, c):
            for u in range(COUNT_UNROLL):
                c = tile(g * COUNT_UNROLL + u, c)
            return c

        n_groups = lax.div(qi + 1, COUNT_UNROLL)
        c = lax.fori_loop(0, n_groups, group, jnp.zeros((COUNT_ROWS, LANES), F32))
        return lax.fori_loop(n_groups * COUNT_UNROLL, qi + 1, tile, c)

    def row_total(c):
        return jnp.broadcast_to(jnp.sum(c, axis=-1, keepdims=True), c.shape)

    slabs = [slice(r0, r0 + COUNT_ROWS) for r0 in range(0, T, COUNT_ROWS)]
    n_causal = (qi * T + 1 + lax.broadcasted_iota(jnp.int32, (T, LANES), 0)).astype(F32)
    keep_all = n_causal <= kf
    lo0 = jnp.where(keep_all, mn, jnp.maximum(mn, class_floor))
    above_max = mx + jnp.maximum(jnp.abs(mx) * (2.0 ** -20), 1e-30)
    hi0 = jnp.where(keep_all, mn, above_max)
    lo_ref[...] = lo0
    hi_ref[...] = hi0
    mid_ref[...] = lo0 + 0.5 * (hi0 - lo0)
    clo_ref[...] = n_causal

    def bis_body(open_widths):
        idle = jnp.zeros((COUNT_ROWS, LANES), F32)
        partial = [lax.cond(w > 0.0, lambda rows=rows: count_lanes(rows, mid_ref[rows]), lambda: idle)
                   for rows, w in zip(slabs, open_widths)]
        new_widths = tuple(jnp.max(hi_ref[rows] - lo_ref[rows]) for rows in slabs)
        for rows, c in zip(slabs, partial):
            lo, hi, mid = lo_ref[rows], hi_ref[rows], mid_ref[rows]
            cnt = row_total(c)
            up = cnt >= kf
            lo = jnp.where(up, mid, lo)
            c_lo = jnp.where(up, cnt, clo_ref[rows])
            hi = jnp.where(up, hi, mid)
            nxt = lo + 0.5 * (hi - lo)
            stop = jnp.logical_or(c_lo == kf, jnp.logical_or(nxt <= lo, nxt >= hi))
            hi = jnp.where(stop, lo, hi)
            lo_ref[rows] = lo
            hi_ref[rows] = hi
            mid_ref[rows] = jnp.where(stop, lo, nxt)
            clo_ref[rows] = c_lo
        return new_widths

    lax.while_loop(lambda ws: functools.reduce(jnp.maximum, ws) > 0.0, bis_body,
                   tuple(jnp.max((hi0 - lo0)[rows]) for rows in slabs))
    thr = lo_ref[...]

    tied_f = jnp.where(clo_ref[...] > kf, 1.0, 0.0)

    @pl.when(jnp.max(tied_f) > 0.0)
    def _():
        above = jnp.concatenate(
            [row_total(count_lanes(rows, thr[rows], strict=True)) for rows in slabs], axis=0)
        need = (kf - above)[:, :1]
        r = lax.broadcasted_iota(jnp.int32, (T, T), 0)
        c = lax.broadcasted_iota(jnp.int32, (T, T), 1)
        before = jnp.where(r < c, 1.0, 0.0).astype(BF16)
        thr_col = thr[:, :1]
        tied_col = tied_f[:, :1] > 0.0

        def body(j, seen):
            s = sc_ref[j]
            eq = jnp.where(s == thr_col, 1.0, 0.0)
            rank = seen + _dot(eq.astype(BF16), before)
            drop = jnp.logical_and(tied_col, jnp.logical_and(eq > 0.0, rank >= need))
            sc_ref[j] = jnp.where(drop, -jnp.inf, s)
            return seen + jnp.sum(eq, axis=-1, keepdims=True)

        lax.fori_loop(0, qi + 1, body, jnp.zeros((T, 1), F32))

    m_ref[...] = jnp.full(m_ref.shape, NEG_BIG, F32)
    l_ref[...] = jnp.zeros(l_ref.shape, F32)
    acc_ref[...] = jnp.zeros(acc_ref.shape, F32)

    def attend(j0, ntiles, band):
        sel = [jnp.where(sh >= thr, 0.0, NEG_BIG) for t in range(ntiles) for sh in halves(sc_ref[j0 + t])]
        start = pl.multiple_of(j0 * T, T)
        keys = pl.ds(start, ntiles * T)
        width = ntiles * T

        def logits_to_scratch(hh):
            sl = slice(hh * A_HEAD_DIM, (hh + 1) * A_HEAD_DIM)
            s_ref[hh % 2, :, :width] = _dot_nt(q_ref[:, sl], k_ref[keys, sl])

        logits_to_scratch(0)
        pending = None
        for hh in range(A_HEADS):
            sl = slice(hh * A_HEAD_DIM, (hh + 1) * A_HEAD_DIM)
            if hh + 1 < A_HEADS:
                logits_to_scratch(hh + 1)
            s = s_ref[hh % 2, :, :width]
            if band is not None:
                s = s + rb_ref[hh, :, band * T:(band + ntiles) * T]
            sh = [s[:, c * LANES:(c + 1) * LANES] + m for c, m in enumerate(sel)]
            m_prev = m_ref[hh]
            m_cur = jnp.max(functools.reduce(jnp.maximum, sh), axis=-1, keepdims=True)
            m_new = jnp.maximum(m_prev, lanes(m_cur))
            alpha = jnp.exp2(m_prev - m_new)
            ph = [jnp.exp2(x - m_new) for x in sh]
            l_ref[hh] = alpha * l_ref[hh] + functools.reduce(jnp.add, ph)
            p = jnp.concatenate([x.astype(BF16) for x in ph], axis=-1)
            acc_ref[hh] = alpha * acc_ref[hh]
            m_ref[hh] = m_new
            if pending is not None:
                acc_ref[pending[0]] += pending[1]
            pending = (hh, _dot(p, v_ref[keys, sl]))
        acc_ref[pending[0]] += pending[1]

    n_far = jnp.maximum(qi - 1, 0)
    done_tiles = 0
    width = FAR_TILES
    while width >= 1:
        trips = lax.div(n_far - done_tiles, width)

        def far_body(i, c, width=width, base=done_tiles):
            attend(base + i * width, width, None)
            return c

        lax.fori_loop(0, trips, far_body, 0)
        done_tiles = done_tiles + trips * width
        width //= 2

    @pl.when(qi > 0)
    def _():
        attend(qi - 1, 2, 0)

    @pl.when(qi == 0)
    def _():
        attend(0, 1, 1)

    for hh in range(A_HEADS):
        sl = slice(hh * A_HEAD_DIM, (hh + 1) * A_HEAD_DIM)
        den = jnp.sum(l_ref[hh], axis=-1, keepdims=True)
        o_ref[:, sl] = (acc_ref[hh] / den * g_ref[:, sl].astype(F32)).astype(BF16)


def _dsa_call(iq, iw, ikz, qa, ka, va, ga, rb, topk):
    bsz, seq, _ = qa.shape
    T = DSA_T
    nq = seq // T
    assert topk <= T and seq % T == 0
    once = pl.Buffered(1)
    return pl.pallas_call(
        functools.partial(_dsa_kernel, topk=topk),
        grid=(bsz, nq),
        in_specs=[
            pl.BlockSpec((None, IDX_PAIRS, T, LANES), lambda b, i: (b, 0, i, 0)),
            pl.BlockSpec((None, T, IDX_HEADS), lambda b, i: (b, i, 0)),
            pl.BlockSpec((None, 2, LANES, seq), lambda b, i: (b, 0, 0, 0), pipeline_mode=once),
            pl.BlockSpec((None, T, A_WIDTH), lambda b, i: (b, i, 0)),
            pl.BlockSpec((None, seq, A_WIDTH), lambda b, i: (b, 0, 0), pipeline_mode=once),
            pl.BlockSpec((None, seq, A_WIDTH), lambda b, i: (b, 0, 0), pipeline_mode=once),
            pl.BlockSpec((None, T, A_WIDTH), lambda b, i: (b, i, 0)),
            pl.BlockSpec((A_HEADS, T, 2 * T), lambda b, i: (0, 0, 0), pipeline_mode=once),
        ],
        out_specs=pl.BlockSpec((None, T, A_WIDTH), lambda b, i: (b, i, 0)),
        out_shape=jax.ShapeDtypeStruct((bsz, seq, A_WIDTH), BF16),
        scratch_shapes=[
            pltpu.VMEM((nq, T, T), F32),
            pltpu.VMEM((IDX_HEADS, T, LANES), F32),
            pltpu.VMEM((T, LANES), F32),
            pltpu.VMEM((T, LANES), F32),
            pltpu.VMEM((T, LANES), F32),
            pltpu.VMEM((T, LANES), F32),
            pltpu.VMEM((2, T, FAR_TILES * T), F32),
            pltpu.VMEM((A_HEADS, T, LANES), F32),
            pltpu.VMEM((A_HEADS, T, LANES), F32),
            pltpu.VMEM((A_HEADS, T, A_HEAD_DIM), F32),
        ],
        compiler_params=_cparams(("arbitrary", "arbitrary")),
        name="dsa",
    )(iq, iw, ikz, qa, ka, va, ga, rb)


def _swa_kernel(sink_ref, q_ref, kc_ref, vc_ref, kcs_ref, vcs_ref, kp_ref, vp_ref, kps_ref, vps_ref,
                g_ref, bias_ref, o_ref):
    first = pl.program_id(1) == 0
    blk = WINDOW
    tiles = B_HEADS // B_KV_HEADS // 2
    low_half = lax.broadcasted_iota(jnp.int32, (2 * blk, LANES), 1) < B_HEAD_DIM
    ones = jnp.ones((2 * blk, LANES), BF16)
    for sb in range(SWA_T // blk):
        rows = slice(sb * blk, (sb + 1) * blk)
        prev = slice((sb - 1) * blk, sb * blk)

        def window(cur_ref, prev_ref):
            before = prev_ref[...] if sb == 0 else cur_ref[prev, :]
            return jnp.concatenate([before, cur_ref[rows, :]], axis=0)

        k_nat, v_nat = window(kc_ref, kp_ref), window(vc_ref, vp_ref)
        k_swp, v_swp = window(kcs_ref, kps_ref), window(vcs_ref, vps_ref)
        for g in range(B_KV_HEADS):
            q = jnp.concatenate(
                [q_ref[rows, (tiles * g + i) * LANES:(tiles * g + i + 1) * LANES] for i in range(tiles)], axis=0)
            res = None
            for e in range(2):
                keep = low_half if e == 0 else jnp.logical_not(low_half)
                k_src, v_src = (k_nat, v_nat) if g == e else (k_swp, v_swp)
                kk = jnp.where(keep, k_src, jnp.zeros_like(k_src))
                vv = jnp.where(keep, v_src, jnp.zeros_like(v_src))
                logits = _dot_nt(q, kk) + bias_ref[2 * g + e]
                l_prev, l_cur = logits[:, :blk], logits[:, blk:]
                if sb == 0:
                    l_prev = jnp.where(first, NEG_BIG, l_prev)
                sink = jnp.concatenate(
                    [jnp.full((blk, blk), sink_ref[2 * (tiles * g + i) + e], F32) for i in range(tiles)], axis=0)
                m = jnp.max(jnp.maximum(l_prev, l_cur), axis=-1, keepdims=True)
                m = jnp.maximum(jnp.broadcast_to(m, sink.shape), sink)
                p = jnp.concatenate([jnp.exp(l_prev - m).astype(BF16), jnp.exp(l_cur - m).astype(BF16)], axis=1)
                pv = _dot(p, jnp.concatenate([vv, ones], axis=1))
                part = pv[:, :LANES] / (pv[:, LANES:] + jnp.exp(sink - m))
                res = part if res is None else res + part
            for i in range(tiles):
                cols = slice((tiles * g + i) * LANES, (tiles * g + i + 1) * LANES)
                o_ref[rows, cols] = (res[i * blk:(i + 1) * blk] * g_ref[rows, cols].astype(F32)).astype(BF16)


def _swa_call(sinks, qb, kb, vb, kbs, vbs, gb, bias):
    bsz, seq, _ = qb.shape
    T = SWA_T
    per = T // WINDOW
    cur = lambda b, i: (b, i, 0)
    prev = lambda b, i: (b, jnp.maximum(i * per - 1, 0), 0)
    kv_cur = pl.BlockSpec((None, T, B_KV_WIDTH), cur)
    kv_prev = pl.BlockSpec((None, WINDOW, B_KV_WIDTH), prev)
    return pl.pallas_call(
        _swa_kernel,
        grid=(bsz, seq // T),
        in_specs=[
            pl.BlockSpec(memory_space=pltpu.SMEM),
            pl.BlockSpec((None, T, B_WIDTH), cur),
            kv_cur, kv_cur, kv_cur, kv_cur,
            kv_prev, kv_prev, kv_prev, kv_prev,
            pl.BlockSpec((None, T, B_WIDTH), cur),
            pl.BlockSpec(bias.shape, lambda b, i: (0, 0, 0)),
        ],
        out_specs=pl.BlockSpec((None, T, B_WIDTH), cur),
        out_shape=jax.ShapeDtypeStruct((bsz, seq, B_WIDTH), BF16),
        compiler_params=_cparams(("parallel", "parallel")),
        name="swa",
    )(sinks, qb, kb, vb, kbs, vbs, kb, vb, kbs, vbs, gb, bias)


def _mem_kernel(q_ref, k_ref, v_ref, g_ref, o_ref):
    for hh in range(C_HEADS):
        sl = slice(hh * C_HEAD_DIM, (hh + 1) * C_HEAD_DIM)
        s = _dot_nt(q_ref[:, sl], k_ref[:, sl])
        m = jnp.max(s, axis=-1, keepdims=True)
        p = jnp.exp(s - m)
        den = jnp.sum(p, axis=-1, keepdims=True)
        o = _dot(p.astype(BF16), v_ref[:, sl]) / den
        o_ref[:, sl] = (o * g_ref[:, sl].astype(F32)).astype(BF16)


def _mem_call(qc, kc, vc, gc):
    bsz, seq, _ = qc.shape
    mlen = kc.shape[1]
    T = SWA_T
    cur = lambda b, i: (b, i, 0)
    whole = lambda b, i: (b, 0, 0)
    return pl.pallas_call(
        _mem_kernel,
        grid=(bsz, seq // T),
        in_specs=[
            pl.BlockSpec((None, T, C_WIDTH), cur),
            pl.BlockSpec((None, mlen, C_WIDTH), whole),
            pl.BlockSpec((None, mlen, C_WIDTH), whole),
            pl.BlockSpec((None, T, C_WIDTH), cur),
        ],
        out_specs=pl.BlockSpec((None, T, C_WIDTH), cur),
        out_shape=jax.ShapeDtypeStruct((bsz, seq, C_WIDTH), BF16),
        compiler_params=_cparams(("parallel", "parallel")),
        name="mem_attn",
    )(qc, kc, vc, gc)


def _final_kernel(x_ref, h_ref, oa_ref, ob_ref, oc_ref, wma_ref, wmb_ref, wmc_ref,
                  ba_ref, bb_ref, bc_ref, wua_ref, wub_ref, wuc_ref, wo_ref, out_ref, acc_ref):
    c = pl.program_id(1)

    @pl.when(c == 0)
    def _():
        acc_ref[...] = jnp.zeros(acc_ref.shape, F32)

    h = h_ref[...]

    def branch(o_ref, wm_ref, b_ref, wu_ref):
        gate = 1.0 / (1.0 + jnp.exp(-(_dot_nt(h, wm_ref[...]) + b_ref[...])))
        return gate * _dot(o_ref[...], wu_ref[...])

    merged = (branch(oa_ref, wma_ref, ba_ref, wua_ref) + branch(ob_ref, wmb_ref, bb_ref, wub_ref)
              + branch(oc_ref, wmc_ref, bc_ref, wuc_ref))
    acc_ref[...] += _dot(merged.astype(BF16), wo_ref[...])

    @pl.when(c == pl.num_programs(1) - 1)
    def _():
        out_ref[...] = x_ref[...] + acc_ref[...]


def _final_call(x2d, h, oa, ob, oc, wmix, gate_bias, wua, wub, wuc, wo):
    n, d = x2d.shape
    tm = FINAL_ROWS
    nch = FINAL_NCHUNK
    cw = d // nch
    row = lambda i, c: (i, 0)

    def col(br):
        return lambda i, c: (0, br * nch + c)

    def wrow(br):
        return lambda i, c: (br * nch + c, 0)

    return pl.pallas_call(
        _final_kernel,
        grid=(n // tm, nch),
        in_specs=[
            pl.BlockSpec((tm, d), row),
            pl.BlockSpec((tm, d), row),
            pl.BlockSpec((tm, A_WIDTH), row),
            pl.BlockSpec((tm, B_WIDTH), row),
            pl.BlockSpec((tm, C_WIDTH), row),
            pl.BlockSpec((cw, d), wrow(0)), pl.BlockSpec((cw, d), wrow(1)), pl.BlockSpec((cw, d), wrow(2)),
            pl.BlockSpec((1, cw), col(0)), pl.BlockSpec((1, cw), col(1)), pl.BlockSpec((1, cw), col(2)),
            pl.BlockSpec((A_WIDTH, cw), lambda i, c: (0, c)),
            pl.BlockSpec((B_WIDTH, cw), lambda i, c: (0, c)),
            pl.BlockSpec((C_WIDTH, cw), lambda i, c: (0, c)),
            pl.BlockSpec((cw, d), lambda i, c: (c, 0)),
        ],
        out_specs=pl.BlockSpec((tm, d), row),
        out_shape=jax.ShapeDtypeStruct((n, d), F32),
        scratch_shapes=[pltpu.VMEM((tm, d), F32)],
        compiler_params=_cparams(("parallel", "arbitrary")),
        name="merge_out",
    )(x2d, h, oa, ob, oc, wmix, wmix, wmix, gate_bias, gate_bias, gate_bias, wua, wub, wuc, wo)


def _w_in_groups(d):
    sizes = (A_WIDTH, A_KV_RANK, IDX_HEADS * IDX_DIM, IDX_DIM, IDX_HEADS, A_WIDTH,
             B_WIDTH, B_KV_WIDTH, B_KV_WIDTH, B_WIDTH, C_WIDTH, C_WIDTH, N_BRANCH * d)
    cuts = np.cumsum((0,) + sizes).tolist()
    (aq, ackv, iq, ik, iw, ag, bq, bk, bv, bg, cq, cg, mix) = [
        (cuts[i], cuts[i + 1]) for i in range(len(sizes))]
    return [
        [(aq[0], ackv[1])],
        [(iq[0], ik[1]), LANES - IDX_DIM, iw, LANES - IDX_HEADS],
        [ag, bg, cg],
        [(bq[0], bv[1]), cq,
         (bk[0] + B_HEAD_DIM, bk[1]), (bk[0], bk[0] + B_HEAD_DIM),
         (bv[0] + B_HEAD_DIM, bv[1]), (bv[0], bv[0] + B_HEAD_DIM)],
        [mix],
    ]


def _group_width(group):
    return sum(p if isinstance(p, int) else p[1] - p[0] for p in group)


def _regroup_kernel(w_ref, *out_refs, groups):
    for o_ref, group in zip(out_refs, groups):
        row = 0
        for part in group:
            if isinstance(part, int):
                o_ref[row:row + part, :] = jnp.zeros((part, o_ref.shape[1]), BF16)
                row += part
            else:
                o_ref[row:row + part[1] - part[0], :] = w_ref[part[0]:part[1], :].astype(BF16)
                row += part[1] - part[0]


def _regroup_w_in(w_in_t):
    cols, d = w_in_t.shape
    groups = _w_in_groups(d)
    lanes = REGROUP_LANES
    return pl.pallas_call(
        functools.partial(_regroup_kernel, groups=groups),
        grid=(d // lanes,),
        in_specs=[pl.BlockSpec((cols, lanes), lambda i: (0, i))],
        out_specs=[pl.BlockSpec((_group_width(g), lanes), lambda i: (0, i)) for g in groups],
        out_shape=[jax.ShapeDtypeStruct((_group_width(g), d), BF16) for g in groups],
        compiler_params=_cparams(("parallel",)),
        name="regroup_w_in",
    )(w_in_t)


def _layer(x, mem, norm_g, w_in, kv_norm_g, w_kv_up, idx_k_ln_g, idx_k_ln_b, q_norm_a, k_norm_a,
           q_norm_b, k_norm_b, sinks_b, mem_norm_g, w_mem_kv, q_norm_c, k_norm_c,
           w_up_a, w_up_b, w_up_c, gate_bias, w_o, rel_bias):
    bsz, seq, d = x.shape
    mlen = mem.shape[1]
    n = bsz * seq
    tm = ROW_TILE
    topk = min(TOPK_MAX, seq // 4)
    x2d = x.reshape(n, d)
    row2 = lambda v: v.reshape(1, -1)

    w_grp_a, w_grp_i, w_grp_g, w_grp_bc, w_mix = _regroup_w_in(jnp.swapaxes(w_in, 0, 1))
    g_x = row2(norm_g)

    qa, ka, va, h = _proj_call(
        _proj_a_kernel, "proj_a", x2d, tm,
        [g_x, w_grp_a, row2(kv_norm_g), w_kv_up.astype(BF16), row2(q_norm_a), row2(k_norm_a)],
        [jax.ShapeDtypeStruct((n, A_WIDTH), BF16)] * 3 + [jax.ShapeDtypeStruct((n, d), BF16)],
        [_row_spec(tm, A_WIDTH)] * 3 + [_row_spec(tm, d)])

    per_b = seq // tm
    iq, ik, iw = _proj_call(
        _proj_i_kernel, "proj_i", h, tm,
        [w_grp_i, row2(idx_k_ln_g), row2(idx_k_ln_b)],
        [jax.ShapeDtypeStruct((bsz, IDX_PAIRS, seq, LANES), BF16),
         jax.ShapeDtypeStruct((n, IDX_DIM), BF16),
         jax.ShapeDtypeStruct((n, IDX_HEADS), F32)],
        [pl.BlockSpec((None, IDX_PAIRS, tm, LANES), lambda i: (i // per_b, 0, i % per_b, 0)),
         _row_spec(tm, IDX_DIM), _row_spec(tm, IDX_HEADS)])

    ga, gb, gc = _proj_call(
        _proj_g_kernel, "proj_g", h, tm, [w_grp_g],
        [jax.ShapeDtypeStruct((n, A_WIDTH), BF16), jax.ShapeDtypeStruct((n, B_WIDTH), BF16),
         jax.ShapeDtypeStruct((n, C_WIDTH), BF16)],
        [_row_spec(tm, A_WIDTH), _row_spec(tm, B_WIDTH), _row_spec(tm, C_WIDTH)])

    bc_widths = (B_WIDTH, B_KV_WIDTH, B_KV_WIDTH, C_WIDTH, B_KV_WIDTH, B_KV_WIDTH)
    qb, kb, vb, qc, kbs, vbs = _proj_call(
        _proj_bc_kernel, "proj_bc", h, tm,
        [w_grp_bc, row2(jnp.tile(q_norm_b, 2)), row2(jnp.tile(k_norm_b, 2)), row2(q_norm_c)],
        [jax.ShapeDtypeStruct((n, w), BF16) for w in bc_widths],
        [_row_spec(tm, w) for w in bc_widths])

    mrows = bsz * mlen
    kc, vc = _proj_call(
        _proj_mem_kernel, "proj_mem", mem.reshape(mrows, d), min(tm, mrows),
        [row2(mem_norm_g), w_mem_kv.astype(BF16), row2(k_norm_c)],
        [jax.ShapeDtypeStruct((mrows, C_WIDTH), BF16)] * 2,
        [_row_spec(min(tm, mrows), C_WIDTH)] * 2)

    T = DSA_T
    i_idx = np.arange(T)[:, None]
    dist_a = T + i_idx - np.arange(2 * T)[None, :]
    rb_a = _bias_tiles(rel_bias, _t5_bucket_np(dist_a), 0, A_HEADS, NUM_BUCKETS - 1, LOG2E)
    i_idx = np.arange(WINDOW)[:, None]
    dist_b = WINDOW + i_idx - np.arange(2 * WINDOW)[None, :]
    bucket_b = np.where((dist_b >= 0) & (dist_b < WINDOW), _t5_bucket_np(dist_b), -1).astype(np.int32)
    bias_b = _bias_tiles(rel_bias, bucket_b, A_HEADS, B_HEADS, None, 1.0)
    tiles_b = B_HEADS // B_KV_HEADS // 2
    slot_heads = [2 * (tiles_b * g + i) + e for g in range(B_KV_HEADS) for e in range(2) for i in range(tiles_b)]
    bias_b = bias_b[np.array(slot_heads)].reshape(2 * B_KV_HEADS, tiles_b * WINDOW, 2 * WINDOW)

    r3 = lambda v, w: v.reshape(bsz, seq, w)
    ikt = jnp.swapaxes(ik.reshape(bsz, seq, IDX_DIM), 1, 2)
    zeros = jnp.zeros_like(ikt)
    ikz = jnp.stack([jnp.concatenate([ikt, zeros], axis=1), jnp.concatenate([zeros, ikt], axis=1)], axis=1)
    oa = _dsa_call(iq, iw.reshape(bsz, seq, IDX_HEADS), ikz,
                   r3(qa, A_WIDTH), r3(ka, A_WIDTH),
                   r3(va, A_WIDTH), r3(ga, A_WIDTH), rb_a, topk)
    ob = _swa_call(sinks_b, r3(qb, B_WIDTH), r3(kb, B_KV_WIDTH), r3(vb, B_KV_WIDTH),
                   r3(kbs, B_KV_WIDTH), r3(vbs, B_KV_WIDTH), r3(gb, B_WIDTH), bias_b)
    oc = _mem_call(r3(qc, C_WIDTH), kc.reshape(bsz, mlen, C_WIDTH), vc.reshape(bsz, mlen, C_WIDTH),
                   r3(gc, C_WIDTH))

    out = _final_call(x2d, h, oa.reshape(n, A_WIDTH), ob.reshape(n, B_WIDTH), oc.reshape(n, C_WIDTH),
                      w_mix, row2(gate_bias), w_up_a.astype(BF16), w_up_b.astype(BF16),
                      w_up_c.astype(BF16), w_o.astype(BF16))
    return out.reshape(bsz, seq, d)


def kernel(x, mem, norm_g, w_in, kv_norm_g, w_kv_up, idx_k_ln_g, idx_k_ln_b, q_norm_a, k_norm_a, q_norm_b, k_norm_b, sinks_b, mem_norm_g, w_mem_kv, q_norm_c, k_norm_c, w_up_a, w_up_b, w_up_c, gate_bias, w_o, rel_bias):
    for l in range(norm_g.shape[0]):
        x = _layer(x, mem, norm_g[l], w_in[l], kv_norm_g[l], w_kv_up[l], idx_k_ln_g[l], idx_k_ln_b[l],
                   q_norm_a[l], k_norm_a[l], q_norm_b[l], k_norm_b[l], sinks_b[l], mem_norm_g[l],
                   w_mem_kv[l], q_norm_c[l], k_norm_c[l], w_up_a[l], w_up_b[l], w_up_c[l],
                   gate_bias[l], w_o[l], rel_bias)
    return x
```

```python
import functools
import math

import numpy as np
import jax
import jax.numpy as jnp
from jax import lax
from jax.experimental import pallas as pl
from jax.experimental.pallas import tpu as pltpu

F32 = jnp.float32
BF16 = jnp.bfloat16

EPS = 1e-6
A_HEADS, A_HEAD_DIM, A_KV_RANK = 6, 128, 256
IDX_HEADS, IDX_DIM, TOPK_MAX = 16, 64, 256
IDX_PAIRS = IDX_HEADS // 2
B_HEADS, B_KV_HEADS, B_HEAD_DIM, WINDOW = 12, 2, 64, 128
C_HEADS, C_HEAD_DIM = 4, 128
NUM_BUCKETS, MAX_DISTANCE = 32, 128
N_BRANCH = 3
A_WIDTH = A_HEADS * A_HEAD_DIM
B_WIDTH = B_HEADS * B_HEAD_DIM
B_KV_WIDTH = B_KV_HEADS * B_HEAD_DIM
C_WIDTH = C_HEADS * C_HEAD_DIM

LANES = 128
VMEM_LIMIT = 56 * 1024 * 1024
NEG_BIG = -1e30

ROW_TILE = 1024
DSA_T = 256
SWA_T = 512
FINAL_ROWS = 512
FINAL_NCHUNK = 4
COUNT_ROWS = 128
REGROUP_LANES = 256
SCORE_TILES = 8
COUNT_UNROLL = 4
FAR_TILES = 4
LOG2E = math.log2(math.e)


def _cparams(sem):
    return pltpu.CompilerParams(dimension_semantics=sem, vmem_limit_bytes=VMEM_LIMIT)


def _dot(a, b):
    return jnp.dot(a, b, preferred_element_type=F32)


def _dot_nt(a, b):
    return lax.dot_general(a, b, (((1,), (1,)), ((), ())), preferred_element_type=F32)


def _rms_rows(x, g):
    ms = jnp.mean(x * x, axis=-1, keepdims=True)
    return x * lax.rsqrt(ms + EPS) * g


def _headnorm128(seg, g):
    ms = jnp.mean(seg * seg, axis=-1, keepdims=True)
    return seg * lax.rsqrt(ms + EPS) * g


def _headnorm64_pair(seg, g2):
    sq = seg * seg
    low = lax.broadcasted_iota(jnp.int32, seg.shape, 1) < B_HEAD_DIM
    s_all = jnp.sum(sq, axis=-1, keepdims=True)
    s_lo = jnp.sum(jnp.where(low, sq, 0.0), axis=-1, keepdims=True)
    ms = jnp.where(low, s_lo, s_all - s_lo) * (1.0 / B_HEAD_DIM)
    return seg * lax.rsqrt(ms + EPS) * g2


def _silu(y):
    return y / (1.0 + jnp.exp(-y))


def _proj_a_kernel(x_ref, g_ref, w_ref, kvg_ref, wkv_ref, qg_ref, kg_ref, qa_ref, ka_ref, va_ref, h_ref):
    h = _rms_rows(x_ref[...], g_ref[...]).astype(BF16)
    h_ref[...] = h
    y = _dot_nt(h, w_ref[...])
    scale = A_HEAD_DIM ** -0.5 * LOG2E
    for hh in range(A_HEADS):
        sl = slice(hh * A_HEAD_DIM, (hh + 1) * A_HEAD_DIM)
        qa_ref[:, sl] = (_headnorm128(y[:, sl], qg_ref[...]) * scale).astype(BF16)
    ckv = _rms_rows(y[:, A_WIDTH:A_WIDTH + A_KV_RANK], kvg_ref[...]).astype(BF16)
    kv = _dot(ckv, wkv_ref[...])
    for hh in range(A_HEADS):
        sl = slice(hh * A_HEAD_DIM, (hh + 1) * A_HEAD_DIM)
        ka_ref[:, sl] = _headnorm128(kv[:, sl], kg_ref[...]).astype(BF16)
    va_ref[...] = kv[:, A_WIDTH:].astype(BF16)


def _proj_i_kernel(h_ref, w_ref, lng_ref, lnb_ref, iq_ref, ik_ref, iw_ref):
    y = _dot_nt(h_ref[...], w_ref[...])
    k0 = IDX_HEADS * IDX_DIM
    for p in range(IDX_PAIRS):
        iq_ref[p] = y[:, p * LANES:(p + 1) * LANES].astype(BF16)
    ik = y[:, k0:k0 + IDX_DIM]
    mu = jnp.mean(ik, axis=-1, keepdims=True)
    d = ik - mu
    var = jnp.mean(d * d, axis=-1, keepdims=True)
    ik_ref[...] = (d * lax.rsqrt(var + EPS) * lng_ref[...] + lnb_ref[...]).astype(BF16)
    w0 = k0 + LANES
    iw_ref[...] = y[:, w0:w0 + IDX_HEADS] * (IDX_HEADS ** -0.5) * (IDX_DIM ** -0.5)


def _proj_g_kernel(h_ref, w_ref, ga_ref, gb_ref, gc_ref):
    y = _silu(_dot_nt(h_ref[...], w_ref[...]))
    ga_ref[...] = y[:, :A_WIDTH].astype(BF16)
    gb_ref[...] = y[:, A_WIDTH:A_WIDTH + B_WIDTH].astype(BF16)
    gc_ref[...] = y[:, A_WIDTH + B_WIDTH:].astype(BF16)


def _proj_bc_kernel(h_ref, w_ref, qbg_ref, kbg_ref, qcg_ref,
                    qb_ref, kb_ref, vb_ref, qc_ref, kbs_ref, vbs_ref):
    y = _dot_nt(h_ref[...], w_ref[...])
    sb = B_HEAD_DIM ** -0.5
    for p in range(B_WIDTH // LANES):
        sl = slice(p * LANES, (p + 1) * LANES)
        qb_ref[:, sl] = (_headnorm64_pair(y[:, sl], qbg_ref[...]) * sb).astype(BF16)
    k0 = B_WIDTH
    kb_ref[...] = _headnorm64_pair(y[:, k0:k0 + B_KV_WIDTH], kbg_ref[...]).astype(BF16)
    vb_ref[...] = y[:, k0 + B_KV_WIDTH:k0 + 2 * B_KV_WIDTH].astype(BF16)
    s0 = k0 + 2 * B_KV_WIDTH + C_WIDTH
    kbs_ref[...] = _headnorm64_pair(y[:, s0:s0 + B_KV_WIDTH], kbg_ref[...]).astype(BF16)
    vbs_ref[...] = y[:, s0 + B_KV_WIDTH:s0 + 2 * B_KV_WIDTH].astype(BF16)
    c0 = k0 + 2 * B_KV_WIDTH
    sc = C_HEAD_DIM ** -0.5
    for hh in range(C_HEADS):
        sl = slice(hh * C_HEAD_DIM, (hh + 1) * C_HEAD_DIM)
        qc_ref[:, sl] = (_headnorm128(y[:, c0 + hh * C_HEAD_DIM:c0 + (hh + 1) * C_HEAD_DIM],
                                      qcg_ref[...]) * sc).astype(BF16)


def _proj_mem_kernel(x_ref, g_ref, w_ref, kg_ref, kc_ref, vc_ref):
    h = _rms_rows(x_ref[...], g_ref[...]).astype(BF16)
    y = _dot(h, w_ref[...])
    for hh in range(C_HEADS):
        sl = slice(hh * C_HEAD_DIM, (hh + 1) * C_HEAD_DIM)
        kc_ref[:, sl] = _headnorm128(y[:, sl], kg_ref[...]).astype(BF16)
    vc_ref[...] = y[:, C_WIDTH:].astype(BF16)


def _row_spec(tm, cols):
    return pl.BlockSpec((tm, cols), lambda i: (i, 0))


def _full_spec(shape):
    nd = len(shape)
    return pl.BlockSpec(shape, lambda i: (0,) * nd, pipeline_mode=pl.Buffered(1))


def _proj_call(kernel_fn, name, x2d, tm, consts, out_shapes, out_specs):
    n, d = x2d.shape
    in_specs = [_row_spec(tm, d)] + [_full_spec(c.shape) for c in consts]
    return pl.pallas_call(
        kernel_fn,
        grid=(n // tm,),
        in_specs=in_specs,
        out_specs=out_specs,
        out_shape=out_shapes,
        compiler_params=_cparams(("parallel",)),
        name=name,
    )(x2d, *consts)


def _t5_bucket_np(dist):
    n = np.maximum(dist, 0)
    max_exact = NUM_BUCKETS // 2
    nf = np.maximum(n, 1).astype(np.float32)
    large = max_exact + (np.log(nf / max_exact) / math.log(MAX_DISTANCE / max_exact)
                         * (NUM_BUCKETS - max_exact)).astype(np.int32)
    large = np.minimum(large, NUM_BUCKETS - 1)
    return np.where(n < max_exact, n, large).astype(np.int32)


def _bias_tile_kernel(tab_ref, bucket_ref, o_ref, *, head0, shift_bucket, scale):
    h = pl.program_id(0) + head0
    bucket = bucket_ref[...]
    acc = jnp.zeros(bucket.shape, F32)
    for b in range(NUM_BUCKETS):
        acc = jnp.where(bucket == b, tab_ref[b, h], acc)
    if shift_bucket is not None:
        acc = acc - tab_ref[shift_bucket, h]
    o_ref[...] = jnp.where(bucket < 0, NEG_BIG, acc * scale)


def _bias_tiles(rel_bias, bucket_np, head0, nheads, shift_bucket, scale):
    r, c = bucket_np.shape
    return pl.pallas_call(
        functools.partial(_bias_tile_kernel, head0=head0, shift_bucket=shift_bucket, scale=scale),
        grid=(nheads,),
        in_specs=[pl.BlockSpec(memory_space=pltpu.SMEM), pl.BlockSpec((r, c), lambda h: (0, 0))],
        out_specs=pl.BlockSpec((None, r, c), lambda h: (h, 0, 0)),
        out_shape=jax.ShapeDtypeStruct((nheads, r, c), F32),
        compiler_params=_cparams(("arbitrary",)),
        name="bias_tiles",
    )(rel_bias, jnp.asarray(bucket_np))


def _dsa_kernel(iq_ref, iw_ref, ikz_ref, q_ref, k_ref, v_ref, g_ref, rb_ref, o_ref,
                sc_ref, wb_ref, lo_ref, hi_ref, mid_ref, clo_ref, s_ref, m_ref, l_ref, acc_ref, *, topk):
    T = DSA_T
    qi = pl.program_id(1)
    kf = float(topk)

    def lanes(col):
        return jnp.broadcast_to(col, (T, LANES))

    def halves(tile):
        return [tile[:, c * LANES:(c + 1) * LANES] for c in range(T // LANES)]

    for hh in range(IDX_HEADS):
        wb_ref[hh] = lanes(iw_ref[:, hh:hh + 1])

    def score_chunk(j, diag):
        keys = pl.ds(pl.multiple_of(j * T, T), T)
        accs = [jnp.zeros((T, LANES), F32) for _ in range(T // LANES)]
        for hh in range(IDX_HEADS):
            s = _dot(iq_ref[hh // 2], ikz_ref[hh % 2, :, keys])
            w = wb_ref[hh]
            accs = [a + w * jnp.maximum(sh, 0.0) for a, sh in zip(accs, halves(s))]
        if diag:
            row = lax.broadcasted_iota(jnp.int32, (T, LANES), 0)
            col = lax.broadcasted_iota(jnp.int32, (T, LANES), 1)
            causal = [col + c * LANES <= row for c in range(T // LANES)]
            lows = [jnp.where(cm, a, jnp.inf) for cm, a in zip(causal, accs)]
            accs = [jnp.where(cm, a, -jnp.inf) for cm, a in zip(causal, accs)]
        else:
            lows = accs
        for c, a in enumerate(accs):
            sc_ref[j, :, c * LANES:(c + 1) * LANES] = a
        return functools.reduce(jnp.minimum, lows), accs

    def p1_body(j, carry):
        mn, mxs = carry
        lo_c, his = score_chunk(j, False)
        return jnp.minimum(mn, lo_c), [jnp.maximum(a, b) for a, b in zip(mxs, his)]

    mn0 = jnp.full((T, LANES), jnp.inf, F32)
    mx0 = [jnp.full((T, LANES), -jnp.inf, F32) for _ in range(T // LANES)]

    carry = (mn0, mx0)
    done_tiles = 0
    width = SCORE_TILES
    while width >= 1:
        trips = lax.div(qi - done_tiles, width)

        def p1_group(g, c, width=width, base=done_tiles):
            for u in range(width):
                c = p1_body(base + g * width + u, c)
            return c

        carry = lax.fori_loop(0, trips, p1_group, carry)
        done_tiles = done_tiles + trips * width
        width //= 2
    mn, mxs = carry
    lo_c, his = score_chunk(qi, True)
    mxs = [jnp.maximum(a, b) for a, b in zip(mxs, his)]
    mn = lanes(jnp.min(jnp.minimum(mn, lo_c), axis=-1, keepdims=True))
    mx = lanes(jnp.max(functools.reduce(jnp.maximum, mxs), axis=-1, keepdims=True))
    class_floor = lanes(jnp.min(functools.reduce(jnp.minimum, mxs), axis=-1, keepdims=True))

    def count_lanes(rows, v, strict=False):
        def tile(j, c):
            for c0 in range(0, T, LANES):
                sh = sc_ref[j, rows, c0:c0 + LANES]
                hit = (sh > v) if strict else (sh >= v)
                c = c + jnp.where(hit, 1.0, 0.0)
            return c

        def group(g, c):
            for u in range(COUNT_UNROLL):
                c = tile(g * COUNT_UNROLL + u, c)
            return c

        n_groups = lax.div(qi + 1, COUNT_UNROLL)
        c = lax.fori_loop(0, n_groups, group, jnp.zeros((COUNT_ROWS, LANES), F32))
        return lax.fori_loop(n_groups * COUNT_UNROLL, qi + 1, tile, c)

    def row_total(c):
        return jnp.broadcast_to(jnp.sum(c, axis=-1, keepdims=True), c.shape)

    slabs = [slice(r0, r0 + COUNT_ROWS) for r0 in range(0, T, COUNT_ROWS)]
    n_causal = (qi * T + 1 + lax.broadcasted_iota(jnp.int32, (T, LANES), 0)).astype(F32)
    keep_all = n_causal <= kf
    lo0 = jnp.where(keep_all, mn, jnp.maximum(mn, class_floor))
    above_max = mx + jnp.maximum(jnp.abs(mx) * (2.0 ** -20), 1e-30)
    hi0 = jnp.where(keep_all, mn, above_max)
    lo_ref[...] = lo0
    hi_ref[...] = hi0
    mid_ref[...] = lo0 + 0.5 * (hi0 - lo0)
    clo_ref[...] = n_causal

    def bis_body(open_widths):
        idle = jnp.zeros((COUNT_ROWS, LANES), F32)
        partial = [lax.cond(w > 0.0, lambda rows=rows: count_lanes(rows, mid_ref[rows]), lambda: idle)
                   for rows, w in zip(slabs, open_widths)]
        new_widths = tuple(jnp.max(hi_ref[rows] - lo_ref[rows]) for rows in slabs)
        for rows, c in zip(slabs, partial):
            lo, hi, mid = lo_ref[rows], hi_ref[rows], mid_ref[rows]
            cnt = row_total(c)
            up = cnt >= kf
            lo = jnp.where(up, mid, lo)
            c_lo = jnp.where(up, cnt, clo_ref[rows])
            hi = jnp.where(up, hi, mid)
            nxt = lo + 0.5 * (hi - lo)
            stop = jnp.logical_or(c_lo == kf, jnp.logical_or(nxt <= lo, nxt >= hi))
            hi = jnp.where(stop, lo, hi)
            lo_ref[rows] = lo
            hi_ref[rows] = hi
            mid_ref[rows] = jnp.where(stop, lo, nxt)
            clo_ref[rows] = c_lo
        return new_widths

    lax.while_loop(lambda ws: functools.reduce(jnp.maximum, ws) > 0.0, bis_body,
                   tuple(jnp.max((hi0 - lo0)[rows]) for rows in slabs))
    thr = lo_ref[...]

    tied_f = jnp.where(clo_ref[...] > kf, 1.0, 0.0)

    @pl.when(jnp.max(tied_f) > 0.0)
    def _():
        above = jnp.concatenate(
            [row_total(count_lanes(rows, thr[rows], strict=True)) for rows in slabs], axis=0)
        need = (kf - above)[:, :1]
        r = lax.broadcasted_iota(jnp.int32, (T, T), 0)
        c = lax.broadcasted_iota(jnp.int32, (T, T), 1)
        before = jnp.where(r < c, 1.0, 0.0).astype(BF16)
        thr_col = thr[:, :1]
        tied_col = tied_f[:, :1] > 0.0

        def body(j, seen):
            s = sc_ref[j]
            eq = jnp.where(s == thr_col, 1.0, 0.0)
            rank = seen + _dot(eq.astype(BF16), before)
            drop = jnp.logical_and(tied_col, jnp.logical_and(eq > 0.0, rank >= need))
            sc_ref[j] = jnp.where(drop, -jnp.inf, s)
            return seen + jnp.sum(eq, axis=-1, keepdims=True)

        lax.fori_loop(0, qi + 1, body, jnp.zeros((T, 1), F32))

    m_ref[...] = jnp.full(m_ref.shape, NEG_BIG, F32)
    l_ref[...] = jnp.zeros(l_ref.shape, F32)
    acc_ref[...] = jnp.zeros(acc_ref.shape, F32)

    def attend(j0, ntiles, band):
        sel = [jnp.where(sh >= thr, 0.0, NEG_BIG) for t in range(ntiles) for sh in halves(sc_ref[j0 + t])]
        start = pl.multiple_of(j0 * T, T)
        keys = pl.ds(start, ntiles * T)
        width = ntiles * T

        def logits_to_scratch(hh):
            sl = slice(hh * A_HEAD_DIM, (hh + 1) * A_HEAD_DIM)
            s_ref[hh % 2, :, :width] = _dot_nt(q_ref[:, sl], k_ref[keys, sl])

        logits_to_scratch(0)
        pending = None
        for hh in range(A_HEADS):
            sl = slice(hh * A_HEAD_DIM, (hh + 1) * A_HEAD_DIM)
            if hh + 1 < A_HEADS:
                logits_to_scratch(hh + 1)
            s = s_ref[hh % 2, :, :width]
            if band is not None:
                s = s + rb_ref[hh, :, band * T:(band + ntiles) * T]
            sh = [s[:, c * LANES:(c + 1) * LANES] + m for c, m in enumerate(sel)]
            m_prev = m_ref[hh]
            m_cur = jnp.max(functools.reduce(jnp.maximum, sh), axis=-1, keepdims=True)
            m_new = jnp.maximum(m_prev, lanes(m_cur))
            alpha = jnp.exp2(m_prev - m_new)
            ph = [jnp.exp2(x - m_new) for x in sh]
            l_ref[hh] = alpha * l_ref[hh] + functools.reduce(jnp.add, ph)
            p = jnp.concatenate([x.astype(BF16) for x in ph], axis=-1)
            acc_ref[hh] = alpha * acc_ref[hh]
            m_ref[hh] = m_new
            if pending is not None:
                acc_ref[pending[0]] += pending[1]
            pending = (hh, _dot(p, v_ref[keys, sl]))
        acc_ref[pending[0]] += pending[1]

    n_far = jnp.maximum(qi - 1, 0)
    done_tiles = 0
    width = FAR_TILES
    while width >= 1:
        trips = lax.div(n_far - done_tiles, width)

        def far_body(i, c, width=width, base=done_tiles):
            attend(base + i * width, width, None)
            return c

        lax.fori_loop(0, trips, far_body, 0)
        done_tiles = done_tiles + trips * width
        width //= 2

    @pl.when(qi > 0)
    def _():
        attend(qi - 1, 2, 0)

    @pl.when(qi == 0)
    def _():
        attend(0, 1, 1)

    for hh in range(A_HEADS):
        sl = slice(hh * A_HEAD_DIM, (hh + 1) * A_HEAD_DIM)
        den = jnp.sum(l_ref[hh], axis=-1, keepdims=True)
        o_ref[:, sl] = (acc_ref[hh] / den * g_ref[:, sl].astype(F32)).astype(BF16)


def _dsa_call(iq, iw, ikz, qa, ka, va, ga, rb, topk):
    bsz, seq, _ = qa.shape
    T = DSA_T
    nq = seq // T
    assert topk <= T and seq % T == 0
    once = pl.Buffered(1)
    return pl.pallas_call(
        functools.partial(_dsa_kernel, topk=topk),
        grid=(bsz, nq),
        in_specs=[
            pl.BlockSpec((None, IDX_PAIRS, T, LANES), lambda b, i: (b, 0, i, 0)),
            pl.BlockSpec((None, T, IDX_HEADS), lambda b, i: (b, i, 0)),
            pl.BlockSpec((None, 2, LANES, seq), lambda b, i: (b, 0, 0, 0), pipeline_mode=once),
            pl.BlockSpec((None, T, A_WIDTH), lambda b, i: (b, i, 0)),
            pl.BlockSpec((None, seq, A_WIDTH), lambda b, i: (b, 0, 0), pipeline_mode=once),
            pl.BlockSpec((None, seq, A_WIDTH), lambda b, i: (b, 0, 0), pipeline_mode=once),
            pl.BlockSpec((None, T, A_WIDTH), lambda b, i: (b, i, 0)),
            pl.BlockSpec((A_HEADS, T, 2 * T), lambda b, i: (0, 0, 0), pipeline_mode=once),
        ],
        out_specs=pl.BlockSpec((None, T, A_WIDTH), lambda b, i: (b, i, 0)),
        out_shape=jax.ShapeDtypeStruct((bsz, seq, A_WIDTH), BF16),
        scratch_shapes=[
            pltpu.VMEM((nq, T, T), F32),
            pltpu.VMEM((IDX_HEADS, T, LANES), F32),
            pltpu.VMEM((T, LANES), F32),
            pltpu.VMEM((T, LANES), F32),
            pltpu.VMEM((T, LANES), F32),
            pltpu.VMEM((T, LANES), F32),
            pltpu.VMEM((2, T, FAR_TILES * T), F32),
            pltpu.VMEM((A_HEADS, T, LANES), F32),
            pltpu.VMEM((A_HEADS, T, LANES), F32),
            pltpu.VMEM((A_HEADS, T, A_HEAD_DIM), F32),
        ],
        compiler_params=_cparams(("arbitrary", "arbitrary")),
        name="dsa",
    )(iq, iw, ikz, qa, ka, va, ga, rb)


def _swa_kernel(sink_ref, q_ref, kc_ref, vc_ref, kcs_ref, vcs_ref, kp_ref, vp_ref, kps_ref, vps_ref,
                g_ref, bias_ref, o_ref):
    first = pl.program_id(1) == 0
    blk = WINDOW
    tiles = B_HEADS // B_KV_HEADS // 2
    low_half = lax.broadcasted_iota(jnp.int32, (2 * blk, LANES), 1) < B_HEAD_DIM
    ones = jnp.ones((2 * blk, LANES), BF16)
    for sb in range(SWA_T // blk):
        rows = slice(sb * blk, (sb + 1) * blk)
        prev = slice((sb - 1) * blk, sb * blk)

        def window(cur_ref, prev_ref):
            before = prev_ref[...] if sb == 0 else cur_ref[prev, :]
            return jnp.concatenate([before, cur_ref[rows, :]], axis=0)

        k_nat, v_nat = window(kc_ref, kp_ref), window(vc_ref, vp_ref)
        k_swp, v_swp = window(kcs_ref, kps_ref), window(vcs_ref, vps_ref)
        for g in range(B_KV_HEADS):
            q = jnp.concatenate(
                [q_ref[rows, (tiles * g + i) * LANES:(tiles * g + i + 1) * LANES] for i in range(tiles)], axis=0)
            res = None
            for e in range(2):
                keep = low_half if e == 0 else jnp.logical_not(low_half)
                k_src, v_src = (k_nat, v_nat) if g == e else (k_swp, v_swp)
                kk = jnp.where(keep, k_src, jnp.zeros_like(k_src))
                vv = jnp.where(keep, v_src, jnp.zeros_like(v_src))
                logits = _dot_nt(q, kk) + bias_ref[2 * g + e]
                l_prev, l_cur = logits[:, :blk], logits[:, blk:]
                if sb == 0:
                    l_prev = jnp.where(first, NEG_BIG, l_prev)
                sink = jnp.concatenate(
                    [jnp.full((blk, blk), sink_ref[2 * (tiles * g + i) + e], F32) for i in range(tiles)], axis=0)
                m = jnp.max(jnp.maximum(l_prev, l_cur), axis=-1, keepdims=True)
                m = jnp.maximum(jnp.broadcast_to(m, sink.shape), sink)
                p = jnp.concatenate([jnp.exp(l_prev - m).astype(BF16), jnp.exp(l_cur - m).astype(BF16)], axis=1)
                pv = _dot(p, jnp.concatenate([vv, ones], axis=1))
                part = pv[:, :LANES] / (pv[:, LANES:] + jnp.exp(sink - m))
                res = part if res is None else res + part
            for i in range(tiles):
                cols = slice((tiles * g + i) * LANES, (tiles * g + i + 1) * LANES)
                o_ref[rows, cols] = (res[i * blk:(i + 1) * blk] * g_ref[rows, cols].astype(F32)).astype(BF16)


def _swa_call(sinks, qb, kb, vb, kbs, vbs, gb, bias):
    bsz, seq, _ = qb.shape
    T = SWA_T
    per = T // WINDOW
    cur = lambda b, i: (b, i, 0)
    prev = lambda b, i: (b, jnp.maximum(i * per - 1, 0), 0)
    kv_cur = pl.BlockSpec((None, T, B_KV_WIDTH), cur)
    kv_prev = pl.BlockSpec((None, WINDOW, B_KV_WIDTH), prev)
    return pl.pallas_call(
        _swa_kernel,
        grid=(bsz, seq // T),
        in_specs=[
            pl.BlockSpec(memory_space=pltpu.SMEM),
            pl.BlockSpec((None, T, B_WIDTH), cur),
            kv_cur, kv_cur, kv_cur, kv_cur,
            kv_prev, kv_prev, kv_prev, kv_prev,
            pl.BlockSpec((None, T, B_WIDTH), cur),
            pl.BlockSpec(bias.shape, lambda b, i: (0, 0, 0)),
        ],
        out_specs=pl.BlockSpec((None, T, B_WIDTH), cur),
        out_shape=jax.ShapeDtypeStruct((bsz, seq, B_WIDTH), BF16),
        compiler_params=_cparams(("parallel", "parallel")),
        name="swa",
    )(sinks, qb, kb, vb, kbs, vbs, kb, vb, kbs, vbs, gb, bias)


def _mem_kernel(q_ref, k_ref, v_ref, g_ref, o_ref):
    for hh in range(C_HEADS):
        sl = slice(hh * C_HEAD_DIM, (hh + 1) * C_HEAD_DIM)
        s = _dot_nt(q_ref[:, sl], k_ref[:, sl])
        m = jnp.max(s, axis=-1, keepdims=True)
        p = jnp.exp(s - m)
        den = jnp.sum(p, axis=-1, keepdims=True)
        o = _dot(p.astype(BF16), v_ref[:, sl]) / den
        o_ref[:, sl] = (o * g_ref[:, sl].astype(F32)).astype(BF16)


def _mem_call(qc, kc, vc, gc):
    bsz, seq, _ = qc.shape
    mlen = kc.shape[1]
    T = SWA_T
    cur = lambda b, i: (b, i, 0)
    whole = lambda b, i: (b, 0, 0)
    return pl.pallas_call(
        _mem_kernel,
        grid=(bsz, seq // T),
        in_specs=[
            pl.BlockSpec((None, T, C_WIDTH), cur),
            pl.BlockSpec((None, mlen, C_WIDTH), whole),
            pl.BlockSpec((None, mlen, C_WIDTH), whole),
            pl.BlockSpec((None, T, C_WIDTH), cur),
        ],
        out_specs=pl.BlockSpec((None, T, C_WIDTH), cur),
        out_shape=jax.ShapeDtypeStruct((bsz, seq, C_WIDTH), BF16),
        compiler_params=_cparams(("parallel", "parallel")),
        name="mem_attn",
    )(qc, kc, vc, gc)


def _final_kernel(x_ref, h_ref, oa_ref, ob_ref, oc_ref, wma_ref, wmb_ref, wmc_ref,
                  ba_ref, bb_ref, bc_ref, wua_ref, wub_ref, wuc_ref, wo_ref, out_ref, acc_ref):
    c = pl.program_id(1)

    @pl.when(c == 0)
    def _():
        acc_ref[...] = jnp.zeros(acc_ref.shape, F32)

    h = h_ref[...]

    def branch(o_ref, wm_ref, b_ref, wu_ref):
        gate = 1.0 / (1.0 + jnp.exp(-(_dot_nt(h, wm_ref[...]) + b_ref[...])))
        return gate * _dot(o_ref[...], wu_ref[...])

    merged = (branch(oa_ref, wma_ref, ba_ref, wua_ref) + branch(ob_ref, wmb_ref, bb_ref, wub_ref)
              + branch(oc_ref, wmc_ref, bc_ref, wuc_ref))
    acc_ref[...] += _dot(merged.astype(BF16), wo_ref[...])

    @pl.when(c == pl.num_programs(1) - 1)
    def _():
        out_ref[...] = x_ref[...] + acc_ref[...]


def _final_call(x2d, h, oa, ob, oc, wmix, gate_bias, wua, wub, wuc, wo):
    n, d = x2d.shape
    tm = FINAL_ROWS
    nch = FINAL_NCHUNK
    cw = d // nch
    row = lambda i, c: (i, 0)

    def col(br):
        return lambda i, c: (0, br * nch + c)

    def wrow(br):
        return lambda i, c: (br * nch + c, 0)

    return pl.pallas_call(
        _final_kernel,
        grid=(n // tm, nch),
        in_specs=[
            pl.BlockSpec((tm, d), row),
            pl.BlockSpec((tm, d), row),
            pl.BlockSpec((tm, A_WIDTH), row),
            pl.BlockSpec((tm, B_WIDTH), row),
            pl.BlockSpec((tm, C_WIDTH), row),
            pl.BlockSpec((cw, d), wrow(0)), pl.BlockSpec((cw, d), wrow(1)), pl.BlockSpec((cw, d), wrow(2)),
            pl.BlockSpec((1, cw), col(0)), pl.BlockSpec((1, cw), col(1)), pl.BlockSpec((1, cw), col(2)),
            pl.BlockSpec((A_WIDTH, cw), lambda i, c: (0, c)),
            pl.BlockSpec((B_WIDTH, cw), lambda i, c: (0, c)),
            pl.BlockSpec((C_WIDTH, cw), lambda i, c: (0, c)),
            pl.BlockSpec((cw, d), lambda i, c: (c, 0)),
        ],
        out_specs=pl.BlockSpec((tm, d), row),
        out_shape=jax.ShapeDtypeStruct((n, d), F32),
        scratch_shapes=[pltpu.VMEM((tm, d), F32)],
        compiler_params=_cparams(("parallel", "arbitrary")),
        name="merge_out",
    )(x2d, h, oa, ob, oc, wmix, wmix, wmix, gate_bias, gate_bias, gate_bias, wua, wub, wuc, wo)


def _w_in_groups(d):
    sizes = (A_WIDTH, A_KV_RANK, IDX_HEADS * IDX_DIM, IDX_DIM, IDX_HEADS, A_WIDTH,
             B_WIDTH, B_KV_WIDTH, B_KV_WIDTH, B_WIDTH, C_WIDTH, C_WIDTH, N_BRANCH * d)
    cuts = np.cumsum((0,) + sizes).tolist()
    (aq, ackv, iq, ik, iw, ag, bq, bk, bv, bg, cq, cg, mix) = [
        (cuts[i], cuts[i + 1]) for i in range(len(sizes))]
    return [
        [(aq[0], ackv[1])],
        [(iq[0], ik[1]), LANES - IDX_DIM, iw, LANES - IDX_HEADS],
        [ag, bg, cg],
        [(bq[0], bv[1]), cq,
         (bk[0] + B_HEAD_DIM, bk[1]), (bk[0], bk[0] + B_HEAD_DIM),
         (bv[0] + B_HEAD_DIM, bv[1]), (bv[0], bv[0] + B_HEAD_DIM)],
        [mix],
    ]


def _group_width(group):
    return sum(p if isinstance(p, int) else p[1] - p[0] for p in group)


def _regroup_kernel(w_ref, *out_refs, groups):
    for o_ref, group in zip(out_refs, groups):
        row = 0
        for part in group:
            if isinstance(part, int):
                o_ref[row:row + part, :] = jnp.zeros((part, o_ref.shape[1]), BF16)
                row += part
            else:
                o_ref[row:row + part[1] - part[0], :] = w_ref[part[0]:part[1], :].astype(BF16)
                row += part[1] - part[0]


def _regroup_w_in(w_in_t):
    cols, d = w_in_t.shape
    groups = _w_in_groups(d)
    lanes = REGROUP_LANES
    return pl.pallas_call(
        functools.partial(_regroup_kernel, groups=groups),
        grid=(d // lanes,),
        in_specs=[pl.BlockSpec((cols, lanes), lambda i: (0, i))],
        out_specs=[pl.BlockSpec((_group_width(g), lanes), lambda i: (0, i)) for g in groups],
        out_shape=[jax.ShapeDtypeStruct((_group_width(g), d), BF16) for g in groups],
        compiler_params=_cparams(("parallel",)),
        name="regroup_w_in",
    )(w_in_t)


def _layer(x, mem, norm_g, w_in, kv_norm_g, w_kv_up, idx_k_ln_g, idx_k_ln_b, q_norm_a, k_norm_a,
           q_norm_b, k_norm_b, sinks_b, mem_norm_g, w_mem_kv, q_norm_c, k_norm_c,
           w_up_a, w_up_b, w_up_c, gate_bias, w_o, rel_bias):
    bsz, seq, d = x.shape
    mlen = mem.shape[1]
    n = bsz * seq
    tm = ROW_TILE
    topk = min(TOPK_MAX, seq // 4)
    x2d = x.reshape(n, d)
    row2 = lambda v: v.reshape(1, -1)

    w_grp_a, w_grp_i, w_grp_g, w_grp_bc, w_mix = _regroup_w_in(jnp.swapaxes(w_in, 0, 1))
    g_x = row2(norm_g)

    qa, ka, va, h = _proj_call(
        _proj_a_kernel, "proj_a", x2d, tm,
        [g_x, w_grp_a, row2(kv_norm_g), w_kv_up.astype(BF16), row2(q_norm_a), row2(k_norm_a)],
        [jax.ShapeDtypeStruct((n, A_WIDTH), BF16)] * 3 + [jax.ShapeDtypeStruct((n, d), BF16)],
        [_row_spec(tm, A_WIDTH)] * 3 + [_row_spec(tm, d)])

    per_b = seq // tm
    iq, ik, iw = _proj_call(
        _proj_i_kernel, "proj_i", h, tm,
        [w_grp_i, row2(idx_k_ln_g), row2(idx_k_ln_b)],
        [jax.ShapeDtypeStruct((bsz, IDX_PAIRS, seq, LANES), BF16),
         jax.ShapeDtypeStruct((n, IDX_DIM), BF16),
         jax.ShapeDtypeStruct((n, IDX_HEADS), F32)],
        [pl.BlockSpec((None, IDX_PAIRS, tm, LANES), lambda i: (i // per_b, 0, i % per_b, 0)),
         _row_spec(tm, IDX_DIM), _row_spec(tm, IDX_HEADS)])

    ga, gb, gc = _proj_call(
        _proj_g_kernel, "proj_g", h, tm, [w_grp_g],
        [jax.ShapeDtypeStruct((n, A_WIDTH), BF16), jax.ShapeDtypeStruct((n, B_WIDTH), BF16),
         jax.ShapeDtypeStruct((n, C_WIDTH), BF16)],
        [_row_spec(tm, A_WIDTH), _row_spec(tm, B_WIDTH), _row_spec(tm, C_WIDTH)])

    bc_widths = (B_WIDTH, B_KV_WIDTH, B_KV_WIDTH, C_WIDTH, B_KV_WIDTH, B_KV_WIDTH)
    qb, kb, vb, qc, kbs, vbs = _proj_call(
        _proj_bc_kernel, "proj_bc", h, tm,
        [w_grp_bc, row2(jnp.tile(q_norm_b, 2)), row2(jnp.tile(k_norm_b, 2)), row2(q_norm_c)],
        [jax.ShapeDtypeStruct((n, w), BF16) for w in bc_widths],
        [_row_spec(tm, w) for w in bc_widths])

    mrows = bsz * mlen
    kc, vc = _proj_call(
        _proj_mem_kernel, "proj_mem", mem.reshape(mrows, d), min(tm, mrows),
        [row2(mem_norm_g), w_mem_kv.astype(BF16), row2(k_norm_c)],
        [jax.ShapeDtypeStruct((mrows, C_WIDTH), BF16)] * 2,
        [_row_spec(min(tm, mrows), C_WIDTH)] * 2)

    T = DSA_T
    i_idx = np.arange(T)[:, None]
    dist_a = T + i_idx - np.arange(2 * T)[None, :]
    rb_a = _bias_tiles(rel_bias, _t5_bucket_np(dist_a), 0, A_HEADS, NUM_BUCKETS - 1, LOG2E)
    i_idx = np.arange(WINDOW)[:, None]
    dist_b = WINDOW + i_idx - np.arange(2 * WINDOW)[None, :]
    bucket_b = np.where((dist_b >= 0) & (dist_b < WINDOW), _t5_bucket_np(dist_b), -1).astype(np.int32)
    bias_b = _bias_tiles(rel_bias, bucket_b, A_HEADS, B_HEADS, None, 1.0)
    tiles_b = B_HEADS // B_KV_HEADS // 2
    slot_heads = [2 * (tiles_b * g + i) + e for g in range(B_KV_HEADS) for e in range(2) for i in range(tiles_b)]
    bias_b = bias_b[np.array(slot_heads)].reshape(2 * B_KV_HEADS, tiles_b * WINDOW, 2 * WINDOW)

    r3 = lambda v, w: v.reshape(bsz, seq, w)
    ikt = jnp.swapaxes(ik.reshape(bsz, seq, IDX_DIM), 1, 2)
    zeros = jnp.zeros_like(ikt)
    ikz = jnp.stack([jnp.concatenate([ikt, zeros], axis=1), jnp.concatenate([zeros, ikt], axis=1)], axis=1)
    oa = _dsa_call(iq, iw.reshape(bsz, seq, IDX_HEADS), ikz,
                   r3(qa, A_WIDTH), r3(ka, A_WIDTH),
                   r3(va, A_WIDTH), r3(ga, A_WIDTH), rb_a, topk)
    ob = _swa_call(sinks_b, r3(qb, B_WIDTH), r3(kb, B_KV_WIDTH), r3(vb, B_KV_WIDTH),
                   r3(kbs, B_KV_WIDTH), r3(vbs, B_KV_WIDTH), r3(gb, B_WIDTH), bias_b)
    oc = _mem_call(r3(qc, C_WIDTH), kc.reshape(bsz, mlen, C_WIDTH), vc.reshape(bsz, mlen, C_WIDTH),
                   r3(gc, C_WIDTH))

    out = _final_call(x2d, h, oa.reshape(n, A_WIDTH), ob.reshape(n, B_WIDTH), oc.reshape(n, C_WIDTH),
                      w_mix, row2(gate_bias), w_up_a.astype(BF16), w_up_b.astype(BF16),
                      w_up_c.astype(BF16), w_o.astype(BF16))
    return out.reshape(bsz, seq, d)


def kernel(x, mem, norm_g, w_in, kv_norm_g, w_kv_up, idx_k_ln_g, idx_k_ln_b, q_norm_a, k_norm_a, q_norm_b, k_norm_b, sinks_b, mem_norm_g, w_mem_kv, q_norm_c, k_norm_c, w_up_a, w_up_b, w_up_c, gate_bias, w_o, rel_bias):
    for l in range(norm_g.shape[0]):
        x = _layer(x, mem, norm_g[l], w_in[l], kv_norm_g[l], w_kv_up[l], idx_k_ln_g[l], idx_k_ln_b[l],
                   q_norm_a[l], k_norm_a[l], q_norm_b[l], k_norm_b[l], sinks_b[l], mem_norm_g[l],
                   w_mem_kv[l], q_norm_c[l], k_norm_c[l], w_up_a[l], w_up_b[l], w_up_c[l],
                   gate_bias[l], w_o[l], rel_bias)
    return x
```

```python
import functools
import math

import numpy as np
import jax
import jax.numpy as jnp
from jax import lax
from jax.experimental import pallas as pl
from jax.experimental.pallas import tpu as pltpu

F32 = jnp.float32
BF16 = jnp.bfloat16

EPS = 1e-6
A_HEADS, A_HEAD_DIM, A_KV_RANK = 6, 128, 256
IDX_HEADS, IDX_DIM, TOPK_MAX = 16, 64, 256
IDX_PAIRS = IDX_HEADS // 2
B_HEADS, B_KV_HEADS, B_HEAD_DIM, WINDOW = 12, 2, 64, 128
C_HEADS, C_HEAD_DIM = 4, 128
NUM_BUCKETS, MAX_DISTANCE = 32, 128
N_BRANCH = 3
A_WIDTH = A_HEADS * A_HEAD_DIM
B_WIDTH = B_HEADS * B_HEAD_DIM
B_KV_WIDTH = B_KV_HEADS * B_HEAD_DIM
C_WIDTH = C_HEADS * C_HEAD_DIM

LANES = 128
VMEM_LIMIT = 56 * 1024 * 1024
NEG_BIG = -1e30
BELOW_ALL = -3e38

ROW_TILE = 1024
DSA_T = 256
SWA_T = 512
FINAL_ROWS = 512
FINAL_NCHUNK = 4
COUNT_ROWS = 128
REGROUP_LANES = 256
SCORE_TILES = 8
COUNT_UNROLL = 4
FAR_TILES = 4
LOG2E = math.log2(math.e)


def _cparams(sem):
    return pltpu.CompilerParams(dimension_semantics=sem, vmem_limit_bytes=VMEM_LIMIT)


def _dot(a, b):
    return jnp.dot(a, b, preferred_element_type=F32)


def _dot_nt(a, b):
    return lax.dot_general(a, b, (((1,), (1,)), ((), ())), preferred_element_type=F32)


def _rms_rows(x, g):
    ms = jnp.mean(x * x, axis=-1, keepdims=True)
    return x * lax.rsqrt(ms + EPS) * g


def _headnorm128(seg, g):
    ms = jnp.mean(seg * seg, axis=-1, keepdims=True)
    return seg * lax.rsqrt(ms + EPS) * g


def _headnorm64_pair(seg, g2):
    sq = seg * seg
    low = lax.broadcasted_iota(jnp.int32, seg.shape, 1) < B_HEAD_DIM
    s_all = jnp.sum(sq, axis=-1, keepdims=True)
    s_lo = jnp.sum(jnp.where(low, sq, 0.0), axis=-1, keepdims=True)
    ms = jnp.where(low, s_lo, s_all - s_lo) * (1.0 / B_HEAD_DIM)
    return seg * lax.rsqrt(ms + EPS) * g2


def _silu(y):
    return y / (1.0 + jnp.exp(-y))


def _proj_a_kernel(x_ref, g_ref, w_ref, kvg_ref, wkv_ref, qg_ref, kg_ref, qa_ref, ka_ref, va_ref, h_ref):
    h = _rms_rows(x_ref[...], g_ref[...]).astype(BF16)
    h_ref[...] = h
    y = _dot_nt(h, w_ref[...])
    scale = A_HEAD_DIM ** -0.5 * LOG2E
    for hh in range(A_HEADS):
        sl = slice(hh * A_HEAD_DIM, (hh + 1) * A_HEAD_DIM)
        qa_ref[:, sl] = (_headnorm128(y[:, sl], qg_ref[...]) * scale).astype(BF16)
    ckv = _rms_rows(y[:, A_WIDTH:A_WIDTH + A_KV_RANK], kvg_ref[...]).astype(BF16)
    kv = _dot(ckv, wkv_ref[...])
    for hh in range(A_HEADS):
        sl = slice(hh * A_HEAD_DIM, (hh + 1) * A_HEAD_DIM)
        ka_ref[:, sl] = _headnorm128(kv[:, sl], kg_ref[...]).astype(BF16)
    va_ref[...] = kv[:, A_WIDTH:].astype(BF16)


def _proj_i_kernel(h_ref, w_ref, lng_ref, lnb_ref, iq_ref, ik_ref, iw_ref):
    y = _dot_nt(h_ref[...], w_ref[...])
    k0 = IDX_HEADS * IDX_DIM
    for p in range(IDX_PAIRS):
        iq_ref[p] = y[:, p * LANES:(p + 1) * LANES].astype(BF16)
    ik = y[:, k0:k0 + IDX_DIM]
    mu = jnp.mean(ik, axis=-1, keepdims=True)
    d = ik - mu
    var = jnp.mean(d * d, axis=-1, keepdims=True)
    ik_ref[...] = (d * lax.rsqrt(var + EPS) * lng_ref[...] + lnb_ref[...]).astype(BF16)
    w0 = k0 + LANES
    iw_ref[...] = y[:, w0:w0 + IDX_HEADS] * (IDX_HEADS ** -0.5) * (IDX_DIM ** -0.5)


def _proj_g_kernel(h_ref, w_ref, ga_ref, gb_ref, gc_ref):
    y = _silu(_dot_nt(h_ref[...], w_ref[...]))
    ga_ref[...] = y[:, :A_WIDTH].astype(BF16)
    gb_ref[...] = y[:, A_WIDTH:A_WIDTH + B_WIDTH].astype(BF16)
    gc_ref[...] = y[:, A_WIDTH + B_WIDTH:].astype(BF16)


def _proj_bc_kernel(h_ref, w_ref, qbg_ref, kbg_ref, qcg_ref,
                    qb_ref, kb_ref, vb_ref, qc_ref, kbs_ref, vbs_ref):
    y = _dot_nt(h_ref[...], w_ref[...])
    sb = B_HEAD_DIM ** -0.5
    for p in range(B_WIDTH // LANES):
        sl = slice(p * LANES, (p + 1) * LANES)
        qb_ref[:, sl] = (_headnorm64_pair(y[:, sl], qbg_ref[...]) * sb).astype(BF16)
    k0 = B_WIDTH
    kb_ref[...] = _headnorm64_pair(y[:, k0:k0 + B_KV_WIDTH], kbg_ref[...]).astype(BF16)
    vb_ref[...] = y[:, k0 + B_KV_WIDTH:k0 + 2 * B_KV_WIDTH].astype(BF16)
    s0 = k0 + 2 * B_KV_WIDTH + C_WIDTH
    kbs_ref[...] = _headnorm64_pair(y[:, s0:s0 + B_KV_WIDTH], kbg_ref[...]).astype(BF16)
    vbs_ref[...] = y[:, s0 + B_KV_WIDTH:s0 + 2 * B_KV_WIDTH].astype(BF16)
    c0 = k0 + 2 * B_KV_WIDTH
    sc = C_HEAD_DIM ** -0.5
    for hh in range(C_HEADS):
        sl = slice(hh * C_HEAD_DIM, (hh + 1) * C_HEAD_DIM)
        qc_ref[:, sl] = (_headnorm128(y[:, c0 + hh * C_HEAD_DIM:c0 + (hh + 1) * C_HEAD_DIM],
                                      qcg_ref[...]) * sc).astype(BF16)


def _proj_mem_kernel(x_ref, g_ref, w_ref, kg_ref, kc_ref, vc_ref):
    h = _rms_rows(x_ref[...], g_ref[...]).astype(BF16)
    y = _dot(h, w_ref[...])
    for hh in range(C_HEADS):
        sl = slice(hh * C_HEAD_DIM, (hh + 1) * C_HEAD_DIM)
        kc_ref[:, sl] = _headnorm128(y[:, sl], kg_ref[...]).astype(BF16)
    vc_ref[...] = y[:, C_WIDTH:].astype(BF16)


def _row_spec(tm, cols):
    return pl.BlockSpec((tm, cols), lambda i: (i, 0))


def _full_spec(shape):
    nd = len(shape)
    return pl.BlockSpec(shape, lambda i: (0,) * nd, pipeline_mode=pl.Buffered(1))


def _proj_call(kernel_fn, name, x2d, tm, consts, out_shapes, out_specs):
    n, d = x2d.shape
    in_specs = [_row_spec(tm, d)] + [_full_spec(c.shape) for c in consts]
    return pl.pallas_call(
        kernel_fn,
        grid=(n // tm,),
        in_specs=in_specs,
        out_specs=out_specs,
        out_shape=out_shapes,
        compiler_params=_cparams(("parallel",)),
        name=name,
    )(x2d, *consts)


def _t5_bucket_np(dist):
    n = np.maximum(dist, 0)
    max_exact = NUM_BUCKETS // 2
    nf = np.maximum(n, 1).astype(np.float32)
    large = max_exact + (np.log(nf / max_exact) / math.log(MAX_DISTANCE / max_exact)
                         * (NUM_BUCKETS - max_exact)).astype(np.int32)
    large = np.minimum(large, NUM_BUCKETS - 1)
    return np.where(n < max_exact, n, large).astype(np.int32)


def _bias_tile_kernel(tab_ref, bucket_ref, o_ref, *, head0, shift_bucket, scale):
    h = pl.program_id(0) + head0
    bucket = bucket_ref[...]
    acc = jnp.zeros(bucket.shape, F32)
    for b in range(NUM_BUCKETS):
        acc = jnp.where(bucket == b, tab_ref[b, h], acc)
    if shift_bucket is not None:
        acc = acc - tab_ref[shift_bucket, h]
    o_ref[...] = jnp.where(bucket < 0, NEG_BIG, acc * scale)


def _bias_tiles(rel_bias, bucket_np, head0, nheads, shift_bucket, scale):
    r, c = bucket_np.shape
    return pl.pallas_call(
        functools.partial(_bias_tile_kernel, head0=head0, shift_bucket=shift_bucket, scale=scale),
        grid=(nheads,),
        in_specs=[pl.BlockSpec(memory_space=pltpu.SMEM), pl.BlockSpec((r, c), lambda h: (0, 0))],
        out_specs=pl.BlockSpec((None, r, c), lambda h: (h, 0, 0)),
        out_shape=jax.ShapeDtypeStruct((nheads, r, c), F32),
        compiler_params=_cparams(("arbitrary",)),
        name="bias_tiles",
    )(rel_bias, jnp.asarray(bucket_np))


def _dsa_kernel(iq_ref, iw_ref, ikz_ref, q_ref, k_ref, v_ref, g_ref, rb_ref, o_ref,
                sc_ref, wb_ref, lo_ref, hi_ref, mid_ref, clo_ref, s_ref, m_ref, l_ref, acc_ref, *, topk):
    T = DSA_T
    qi = pl.program_id(1)
    kf = float(topk)

    def lanes(col):
        return jnp.broadcast_to(col, (T, LANES))

    def halves(tile):
        return [tile[:, c * LANES:(c + 1) * LANES] for c in range(T // LANES)]

    for hh in range(IDX_HEADS):
        wb_ref[hh] = lanes(iw_ref[:, hh:hh + 1])

    def score_chunk(j, diag):
        keys = pl.ds(pl.multiple_of(j * T, T), T)
        accs = [jnp.zeros((T, LANES), F32) for _ in range(T // LANES)]
        for hh in range(IDX_HEADS):
            s = _dot(iq_ref[hh // 2], ikz_ref[hh % 2, :, keys])
            w = wb_ref[hh]
            accs = [a + w * jnp.maximum(sh, 0.0) for a, sh in zip(accs, halves(s))]
        if diag:
            row = lax.broadcasted_iota(jnp.int32, (T, LANES), 0)
            col = lax.broadcasted_iota(jnp.int32, (T, LANES), 1)
            causal = [col + c * LANES <= row for c in range(T // LANES)]
            accs = [jnp.where(cm, a, -jnp.inf) for cm, a in zip(causal, accs)]
        for c, a in enumerate(accs):
            sc_ref[j, :, c * LANES:(c + 1) * LANES] = a
        return accs

    def p1_body(j, mxs):
        return [jnp.maximum(a, b) for a, b in zip(mxs, score_chunk(j, False))]

    carry = [jnp.full((T, LANES), -jnp.inf, F32) for _ in range(T // LANES)]
    done_tiles = 0
    width = SCORE_TILES
    while width >= 1:
        trips = lax.div(qi - done_tiles, width)

        def p1_group(g, c, width=width, base=done_tiles):
            for u in range(width):
                c = p1_body(base + g * width + u, c)
            return c

        carry = lax.fori_loop(0, trips, p1_group, carry)
        done_tiles = done_tiles + trips * width
        width //= 2
    mxs = [jnp.maximum(a, b) for a, b in zip(carry, score_chunk(qi, True))]
    mx = lanes(jnp.max(functools.reduce(jnp.maximum, mxs), axis=-1, keepdims=True))
    class_floor = lanes(jnp.min(functools.reduce(jnp.minimum, mxs), axis=-1, keepdims=True))

    def count_lanes(rows, v, strict=False):
        def tile(j, c):
            for c0 in range(0, T, LANES):
                sh = sc_ref[j, rows, c0:c0 + LANES]
                hit = (sh > v) if strict else (sh >= v)
                c = c + jnp.where(hit, 1.0, 0.0)
            return c

        def group(g, c):
            for u in range(COUNT_UNROLL):
                c = tile(g * COUNT_UNROLL + u, c)
            return c

        n_groups = lax.div(qi + 1, COUNT_UNROLL)
        c = lax.fori_loop(0, n_groups, group, jnp.zeros((COUNT_ROWS, LANES), F32))
        return lax.fori_loop(n_groups * COUNT_UNROLL, qi + 1, tile, c)

    def row_total(c):
        return jnp.broadcast_to(jnp.sum(c, axis=-1, keepdims=True), c.shape)

    slabs = [slice(r0, r0 + COUNT_ROWS) for r0 in range(0, T, COUNT_ROWS)]
    n_causal = (qi * T + 1 + lax.broadcasted_iota(jnp.int32, (T, LANES), 0)).astype(F32)
    keep_all = n_causal <= kf
    lo0 = jnp.where(keep_all, BELOW_ALL, class_floor)
    above_max = mx + jnp.maximum(jnp.abs(mx) * (2.0 ** -20), 1e-30)
    hi0 = jnp.where(keep_all, BELOW_ALL, above_max)
    lo_ref[...] = lo0
    hi_ref[...] = hi0
    mid_ref[...] = lo0 + 0.5 * (hi0 - lo0)
    clo_ref[...] = n_causal

    def bis_body(open_widths):
        idle = jnp.zeros((COUNT_ROWS, LANES), F32)
        partial = [lax.cond(w > 0.0, lambda rows=rows: count_lanes(rows, mid_ref[rows]), lambda: idle)
                   for rows, w in zip(slabs, open_widths)]
        new_widths = tuple(jnp.max(hi_ref[rows] - lo_ref[rows]) for rows in slabs)
        for rows, c in zip(slabs, partial):
            lo, hi, mid = lo_ref[rows], hi_ref[rows], mid_ref[rows]
            cnt = row_total(c)
            up = cnt >= kf
            lo = jnp.where(up, mid, lo)
            c_lo = jnp.where(up, cnt, clo_ref[rows])
            hi = jnp.where(up, hi, mid)
            nxt = lo + 0.5 * (hi - lo)
            stop = jnp.logical_or(c_lo == kf, jnp.logical_or(nxt <= lo, nxt >= hi))
            hi = jnp.where(stop, lo, hi)
            lo_ref[rows] = lo
            hi_ref[rows] = hi
            mid_ref[rows] = jnp.where(stop, lo, nxt)
            clo_ref[rows] = c_lo
        return new_widths

    lax.while_loop(lambda ws: functools.reduce(jnp.maximum, ws) > 0.0, bis_body,
                   tuple(jnp.max((hi0 - lo0)[rows]) for rows in slabs))
    thr = lo_ref[...]

    tied_f = jnp.where(clo_ref[...] > kf, 1.0, 0.0)

    @pl.when(jnp.max(tied_f) > 0.0)
    def _():
        above = jnp.concatenate(
            [row_total(count_lanes(rows, thr[rows], strict=True)) for rows in slabs], axis=0)
        need = (kf - above)[:, :1]
        r = lax.broadcasted_iota(jnp.int32, (T, T), 0)
        c = lax.broadcasted_iota(jnp.int32, (T, T), 1)
        before = jnp.where(r < c, 1.0, 0.0).astype(BF16)
        thr_col = thr[:, :1]
        tied_col = tied_f[:, :1] > 0.0

        def body(j, seen):
            s = sc_ref[j]
            eq = jnp.where(s == thr_col, 1.0, 0.0)
            rank = seen + _dot(eq.astype(BF16), before)
            drop = jnp.logical_and(tied_col, jnp.logical_and(eq > 0.0, rank >= need))
            sc_ref[j] = jnp.where(drop, -jnp.inf, s)
            return seen + jnp.sum(eq, axis=-1, keepdims=True)

        lax.fori_loop(0, qi + 1, body, jnp.zeros((T, 1), F32))

    m_ref[...] = jnp.full(m_ref.shape, NEG_BIG, F32)
    l_ref[...] = jnp.zeros(l_ref.shape, F32)
    acc_ref[...] = jnp.zeros(acc_ref.shape, F32)

    def attend(j0, ntiles, band):
        sel = [jnp.where(sh >= thr, 0.0, NEG_BIG) for t in range(ntiles) for sh in halves(sc_ref[j0 + t])]
        start = pl.multiple_of(j0 * T, T)
        keys = pl.ds(start, ntiles * T)
        width = ntiles * T

        def logits_to_scratch(hh):
            sl = slice(hh * A_HEAD_DIM, (hh + 1) * A_HEAD_DIM)
            s_ref[hh % 2, :, :width] = _dot_nt(q_ref[:, sl], k_ref[keys, sl])

        logits_to_scratch(0)
        pending = None
        for hh in range(A_HEADS):
            sl = slice(hh * A_HEAD_DIM, (hh + 1) * A_HEAD_DIM)
            if hh + 1 < A_HEADS:
                logits_to_scratch(hh + 1)
            s = s_ref[hh % 2, :, :width]
            if band is not None:
                s = s + rb_ref[hh, :, band * T:(band + ntiles) * T]
            sh = [s[:, c * LANES:(c + 1) * LANES] + m for c, m in enumerate(sel)]
            m_prev = m_ref[hh]
            m_cur = jnp.max(functools.reduce(jnp.maximum, sh), axis=-1, keepdims=True)
            m_new = jnp.maximum(m_prev, lanes(m_cur))
            alpha = jnp.exp2(m_prev - m_new)
            ph = [jnp.exp2(x - m_new) for x in sh]
            l_ref[hh] = alpha * l_ref[hh] + functools.reduce(jnp.add, ph)
            p = jnp.concatenate([x.astype(BF16) for x in ph], axis=-1)
            acc_ref[hh] = alpha * acc_ref[hh]
            m_ref[hh] = m_new
            if pending is not None:
                acc_ref[pending[0]] += pending[1]
            pending = (hh, _dot(p, v_ref[keys, sl]))
        acc_ref[pending[0]] += pending[1]

    n_far = jnp.maximum(qi - 1, 0)
    done_tiles = 0
    width = FAR_TILES
    while width >= 1:
        trips = lax.div(n_far - done_tiles, width)

        def far_body(i, c, width=width, base=done_tiles):
            attend(base + i * width, width, None)
            return c

        lax.fori_loop(0, trips, far_body, 0)
        done_tiles = done_tiles + trips * width
        width //= 2

    @pl.when(qi > 0)
    def _():
        attend(qi - 1, 2, 0)

    @pl.when(qi == 0)
    def _():
        attend(0, 1, 1)

    for hh in range(A_HEADS):
        sl = slice(hh * A_HEAD_DIM, (hh + 1) * A_HEAD_DIM)
        den = jnp.sum(l_ref[hh], axis=-1, keepdims=True)
        o_ref[:, sl] = (acc_ref[hh] / den * g_ref[:, sl].astype(F32)).astype(BF16)


def _dsa_call(iq, iw, ikz, qa, ka, va, ga, rb, topk):
    bsz, seq, _ = qa.shape
    T = DSA_T
    nq = seq // T
    assert topk <= T and seq % T == 0
    once = pl.Buffered(1)
    return pl.pallas_call(
        functools.partial(_dsa_kernel, topk=topk),
        grid=(bsz, nq),
        in_specs=[
            pl.BlockSpec((None, IDX_PAIRS, T, LANES), lambda b, i: (b, 0, i, 0)),
            pl.BlockSpec((None, T, IDX_HEADS), lambda b, i: (b, i, 0)),
            pl.BlockSpec((None, 2, LANES, seq), lambda b, i: (b, 0, 0, 0), pipeline_mode=once),
            pl.BlockSpec((None, T, A_WIDTH), lambda b, i: (b, i, 0)),
            pl.BlockSpec((None, seq, A_WIDTH), lambda b, i: (b, 0, 0), pipeline_mode=once),
            pl.BlockSpec((None, seq, A_WIDTH), lambda b, i: (b, 0, 0), pipeline_mode=once),
            pl.BlockSpec((None, T, A_WIDTH), lambda b, i: (b, i, 0)),
            pl.BlockSpec((A_HEADS, T, 2 * T), lambda b, i: (0, 0, 0), pipeline_mode=once),
        ],
        out_specs=pl.BlockSpec((None, T, A_WIDTH), lambda b, i: (b, i, 0)),
        out_shape=jax.ShapeDtypeStruct((bsz, seq, A_WIDTH), BF16),
        scratch_shapes=[
            pltpu.VMEM((nq, T, T), F32),
            pltpu.VMEM((IDX_HEADS, T, LANES), F32),
            pltpu.VMEM((T, LANES), F32),
            pltpu.VMEM((T, LANES), F32),
            pltpu.VMEM((T, LANES), F32),
            pltpu.VMEM((T, LANES), F32),
            pltpu.VMEM((2, T, FAR_TILES * T), F32),
            pltpu.VMEM((A_HEADS, T, LANES), F32),
            pltpu.VMEM((A_HEADS, T, LANES), F32),
            pltpu.VMEM((A_HEADS, T, A_HEAD_DIM), F32),
        ],
        compiler_params=_cparams(("arbitrary", "arbitrary")),
        name="dsa",
    )(iq, iw, ikz, qa, ka, va, ga, rb)


def _swa_kernel(sink_ref, q_ref, kc_ref, vc_ref, kcs_ref, vcs_ref, kp_ref, vp_ref, kps_ref, vps_ref,
                g_ref, bias_ref, o_ref):
    first = pl.program_id(1) == 0
    blk = WINDOW
    tiles = B_HEADS // B_KV_HEADS // 2
    low_half = lax.broadcasted_iota(jnp.int32, (2 * blk, LANES), 1) < B_HEAD_DIM
    ones = jnp.ones((2 * blk, LANES), BF16)
    for sb in range(SWA_T // blk):
        rows = slice(sb * blk, (sb + 1) * blk)
        prev = slice((sb - 1) * blk, sb * blk)

        def window(cur_ref, prev_ref):
            before = prev_ref[...] if sb == 0 else cur_ref[prev, :]
            return jnp.concatenate([before, cur_ref[rows, :]], axis=0)

        k_nat, v_nat = window(kc_ref, kp_ref), window(vc_ref, vp_ref)
        k_swp, v_swp = window(kcs_ref, kps_ref), window(vcs_ref, vps_ref)
        for g in range(B_KV_HEADS):
            q = jnp.concatenate(
                [q_ref[rows, (tiles * g + i) * LANES:(tiles * g + i + 1) * LANES] for i in range(tiles)], axis=0)
            res = None
            for e in range(2):
                keep = low_half if e == 0 else jnp.logical_not(low_half)
                k_src, v_src = (k_nat, v_nat) if g == e else (k_swp, v_swp)
                kk = jnp.where(keep, k_src, jnp.zeros_like(k_src))
                vv = jnp.where(keep, v_src, jnp.zeros_like(v_src))
                logits = _dot_nt(q, kk) + bias_ref[2 * g + e]
                l_prev, l_cur = logits[:, :blk], logits[:, blk:]
                if sb == 0:
                    l_prev = jnp.where(first, NEG_BIG, l_prev)
                sink = jnp.concatenate(
                    [jnp.full((blk, blk), sink_ref[2 * (tiles * g + i) + e], F32) for i in range(tiles)], axis=0)
                m = jnp.max(jnp.maximum(l_prev, l_cur), axis=-1, keepdims=True)
                m = jnp.maximum(jnp.broadcast_to(m, sink.shape), sink)
                p = jnp.concatenate([jnp.exp(l_prev - m).astype(BF16), jnp.exp(l_cur - m).astype(BF16)], axis=1)
                pv = _dot(p, jnp.concatenate([vv, ones], axis=1))
                part = pv[:, :LANES] / (pv[:, LANES:] + jnp.exp(sink - m))
                res = part if res is None else res + part
            for i in range(tiles):
                cols = slice((tiles * g + i) * LANES, (tiles * g + i + 1) * LANES)
                o_ref[rows, cols] = (res[i * blk:(i + 1) * blk] * g_ref[rows, cols].astype(F32)).astype(BF16)


def _swa_call(sinks, qb, kb, vb, kbs, vbs, gb, bias):
    bsz, seq, _ = qb.shape
    T = SWA_T
    per = T // WINDOW
    cur = lambda b, i: (b, i, 0)
    prev = lambda b, i: (b, jnp.maximum(i * per - 1, 0), 0)
    kv_cur = pl.BlockSpec((None, T, B_KV_WIDTH), cur)
    kv_prev = pl.BlockSpec((None, WINDOW, B_KV_WIDTH), prev)
    return pl.pallas_call(
        _swa_kernel,
        grid=(bsz, seq // T),
        in_specs=[
            pl.BlockSpec(memory_space=pltpu.SMEM),
            pl.BlockSpec((None, T, B_WIDTH), cur),
            kv_cur, kv_cur, kv_cur, kv_cur,
            kv_prev, kv_prev, kv_prev, kv_prev,
            pl.BlockSpec((None, T, B_WIDTH), cur),
            pl.BlockSpec(bias.shape, lambda b, i: (0, 0, 0)),
        ],
        out_specs=pl.BlockSpec((None, T, B_WIDTH), cur),
        out_shape=jax.ShapeDtypeStruct((bsz, seq, B_WIDTH), BF16),
        compiler_params=_cparams(("parallel", "parallel")),
        name="swa",
    )(sinks, qb, kb, vb, kbs, vbs, kb, vb, kbs, vbs, gb, bias)


def _mem_kernel(q_ref, k_ref, v_ref, g_ref, o_ref):
    for hh in range(C_HEADS):
        sl = slice(hh * C_HEAD_DIM, (hh + 1) * C_HEAD_DIM)
        s = _dot_nt(q_ref[:, sl], k_ref[:, sl])
        m = jnp.max(s, axis=-1, keepdims=True)
        p = jnp.exp(s - m)
        den = jnp.sum(p, axis=-1, keepdims=True)
        o = _dot(p.astype(BF16), v_ref[:, sl]) / den
        o_ref[:, sl] = (o * g_ref[:, sl].astype(F32)).astype(BF16)


def _mem_call(qc, kc, vc, gc):
    bsz, seq, _ = qc.shape
    mlen = kc.shape[1]
    T = SWA_T
    cur = lambda b, i: (b, i, 0)
    whole = lambda b, i: (b, 0, 0)
    return pl.pallas_call(
        _mem_kernel,
        grid=(bsz, seq // T),
        in_specs=[
            pl.BlockSpec((None, T, C_WIDTH), cur),
            pl.BlockSpec((None, mlen, C_WIDTH), whole),
            pl.BlockSpec((None, mlen, C_WIDTH), whole),
            pl.BlockSpec((None, T, C_WIDTH), cur),
        ],
        out_specs=pl.BlockSpec((None, T, C_WIDTH), cur),
        out_shape=jax.ShapeDtypeStruct((bsz, seq, C_WIDTH), BF16),
        compiler_params=_cparams(("parallel", "parallel")),
        name="mem_attn",
    )(qc, kc, vc, gc)


def _final_kernel(x_ref, h_ref, oa_ref, ob_ref, oc_ref, wma_ref, wmb_ref, wmc_ref,
                  ba_ref, bb_ref, bc_ref, wua_ref, wub_ref, wuc_ref, wo_ref, out_ref, acc_ref):
    c = pl.program_id(1)

    @pl.when(c == 0)
    def _():
        acc_ref[...] = jnp.zeros(acc_ref.shape, F32)

    h = h_ref[...]

    def branch(o_ref, wm_ref, b_ref, wu_ref):
        gate = 1.0 / (1.0 + jnp.exp(-(_dot_nt(h, wm_ref[...]) + b_ref[...])))
        return gate * _dot(o_ref[...], wu_ref[...])

    merged = (branch(oa_ref, wma_ref, ba_ref, wua_ref) + branch(ob_ref, wmb_ref, bb_ref, wub_ref)
              + branch(oc_ref, wmc_ref, bc_ref, wuc_ref))
    acc_ref[...] += _dot(merged.astype(BF16), wo_ref[...])

    @pl.when(c == pl.num_programs(1) - 1)
    def _():
        out_ref[...] = x_ref[...] + acc_ref[...]


def _final_call(x2d, h, oa, ob, oc, wmix, gate_bias, wua, wub, wuc, wo):
    n, d = x2d.shape
    tm = FINAL_ROWS
    nch = FINAL_NCHUNK
    cw = d // nch
    row = lambda i, c: (i, 0)

    def col(br):
        return lambda i, c: (0, br * nch + c)

    def wrow(br):
        return lambda i, c: (br * nch + c, 0)

    return pl.pallas_call(
        _final_kernel,
        grid=(n // tm, nch),
        in_specs=[
            pl.BlockSpec((tm, d), row),
            pl.BlockSpec((tm, d), row),
            pl.BlockSpec((tm, A_WIDTH), row),
            pl.BlockSpec((tm, B_WIDTH), row),
            pl.BlockSpec((tm, C_WIDTH), row),
            pl.BlockSpec((cw, d), wrow(0)), pl.BlockSpec((cw, d), wrow(1)), pl.BlockSpec((cw, d), wrow(2)),
            pl.BlockSpec((1, cw), col(0)), pl.BlockSpec((1, cw), col(1)), pl.BlockSpec((1, cw), col(2)),
            pl.BlockSpec((A_WIDTH, cw), lambda i, c: (0, c)),
            pl.BlockSpec((B_WIDTH, cw), lambda i, c: (0, c)),
            pl.BlockSpec((C_WIDTH, cw), lambda i, c: (0, c)),
            pl.BlockSpec((cw, d), lambda i, c: (c, 0)),
        ],
        out_specs=pl.BlockSpec((tm, d), row),
        out_shape=jax.ShapeDtypeStruct((n, d), F32),
        scratch_shapes=[pltpu.VMEM((tm, d), F32)],
        compiler_params=_cparams(("parallel", "arbitrary")),
        name="merge_out",
    )(x2d, h, oa, ob, oc, wmix, wmix, wmix, gate_bias, gate_bias, gate_bias, wua, wub, wuc, wo)


def _w_in_groups(d):
    sizes = (A_WIDTH, A_KV_RANK, IDX_HEADS * IDX_DIM, IDX_DIM, IDX_HEADS, A_WIDTH,
             B_WIDTH, B_KV_WIDTH, B_KV_WIDTH, B_WIDTH, C_WIDTH, C_WIDTH, N_BRANCH * d)
    cuts = np.cumsum((0,) + sizes).tolist()
    (aq, ackv, iq, ik, iw, ag, bq, bk, bv, bg, cq, cg, mix) = [
        (cuts[i], cuts[i + 1]) for i in range(len(sizes))]
    return [
        [(aq[0], ackv[1])],
        [(iq[0], ik[1]), LANES - IDX_DIM, iw, LANES - IDX_HEADS],
        [ag, bg, cg],
        [(bq[0], bv[1]), cq,
         (bk[0] + B_HEAD_DIM, bk[1]), (bk[0], bk[0] + B_HEAD_DIM),
         (bv[0] + B_HEAD_DIM, bv[1]), (bv[0], bv[0] + B_HEAD_DIM)],
        [mix],
    ]


def _group_width(group):
    return sum(p if isinstance(p, int) else p[1] - p[0] for p in group)


def _regroup_kernel(w_ref, *out_refs, groups):
    for o_ref, group in zip(out_refs, groups):
        row = 0
        for part in group:
            if isinstance(part, int):
                o_ref[row:row + part, :] = jnp.zeros((part, o_ref.shape[1]), BF16)
                row += part
            else:
                o_ref[row:row + part[1] - part[0], :] = w_ref[part[0]:part[1], :].astype(BF16)
                row += part[1] - part[0]


def _regroup_w_in(w_in_t):
    cols, d = w_in_t.shape
    groups = _w_in_groups(d)
    lanes = REGROUP_LANES
    return pl.pallas_call(
        functools.partial(_regroup_kernel, groups=groups),
        grid=(d // lanes,),
        in_specs=[pl.BlockSpec((cols, lanes), lambda i: (0, i))],
        out_specs=[pl.BlockSpec((_group_width(g), lanes), lambda i: (0, i)) for g in groups],
        out_shape=[jax.ShapeDtypeStruct((_group_width(g), d), BF16) for g in groups],
        compiler_params=_cparams(("parallel",)),
        name="regroup_w_in",
    )(w_in_t)


def _layer(x, mem, norm_g, w_in, kv_norm_g, w_kv_up, idx_k_ln_g, idx_k_ln_b, q_norm_a, k_norm_a,
           q_norm_b, k_norm_b, sinks_b, mem_norm_g, w_mem_kv, q_norm_c, k_norm_c,
           w_up_a, w_up_b, w_up_c, gate_bias, w_o, rel_bias):
    bsz, seq, d = x.shape
    mlen = mem.shape[1]
    n = bsz * seq
    tm = ROW_TILE
    topk = min(TOPK_MAX, seq // 4)
    x2d = x.reshape(n, d)
    row2 = lambda v: v.reshape(1, -1)

    w_grp_a, w_grp_i, w_grp_g, w_grp_bc, w_mix = _regroup_w_in(jnp.swapaxes(w_in, 0, 1))
    g_x = row2(norm_g)

    qa, ka, va, h = _proj_call(
        _proj_a_kernel, "proj_a", x2d, tm,
        [g_x, w_grp_a, row2(kv_norm_g), w_kv_up.astype(BF16), row2(q_norm_a), row2(k_norm_a)],
        [jax.ShapeDtypeStruct((n, A_WIDTH), BF16)] * 3 + [jax.ShapeDtypeStruct((n, d), BF16)],
        [_row_spec(tm, A_WIDTH)] * 3 + [_row_spec(tm, d)])

    per_b = seq // tm
    iq, ik, iw = _proj_call(
        _proj_i_kernel, "proj_i", h, tm,
        [w_grp_i, row2(idx_k_ln_g), row2(idx_k_ln_b)],
        [jax.ShapeDtypeStruct((bsz, IDX_PAIRS, seq, LANES), BF16),
         jax.ShapeDtypeStruct((n, IDX_DIM), BF16),
         jax.ShapeDtypeStruct((n, IDX_HEADS), F32)],
        [pl.BlockSpec((None, IDX_PAIRS, tm, LANES), lambda i: (i // per_b, 0, i % per_b, 0)),
         _row_spec(tm, IDX_DIM), _row_spec(tm, IDX_HEADS)])

    ga, gb, gc = _proj_call(
        _proj_g_kernel, "proj_g", h, tm, [w_grp_g],
        [jax.ShapeDtypeStruct((n, A_WIDTH), BF16), jax.ShapeDtypeStruct((n, B_WIDTH), BF16),
         jax.ShapeDtypeStruct((n, C_WIDTH), BF16)],
        [_row_spec(tm, A_WIDTH), _row_spec(tm, B_WIDTH), _row_spec(tm, C_WIDTH)])

    bc_widths = (B_WIDTH, B_KV_WIDTH, B_KV_WIDTH, C_WIDTH, B_KV_WIDTH, B_KV_WIDTH)
    qb, kb, vb, qc, kbs, vbs = _proj_call(
        _proj_bc_kernel, "proj_bc", h, tm,
        [w_grp_bc, row2(jnp.tile(q_norm_b, 2)), row2(jnp.tile(k_norm_b, 2)), row2(q_norm_c)],
        [jax.ShapeDtypeStruct((n, w), BF16) for w in bc_widths],
        [_row_spec(tm, w) for w in bc_widths])

    mrows = bsz * mlen
    kc, vc = _proj_call(
        _proj_mem_kernel, "proj_mem", mem.reshape(mrows, d), min(tm, mrows),
        [row2(mem_norm_g), w_mem_kv.astype(BF16), row2(k_norm_c)],
        [jax.ShapeDtypeStruct((mrows, C_WIDTH), BF16)] * 2,
        [_row_spec(min(tm, mrows), C_WIDTH)] * 2)

    T = DSA_T
    i_idx = np.arange(T)[:, None]
    dist_a = T + i_idx - np.arange(2 * T)[None, :]
    rb_a = _bias_tiles(rel_bias, _t5_bucket_np(dist_a), 0, A_HEADS, NUM_BUCKETS - 1, LOG2E)
    i_idx = np.arange(WINDOW)[:, None]
    dist_b = WINDOW + i_idx - np.arange(2 * WINDOW)[None, :]
    bucket_b = np.where((dist_b >= 0) & (dist_b < WINDOW), _t5_bucket_np(dist_b), -1).astype(np.int32)
    bias_b = _bias_tiles(rel_bias, bucket_b, A_HEADS, B_HEADS, None, 1.0)
    tiles_b = B_HEADS // B_KV_HEADS // 2
    slot_heads = [2 * (tiles_b * g + i) + e for g in range(B_KV_HEADS) for e in range(2) for i in range(tiles_b)]
    bias_b = bias_b[np.array(slot_heads)].reshape(2 * B_KV_HEADS, tiles_b * WINDOW, 2 * WINDOW)

    r3 = lambda v, w: v.reshape(bsz, seq, w)
    ikt = jnp.swapaxes(ik.reshape(bsz, seq, IDX_DIM), 1, 2)
    zeros = jnp.zeros_like(ikt)
    ikz = jnp.stack([jnp.concatenate([ikt, zeros], axis=1), jnp.concatenate([zeros, ikt], axis=1)], axis=1)
    oa = _dsa_call(iq, iw.reshape(bsz, seq, IDX_HEADS), ikz,
                   r3(qa, A_WIDTH), r3(ka, A_WIDTH),
                   r3(va, A_WIDTH), r3(ga, A_WIDTH), rb_a, topk)
    ob = _swa_call(sinks_b, r3(qb, B_WIDTH), r3(kb, B_KV_WIDTH), r3(vb, B_KV_WIDTH),
                   r3(kbs, B_KV_WIDTH), r3(vbs, B_KV_WIDTH), r3(gb, B_WIDTH), bias_b)
    oc = _mem_call(r3(qc, C_WIDTH), kc.reshape(bsz, mlen, C_WIDTH), vc.reshape(bsz, mlen, C_WIDTH),
                   r3(gc, C_WIDTH))

    out = _final_call(x2d, h, oa.reshape(n, A_WIDTH), ob.reshape(n, B_WIDTH), oc.reshape(n, C_WIDTH),
                      w_mix, row2(gate_bias), w_up_a.astype(BF16), w_up_b.astype(BF16),
                      w_up_c.astype(BF16), w_o.astype(BF16))
    return out.reshape(bsz, seq, d)


def kernel(x, mem, norm_g, w_in, kv_norm_g, w_kv_up, idx_k_ln_g, idx_k_ln_b, q_norm_a, k_norm_a, q_norm_b, k_norm_b, sinks_b, mem_norm_g, w_mem_kv, q_norm_c, k_norm_c, w_up_a, w_up_b, w_up_c, gate_bias, w_o, rel_bias):
    for l in range(norm_g.shape[0]):
        x = _layer(x, mem, norm_g[l], w_in[l], kv_norm_g[l], w_kv_up[l], idx_k_ln_g[l], idx_k_ln_b[l],
                   q_norm_a[l], k_norm_a[l], q_norm_b[l], k_norm_b[l], sinks_b[l], mem_norm_g[l],
                   w_mem_kv[l], q_norm_c[l], k_norm_c[l], w_up_a[l], w_up_b[l], w_up_c[l],
                   gate_bias[l], w_o[l], rel_bias)
    return x
```

```python
import functools
import math

import numpy as np
import jax
import jax.numpy as jnp
from jax import lax
from jax.experimental import pallas as pl
from jax.experimental.pallas import tpu as pltpu

F32 = jnp.float32
BF16 = jnp.bfloat16

EPS = 1e-6
A_HEADS, A_HEAD_DIM, A_KV_RANK = 6, 128, 256
IDX_HEADS, IDX_DIM, TOPK_MAX = 16, 64, 256
IDX_PAIRS = IDX_HEADS // 2
B_HEADS, B_KV_HEADS, B_HEAD_DIM, WINDOW = 12, 2, 64, 128
C_HEADS, C_HEAD_DIM = 4, 128
NUM_BUCKETS, MAX_DISTANCE = 32, 128
N_BRANCH = 3
A_WIDTH = A_HEADS * A_HEAD_DIM
B_WIDTH = B_HEADS * B_HEAD_DIM
B_KV_WIDTH = B_KV_HEADS * B_HEAD_DIM
C_WIDTH = C_HEADS * C_HEAD_DIM

LANES = 128
VMEM_LIMIT = 56 * 1024 * 1024
NEG_BIG = -1e30
DSA_VMEM_LIMIT = 60 * 1024 * 1024
BELOW_ALL = -3e38

ROW_TILE = 1024
DSA_T = 256
SWA_T = 512
FINAL_ROWS = 512
FINAL_NCHUNK = 4
COUNT_ROWS = 128
REGROUP_LANES = 256
SCORE_TILES = 8
COUNT_UNROLL = 4
FAR_TILES = 4
LOG2E = math.log2(math.e)


def _cparams(sem):
    return pltpu.CompilerParams(dimension_semantics=sem, vmem_limit_bytes=VMEM_LIMIT)


def _dot(a, b):
    return jnp.dot(a, b, preferred_element_type=F32)


def _dot_nt(a, b):
    return lax.dot_general(a, b, (((1,), (1,)), ((), ())), preferred_element_type=F32)


def _rms_rows(x, g):
    ms = jnp.mean(x * x, axis=-1, keepdims=True)
    return x * lax.rsqrt(ms + EPS) * g


def _headnorm128(seg, g):
    ms = jnp.mean(seg * seg, axis=-1, keepdims=True)
    return seg * lax.rsqrt(ms + EPS) * g


def _headnorm64_pair(seg, g2):
    sq = seg * seg
    low = lax.broadcasted_iota(jnp.int32, seg.shape, 1) < B_HEAD_DIM
    s_all = jnp.sum(sq, axis=-1, keepdims=True)
    s_lo = jnp.sum(jnp.where(low, sq, 0.0), axis=-1, keepdims=True)
    ms = jnp.where(low, s_lo, s_all - s_lo) * (1.0 / B_HEAD_DIM)
    return seg * lax.rsqrt(ms + EPS) * g2


def _silu(y):
    return y / (1.0 + jnp.exp(-y))


def _proj_a_kernel(x_ref, g_ref, w_ref, kvg_ref, wkv_ref, qg_ref, kg_ref, qa_ref, ka_ref, va_ref, h_ref):
    h = _rms_rows(x_ref[...], g_ref[...]).astype(BF16)
    h_ref[...] = h
    y = _dot_nt(h, w_ref[...])
    scale = A_HEAD_DIM ** -0.5 * LOG2E
    for hh in range(A_HEADS):
        sl = slice(hh * A_HEAD_DIM, (hh + 1) * A_HEAD_DIM)
        qa_ref[:, sl] = (_headnorm128(y[:, sl], qg_ref[...]) * scale).astype(BF16)
    ckv = _rms_rows(y[:, A_WIDTH:A_WIDTH + A_KV_RANK], kvg_ref[...]).astype(BF16)
    kv = _dot(ckv, wkv_ref[...])
    for hh in range(A_HEADS):
        sl = slice(hh * A_HEAD_DIM, (hh + 1) * A_HEAD_DIM)
        ka_ref[:, sl] = _headnorm128(kv[:, sl], kg_ref[...]).astype(BF16)
    va_ref[...] = kv[:, A_WIDTH:].astype(BF16)


def _proj_i_kernel(h_ref, w_ref, lng_ref, lnb_ref, iq_ref, ik_ref, iw_ref):
    y = _dot_nt(h_ref[...], w_ref[...])
    k0 = IDX_HEADS * IDX_DIM
    for p in range(IDX_PAIRS):
        iq_ref[p] = y[:, p * LANES:(p + 1) * LANES].astype(BF16)
    ik = y[:, k0:k0 + IDX_DIM]
    mu = jnp.mean(ik, axis=-1, keepdims=True)
    d = ik - mu
    var = jnp.mean(d * d, axis=-1, keepdims=True)
    ik_ref[...] = (d * lax.rsqrt(var + EPS) * lng_ref[...] + lnb_ref[...]).astype(BF16)
    w0 = k0 + LANES
    iw_ref[...] = y[:, w0:w0 + IDX_HEADS] * (IDX_HEADS ** -0.5) * (IDX_DIM ** -0.5)


def _proj_g_kernel(h_ref, w_ref, ga_ref, gb_ref, gc_ref):
    y = _silu(_dot_nt(h_ref[...], w_ref[...]))
    ga_ref[...] = y[:, :A_WIDTH].astype(BF16)
    gb_ref[...] = y[:, A_WIDTH:A_WIDTH + B_WIDTH].astype(BF16)
    gc_ref[...] = y[:, A_WIDTH + B_WIDTH:].astype(BF16)


def _proj_bc_kernel(h_ref, w_ref, qbg_ref, kbg_ref, qcg_ref,
                    qb_ref, kb_ref, vb_ref, qc_ref, kbs_ref, vbs_ref):
    y = _dot_nt(h_ref[...], w_ref[...])
    sb = B_HEAD_DIM ** -0.5
    for p in range(B_WIDTH // LANES):
        sl = slice(p * LANES, (p + 1) * LANES)
        qb_ref[:, sl] = (_headnorm64_pair(y[:, sl], qbg_ref[...]) * sb).astype(BF16)
    k0 = B_WIDTH
    kb_ref[...] = _headnorm64_pair(y[:, k0:k0 + B_KV_WIDTH], kbg_ref[...]).astype(BF16)
    vb_ref[...] = y[:, k0 + B_KV_WIDTH:k0 + 2 * B_KV_WIDTH].astype(BF16)
    s0 = k0 + 2 * B_KV_WIDTH + C_WIDTH
    kbs_ref[...] = _headnorm64_pair(y[:, s0:s0 + B_KV_WIDTH], kbg_ref[...]).astype(BF16)
    vbs_ref[...] = y[:, s0 + B_KV_WIDTH:s0 + 2 * B_KV_WIDTH].astype(BF16)
    c0 = k0 + 2 * B_KV_WIDTH
    sc = C_HEAD_DIM ** -0.5
    for hh in range(C_HEADS):
        sl = slice(hh * C_HEAD_DIM, (hh + 1) * C_HEAD_DIM)
        qc_ref[:, sl] = (_headnorm128(y[:, c0 + hh * C_HEAD_DIM:c0 + (hh + 1) * C_HEAD_DIM],
                                      qcg_ref[...]) * sc).astype(BF16)


def _proj_mem_kernel(x_ref, g_ref, w_ref, kg_ref, kc_ref, vc_ref):
    h = _rms_rows(x_ref[...], g_ref[...]).astype(BF16)
    y = _dot(h, w_ref[...])
    for hh in range(C_HEADS):
        sl = slice(hh * C_HEAD_DIM, (hh + 1) * C_HEAD_DIM)
        kc_ref[:, sl] = _headnorm128(y[:, sl], kg_ref[...]).astype(BF16)
    vc_ref[...] = y[:, C_WIDTH:].astype(BF16)


def _row_spec(tm, cols):
    return pl.BlockSpec((tm, cols), lambda i: (i, 0))


def _full_spec(shape):
    nd = len(shape)
    return pl.BlockSpec(shape, lambda i: (0,) * nd, pipeline_mode=pl.Buffered(1))


def _proj_call(kernel_fn, name, x2d, tm, consts, out_shapes, out_specs):
    n, d = x2d.shape
    in_specs = [_row_spec(tm, d)] + [_full_spec(c.shape) for c in consts]
    return pl.pallas_call(
        kernel_fn,
        grid=(n // tm,),
        in_specs=in_specs,
        out_specs=out_specs,
        out_shape=out_shapes,
        compiler_params=_cparams(("parallel",)),
        name=name,
    )(x2d, *consts)


def _t5_bucket_np(dist):
    n = np.maximum(dist, 0)
    max_exact = NUM_BUCKETS // 2
    nf = np.maximum(n, 1).astype(np.float32)
    large = max_exact + (np.log(nf / max_exact) / math.log(MAX_DISTANCE / max_exact)
                         * (NUM_BUCKETS - max_exact)).astype(np.int32)
    large = np.minimum(large, NUM_BUCKETS - 1)
    return np.where(n < max_exact, n, large).astype(np.int32)


def _bias_tile_kernel(tab_ref, bucket_ref, o_ref, *, head0, shift_bucket, scale):
    h = pl.program_id(0) + head0
    bucket = bucket_ref[...]
    acc = jnp.zeros(bucket.shape, F32)
    for b in range(NUM_BUCKETS):
        acc = jnp.where(bucket == b, tab_ref[b, h], acc)
    if shift_bucket is not None:
        acc = acc - tab_ref[shift_bucket, h]
    o_ref[...] = jnp.where(bucket < 0, NEG_BIG, acc * scale)


def _bias_tiles(rel_bias, bucket_np, head0, nheads, shift_bucket, scale):
    r, c = bucket_np.shape
    return pl.pallas_call(
        functools.partial(_bias_tile_kernel, head0=head0, shift_bucket=shift_bucket, scale=scale),
        grid=(nheads,),
        in_specs=[pl.BlockSpec(memory_space=pltpu.SMEM), pl.BlockSpec((r, c), lambda h: (0, 0))],
        out_specs=pl.BlockSpec((None, r, c), lambda h: (h, 0, 0)),
        out_shape=jax.ShapeDtypeStruct((nheads, r, c), F32),
        compiler_params=_cparams(("arbitrary",)),
        name="bias_tiles",
    )(rel_bias, jnp.asarray(bucket_np))


def _dsa_kernel(iq_ref, iw_ref, ikz_ref, q_ref, k_ref, v_ref, g_ref, rb_ref, o_ref,
                sc_ref, scb_ref, wb_ref, lo_ref, hi_ref, mid_ref, clo_ref, s_ref, m_ref, l_ref, acc_ref, *, topk):
    T = DSA_T
    qi = pl.program_id(1)
    kf = float(topk)

    def lanes(col):
        return jnp.broadcast_to(col, (T, LANES))

    def halves(tile):
        return [tile[:, c * LANES:(c + 1) * LANES] for c in range(T // LANES)]

    for hh in range(IDX_HEADS):
        wb_ref[hh] = lanes(iw_ref[:, hh:hh + 1])

    def score_chunk(j, diag):
        keys = pl.ds(pl.multiple_of(j * T, T), T)
        accs = [jnp.zeros((T, LANES), F32) for _ in range(T // LANES)]
        for hh in range(IDX_HEADS):
            s = _dot(iq_ref[hh // 2], ikz_ref[hh % 2, :, keys])
            w = wb_ref[hh]
            accs = [a + w * jnp.maximum(sh, 0.0) for a, sh in zip(accs, halves(s))]
        if diag:
            row = lax.broadcasted_iota(jnp.int32, (T, LANES), 0)
            col = lax.broadcasted_iota(jnp.int32, (T, LANES), 1)
            causal = [col + c * LANES <= row for c in range(T // LANES)]
            accs = [jnp.where(cm, a, -jnp.inf) for cm, a in zip(causal, accs)]
        for c, a in enumerate(accs):
            sc_ref[j, :, c * LANES:(c + 1) * LANES] = a
            scb_ref[j, :, c * LANES:(c + 1) * LANES] = a.astype(BF16)
        return accs

    def p1_body(j, mxs):
        return [jnp.maximum(a, b) for a, b in zip(mxs, score_chunk(j, False))]

    carry = [jnp.full((T, LANES), -jnp.inf, F32) for _ in range(T // LANES)]
    done_tiles = 0
    width = SCORE_TILES
    while width >= 1:
        trips = lax.div(qi - done_tiles, width)

        def p1_group(g, c, width=width, base=done_tiles):
            for u in range(width):
                c = p1_body(base + g * width + u, c)
            return c

        carry = lax.fori_loop(0, trips, p1_group, carry)
        done_tiles = done_tiles + trips * width
        width //= 2
    mxs = [jnp.maximum(a, b) for a, b in zip(carry, score_chunk(qi, True))]
    mx = lanes(jnp.max(functools.reduce(jnp.maximum, mxs), axis=-1, keepdims=True))
    class_floor = lanes(jnp.min(functools.reduce(jnp.minimum, mxs), axis=-1, keepdims=True))

    def count_lanes(rows, v, strict=False):
        def tile(j, c):
            for c0 in range(0, T, LANES):
                sh = sc_ref[j, rows, c0:c0 + LANES]
                hit = (sh > v) if strict else (sh >= v)
                c = c + jnp.where(hit, 1.0, 0.0)
            return c

        def group(g, c):
            for u in range(COUNT_UNROLL):
                c = tile(g * COUNT_UNROLL + u, c)
            return c

        n_groups = lax.div(qi + 1, COUNT_UNROLL)
        c = lax.fori_loop(0, n_groups, group, jnp.zeros((COUNT_ROWS, LANES), F32))
        return lax.fori_loop(n_groups * COUNT_UNROLL, qi + 1, tile, c)

    def row_total(c):
        return jnp.broadcast_to(jnp.sum(c, axis=-1, keepdims=True), c.shape)

    slabs = [slice(r0, r0 + COUNT_ROWS) for r0 in range(0, T, COUNT_ROWS)]
    n_causal = (qi * T + 1 + lax.broadcasted_iota(jnp.int32, (T, LANES), 0)).astype(F32)
    keep_all = n_causal <= kf
    lo0 = jnp.where(keep_all, BELOW_ALL, class_floor)
    above_max = mx + jnp.maximum(jnp.abs(mx) * (2.0 ** -20), 1e-30)
    hi0 = jnp.where(keep_all, BELOW_ALL, above_max)
    def rnd(v):
        return v.astype(BF16).astype(F32)

    def count_lanes_bf16(rows, v):
        vb = v.astype(BF16)
        one, zero = jnp.ones((), BF16), jnp.zeros((), BF16)

        def tile(j, c):
            for c0 in range(0, T, LANES):
                c = c + jnp.where(scb_ref[j, rows, c0:c0 + LANES] >= vb, one, zero)
            return c

        def group(g, c):
            for u in range(COUNT_UNROLL):
                c = tile(g * COUNT_UNROLL + u, c)
            return c

        n_groups = lax.div(qi + 1, COUNT_UNROLL)
        c = lax.fori_loop(0, n_groups, group, jnp.zeros((COUNT_ROWS, LANES), BF16))
        return lax.fori_loop(n_groups * COUNT_UNROLL, qi + 1, tile, c)

    lo_c = jnp.where(keep_all, BELOW_ALL, rnd(lo0 - jnp.abs(lo0) * (2.0 ** -7) - 1e-30))
    hi_c = jnp.where(keep_all, BELOW_ALL, rnd(hi0 + jnp.abs(hi0) * (2.0 ** -6) + 1e-30))
    mid_c = rnd(lo_c + 0.5 * (hi_c - lo_c))
    mid_c = jnp.where(jnp.logical_or(mid_c <= lo_c, mid_c >= hi_c), lo_c, mid_c)
    lo_ref[...] = lo_c
    hi_ref[...] = hi_c
    mid_ref[...] = mid_c

    def coarse_body(open_widths):
        idle = jnp.zeros((COUNT_ROWS, LANES), BF16)
        partial = [lax.cond(w > 0.0, lambda rows=rows: count_lanes_bf16(rows, mid_ref[rows]), lambda: idle)
                   for rows, w in zip(slabs, open_widths)]
        new_widths = tuple(jnp.max(mid_ref[rows] - lo_ref[rows]) for rows in slabs)
        for rows, c in zip(slabs, partial):
            lo, hi, mid = lo_ref[rows], hi_ref[rows], mid_ref[rows]
            stopped = mid <= lo
            cnt = row_total(c.astype(F32))
            up = cnt >= kf
            lo = jnp.where(jnp.logical_or(stopped, jnp.logical_not(up)), lo, mid)
            hi = jnp.where(jnp.logical_or(stopped, up), hi, mid)
            nxt = rnd(lo + 0.5 * (hi - lo))
            stop = jnp.logical_or(jnp.logical_or(stopped, cnt == kf), jnp.logical_or(nxt <= lo, nxt >= hi))
            lo_ref[rows] = lo
            hi_ref[rows] = hi
            mid_ref[rows] = jnp.where(stop, lo, nxt)
        return new_widths

    lax.while_loop(lambda ws: functools.reduce(jnp.maximum, ws) > 0.0, coarse_body,
                   tuple(jnp.max((mid_c - lo_c)[rows]) for rows in slabs))

    lo_c, hi_c = lo_ref[...], hi_ref[...]
    lo0 = jnp.where(keep_all, BELOW_ALL, jnp.maximum(lo0, lo_c - jnp.abs(lo_c) * (2.0 ** -7) - 1e-30))
    hi0 = jnp.where(keep_all, BELOW_ALL, jnp.minimum(hi0, hi_c))
    lo_ref[...] = lo0
    hi_ref[...] = hi0
    mid_ref[...] = lo0 + 0.5 * (hi0 - lo0)
    clo_ref[...] = n_causal

    def bis_body(open_widths):
        idle = jnp.zeros((COUNT_ROWS, LANES), F32)
        partial = [lax.cond(w > 0.0, lambda rows=rows: count_lanes(rows, mid_ref[rows]), lambda: idle)
                   for rows, w in zip(slabs, open_widths)]
        new_widths = tuple(jnp.max(hi_ref[rows] - lo_ref[rows]) for rows in slabs)
        for rows, c in zip(slabs, partial):
            lo, hi, mid = lo_ref[rows], hi_ref[rows], mid_ref[rows]
            cnt = row_total(c)
            up = cnt >= kf
            lo = jnp.where(up, mid, lo)
            c_lo = jnp.where(up, cnt, clo_ref[rows])
            hi = jnp.where(up, hi, mid)
            nxt = lo + 0.5 * (hi - lo)
            stop = jnp.logical_or(c_lo == kf, jnp.logical_or(nxt <= lo, nxt >= hi))
            hi = jnp.where(stop, lo, hi)
            lo_ref[rows] = lo
            hi_ref[rows] = hi
            mid_ref[rows] = jnp.where(stop, lo, nxt)
            clo_ref[rows] = c_lo
        return new_widths

    lax.while_loop(lambda ws: functools.reduce(jnp.maximum, ws) > 0.0, bis_body,
                   tuple(jnp.max((hi0 - lo0)[rows]) for rows in slabs))
    thr = lo_ref[...]

    tied_f = jnp.where(clo_ref[...] > kf, 1.0, 0.0)

    @pl.when(jnp.max(tied_f) > 0.0)
    def _():
        above = jnp.concatenate(
            [row_total(count_lanes(rows, thr[rows], strict=True)) for rows in slabs], axis=0)
        need = (kf - above)[:, :1]
        r = lax.broadcasted_iota(jnp.int32, (T, T), 0)
        c = lax.broadcasted_iota(jnp.int32, (T, T), 1)
        before = jnp.where(r < c, 1.0, 0.0).astype(BF16)
        thr_col = thr[:, :1]
        tied_col = tied_f[:, :1] > 0.0

        def body(j, seen):
            s = sc_ref[j]
            eq = jnp.where(s == thr_col, 1.0, 0.0)
            rank = seen + _dot(eq.astype(BF16), before)
            drop = jnp.logical_and(tied_col, jnp.logical_and(eq > 0.0, rank >= need))
            sc_ref[j] = jnp.where(drop, -jnp.inf, s)
            return seen + jnp.sum(eq, axis=-1, keepdims=True)

        lax.fori_loop(0, qi + 1, body, jnp.zeros((T, 1), F32))

    m_ref[...] = jnp.full(m_ref.shape, NEG_BIG, F32)
    l_ref[...] = jnp.zeros(l_ref.shape, F32)
    acc_ref[...] = jnp.zeros(acc_ref.shape, F32)

    def attend(j0, ntiles, band):
        sel = [jnp.where(sh >= thr, 0.0, NEG_BIG) for t in range(ntiles) for sh in halves(sc_ref[j0 + t])]
        start = pl.multiple_of(j0 * T, T)
        keys = pl.ds(start, ntiles * T)
        width = ntiles * T

        def logits_to_scratch(hh):
            sl = slice(hh * A_HEAD_DIM, (hh + 1) * A_HEAD_DIM)
            s_ref[hh % 2, :, :width] = _dot_nt(q_ref[:, sl], k_ref[keys, sl])

        logits_to_scratch(0)
        pending = None
        for hh in range(A_HEADS):
            sl = slice(hh * A_HEAD_DIM, (hh + 1) * A_HEAD_DIM)
            if hh + 1 < A_HEADS:
                logits_to_scratch(hh + 1)
            s = s_ref[hh % 2, :, :width]
            if band is not None:
                s = s + rb_ref[hh, :, band * T:(band + ntiles) * T]
            sh = [s[:, c * LANES:(c + 1) * LANES] + m for c, m in enumerate(sel)]
            m_prev = m_ref[hh]
            m_cur = jnp.max(functools.reduce(jnp.maximum, sh), axis=-1, keepdims=True)
            m_new = jnp.maximum(m_prev, lanes(m_cur))
            alpha = jnp.exp2(m_prev - m_new)
            ph = [jnp.exp2(x - m_new) for x in sh]
            l_ref[hh] = alpha * l_ref[hh] + functools.reduce(jnp.add, ph)
            p = jnp.concatenate([x.astype(BF16) for x in ph], axis=-1)
            acc_ref[hh] = alpha * acc_ref[hh]
            m_ref[hh] = m_new
            if pending is not None:
                acc_ref[pending[0]] += pending[1]
            pending = (hh, _dot(p, v_ref[keys, sl]))
        acc_ref[pending[0]] += pending[1]

    n_far = jnp.maximum(qi - 1, 0)
    done_tiles = 0
    width = FAR_TILES
    while width >= 1:
        trips = lax.div(n_far - done_tiles, width)

        def far_body(i, c, width=width, base=done_tiles):
            attend(base + i * width, width, None)
            return c

        lax.fori_loop(0, trips, far_body, 0)
        done_tiles = done_tiles + trips * width
        width //= 2

    @pl.when(qi > 0)
    def _():
        attend(qi - 1, 2, 0)

    @pl.when(qi == 0)
    def _():
        attend(0, 1, 1)

    for hh in range(A_HEADS):
        sl = slice(hh * A_HEAD_DIM, (hh + 1) * A_HEAD_DIM)
        den = jnp.sum(l_ref[hh], axis=-1, keepdims=True)
        o_ref[:, sl] = (acc_ref[hh] / den * g_ref[:, sl].astype(F32)).astype(BF16)


def _dsa_call(iq, iw, ikz, qa, ka, va, ga, rb, topk):
    bsz, seq, _ = qa.shape
    T = DSA_T
    nq = seq // T
    assert topk <= T and seq % T == 0
    once = pl.Buffered(1)
    return pl.pallas_call(
        functools.partial(_dsa_kernel, topk=topk),
        grid=(bsz, nq),
        in_specs=[
            pl.BlockSpec((None, IDX_PAIRS, T, LANES), lambda b, i: (b, 0, i, 0)),
            pl.BlockSpec((None, T, IDX_HEADS), lambda b, i: (b, i, 0)),
            pl.BlockSpec((None, 2, LANES, seq), lambda b, i: (b, 0, 0, 0), pipeline_mode=once),
            pl.BlockSpec((None, T, A_WIDTH), lambda b, i: (b, i, 0)),
            pl.BlockSpec((None, seq, A_WIDTH), lambda b, i: (b, 0, 0), pipeline_mode=once),
            pl.BlockSpec((None, seq, A_WIDTH), lambda b, i: (b, 0, 0), pipeline_mode=once),
            pl.BlockSpec((None, T, A_WIDTH), lambda b, i: (b, i, 0)),
            pl.BlockSpec((A_HEADS, T, 2 * T), lambda b, i: (0, 0, 0), pipeline_mode=once),
        ],
        out_specs=pl.BlockSpec((None, T, A_WIDTH), lambda b, i: (b, i, 0)),
        out_shape=jax.ShapeDtypeStruct((bsz, seq, A_WIDTH), BF16),
        scratch_shapes=[
            pltpu.VMEM((nq, T, T), F32),
            pltpu.VMEM((nq, T, T), BF16),
            pltpu.VMEM((IDX_HEADS, T, LANES), F32),
            pltpu.VMEM((T, LANES), F32),
            pltpu.VMEM((T, LANES), F32),
            pltpu.VMEM((T, LANES), F32),
            pltpu.VMEM((T, LANES), F32),
            pltpu.VMEM((2, T, FAR_TILES * T), F32),
            pltpu.VMEM((A_HEADS, T, LANES), F32),
            pltpu.VMEM((A_HEADS, T, LANES), F32),
            pltpu.VMEM((A_HEADS, T, A_HEAD_DIM), F32),
        ],
        compiler_params=pltpu.CompilerParams(dimension_semantics=("arbitrary", "arbitrary"),
                                             vmem_limit_bytes=DSA_VMEM_LIMIT),
        name="dsa",
    )(iq, iw, ikz, qa, ka, va, ga, rb)


def _swa_kernel(sink_ref, q_ref, kc_ref, vc_ref, kcs_ref, vcs_ref, kp_ref, vp_ref, kps_ref, vps_ref,
                g_ref, bias_ref, o_ref):
    first = pl.program_id(1) == 0
    blk = WINDOW
    tiles = B_HEADS // B_KV_HEADS // 2
    low_half = lax.broadcasted_iota(jnp.int32, (2 * blk, LANES), 1) < B_HEAD_DIM
    ones = jnp.ones((2 * blk, LANES), BF16)
    for sb in range(SWA_T // blk):
        rows = slice(sb * blk, (sb + 1) * blk)
        prev = slice((sb - 1) * blk, sb * blk)

        def window(cur_ref, prev_ref):
            before = prev_ref[...] if sb == 0 else cur_ref[prev, :]
            return jnp.concatenate([before, cur_ref[rows, :]], axis=0)

        k_nat, v_nat = window(kc_ref, kp_ref), window(vc_ref, vp_ref)
        k_swp, v_swp = window(kcs_ref, kps_ref), window(vcs_ref, vps_ref)
        for g in range(B_KV_HEADS):
            q = jnp.concatenate(
                [q_ref[rows, (tiles * g + i) * LANES:(tiles * g + i + 1) * LANES] for i in range(tiles)], axis=0)
            res = None
            for e in range(2):
                keep = low_half if e == 0 else jnp.logical_not(low_half)
                k_src, v_src = (k_nat, v_nat) if g == e else (k_swp, v_swp)
                kk = jnp.where(keep, k_src, jnp.zeros_like(k_src))
                vv = jnp.where(keep, v_src, jnp.zeros_like(v_src))
                logits = _dot_nt(q, kk) + bias_ref[2 * g + e]
                l_prev, l_cur = logits[:, :blk], logits[:, blk:]
                if sb == 0:
                    l_prev = jnp.where(first, NEG_BIG, l_prev)
                sink = jnp.concatenate(
                    [jnp.full((blk, blk), sink_ref[2 * (tiles * g + i) + e], F32) for i in range(tiles)], axis=0)
                m = jnp.max(jnp.maximum(l_prev, l_cur), axis=-1, keepdims=True)
                m = jnp.maximum(jnp.broadcast_to(m, sink.shape), sink)
                p = jnp.concatenate([jnp.exp(l_prev - m).astype(BF16), jnp.exp(l_cur - m).astype(BF16)], axis=1)
                pv = _dot(p, jnp.concatenate([vv, ones], axis=1))
                part = pv[:, :LANES] / (pv[:, LANES:] + jnp.exp(sink - m))
                res = part if res is None else res + part
            for i in range(tiles):
                cols = slice((tiles * g + i) * LANES, (tiles * g + i + 1) * LANES)
                o_ref[rows, cols] = (res[i * blk:(i + 1) * blk] * g_ref[rows, cols].astype(F32)).astype(BF16)


def _swa_call(sinks, qb, kb, vb, kbs, vbs, gb, bias):
    bsz, seq, _ = qb.shape
    T = SWA_T
    per = T // WINDOW
    cur = lambda b, i: (b, i, 0)
    prev = lambda b, i: (b, jnp.maximum(i * per - 1, 0), 0)
    kv_cur = pl.BlockSpec((None, T, B_KV_WIDTH), cur)
    kv_prev = pl.BlockSpec((None, WINDOW, B_KV_WIDTH), prev)
    return pl.pallas_call(
        _swa_kernel,
        grid=(bsz, seq // T),
        in_specs=[
            pl.BlockSpec(memory_space=pltpu.SMEM),
            pl.BlockSpec((None, T, B_WIDTH), cur),
            kv_cur, kv_cur, kv_cur, kv_cur,
            kv_prev, kv_prev, kv_prev, kv_prev,
            pl.BlockSpec((None, T, B_WIDTH), cur),
            pl.BlockSpec(bias.shape, lambda b, i: (0, 0, 0)),
        ],
        out_specs=pl.BlockSpec((None, T, B_WIDTH), cur),
        out_shape=jax.ShapeDtypeStruct((bsz, seq, B_WIDTH), BF16),
        compiler_params=_cparams(("parallel", "parallel")),
        name="swa",
    )(sinks, qb, kb, vb, kbs, vbs, kb, vb, kbs, vbs, gb, bias)


def _mem_kernel(q_ref, k_ref, v_ref, g_ref, o_ref):
    for hh in range(C_HEADS):
        sl = slice(hh * C_HEAD_DIM, (hh + 1) * C_HEAD_DIM)
        s = _dot_nt(q_ref[:, sl], k_ref[:, sl])
        m = jnp.max(s, axis=-1, keepdims=True)
        p = jnp.exp(s - m)
        den = jnp.sum(p, axis=-1, keepdims=True)
        o = _dot(p.astype(BF16), v_ref[:, sl]) / den
        o_ref[:, sl] = (o * g_ref[:, sl].astype(F32)).astype(BF16)


def _mem_call(qc, kc, vc, gc):
    bsz, seq, _ = qc.shape
    mlen = kc.shape[1]
    T = SWA_T
    cur = lambda b, i: (b, i, 0)
    whole = lambda b, i: (b, 0, 0)
    return pl.pallas_call(
        _mem_kernel,
        grid=(bsz, seq // T),
        in_specs=[
            pl.BlockSpec((None, T, C_WIDTH), cur),
            pl.BlockSpec((None, mlen, C_WIDTH), whole),
            pl.BlockSpec((None, mlen, C_WIDTH), whole),
            pl.BlockSpec((None, T, C_WIDTH), cur),
        ],
        out_specs=pl.BlockSpec((None, T, C_WIDTH), cur),
        out_shape=jax.ShapeDtypeStruct((bsz, seq, C_WIDTH), BF16),
        compiler_params=_cparams(("parallel", "parallel")),
        name="mem_attn",
    )(qc, kc, vc, gc)


def _final_kernel(x_ref, h_ref, oa_ref, ob_ref, oc_ref, wma_ref, wmb_ref, wmc_ref,
                  ba_ref, bb_ref, bc_ref, wua_ref, wub_ref, wuc_ref, wo_ref, out_ref, acc_ref):
    c = pl.program_id(1)

    @pl.when(c == 0)
    def _():
        acc_ref[...] = jnp.zeros(acc_ref.shape, F32)

    h = h_ref[...]

    def branch(o_ref, wm_ref, b_ref, wu_ref):
        gate = 1.0 / (1.0 + jnp.exp(-(_dot_nt(h, wm_ref[...]) + b_ref[...])))
        return gate * _dot(o_ref[...], wu_ref[...])

    merged = (branch(oa_ref, wma_ref, ba_ref, wua_ref) + branch(ob_ref, wmb_ref, bb_ref, wub_ref)
              + branch(oc_ref, wmc_ref, bc_ref, wuc_ref))
    acc_ref[...] += _dot(merged.astype(BF16), wo_ref[...])

    @pl.when(c == pl.num_programs(1) - 1)
    def _():
        out_ref[...] = x_ref[...] + acc_ref[...]


def _final_call(x2d, h, oa, ob, oc, wmix, gate_bias, wua, wub, wuc, wo):
    n, d = x2d.shape
    tm = FINAL_ROWS
    nch = FINAL_NCHUNK
    cw = d // nch
    row = lambda i, c: (i, 0)

    def col(br):
        return lambda i, c: (0, br * nch + c)

    def wrow(br):
        return lambda i, c: (br * nch + c, 0)

    return pl.pallas_call(
        _final_kernel,
        grid=(n // tm, nch),
        in_specs=[
            pl.BlockSpec((tm, d), row),
            pl.BlockSpec((tm, d), row),
            pl.BlockSpec((tm, A_WIDTH), row),
            pl.BlockSpec((tm, B_WIDTH), row),
            pl.BlockSpec((tm, C_WIDTH), row),
            pl.BlockSpec((cw, d), wrow(0)), pl.BlockSpec((cw, d), wrow(1)), pl.BlockSpec((cw, d), wrow(2)),
            pl.BlockSpec((1, cw), col(0)), pl.BlockSpec((1, cw), col(1)), pl.BlockSpec((1, cw), col(2)),
            pl.BlockSpec((A_WIDTH, cw), lambda i, c: (0, c)),
            pl.BlockSpec((B_WIDTH, cw), lambda i, c: (0, c)),
            pl.BlockSpec((C_WIDTH, cw), lambda i, c: (0, c)),
            pl.BlockSpec((cw, d), lambda i, c: (c, 0)),
        ],
        out_specs=pl.BlockSpec((tm, d), row),
        out_shape=jax.ShapeDtypeStruct((n, d), F32),
        scratch_shapes=[pltpu.VMEM((tm, d), F32)],
        compiler_params=_cparams(("parallel", "arbitrary")),
        name="merge_out",
    )(x2d, h, oa, ob, oc, wmix, wmix, wmix, gate_bias, gate_bias, gate_bias, wua, wub, wuc, wo)


def _w_in_groups(d):
    sizes = (A_WIDTH, A_KV_RANK, IDX_HEADS * IDX_DIM, IDX_DIM, IDX_HEADS, A_WIDTH,
             B_WIDTH, B_KV_WIDTH, B_KV_WIDTH, B_WIDTH, C_WIDTH, C_WIDTH, N_BRANCH * d)
    cuts = np.cumsum((0,) + sizes).tolist()
    (aq, ackv, iq, ik, iw, ag, bq, bk, bv, bg, cq, cg, mix) = [
        (cuts[i], cuts[i + 1]) for i in range(len(sizes))]
    return [
        [(aq[0], ackv[1])],
        [(iq[0], ik[1]), LANES - IDX_DIM, iw, LANES - IDX_HEADS],
        [ag, bg, cg],
        [(bq[0], bv[1]), cq,
         (bk[0] + B_HEAD_DIM, bk[1]), (bk[0], bk[0] + B_HEAD_DIM),
         (bv[0] + B_HEAD_DIM, bv[1]), (bv[0], bv[0] + B_HEAD_DIM)],
        [mix],
    ]


def _group_width(group):
    return sum(p if isinstance(p, int) else p[1] - p[0] for p in group)


def _regroup_kernel(w_ref, *out_refs, groups):
    for o_ref, group in zip(out_refs, groups):
        row = 0
        for part in group:
            if isinstance(part, int):
                o_ref[row:row + part, :] = jnp.zeros((part, o_ref.shape[1]), BF16)
                row += part
            else:
                o_ref[row:row + part[1] - part[0], :] = w_ref[part[0]:part[1], :].astype(BF16)
                row += part[1] - part[0]


def _regroup_w_in(w_in_t):
    cols, d = w_in_t.shape
    groups = _w_in_groups(d)
    lanes = REGROUP_LANES
    return pl.pallas_call(
        functools.partial(_regroup_kernel, groups=groups),
        grid=(d // lanes,),
        in_specs=[pl.BlockSpec((cols, lanes), lambda i: (0, i))],
        out_specs=[pl.BlockSpec((_group_width(g), lanes), lambda i: (0, i)) for g in groups],
        out_shape=[jax.ShapeDtypeStruct((_group_width(g), d), BF16) for g in groups],
        compiler_params=_cparams(("parallel",)),
        name="regroup_w_in",
    )(w_in_t)


def _layer(x, mem, norm_g, w_in, kv_norm_g, w_kv_up, idx_k_ln_g, idx_k_ln_b, q_norm_a, k_norm_a,
           q_norm_b, k_norm_b, sinks_b, mem_norm_g, w_mem_kv, q_norm_c, k_norm_c,
           w_up_a, w_up_b, w_up_c, gate_bias, w_o, rel_bias):
    bsz, seq, d = x.shape
    mlen = mem.shape[1]
    n = bsz * seq
    tm = ROW_TILE
    topk = min(TOPK_MAX, seq // 4)
    x2d = x.reshape(n, d)
    row2 = lambda v: v.reshape(1, -1)

    w_grp_a, w_grp_i, w_grp_g, w_grp_bc, w_mix = _regroup_w_in(jnp.swapaxes(w_in, 0, 1))
    g_x = row2(norm_g)

    qa, ka, va, h = _proj_call(
        _proj_a_kernel, "proj_a", x2d, tm,
        [g_x, w_grp_a, row2(kv_norm_g), w_kv_up.astype(BF16), row2(q_norm_a), row2(k_norm_a)],
        [jax.ShapeDtypeStruct((n, A_WIDTH), BF16)] * 3 + [jax.ShapeDtypeStruct((n, d), BF16)],
        [_row_spec(tm, A_WIDTH)] * 3 + [_row_spec(tm, d)])

    per_b = seq // tm
    iq, ik, iw = _proj_call(
        _proj_i_kernel, "proj_i", h, tm,
        [w_grp_i, row2(idx_k_ln_g), row2(idx_k_ln_b)],
        [jax.ShapeDtypeStruct((bsz, IDX_PAIRS, seq, LANES), BF16),
         jax.ShapeDtypeStruct((n, IDX_DIM), BF16),
         jax.ShapeDtypeStruct((n, IDX_HEADS), F32)],
        [pl.BlockSpec((None, IDX_PAIRS, tm, LANES), lambda i: (i // per_b, 0, i % per_b, 0)),
         _row_spec(tm, IDX_DIM), _row_spec(tm, IDX_HEADS)])

    ga, gb, gc = _proj_call(
        _proj_g_kernel, "proj_g", h, tm, [w_grp_g],
        [jax.ShapeDtypeStruct((n, A_WIDTH), BF16), jax.ShapeDtypeStruct((n, B_WIDTH), BF16),
         jax.ShapeDtypeStruct((n, C_WIDTH), BF16)],
        [_row_spec(tm, A_WIDTH), _row_spec(tm, B_WIDTH), _row_spec(tm, C_WIDTH)])

    bc_widths = (B_WIDTH, B_KV_WIDTH, B_KV_WIDTH, C_WIDTH, B_KV_WIDTH, B_KV_WIDTH)
    qb, kb, vb, qc, kbs, vbs = _proj_call(
        _proj_bc_kernel, "proj_bc", h, tm,
        [w_grp_bc, row2(jnp.tile(q_norm_b, 2)), row2(jnp.tile(k_norm_b, 2)), row2(q_norm_c)],
        [jax.ShapeDtypeStruct((n, w), BF16) for w in bc_widths],
        [_row_spec(tm, w) for w in bc_widths])

    mrows = bsz * mlen
    kc, vc = _proj_call(
        _proj_mem_kernel, "proj_mem", mem.reshape(mrows, d), min(tm, mrows),
        [row2(mem_norm_g), w_mem_kv.astype(BF16), row2(k_norm_c)],
        [jax.ShapeDtypeStruct((mrows, C_WIDTH), BF16)] * 2,
        [_row_spec(min(tm, mrows), C_WIDTH)] * 2)

    T = DSA_T
    i_idx = np.arange(T)[:, None]
    dist_a = T + i_idx - np.arange(2 * T)[None, :]
    rb_a = _bias_tiles(rel_bias, _t5_bucket_np(dist_a), 0, A_HEADS, NUM_BUCKETS - 1, LOG2E)
    i_idx = np.arange(WINDOW)[:, None]
    dist_b = WINDOW + i_idx - np.arange(2 * WINDOW)[None, :]
    bucket_b = np.where((dist_b >= 0) & (dist_b < WINDOW), _t5_bucket_np(dist_b), -1).astype(np.int32)
    bias_b = _bias_tiles(rel_bias, bucket_b, A_HEADS, B_HEADS, None, 1.0)
    tiles_b = B_HEADS // B_KV_HEADS // 2
    slot_heads = [2 * (tiles_b * g + i) + e for g in range(B_KV_HEADS) for e in range(2) for i in range(tiles_b)]
    bias_b = bias_b[np.array(slot_heads)].reshape(2 * B_KV_HEADS, tiles_b * WINDOW, 2 * WINDOW)

    r3 = lambda v, w: v.reshape(bsz, seq, w)
    ikt = jnp.swapaxes(ik.reshape(bsz, seq, IDX_DIM), 1, 2)
    zeros = jnp.zeros_like(ikt)
    ikz = jnp.stack([jnp.concatenate([ikt, zeros], axis=1), jnp.concatenate([zeros, ikt], axis=1)], axis=1)
    oa = _dsa_call(iq, iw.reshape(bsz, seq, IDX_HEADS), ikz,
                   r3(qa, A_WIDTH), r3(ka, A_WIDTH),
                   r3(va, A_WIDTH), r3(ga, A_WIDTH), rb_a, topk)
    ob = _swa_call(sinks_b, r3(qb, B_WIDTH), r3(kb, B_KV_WIDTH), r3(vb, B_KV_WIDTH),
                   r3(kbs, B_KV_WIDTH), r3(vbs, B_KV_WIDTH), r3(gb, B_WIDTH), bias_b)
    oc = _mem_call(r3(qc, C_WIDTH), kc.reshape(bsz, mlen, C_WIDTH), vc.reshape(bsz, mlen, C_WIDTH),
                   r3(gc, C_WIDTH))

    out = _final_call(x2d, h, oa.reshape(n, A_WIDTH), ob.reshape(n, B_WIDTH), oc.reshape(n, C_WIDTH),
                      w_mix, row2(gate_bias), w_up_a.astype(BF16), w_up_b.astype(BF16),
                      w_up_c.astype(BF16), w_o.astype(BF16))
    return out.reshape(bsz, seq, d)


def kernel(x, mem, norm_g, w_in, kv_norm_g, w_kv_up, idx_k_ln_g, idx_k_ln_b, q_norm_a, k_norm_a, q_norm_b, k_norm_b, sinks_b, mem_norm_g, w_mem_kv, q_norm_c, k_norm_c, w_up_a, w_up_b, w_up_c, gate_bias, w_o, rel_bias):
    for l in range(norm_g.shape[0]):
        x = _layer(x, mem, norm_g[l], w_in[l], kv_norm_g[l], w_kv_up[l], idx_k_ln_g[l], idx_k_ln_b[l],
                   q_norm_a[l], k_norm_a[l], q_norm_b[l], k_norm_b[l], sinks_b[l], mem_norm_g[l],
                   w_mem_kv[l], q_norm_c[l], k_norm_c[l], w_up_a[l], w_up_b[l], w_up_c[l],
                   gate_bias[l], w_o[l], rel_bias)
    return x
```

```python
import functools
import math

import numpy as np
import jax
import jax.numpy as jnp
from jax import lax
from jax.experimental import pallas as pl
from jax.experimental.pallas import tpu as pltpu

F32 = jnp.float32
BF16 = jnp.bfloat16

EPS = 1e-6
A_HEADS, A_HEAD_DIM, A_KV_RANK = 6, 128, 256
IDX_HEADS, IDX_DIM, TOPK_MAX = 16, 64, 256
IDX_PAIRS = IDX_HEADS // 2
B_HEADS, B_KV_HEADS, B_HEAD_DIM, WINDOW = 12, 2, 64, 128
C_HEADS, C_HEAD_DIM = 4, 128
NUM_BUCKETS, MAX_DISTANCE = 32, 128
N_BRANCH = 3
A_WIDTH = A_HEADS * A_HEAD_DIM
B_WIDTH = B_HEADS * B_HEAD_DIM
B_KV_WIDTH = B_KV_HEADS * B_HEAD_DIM
C_WIDTH = C_HEADS * C_HEAD_DIM

LANES = 128
VMEM_LIMIT = 56 * 1024 * 1024
NEG_BIG = -1e30
BELOW_ALL = -3e38

ROW_TILE = 1024
DSA_T = 256
SWA_T = 512
FINAL_ROWS = 512
FINAL_NCHUNK = 4
COUNT_ROWS = 128
REGROUP_LANES = 256
SCORE_TILES = 8
COUNT_UNROLL = 4
FAR_TILES = 4
LOG2E = math.log2(math.e)


def _cparams(sem):
    return pltpu.CompilerParams(dimension_semantics=sem, vmem_limit_bytes=VMEM_LIMIT)


def _dot(a, b):
    return jnp.dot(a, b, preferred_element_type=F32)


def _dot_nt(a, b):
    return lax.dot_general(a, b, (((1,), (1,)), ((), ())), preferred_element_type=F32)


def _rms_rows(x, g):
    ms = jnp.mean(x * x, axis=-1, keepdims=True)
    return x * lax.rsqrt(ms + EPS) * g


def _headnorm128(seg, g):
    ms = jnp.mean(seg * seg, axis=-1, keepdims=True)
    return seg * lax.rsqrt(ms + EPS) * g


def _headnorm64_pair(seg, g2):
    sq = seg * seg
    low = lax.broadcasted_iota(jnp.int32, seg.shape, 1) < B_HEAD_DIM
    s_all = jnp.sum(sq, axis=-1, keepdims=True)
    s_lo = jnp.sum(jnp.where(low, sq, 0.0), axis=-1, keepdims=True)
    ms = jnp.where(low, s_lo, s_all - s_lo) * (1.0 / B_HEAD_DIM)
    return seg * lax.rsqrt(ms + EPS) * g2


def _silu(y):
    return y / (1.0 + jnp.exp(-y))


def _proj_a_kernel(x_ref, g_ref, w_ref, kvg_ref, wkv_ref, qg_ref, kg_ref, qa_ref, ka_ref, va_ref, h_ref):
    h = _rms_rows(x_ref[...], g_ref[...]).astype(BF16)
    h_ref[...] = h
    y = _dot_nt(h, w_ref[...])
    scale = A_HEAD_DIM ** -0.5 * LOG2E
    for hh in range(A_HEADS):
        sl = slice(hh * A_HEAD_DIM, (hh + 1) * A_HEAD_DIM)
        qa_ref[:, sl] = (_headnorm128(y[:, sl], qg_ref[...]) * scale).astype(BF16)
    ckv = _rms_rows(y[:, A_WIDTH:A_WIDTH + A_KV_RANK], kvg_ref[...]).astype(BF16)
    kv = _dot(ckv, wkv_ref[...])
    for hh in range(A_HEADS):
        sl = slice(hh * A_HEAD_DIM, (hh + 1) * A_HEAD_DIM)
        ka_ref[:, sl] = _headnorm128(kv[:, sl], kg_ref[...]).astype(BF16)
    va_ref[...] = kv[:, A_WIDTH:].astype(BF16)


def _proj_i_kernel(h_ref, w_ref, lng_ref, lnb_ref, iq_ref, ik_ref, iw_ref):
    y = _dot_nt(h_ref[...], w_ref[...])
    k0 = IDX_HEADS * IDX_DIM
    for p in range(IDX_PAIRS):
        iq_ref[p] = y[:, p * LANES:(p + 1) * LANES].astype(BF16)
    ik = y[:, k0:k0 + IDX_DIM]
    mu = jnp.mean(ik, axis=-1, keepdims=True)
    d = ik - mu
    var = jnp.mean(d * d, axis=-1, keepdims=True)
    ik_ref[...] = (d * lax.rsqrt(var + EPS) * lng_ref[...] + lnb_ref[...]).astype(BF16)
    w0 = k0 + LANES
    iw_ref[...] = y[:, w0:w0 + IDX_HEADS] * (IDX_HEADS ** -0.5) * (IDX_DIM ** -0.5)


def _proj_g_kernel(h_ref, w_ref, ga_ref, gb_ref, gc_ref):
    y = _silu(_dot_nt(h_ref[...], w_ref[...]))
    ga_ref[...] = y[:, :A_WIDTH].astype(BF16)
    gb_ref[...] = y[:, A_WIDTH:A_WIDTH + B_WIDTH].astype(BF16)
    gc_ref[...] = y[:, A_WIDTH + B_WIDTH:].astype(BF16)


def _proj_bc_kernel(h_ref, w_ref, qbg_ref, kbg_ref, qcg_ref,
                    qb_ref, kb_ref, vb_ref, qc_ref, kbs_ref, vbs_ref):
    y = _dot_nt(h_ref[...], w_ref[...])
    sb = B_HEAD_DIM ** -0.5
    for p in range(B_WIDTH // LANES):
        sl = slice(p * LANES, (p + 1) * LANES)
        qb_ref[:, sl] = (_headnorm64_pair(y[:, sl], qbg_ref[...]) * sb).astype(BF16)
    k0 = B_WIDTH
    kb_ref[...] = _headnorm64_pair(y[:, k0:k0 + B_KV_WIDTH], kbg_ref[...]).astype(BF16)
    vb_ref[...] = y[:, k0 + B_KV_WIDTH:k0 + 2 * B_KV_WIDTH].astype(BF16)
    s0 = k0 + 2 * B_KV_WIDTH + C_WIDTH
    kbs_ref[...] = _headnorm64_pair(y[:, s0:s0 + B_KV_WIDTH], kbg_ref[...]).astype(BF16)
    vbs_ref[...] = y[:, s0 + B_KV_WIDTH:s0 + 2 * B_KV_WIDTH].astype(BF16)
    c0 = k0 + 2 * B_KV_WIDTH
    sc = C_HEAD_DIM ** -0.5
    for hh in range(C_HEADS):
        sl = slice(hh * C_HEAD_DIM, (hh + 1) * C_HEAD_DIM)
        qc_ref[:, sl] = (_headnorm128(y[:, c0 + hh * C_HEAD_DIM:c0 + (hh + 1) * C_HEAD_DIM],
                                      qcg_ref[...]) * sc).astype(BF16)


def _proj_mem_kernel(x_ref, g_ref, w_ref, kg_ref, kc_ref, vc_ref):
    h = _rms_rows(x_ref[...], g_ref[...]).astype(BF16)
    y = _dot(h, w_ref[...])
    for hh in range(C_HEADS):
        sl = slice(hh * C_HEAD_DIM, (hh + 1) * C_HEAD_DIM)
        kc_ref[:, sl] = _headnorm128(y[:, sl], kg_ref[...]).astype(BF16)
    vc_ref[...] = y[:, C_WIDTH:].astype(BF16)


def _row_spec(tm, cols):
    return pl.BlockSpec((tm, cols), lambda i: (i, 0))


def _full_spec(shape):
    nd = len(shape)
    return pl.BlockSpec(shape, lambda i: (0,) * nd, pipeline_mode=pl.Buffered(1))


def _proj_call(kernel_fn, name, x2d, tm, consts, out_shapes, out_specs):
    n, d = x2d.shape
    in_specs = [_row_spec(tm, d)] + [_full_spec(c.shape) for c in consts]
    return pl.pallas_call(
        kernel_fn,
        grid=(n // tm,),
        in_specs=in_specs,
        out_specs=out_specs,
        out_shape=out_shapes,
        compiler_params=_cparams(("parallel",)),
        name=name,
    )(x2d, *consts)


def _t5_bucket_np(dist):
    n = np.maximum(dist, 0)
    max_exact = NUM_BUCKETS // 2
    nf = np.maximum(n, 1).astype(np.float32)
    large = max_exact + (np.log(nf / max_exact) / math.log(MAX_DISTANCE / max_exact)
                         * (NUM_BUCKETS - max_exact)).astype(np.int32)
    large = np.minimum(large, NUM_BUCKETS - 1)
    return np.where(n < max_exact, n, large).astype(np.int32)


def _bias_tile_kernel(tab_ref, bucket_ref, o_ref, *, head0, shift_bucket, scale):
    h = pl.program_id(0) + head0
    bucket = bucket_ref[...]
    acc = jnp.zeros(bucket.shape, F32)
    for b in range(NUM_BUCKETS):
        acc = jnp.where(bucket == b, tab_ref[b, h], acc)
    if shift_bucket is not None:
        acc = acc - tab_ref[shift_bucket, h]
    o_ref[...] = jnp.where(bucket < 0, NEG_BIG, acc * scale)


def _bias_tiles(rel_bias, bucket_np, head0, nheads, shift_bucket, scale):
    r, c = bucket_np.shape
    return pl.pallas_call(
        functools.partial(_bias_tile_kernel, head0=head0, shift_bucket=shift_bucket, scale=scale),
        grid=(nheads,),
        in_specs=[pl.BlockSpec(memory_space=pltpu.SMEM), pl.BlockSpec((r, c), lambda h: (0, 0))],
        out_specs=pl.BlockSpec((None, r, c), lambda h: (h, 0, 0)),
        out_shape=jax.ShapeDtypeStruct((nheads, r, c), F32),
        compiler_params=_cparams(("arbitrary",)),
        name="bias_tiles",
    )(rel_bias, jnp.asarray(bucket_np))


def _dsa_kernel(iq_ref, iw_ref, ikz_ref, q_ref, k_ref, v_ref, g_ref, rb_ref, o_ref,
                sc_ref, wb_ref, lo_ref, hi_ref, mid_ref, clo_ref, s_ref, m_ref, l_ref, acc_ref, *, topk):
    T = DSA_T
    qi = pl.program_id(1)
    kf = float(topk)

    def lanes(col):
        return jnp.broadcast_to(col, (T, LANES))

    def halves(tile):
        return [tile[:, c * LANES:(c + 1) * LANES] for c in range(T // LANES)]

    for hh in range(IDX_HEADS):
        wb_ref[hh] = lanes(iw_ref[:, hh:hh + 1])

    def score_chunk(j, diag):
        keys = pl.ds(pl.multiple_of(j * T, T), T)
        accs = [jnp.zeros((T, LANES), F32) for _ in range(T // LANES)]
        for hh in range(IDX_HEADS):
            s = _dot(iq_ref[hh // 2], ikz_ref[hh % 2, :, keys])
            w = wb_ref[hh]
            accs = [a + w * jnp.maximum(sh, 0.0) for a, sh in zip(accs, halves(s))]
        if diag:
            row = lax.broadcasted_iota(jnp.int32, (T, LANES), 0)
            col = lax.broadcasted_iota(jnp.int32, (T, LANES), 1)
            causal = [col + c * LANES <= row for c in range(T // LANES)]
            accs = [jnp.where(cm, a, -jnp.inf) for cm, a in zip(causal, accs)]
        for c, a in enumerate(accs):
            sc_ref[j, :, c * LANES:(c + 1) * LANES] = a
        return accs

    def p1_body(j, mxs):
        return [jnp.maximum(a, b) for a, b in zip(mxs, score_chunk(j, False))]

    carry = [jnp.full((T, LANES), -jnp.inf, F32) for _ in range(T // LANES)]
    done_tiles = 0
    width = SCORE_TILES
    while width >= 1:
        trips = lax.div(qi - done_tiles, width)

        def p1_group(g, c, width=width, base=done_tiles):
            for u in range(width):
                c = p1_body(base + g * width + u, c)
            return c

        carry = lax.fori_loop(0, trips, p1_group, carry)
        done_tiles = done_tiles + trips * width
        width //= 2
    mxs = [jnp.maximum(a, b) for a, b in zip(carry, score_chunk(qi, True))]
    mx = lanes(jnp.max(functools.reduce(jnp.maximum, mxs), axis=-1, keepdims=True))
    class_floor = lanes(jnp.min(functools.reduce(jnp.minimum, mxs), axis=-1, keepdims=True))

    def count_lanes(rows, v, strict=False):
        def tile(j, c):
            for c0 in range(0, T, LANES):
                sh = sc_ref[j, rows, c0:c0 + LANES]
                hit = (sh > v) if strict else (sh >= v)
                c = c + jnp.where(hit, 1.0, 0.0)
            return c

        def group(g, c):
            for u in range(COUNT_UNROLL):
                c = tile(g * COUNT_UNROLL + u, c)
            return c

        n_groups = lax.div(qi + 1, COUNT_UNROLL)
        c = lax.fori_loop(0, n_groups, group, jnp.zeros((COUNT_ROWS, LANES), F32))
        return lax.fori_loop(n_groups * COUNT_UNROLL, qi + 1, tile, c)

    def row_total(c):
        return jnp.broadcast_to(jnp.sum(c, axis=-1, keepdims=True), c.shape)

    slabs = [slice(r0, r0 + COUNT_ROWS) for r0 in range(0, T, COUNT_ROWS)]
    n_causal = (qi * T + 1 + lax.broadcasted_iota(jnp.int32, (T, LANES), 0)).astype(F32)
    keep_all = n_causal <= kf
    lo0 = jnp.where(keep_all, BELOW_ALL, class_floor)
    above_max = mx + jnp.maximum(jnp.abs(mx) * (2.0 ** -20), 1e-30)
    hi0 = jnp.where(keep_all, BELOW_ALL, above_max)
    lo_ref[...] = lo0
    hi_ref[...] = hi0
    mid_ref[...] = lo0 + 0.5 * (hi0 - lo0)
    clo_ref[...] = n_causal

    def bis_body(open_widths):
        idle = jnp.zeros((COUNT_ROWS, LANES), F32)
        partial = [lax.cond(w > 0.0, lambda rows=rows: count_lanes(rows, mid_ref[rows]), lambda: idle)
                   for rows, w in zip(slabs, open_widths)]
        new_widths = tuple(jnp.max(hi_ref[rows] - lo_ref[rows]) for rows in slabs)
        for rows, c in zip(slabs, partial):
            lo, hi, mid = lo_ref[rows], hi_ref[rows], mid_ref[rows]
            cnt = row_total(c)
            up = cnt >= kf
            lo = jnp.where(up, mid, lo)
            c_lo = jnp.where(up, cnt, clo_ref[rows])
            hi = jnp.where(up, hi, mid)
            nxt = lo + 0.5 * (hi - lo)
            stop = jnp.logical_or(c_lo == kf, jnp.logical_or(nxt <= lo, nxt >= hi))
            hi = jnp.where(stop, lo, hi)
            lo_ref[rows] = lo
            hi_ref[rows] = hi
            mid_ref[rows] = jnp.where(stop, lo, nxt)
            clo_ref[rows] = c_lo
        return new_widths

    lax.while_loop(lambda ws: functools.reduce(jnp.maximum, ws) > 0.0, bis_body,
                   tuple(jnp.max((hi0 - lo0)[rows]) for rows in slabs))
    thr = lo_ref[...]

    tied_f = jnp.where(clo_ref[...] > kf, 1.0, 0.0)

    @pl.when(jnp.max(tied_f) > 0.0)
    def _():
        above = jnp.concatenate(
            [row_total(count_lanes(rows, thr[rows], strict=True)) for rows in slabs], axis=0)
        need = (kf - above)[:, :1]
        r = lax.broadcasted_iota(jnp.int32, (T, T), 0)
        c = lax.broadcasted_iota(jnp.int32, (T, T), 1)
        before = jnp.where(r < c, 1.0, 0.0).astype(BF16)
        thr_col = thr[:, :1]
        tied_col = tied_f[:, :1] > 0.0

        def body(j, seen):
            s = sc_ref[j]
            eq = jnp.where(s == thr_col, 1.0, 0.0)
            rank = seen + _dot(eq.astype(BF16), before)
            drop = jnp.logical_and(tied_col, jnp.logical_and(eq > 0.0, rank >= need))
            sc_ref[j] = jnp.where(drop, -jnp.inf, s)
            return seen + jnp.sum(eq, axis=-1, keepdims=True)

        lax.fori_loop(0, qi + 1, body, jnp.zeros((T, 1), F32))

    m_ref[...] = jnp.full(m_ref.shape, NEG_BIG, F32)
    l_ref[...] = jnp.zeros(l_ref.shape, F32)
    acc_ref[...] = jnp.zeros(acc_ref.shape, F32)

    def attend(j0, ntiles, band):
        sel = [jnp.where(sh >= thr, 0.0, NEG_BIG) for t in range(ntiles) for sh in halves(sc_ref[j0 + t])]
        start = pl.multiple_of(j0 * T, T)
        keys = pl.ds(start, ntiles * T)
        width = ntiles * T

        def logits_to_scratch(hh):
            sl = slice(hh * A_HEAD_DIM, (hh + 1) * A_HEAD_DIM)
            s_ref[hh % 2, :, :width] = _dot_nt(q_ref[:, sl], k_ref[keys, sl])

        logits_to_scratch(0)
        pending = None
        for hh in range(A_HEADS):
            sl = slice(hh * A_HEAD_DIM, (hh + 1) * A_HEAD_DIM)
            if hh + 1 < A_HEADS:
                logits_to_scratch(hh + 1)
            s = s_ref[hh % 2, :, :width]
            if band is not None:
                s = s + rb_ref[hh, :, band * T:(band + ntiles) * T]
            sh = [s[:, c * LANES:(c + 1) * LANES] + m for c, m in enumerate(sel)]
            m_prev = m_ref[hh]
            m_cur = jnp.max(functools.reduce(jnp.maximum, sh), axis=-1, keepdims=True)
            m_new = jnp.maximum(m_prev, lanes(m_cur))
            alpha = jnp.exp2(m_prev - m_new)
            ph = [jnp.exp2(x - m_new) for x in sh]
            l_ref[hh] = alpha * l_ref[hh] + functools.reduce(jnp.add, ph)
            p = jnp.concatenate([x.astype(BF16) for x in ph], axis=-1)
            acc_ref[hh] = alpha * acc_ref[hh]
            m_ref[hh] = m_new
            if pending is not None:
                acc_ref[pending[0]] += pending[1]
            pending = (hh, _dot(p, v_ref[keys, sl]))
        acc_ref[pending[0]] += pending[1]

    n_far = jnp.maximum(qi - 1, 0)
    done_tiles = 0
    width = FAR_TILES
    while width >= 1:
        trips = lax.div(n_far - done_tiles, width)

        def far_body(i, c, width=width, base=done_tiles):
            attend(base + i * width, width, None)
            return c

        lax.fori_loop(0, trips, far_body, 0)
        done_tiles = done_tiles + trips * width
        width //= 2

    @pl.when(qi > 0)
    def _():
        attend(qi - 1, 2, 0)

    @pl.when(qi == 0)
    def _():
        attend(0, 1, 1)

    for hh in range(A_HEADS):
        sl = slice(hh * A_HEAD_DIM, (hh + 1) * A_HEAD_DIM)
        den = jnp.sum(l_ref[hh], axis=-1, keepdims=True)
        o_ref[:, sl] = (acc_ref[hh] / den * g_ref[:, sl].astype(F32)).astype(BF16)


def _dsa_call(iq, iw, ikz, qa, ka, va, ga, rb, topk):
    bsz, seq, _ = qa.shape
    T = DSA_T
    nq = seq // T
    assert topk <= T and seq % T == 0
    once = pl.Buffered(1)
    return pl.pallas_call(
        functools.partial(_dsa_kernel, topk=topk),
        grid=(bsz, nq),
        in_specs=[
            pl.BlockSpec((None, IDX_PAIRS, T, LANES), lambda b, i: (b, 0, i, 0)),
            pl.BlockSpec((None, T, IDX_HEADS), lambda b, i: (b, i, 0)),
            pl.BlockSpec((None, 2, LANES, seq), lambda b, i: (b, 0, 0, 0), pipeline_mode=once),
            pl.BlockSpec((None, T, A_WIDTH), lambda b, i: (b, i, 0)),
            pl.BlockSpec((None, seq, A_WIDTH), lambda b, i: (b, 0, 0), pipeline_mode=once),
            pl.BlockSpec((None, seq, A_WIDTH), lambda b, i: (b, 0, 0), pipeline_mode=once),
            pl.BlockSpec((None, T, A_WIDTH), lambda b, i: (b, i, 0)),
            pl.BlockSpec((A_HEADS, T, 2 * T), lambda b, i: (0, 0, 0), pipeline_mode=once),
        ],
        out_specs=pl.BlockSpec((None, T, A_WIDTH), lambda b, i: (b, i, 0)),
        out_shape=jax.ShapeDtypeStruct((bsz, seq, A_WIDTH), BF16),
        scratch_shapes=[
            pltpu.VMEM((nq, T, T), F32),
            pltpu.VMEM((IDX_HEADS, T, LANES), F32),
            pltpu.VMEM((T, LANES), F32),
            pltpu.VMEM((T, LANES), F32),
            pltpu.VMEM((T, LANES), F32),
            pltpu.VMEM((T, LANES), F32),
            pltpu.VMEM((2, T, FAR_TILES * T), F32),
            pltpu.VMEM((A_HEADS, T, LANES), F32),
            pltpu.VMEM((A_HEADS, T, LANES), F32),
            pltpu.VMEM((A_HEADS, T, A_HEAD_DIM), F32),
        ],
        compiler_params=_cparams(("arbitrary", "arbitrary")),
        name="dsa",
    )(iq, iw, ikz, qa, ka, va, ga, rb)


def _swa_kernel(sink_ref, q_ref, kc_ref, vc_ref, kcs_ref, vcs_ref, kp_ref, vp_ref, kps_ref, vps_ref,
                g_ref, bias_ref, qm_ref, km_ref, vm_ref, gm_ref, o_ref, om_ref):
    first = pl.program_id(1) == 0
    blk = WINDOW
    tiles = B_HEADS // B_KV_HEADS // 2
    low_half = lax.broadcasted_iota(jnp.int32, (2 * blk, LANES), 1) < B_HEAD_DIM
    ones = jnp.ones((2 * blk, LANES), BF16)
    for sb in range(SWA_T // blk):
        rows = slice(sb * blk, (sb + 1) * blk)
        prev = slice((sb - 1) * blk, sb * blk)

        def window(cur_ref, prev_ref):
            before = prev_ref[...] if sb == 0 else cur_ref[prev, :]
            return jnp.concatenate([before, cur_ref[rows, :]], axis=0)

        k_nat, v_nat = window(kc_ref, kp_ref), window(vc_ref, vp_ref)
        k_swp, v_swp = window(kcs_ref, kps_ref), window(vcs_ref, vps_ref)
        for g in range(B_KV_HEADS):
            q = jnp.concatenate(
                [q_ref[rows, (tiles * g + i) * LANES:(tiles * g + i + 1) * LANES] for i in range(tiles)], axis=0)
            res = None
            for e in range(2):
                keep = low_half if e == 0 else jnp.logical_not(low_half)
                k_src, v_src = (k_nat, v_nat) if g == e else (k_swp, v_swp)
                kk = jnp.where(keep, k_src, jnp.zeros_like(k_src))
                vv = jnp.where(keep, v_src, jnp.zeros_like(v_src))
                logits = _dot_nt(q, kk) + bias_ref[2 * g + e]
                l_prev, l_cur = logits[:, :blk], logits[:, blk:]
                if sb == 0:
                    l_prev = jnp.where(first, NEG_BIG, l_prev)
                sink = jnp.concatenate(
                    [jnp.full((blk, blk), sink_ref[2 * (tiles * g + i) + e], F32) for i in range(tiles)], axis=0)
                m = jnp.max(jnp.maximum(l_prev, l_cur), axis=-1, keepdims=True)
                m = jnp.maximum(jnp.broadcast_to(m, sink.shape), sink)
                p = jnp.concatenate([jnp.exp(l_prev - m).astype(BF16), jnp.exp(l_cur - m).astype(BF16)], axis=1)
                pv = _dot(p, jnp.concatenate([vv, ones], axis=1))
                part = pv[:, :LANES] / (pv[:, LANES:] + jnp.exp(sink - m))
                res = part if res is None else res + part
            for i in range(tiles):
                cols = slice((tiles * g + i) * LANES, (tiles * g + i + 1) * LANES)
                o_ref[rows, cols] = (res[i * blk:(i + 1) * blk] * g_ref[rows, cols].astype(F32)).astype(BF16)
    _mem_kernel(qm_ref, km_ref, vm_ref, gm_ref, om_ref)


def _swa_call(sinks, qb, kb, vb, kbs, vbs, gb, bias, qc, kc, vc, gc):
    bsz, seq, _ = qb.shape
    mlen = kc.shape[1]
    whole = lambda b, i: (b, 0, 0)
    T = SWA_T
    per = T // WINDOW
    cur = lambda b, i: (b, i, 0)
    prev = lambda b, i: (b, jnp.maximum(i * per - 1, 0), 0)
    kv_cur = pl.BlockSpec((None, T, B_KV_WIDTH), cur)
    kv_prev = pl.BlockSpec((None, WINDOW, B_KV_WIDTH), prev)
    return pl.pallas_call(
        _swa_kernel,
        grid=(bsz, seq // T),
        in_specs=[
            pl.BlockSpec(memory_space=pltpu.SMEM),
            pl.BlockSpec((None, T, B_WIDTH), cur),
            kv_cur, kv_cur, kv_cur, kv_cur,
            kv_prev, kv_prev, kv_prev, kv_prev,
            pl.BlockSpec((None, T, B_WIDTH), cur),
            pl.BlockSpec(bias.shape, lambda b, i: (0, 0, 0)),
            pl.BlockSpec((None, T, C_WIDTH), cur),
            pl.BlockSpec((None, mlen, C_WIDTH), whole),
            pl.BlockSpec((None, mlen, C_WIDTH), whole),
            pl.BlockSpec((None, T, C_WIDTH), cur),
        ],
        out_specs=[pl.BlockSpec((None, T, B_WIDTH), cur), pl.BlockSpec((None, T, C_WIDTH), cur)],
        out_shape=[jax.ShapeDtypeStruct((bsz, seq, B_WIDTH), BF16), jax.ShapeDtypeStruct((bsz, seq, C_WIDTH), BF16)],
        compiler_params=_cparams(("parallel", "parallel")),
        name="swa_mem",
    )(sinks, qb, kb, vb, kbs, vbs, kb, vb, kbs, vbs, gb, bias, qc, kc, vc, gc)


def _mem_kernel(q_ref, k_ref, v_ref, g_ref, o_ref):
    for hh in range(C_HEADS):
        sl = slice(hh * C_HEAD_DIM, (hh + 1) * C_HEAD_DIM)
        s = _dot_nt(q_ref[:, sl], k_ref[:, sl])
        m = jnp.max(s, axis=-1, keepdims=True)
        p = jnp.exp(s - m)
        den = jnp.sum(p, axis=-1, keepdims=True)
        o = _dot(p.astype(BF16), v_ref[:, sl]) / den
        o_ref[:, sl] = (o * g_ref[:, sl].astype(F32)).astype(BF16)


def _mem_call(qc, kc, vc, gc):
    bsz, seq, _ = qc.shape
    mlen = kc.shape[1]
    T = SWA_T
    cur = lambda b, i: (b, i, 0)
    whole = lambda b, i: (b, 0, 0)
    return pl.pallas_call(
        _mem_kernel,
        grid=(bsz, seq // T),
        in_specs=[
            pl.BlockSpec((None, T, C_WIDTH), cur),
            pl.BlockSpec((None, mlen, C_WIDTH), whole),
            pl.BlockSpec((None, mlen, C_WIDTH), whole),
            pl.BlockSpec((None, T, C_WIDTH), cur),
        ],
        out_specs=pl.BlockSpec((None, T, C_WIDTH), cur),
        out_shape=jax.ShapeDtypeStruct((bsz, seq, C_WIDTH), BF16),
        compiler_params=_cparams(("parallel", "parallel")),
        name="mem_attn",
    )(qc, kc, vc, gc)


def _final_kernel(x_ref, h_ref, oa_ref, ob_ref, oc_ref, wma_ref, wmb_ref, wmc_ref,
                  ba_ref, bb_ref, bc_ref, wua_ref, wub_ref, wuc_ref, wo_ref, out_ref, acc_ref):
    c = pl.program_id(1)

    @pl.when(c == 0)
    def _():
        acc_ref[...] = jnp.zeros(acc_ref.shape, F32)

    h = h_ref[...]

    def branch(o_ref, wm_ref, b_ref, wu_ref):
        gate = 1.0 / (1.0 + jnp.exp(-(_dot_nt(h, wm_ref[...]) + b_ref[...])))
        return gate * _dot(o_ref[...], wu_ref[...])

    merged = (branch(oa_ref, wma_ref, ba_ref, wua_ref) + branch(ob_ref, wmb_ref, bb_ref, wub_ref)
              + branch(oc_ref, wmc_ref, bc_ref, wuc_ref))
    acc_ref[...] += _dot(merged.astype(BF16), wo_ref[...])

    @pl.when(c == pl.num_programs(1) - 1)
    def _():
        out_ref[...] = x_ref[...] + acc_ref[...]


def _final_call(x2d, h, oa, ob, oc, wmix, gate_bias, wua, wub, wuc, wo):
    n, d = x2d.shape
    tm = FINAL_ROWS
    nch = FINAL_NCHUNK
    cw = d // nch
    row = lambda i, c: (i, 0)

    def col(br):
        return lambda i, c: (0, br * nch + c)

    def wrow(br):
        return lambda i, c: (br * nch + c, 0)

    return pl.pallas_call(
        _final_kernel,
        grid=(n // tm, nch),
        in_specs=[
            pl.BlockSpec((tm, d), row),
            pl.BlockSpec((tm, d), row),
            pl.BlockSpec((tm, A_WIDTH), row),
            pl.BlockSpec((tm, B_WIDTH), row),
            pl.BlockSpec((tm, C_WIDTH), row),
            pl.BlockSpec((cw, d), wrow(0)), pl.BlockSpec((cw, d), wrow(1)), pl.BlockSpec((cw, d), wrow(2)),
            pl.BlockSpec((1, cw), col(0)), pl.BlockSpec((1, cw), col(1)), pl.BlockSpec((1, cw), col(2)),
            pl.BlockSpec((A_WIDTH, cw), lambda i, c: (0, c)),
            pl.BlockSpec((B_WIDTH, cw), lambda i, c: (0, c)),
            pl.BlockSpec((C_WIDTH, cw), lambda i, c: (0, c)),
            pl.BlockSpec((cw, d), lambda i, c: (c, 0)),
        ],
        out_specs=pl.BlockSpec((tm, d), row),
        out_shape=jax.ShapeDtypeStruct((n, d), F32),
        scratch_shapes=[pltpu.VMEM((tm, d), F32)],
        compiler_params=_cparams(("parallel", "arbitrary")),
        name="merge_out",
    )(x2d, h, oa, ob, oc, wmix, wmix, wmix, gate_bias, gate_bias, gate_bias, wua, wub, wuc, wo)


def _w_in_groups(d):
    sizes = (A_WIDTH, A_KV_RANK, IDX_HEADS * IDX_DIM, IDX_DIM, IDX_HEADS, A_WIDTH,
             B_WIDTH, B_KV_WIDTH, B_KV_WIDTH, B_WIDTH, C_WIDTH, C_WIDTH, N_BRANCH * d)
    cuts = np.cumsum((0,) + sizes).tolist()
    (aq, ackv, iq, ik, iw, ag, bq, bk, bv, bg, cq, cg, mix) = [
        (cuts[i], cuts[i + 1]) for i in range(len(sizes))]
    return [
        [(aq[0], ackv[1])],
        [(iq[0], ik[1]), LANES - IDX_DIM, iw, LANES - IDX_HEADS],
        [ag, bg, cg],
        [(bq[0], bv[1]), cq,
         (bk[0] + B_HEAD_DIM, bk[1]), (bk[0], bk[0] + B_HEAD_DIM),
         (bv[0] + B_HEAD_DIM, bv[1]), (bv[0], bv[0] + B_HEAD_DIM)],
        [mix],
    ]


def _group_width(group):
    return sum(p if isinstance(p, int) else p[1] - p[0] for p in group)


def _regroup_kernel(w_ref, *out_refs, groups):
    for o_ref, group in zip(out_refs, groups):
        row = 0
        for part in group:
            if isinstance(part, int):
                o_ref[row:row + part, :] = jnp.zeros((part, o_ref.shape[1]), BF16)
                row += part
            else:
                o_ref[row:row + part[1] - part[0], :] = w_ref[part[0]:part[1], :].astype(BF16)
                row += part[1] - part[0]


def _regroup_w_in(w_in_t):
    cols, d = w_in_t.shape
    groups = _w_in_groups(d)
    lanes = REGROUP_LANES
    return pl.pallas_call(
        functools.partial(_regroup_kernel, groups=groups),
        grid=(d // lanes,),
        in_specs=[pl.BlockSpec((cols, lanes), lambda i: (0, i))],
        out_specs=[pl.BlockSpec((_group_width(g), lanes), lambda i: (0, i)) for g in groups],
        out_shape=[jax.ShapeDtypeStruct((_group_width(g), d), BF16) for g in groups],
        compiler_params=_cparams(("parallel",)),
        name="regroup_w_in",
    )(w_in_t)


def _layer(x, mem, norm_g, w_in, kv_norm_g, w_kv_up, idx_k_ln_g, idx_k_ln_b, q_norm_a, k_norm_a,
           q_norm_b, k_norm_b, sinks_b, mem_norm_g, w_mem_kv, q_norm_c, k_norm_c,
           w_up_a, w_up_b, w_up_c, gate_bias, w_o, rel_bias):
    bsz, seq, d = x.shape
    mlen = mem.shape[1]
    n = bsz * seq
    tm = ROW_TILE
    topk = min(TOPK_MAX, seq // 4)
    x2d = x.reshape(n, d)
    row2 = lambda v: v.reshape(1, -1)

    w_grp_a, w_grp_i, w_grp_g, w_grp_bc, w_mix = _regroup_w_in(jnp.swapaxes(w_in, 0, 1))
    g_x = row2(norm_g)

    qa, ka, va, h = _proj_call(
        _proj_a_kernel, "proj_a", x2d, tm,
        [g_x, w_grp_a, row2(kv_norm_g), w_kv_up.astype(BF16), row2(q_norm_a), row2(k_norm_a)],
        [jax.ShapeDtypeStruct((n, A_WIDTH), BF16)] * 3 + [jax.ShapeDtypeStruct((n, d), BF16)],
        [_row_spec(tm, A_WIDTH)] * 3 + [_row_spec(tm, d)])

    per_b = seq // tm
    iq, ik, iw = _proj_call(
        _proj_i_kernel, "proj_i", h, tm,
        [w_grp_i, row2(idx_k_ln_g), row2(idx_k_ln_b)],
        [jax.ShapeDtypeStruct((bsz, IDX_PAIRS, seq, LANES), BF16),
         jax.ShapeDtypeStruct((n, IDX_DIM), BF16),
         jax.ShapeDtypeStruct((n, IDX_HEADS), F32)],
        [pl.BlockSpec((None, IDX_PAIRS, tm, LANES), lambda i: (i // per_b, 0, i % per_b, 0)),
         _row_spec(tm, IDX_DIM), _row_spec(tm, IDX_HEADS)])

    ga, gb, gc = _proj_call(
        _proj_g_kernel, "proj_g", h, tm, [w_grp_g],
        [jax.ShapeDtypeStruct((n, A_WIDTH), BF16), jax.ShapeDtypeStruct((n, B_WIDTH), BF16),
         jax.ShapeDtypeStruct((n, C_WIDTH), BF16)],
        [_row_spec(tm, A_WIDTH), _row_spec(tm, B_WIDTH), _row_spec(tm, C_WIDTH)])

    bc_widths = (B_WIDTH, B_KV_WIDTH, B_KV_WIDTH, C_WIDTH, B_KV_WIDTH, B_KV_WIDTH)
    qb, kb, vb, qc, kbs, vbs = _proj_call(
        _proj_bc_kernel, "proj_bc", h, tm,
        [w_grp_bc, row2(jnp.tile(q_norm_b, 2)), row2(jnp.tile(k_norm_b, 2)), row2(q_norm_c)],
        [jax.ShapeDtypeStruct((n, w), BF16) for w in bc_widths],
        [_row_spec(tm, w) for w in bc_widths])

    mrows = bsz * mlen
    kc, vc = _proj_call(
        _proj_mem_kernel, "proj_mem", mem.reshape(mrows, d), min(tm, mrows),
        [row2(mem_norm_g), w_mem_kv.astype(BF16), row2(k_norm_c)],
        [jax.ShapeDtypeStruct((mrows, C_WIDTH), BF16)] * 2,
        [_row_spec(min(tm, mrows), C_WIDTH)] * 2)

    T = DSA_T
    i_idx = np.arange(T)[:, None]
    dist_a = T + i_idx - np.arange(2 * T)[None, :]
    rb_a = _bias_tiles(rel_bias, _t5_bucket_np(dist_a), 0, A_HEADS, NUM_BUCKETS - 1, LOG2E)
    i_idx = np.arange(WINDOW)[:, None]
    dist_b = WINDOW + i_idx - np.arange(2 * WINDOW)[None, :]
    bucket_b = np.where((dist_b >= 0) & (dist_b < WINDOW), _t5_bucket_np(dist_b), -1).astype(np.int32)
    bias_b = _bias_tiles(rel_bias, bucket_b, A_HEADS, B_HEADS, None, 1.0)
    tiles_b = B_HEADS // B_KV_HEADS // 2
    slot_heads = [2 * (tiles_b * g + i) + e for g in range(B_KV_HEADS) for e in range(2) for i in range(tiles_b)]
    bias_b = bias_b[np.array(slot_heads)].reshape(2 * B_KV_HEADS, tiles_b * WINDOW, 2 * WINDOW)

    r3 = lambda v, w: v.reshape(bsz, seq, w)
    ikt = jnp.swapaxes(ik.reshape(bsz, seq, IDX_DIM), 1, 2)
    zeros = jnp.zeros_like(ikt)
    ikz = jnp.stack([jnp.concatenate([ikt, zeros], axis=1), jnp.concatenate([zeros, ikt], axis=1)], axis=1)
    oa = _dsa_call(iq, iw.reshape(bsz, seq, IDX_HEADS), ikz,
                   r3(qa, A_WIDTH), r3(ka, A_WIDTH),
                   r3(va, A_WIDTH), r3(ga, A_WIDTH), rb_a, topk)
    ob, oc = _swa_call(sinks_b, r3(qb, B_WIDTH), r3(kb, B_KV_WIDTH), r3(vb, B_KV_WIDTH),
                       r3(kbs, B_KV_WIDTH), r3(vbs, B_KV_WIDTH), r3(gb, B_WIDTH), bias_b,
                       r3(qc, C_WIDTH), kc.reshape(bsz, mlen, C_WIDTH), vc.reshape(bsz, mlen, C_WIDTH),
                       r3(gc, C_WIDTH))

    out = _final_call(x2d, h, oa.reshape(n, A_WIDTH), ob.reshape(n, B_WIDTH), oc.reshape(n, C_WIDTH),
                      w_mix, row2(gate_bias), w_up_a.astype(BF16), w_up_b.astype(BF16),
                      w_up_c.astype(BF16), w_o.astype(BF16))
    return out.reshape(bsz, seq, d)


def kernel(x, mem, norm_g, w_in, kv_norm_g, w_kv_up, idx_k_ln_g, idx_k_ln_b, q_norm_a, k_norm_a, q_norm_b, k_norm_b, sinks_b, mem_norm_g, w_mem_kv, q_norm_c, k_norm_c, w_up_a, w_up_b, w_up_c, gate_bias, w_o, rel_bias):
    for l in range(norm_g.shape[0]):
        x = _layer(x, mem, norm_g[l], w_in[l], kv_norm_g[l], w_kv_up[l], idx_k_ln_g[l], idx_k_ln_b[l],
                   q_norm_a[l], k_norm_a[l], q_norm_b[l], k_norm_b[l], sinks_b[l], mem_norm_g[l],
                   w_mem_kv[l], q_norm_c[l], k_norm_c[l], w_up_a[l], w_up_b[l], w_up_c[l],
                   gate_bias[l], w_o[l], rel_bias)
    return x
```
